```python
import functools
import math
import jax
import jax.numpy as jnp
from jax import lax
import numpy as np

D_MODEL = 1024
BATCH = 16
SEQ = 4096
DEPTH = 2

GRID_W = 64
CTX_LEN = 256
HEAD_DIM = 64
ROPE_BASE = 10000.0
Q_BLOCK = 128
NORM_EPS = 1e-6
N_MOD = 6

HY_CH = D_MODEL // 2
HY_ORDER = 2
HY_BANDS = 16
HY_EMB = 1 + 2 * HY_BANDS
HY_HID = 64
HY_DIRS = 2
HY_DECAY_TARGET = 1e-2
HY_FAST_DECAY = 0.3
HY_SLOW_DECAY = 1.5
HY_FILTER_EPS = 1e-6

SWA_Q_HEADS = (D_MODEL // 2) // HEAD_DIM
SWA_KV_HEADS = SWA_Q_HEADS // 4
SWA_WINDOW = 128

GA_Q_HEADS = D_MODEL // HEAD_DIM
GA_KV_HEADS = GA_Q_HEADS // 4

N_GROUPS = 4
EXP_PER_GROUP = 8
N_EXPERTS = N_GROUPS * EXP_PER_GROUP
TOP_K = 2
D_EXPERT = D_MODEL // 2
MOE_BLOCK = 256

kernel_name = 'hybrid_hyena_swa_axialgqa_hmoe_dit'


def _rmsnorm(x, g):
    xf = x.astype(jnp.float32)
    y = xf * lax.rsqrt(jnp.mean(xf * xf, axis=-1, keepdims=True) + NORM_EPS)
    return (y * g.astype(jnp.float32)).astype(x.dtype)


def _axial_rope_tables(n_lat):
    rows = n_lat // GRID_W
    row, col = jnp.meshgrid(jnp.arange(rows), jnp.arange(GRID_W), indexing='ij')
    d_axis = HEAD_DIM // 2
    inv = ROPE_BASE ** (-jnp.arange(0, d_axis, 2, dtype=jnp.float32) / d_axis)
    ang_r = row.reshape(-1).astype(jnp.float32)[:, None] * inv[None, :]
    ang_c = col.reshape(-1).astype(jnp.float32)[:, None] * inv[None, :]
    return (jnp.cos(ang_r), jnp.sin(ang_r), jnp.cos(ang_c), jnp.sin(ang_c))


def _rotate(x, cos, sin):
    shape = (1, cos.shape[0]) + (1,) * (x.ndim - 3) + (cos.shape[1],)
    cos = cos.reshape(shape)
    sin = sin.reshape(shape)
    x1, x2 = jnp.split(x, 2, axis=-1)
    return jnp.concatenate([x1 * cos - x2 * sin, x2 * cos + x1 * sin], axis=-1)


def _apply_axial_rope(x, tables):
    cr, sr, cc, sc = tables
    d_axis = HEAD_DIM // 2
    xf = x.astype(jnp.float32)
    out = jnp.concatenate([_rotate(xf[..., :d_axis], cr, sr), _rotate(xf[..., d_axis:], cc, sc)], axis=-1)
    return out.astype(x.dtype)


def _joint_softmax(scores):
    m = functools.reduce(jnp.maximum, [jnp.max(s, axis=-1, keepdims=True) for s in scores])
    e = [jnp.exp(s - m) for s in scores]
    denom = functools.reduce(jnp.add, [jnp.sum(t, axis=-1, keepdims=True) for t in e])
    return [t / denom for t in e]


def _scores(q, k):
    return jnp.einsum('bqhgd,bkhd->bhgqk', q, k).astype(jnp.float32) * (HEAD_DIM ** -0.5)


def _values(p, v):
    return jnp.einsum('bhgqk,bkhd->bqhgd', p.astype(v.dtype), v)


def _sink_scores(sink, like):
    b, h, g, q, _ = like.shape
    return jnp.broadcast_to(sink.astype(jnp.float32)[None, :, :, None, None], (b, h, g, q, 1))


def _windowed_attention(q, k, v, kc, vc, sink):
    b, n, hkv, g, dh = q.shape
    nb = n // Q_BLOCK
    pad = ((0, 0), (Q_BLOCK, Q_BLOCK), (0, 0), (0, 0))
    kp = jnp.pad(k, pad)
    vp = jnp.pad(v, pad)
    qi = jnp.arange(Q_BLOCK)
    ki = jnp.arange(3 * Q_BLOCK)

    def block(i):
        q0 = i * Q_BLOCK
        qb = lax.dynamic_slice_in_dim(q, q0, Q_BLOCK, axis=1)
        kb = lax.dynamic_slice_in_dim(kp, q0, 3 * Q_BLOCK, axis=1)
        vb = lax.dynamic_slice_in_dim(vp, q0, 3 * Q_BLOCK, axis=1)
        t = q0 + qi
        s = q0 - Q_BLOCK + ki
        valid = (jnp.abs(t[:, None] - s[None, :]) <= SWA_WINDOW) & (s >= 0)[None, :] & (s < n)[None, :]
        s_lat = jnp.where(valid, _scores(qb, kb), -jnp.inf)
        s_ctx = _scores(qb, kc)
        p_ctx, p_lat, _ = _joint_softmax([s_ctx, s_lat, _sink_scores(sink, s_ctx)])
        return _values(p_ctx, vc) + _values(p_lat, vb)

    out = lax.map(block, jnp.arange(nb))
    return jnp.moveaxis(out, 0, 1).reshape(b, n, hkv * g * dh)


def _dense_attention(q, k, v, kc, vc):
    b, n, hkv, g, dh = q.shape
    nb = n // Q_BLOCK

    def block(i):
        qb = lax.dynamic_slice_in_dim(q, i * Q_BLOCK, Q_BLOCK, axis=1)
        p_ctx, p_lat = _joint_softmax([_scores(qb, kc), _scores(qb, k)])
        return _values(p_ctx, vc) + _values(p_lat, v)

    out = lax.map(block, jnp.arange(nb))
    return jnp.moveaxis(out, 0, 1).reshape(b, n, hkv * g * dh)


def _context_attention(qc, kc, vc, sink=None):
    b, lc, hkv, g, dh = qc.shape
    s = _scores(qc, kc)
    parts = [s] if sink is None else [s, _sink_scores(sink, s)]
    p = _joint_softmax(parts)[0]
    return _values(p, vc).reshape(b, lc, hkv * g * dh)


def _short_conv(u, w, b):
    n = u.shape[1]
    up = jnp.pad(u, ((0, 0), (1, 1), (0, 0)))
    return up[:, :n] * w[0] + up[:, 1:n + 1] * w[1] + up[:, 2:] * w[2] + b


def _hyena_filters(n, w1, b1, w2, b2, w3, freq):
    f32 = jnp.float32
    t = jnp.arange(n, dtype=f32)
    t_norm = t / max(n - 1, 1)
    bands = jnp.linspace(1e-4, HY_BANDS - 1, HY_BANDS, dtype=f32)
    ang = (2.0 * math.pi / n) * t[:, None] * bands[None, :]
    z = jnp.concatenate([t_norm[:, None], jnp.cos(ang), -jnp.sin(ang)], axis=-1)
    fr = freq.astype(f32)
    hdn = jnp.sin(fr * (z @ w1.astype(f32) + b1.astype(f32)))
    hdn = jnp.sin(fr * (hdn @ w2.astype(f32) + b2.astype(f32)))
    h = (hdn @ w3.astype(f32)).reshape(n, HY_DIRS, HY_ORDER, HY_CH)
    max_decay = math.log(HY_DECAY_TARGET) / HY_FAST_DECAY
    min_decay = math.log(HY_DECAY_TARGET) / HY_SLOW_DECAY
    deltas = jnp.linspace(min_decay, max_decay, HY_CH, dtype=f32)
    window = jnp.exp(-t_norm[:, None] * jnp.abs(deltas)[None, :])
    h = h * window[:, None, None, :]
    fwd = h[:, 0]
    bwd = h[1:, 1][::-1]
    filt = jnp.concatenate([fwd, jnp.zeros((1, HY_ORDER, HY_CH), f32), bwd], axis=0)
    return filt / (jnp.sum(jnp.abs(filt), axis=0, keepdims=True) + HY_FILTER_EPS)


def _long_conv(u, filt, skip):
    n = u.shape[1]
    uf = u.astype(jnp.float32)
    spec = jnp.fft.rfft(uf, n=2 * n, axis=1) * jnp.fft.rfft(filt, n=2 * n, axis=0)[None]
    y = jnp.fft.irfft(spec, n=2 * n, axis=1)[:, :n]
    return (y + uf * skip.astype(jnp.float32)).astype(u.dtype)


def _hyena(u, conv_w, conv_b, f_w1, f_b1, f_w2, f_b2, f_w3, f_freq, skip):
    u = _short_conv(u, conv_w, conv_b)
    v, x1, x2 = jnp.split(u, 3, axis=-1)
    filt = _hyena_filters(u.shape[1], f_w1, f_b1, f_w2, f_b2, f_w3, f_freq)
    z = x1 * _long_conv(v, filt[:, 0], skip[0])
    return x2 * _long_conv(z, filt[:, 1], skip[1])


def _even_mixer(h, hc, rope, w_in, w_out, conv_w, conv_b, f_w1, f_b1, f_w2, f_b2, f_w3, f_freq, skip, sink, with_ctx):
    b, n, _ = h.shape
    lc = hc.shape[1]
    g = SWA_Q_HEADS // SWA_KV_HEADS
    d_hy = 3 * HY_CH
    d_q = SWA_Q_HEADS * HEAD_DIM
    d_kv = SWA_KV_HEADS * HEAD_DIM
    hy_args = (conv_w, conv_b, f_w1, f_b1, f_w2, f_b2, f_w3, f_freq, skip)
    sink_hg = sink.reshape(SWA_KV_HEADS, g)
    p = h @ w_in
    q = _apply_axial_rope(p[..., d_hy:d_hy + d_q].reshape(b, n, SWA_Q_HEADS, HEAD_DIM), rope)
    q = q.reshape(b, n, SWA_KV_HEADS, g, HEAD_DIM)
    k = _apply_axial_rope(p[..., d_hy + d_q:d_hy + d_q + d_kv].reshape(b, n, SWA_KV_HEADS, HEAD_DIM), rope)
    v = p[..., d_hy + d_q + d_kv:].reshape(b, n, SWA_KV_HEADS, HEAD_DIM)
    pkv = hc @ w_in[:, d_hy + d_q:]
    kc = pkv[..., :d_kv].reshape(b, lc, SWA_KV_HEADS, HEAD_DIM)
    vc = pkv[..., d_kv:].reshape(b, lc, SWA_KV_HEADS, HEAD_DIM)
    y_hy = _hyena(p[..., :d_hy], *hy_args)
    y_att = _windowed_attention(q, k, v, kc, vc, sink_hg)
    y = jnp.concatenate([y_hy, y_att], axis=-1) @ w_out
    if not with_ctx:
        return y, None
    pqc = hc @ w_in[:, :d_hy + d_q]
    qc = pqc[..., d_hy:].reshape(b, lc, SWA_KV_HEADS, g, HEAD_DIM)
    yc_hy = _hyena(pqc[..., :d_hy], *hy_args)
    yc = jnp.concatenate([yc_hy, _context_attention(qc, kc, vc, sink_hg)], axis=-1) @ w_out
    return y, yc


def _odd_mixer(h, hc, rope, w_qkv, w_out, qn_g, kn_g, with_ctx):
    b, n, _ = h.shape
    lc = hc.shape[1]
    g = GA_Q_HEADS // GA_KV_HEADS
    d_q = GA_Q_HEADS * HEAD_DIM
    d_kv = GA_KV_HEADS * HEAD_DIM
    p = h @ w_qkv
    q = _rmsnorm(p[..., :d_q].reshape(b, n, GA_Q_HEADS, HEAD_DIM), qn_g)
    k = _rmsnorm(p[..., d_q:d_q + d_kv].reshape(b, n, GA_KV_HEADS, HEAD_DIM), kn_g)
    v = p[..., d_q + d_kv:].reshape(b, n, GA_KV_HEADS, HEAD_DIM)
    q = _apply_axial_rope(q, rope).reshape(b, n, GA_KV_HEADS, g, HEAD_DIM)
    k = _apply_axial_rope(k, rope)
    pkv = hc @ w_qkv[:, d_q:]
    kc = _rmsnorm(pkv[..., :d_kv].reshape(b, lc, GA_KV_HEADS, HEAD_DIM), kn_g)
    vc = pkv[..., d_kv:].reshape(b, lc, GA_KV_HEADS, HEAD_DIM)
    y = _dense_attention(q, k, v, kc, vc) @ w_out
    if not with_ctx:
        return y, None
    qc = _rmsnorm((hc @ w_qkv[:, :d_q]).reshape(b, lc, GA_Q_HEADS, HEAD_DIM), qn_g)
    yc = _context_attention(qc.reshape(b, lc, GA_KV_HEADS, g, HEAD_DIM), kc, vc) @ w_out
    return y, yc


def _grouped_experts(t, experts, gates, w_gate, w_up, w_down):
    n_tok, d = t.shape
    n_assign = n_tok * TOP_K
    e_flat = experts.reshape(n_assign)
    g_flat = gates.reshape(n_assign)
    tok = jnp.broadcast_to(jnp.arange(n_tok)[:, None], (n_tok, TOP_K)).reshape(n_assign)
    order = jnp.argsort(e_flat)
    e_s = e_flat[order]
    tok_s = tok[order]
    g_s = g_flat[order]
    counts = jnp.bincount(e_flat, length=N_EXPERTS)
    padded = ((counts + MOE_BLOCK - 1) // MOE_BLOCK) * MOE_BLOCK
    ends = jnp.cumsum(padded)
    pstart = ends - padded
    start = jnp.cumsum(counts) - counts
    slot = pstart[e_s] + (jnp.arange(n_assign) - start[e_s])
    n_blocks = -(-n_assign // MOE_BLOCK) + N_EXPERTS
    n_slots = n_blocks * MOE_BLOCK
    slot_tok = jnp.zeros((n_slots,), jnp.int32).at[slot].set(tok_s.astype(jnp.int32))
    slot_gate = jnp.zeros((n_slots,), jnp.float32).at[slot].set(g_s)
    block_exp = jnp.minimum(jnp.searchsorted(ends, jnp.arange(n_blocks) * MOE_BLOCK, side='right'), N_EXPERTS - 1)

    def run(args):
        e, toks, gb = args
        xb = t[toks]
        hid = jax.nn.silu(xb @ w_gate[e]) * (xb @ w_up[e])
        return (hid @ w_down[e]) * gb.astype(t.dtype)[:, None]

    out = lax.map(run, (block_exp, slot_tok.reshape(n_blocks, MOE_BLOCK), slot_gate.reshape(n_blocks, MOE_BLOCK)))
    return jnp.zeros((n_tok, d), t.dtype).at[slot_tok].add(out.reshape(n_slots, d))


def _hier_moe(t, wg, bg, we, be, w_gate, w_up, w_down):
    n_tok = t.shape[0]
    tf = t.astype(jnp.float32)
    g_logits = tf @ wg.astype(jnp.float32) + bg.astype(jnp.float32)
    _, grp = lax.top_k(g_logits, 1)
    p_grp = jnp.take_along_axis(jax.nn.softmax(g_logits, axis=-1), grp, axis=-1)
    e_logits = (tf @ we.astype(jnp.float32) + be.astype(jnp.float32)).reshape(n_tok, N_GROUPS, EXP_PER_GROUP)
    e_logits = jnp.take_along_axis(e_logits, grp[:, :, None], axis=1)[:, 0]
    top_l, top_i = lax.top_k(e_logits, TOP_K)
    gates = p_grp * jax.nn.softmax(top_l, axis=-1)
    experts = grp * EXP_PER_GROUP + top_i
    return _grouped_experts(t, experts, gates, w_gate, w_up, w_down)


def setup_inputs(seed: int = 0) -> dict:
    key = jax.random.key(seed)
    ks = list(jax.random.split(key, 32))
    f32 = jnp.float32
    d = D_MODEL
    n_even = (DEPTH + 1) // 2
    n_odd = DEPTH // 2
    d_in_even = 3 * HY_CH + (SWA_Q_HEADS + 2 * SWA_KV_HEADS) * HEAD_DIM
    d_mix_even = HY_CH + SWA_Q_HEADS * HEAD_DIM
    d_qkv_odd = (GA_Q_HEADS + 2 * GA_KV_HEADS) * HEAD_DIM
    d_mix_odd = GA_Q_HEADS * HEAD_DIM

    def nrm(i, shape, scale):
        return jax.random.normal(ks[i], shape, f32) * scale

    return {
        'x': nrm(0, (BATCH, SEQ, d), 1.0),
        'c': nrm(1, (BATCH, d), 1.0),
        'ctx': nrm(2, (BATCH, CTX_LEN, d), 1.0),
        'c_ctx': nrm(3, (d,), 1.0),
        'ada_w': nrm(4, (DEPTH, d, N_MOD * d), 0.5 * d ** -0.5),
        'ada_b': nrm(5, (DEPTH, N_MOD * d), 0.02),
        'norm1_g': 1.0 + nrm(6, (DEPTH, d), 0.02),
        'norm2_g': 1.0 + nrm(7, (DEPTH, d), 0.02),
        'ev_w_in': nrm(8, (n_even, d, d_in_even), d ** -0.5),
        'ev_w_out': nrm(9, (n_even, d_mix_even, d), d_mix_even ** -0.5),
        'hy_conv_w': nrm(10, (n_even, 3, 3 * HY_CH), 3 ** -0.5),
        'hy_conv_b': nrm(11, (n_even, 3 * HY_CH), 0.02),
        'hy_f_w1': nrm(12, (n_even, HY_EMB, HY_HID), 1.0),
        'hy_f_b1': nrm(13, (n_even, HY_HID), 0.02),
        'hy_f_w2': nrm(14, (n_even, HY_HID, HY_HID), HY_HID ** -0.5),
        'hy_f_b2': nrm(15, (n_even, HY_HID), 0.02),
        'hy_f_w3': nrm(16, (n_even, HY_HID, HY_DIRS * HY_ORDER * HY_CH), HY_HID ** -0.5),
        'hy_f_freq': 1.0 + nrm(17, (n_even, HY_HID), 0.02),
        'hy_skip': nrm(18, (n_even, HY_ORDER, HY_CH), 0.1),
        'swa_sink': nrm(19, (n_even, SWA_Q_HEADS), 0.5),
        'od_w_qkv': nrm(20, (n_odd, d, d_qkv_odd), d ** -0.5),
        'od_w_out': nrm(21, (n_odd, d_mix_odd, d), d_mix_odd ** -0.5),
        'od_q_norm_g': 1.0 + nrm(22, (n_odd, HEAD_DIM), 0.02),
        'od_k_norm_g': 1.0 + nrm(23, (n_odd, HEAD_DIM), 0.02),
        'rt_group_w': nrm(24, (DEPTH, d, N_GROUPS), d ** -0.5),
        'rt_group_b': nrm(25, (DEPTH, N_GROUPS), 0.01),
        'rt_exp_w': nrm(26, (DEPTH, d, N_EXPERTS), d ** -0.5),
        'rt_exp_b': nrm(27, (DEPTH, N_EXPERTS), 0.01),
        'moe_w_gate': nrm(28, (DEPTH, N_EXPERTS, d, D_EXPERT), d ** -0.5),
        'moe_w_up': nrm(29, (DEPTH, N_EXPERTS, d, D_EXPERT), d ** -0.5),
        'moe_w_down': nrm(30, (DEPTH, N_EXPERTS, D_EXPERT, d), D_EXPERT ** -0.5),
        'final_norm_g': 1.0 + nrm(31, (d,), 0.02),
    }


def reference(x, c, ctx, c_ctx, ada_w, ada_b, norm1_g, norm2_g, ev_w_in, ev_w_out, hy_conv_w, hy_conv_b,
              hy_f_w1, hy_f_b1, hy_f_w2, hy_f_b2, hy_f_w3, hy_f_freq, hy_skip, swa_sink, od_w_qkv, od_w_out,
              od_q_norm_g, od_k_norm_g, rt_group_w, rt_group_b, rt_exp_w, rt_exp_b, moe_w_gate, moe_w_up,
              moe_w_down, final_norm_g):
    b, n, d = x.shape
    rope = _axial_rope_tables(n)
    xc = ctx
    sc = jax.nn.silu(c)
    scc = jax.nn.silu(c_ctx)
    for layer in range(DEPTH):
        with_ctx = layer < DEPTH - 1
        mod = (sc @ ada_w[layer] + ada_b[layer]).reshape(b, N_MOD, 1, d)
        modc = (scc @ ada_w[layer] + ada_b[layer]).reshape(N_MOD, d)
        h = _rmsnorm(x, norm1_g[layer]) * (1 + mod[:, 1]) + mod[:, 0]
        hc = _rmsnorm(xc, norm1_g[layer]) * (1 + modc[1]) + modc[0]
        if layer % 2 == 0:
            e = layer // 2
            y, yc = _even_mixer(h, hc, rope, ev_w_in[e], ev_w_out[e], hy_conv_w[e], hy_conv_b[e], hy_f_w1[e],
                                hy_f_b1[e], hy_f_w2[e], hy_f_b2[e], hy_f_w3[e], hy_f_freq[e], hy_skip[e],
                                swa_sink[e], with_ctx)
        else:
            o = layer // 2
            y, yc = _odd_mixer(h, hc, rope, od_w_qkv[o], od_w_out[o], od_q_norm_g[o], od_k_norm_g[o], with_ctx)
        x = x + mod[:, 2] * y
        h2 = _rmsnorm(x, norm2_g[layer]) * (1 + mod[:, 4]) + mod[:, 3]
        moe_w = (rt_group_w[layer], rt_group_b[layer], rt_exp_w[layer], rt_exp_b[layer],
                 moe_w_gate[layer], moe_w_up[layer], moe_w_down[layer])
        if with_ctx:
            xc = xc + modc[2] * yc
            h2c = _rmsnorm(xc, norm2_g[layer]) * (1 + modc[4]) + modc[3]
            tokens = jnp.concatenate([h2.reshape(b * n, d), h2c.reshape(-1, d)], axis=0)
            out = _hier_moe(tokens, *moe_w)
            x = x + mod[:, 5] * out[:b * n].reshape(b, n, d)
            xc = xc + modc[5] * out[b * n:].reshape(xc.shape)
        else:
            x = x + mod[:, 5] * _hier_moe(h2.reshape(b * n, d), *moe_w).reshape(b, n, d)
    return _rmsnorm(x, final_norm_g)
```

```python
import functools
import math

import numpy as np
import jax
import jax.numpy as jnp
from jax import lax
from jax.experimental import pallas as pl
from jax.experimental.pallas import tpu as pltpu

F32 = jnp.float32
BF16 = jnp.bfloat16

HEAD_DIM = 64
GRID_W = 64
ROPE_BASE = 10000.0
Q_BLOCK = 128
NORM_EPS = 1e-6
N_MOD = 6
HY_ORDER = 2
HY_BANDS = 16
HY_DIRS = 2
HY_DECAY_TARGET = 1e-2
HY_FAST_DECAY = 0.3
HY_SLOW_DECAY = 1.5
HY_FILTER_EPS = 1e-6
SWA_WINDOW = 128
N_GROUPS = 4
EXP_PER_GROUP = 8
N_EXPERTS = N_GROUPS * EXP_PER_GROUP
TOP_K = 2
MOE_BLOCK = 256

LANES = 128
VMEM_LIMIT_BYTES = 56 * 1024 * 1024


def _params(*sem):
    return pltpu.CompilerParams(dimension_semantics=sem, vmem_limit_bytes=VMEM_LIMIT_BYTES)


def _rope_tables(n):
    d_axis = HEAD_DIM // 2
    t = jnp.arange(n)
    inv = ROPE_BASE ** (-jnp.arange(0, d_axis, 2, dtype=F32) / d_axis)
    ang_r = (t // GRID_W).astype(F32)[:, None] * inv[None, :]
    ang_c = (t % GRID_W).astype(F32)[:, None] * inv[None, :]
    cos = jnp.concatenate([jnp.cos(ang_r)] * 2 + [jnp.cos(ang_c)] * 2, axis=-1)
    sin = jnp.concatenate([-jnp.sin(ang_r), jnp.sin(ang_r), -jnp.sin(ang_c), jnp.sin(ang_c)], axis=-1)
    return jnp.tile(cos, (1, 2)), jnp.tile(sin, (1, 2))


def _head_mean_matrix():
    i = np.arange(LANES)
    return jnp.asarray((i[:, None] // HEAD_DIM == i[None, :] // HEAD_DIM) / HEAD_DIM, dtype=BF16)


def _proj_body(x_ref, g_ref, sc_ref, sh_ref, w_ref, cos_ref, sin_ref, ng_ref, bd_ref, *out_refs, segs):
    x = x_ref[0]
    h = x * lax.rsqrt(jnp.mean(x * x, axis=-1, keepdims=True) + NORM_EPS) * g_ref[...]
    hb = (h * (1.0 + sc_ref[0]) + sh_ref[0]).astype(BF16)
    for o_ref, (c0, width, kind, norm_row, rope, out_scale) in zip(out_refs, segs):
        seg = jnp.dot(hb, w_ref[:, c0:c0 + width], preferred_element_type=F32)
        if kind == "f32":
            o_ref[0] = seg
            continue
        if kind == "bf16":
            o_ref[0] = seg.astype(BF16)
            continue
        for j in range(width // LANES):
            ch = seg[:, j * LANES:(j + 1) * LANES]
            if norm_row is not None:
                sq = ch * ch
                hi = sq.astype(BF16)
                lo = (sq - hi.astype(F32)).astype(BF16)
                ms = (jnp.dot(hi, bd_ref[...], preferred_element_type=F32)
                      + jnp.dot(lo, bd_ref[...], preferred_element_type=F32))
                ch = ch * lax.rsqrt(ms + NORM_EPS) * ng_ref[norm_row:norm_row + 1, :]
            if rope:
                lane = lax.broadcasted_iota(jnp.int32, ch.shape, 1)
                partner = jnp.where(lane % 32 < 16, pltpu.roll(ch, LANES - 16, 1), pltpu.roll(ch, 16, 1))
                ch = ch * cos_ref[...] + partner * sin_ref[...]
            if out_scale != 1.0:
                ch = ch * out_scale
            o_ref[0, :, j * LANES:(j + 1) * LANES] = ch.astype(BF16)


def _proj(x, g, scale, shift, w, segs, rope_tabs=None, norm_g=None, tm=512):
    b, n, d = x.shape
    tm = min(tm, n)
    bm = scale.shape[0]
    mod_map = (lambda bi, i: (bi, 0, 0)) if bm > 1 else (lambda bi, i: (0, 0, 0))
    if rope_tabs is None:
        cos = sin = jnp.zeros((8, LANES), F32)
        tab_spec = pl.BlockSpec((8, LANES), lambda bi, i: (0, 0))
    else:
        cos, sin = rope_tabs
        tab_spec = pl.BlockSpec((tm, LANES), lambda bi, i: (i, 0))
    if norm_g is None:
        norm_g = jnp.ones((8, LANES), F32)
    out_shape = [jax.ShapeDtypeStruct((b, n, s[1]), F32 if s[2] == "f32" else BF16) for s in segs]
    out_specs = [pl.BlockSpec((1, tm, s[1]), lambda bi, i: (bi, i, 0)) for s in segs]
    return pl.pallas_call(
        functools.partial(_proj_body, segs=tuple(segs)),
        grid=(b, n // tm),
        in_specs=[
            pl.BlockSpec((1, tm, d), lambda bi, i: (bi, i, 0)),
            pl.BlockSpec((1, d), lambda bi, i: (0, 0)),
            pl.BlockSpec((1, 1, d), mod_map),
            pl.BlockSpec((1, 1, d), mod_map),
            pl.BlockSpec(w.shape, lambda bi, i: (0, 0)),
            tab_spec,
            tab_spec,
            pl.BlockSpec(norm_g.shape, lambda bi, i: (0, 0)),
            pl.BlockSpec((LANES, LANES), lambda bi, i: (0, 0)),
        ],
        out_specs=out_specs,
        out_shape=out_shape,
        compiler_params=_params("parallel", "parallel"),
        name="proj",
    )(x, g.reshape(1, d), scale, shift, w, cos, sin, norm_g, _head_mean_matrix())


def _stack_heads(q, j, g):
    return jnp.concatenate(
        [q[:, (j * g + gg) * HEAD_DIM:(j * g + gg + 1) * HEAD_DIM] for gg in range(g)], axis=0)


def _sink_column(sink_ref, j, g, qb):
    return jnp.concatenate([jnp.full((qb, 1), sink_ref[j * g + gg], F32) for gg in range(g)], axis=0)


def _swa_body(sink_ref, q_ref, kt_ref, v_ref, kct_ref, vc_ref, o_ref, *, n, hkv, g, qb, win):
    i = pl.program_id(1)
    start = pl.multiple_of(i * qb, qb)
    q = q_ref[0]
    rows = lax.broadcasted_iota(jnp.int32, (g * qb, 3 * qb), 0) % qb
    cols = lax.broadcasted_iota(jnp.int32, (g * qb, 3 * qb), 1)
    key_pos = cols + (i - 1) * qb
    valid = (jnp.abs(rows + qb - cols) <= win) & (key_pos >= 0) & (key_pos < n)
    for j in range(hkv):
        hs = slice(j * HEAD_DIM, (j + 1) * HEAD_DIM)
        q4 = _stack_heads(q, j, g)
        s_lat = jnp.dot(q4, kt_ref[0, hs, pl.ds(start, 3 * qb)], preferred_element_type=F32)
        s_lat = jnp.where(valid, s_lat, -jnp.inf)
        s_ctx = jnp.dot(q4, kct_ref[0, hs, :], preferred_element_type=F32)
        s_sink = _sink_column(sink_ref, j, g, qb)
        m = jnp.maximum(jnp.maximum(jnp.max(s_lat, axis=-1, keepdims=True),
                                    jnp.max(s_ctx, axis=-1, keepdims=True)), s_sink)
        e_lat = jnp.exp(s_lat - m)
        e_ctx = jnp.exp(s_ctx - m)
        den = (jnp.sum(e_lat, axis=-1, keepdims=True) + jnp.sum(e_ctx, axis=-1, keepdims=True)
               + jnp.exp(s_sink - m))
        o = (jnp.dot(e_ctx.astype(BF16), vc_ref[0, :, hs], preferred_element_type=F32)
             + jnp.dot(e_lat.astype(BF16), v_ref[0, pl.ds(start, 3 * qb), hs], preferred_element_type=F32))
        o = o / den
        for gg in range(g):
            c0 = (j * g + gg) * HEAD_DIM
            o_ref[0, :, c0:c0 + HEAD_DIM] = o[gg * qb:(gg + 1) * qb].astype(BF16)


def _windowed_attention(q, k, v, kc, vc, sink, hkv):
    b, n, dq = q.shape
    g = dq // HEAD_DIM // hkv
    qb = Q_BLOCK
    lc = kc.shape[1]
    dkv = hkv * HEAD_DIM
    kt = jnp.swapaxes(jnp.pad(k, ((0, 0), (qb, qb), (0, 0))), 1, 2)
    vp = jnp.pad(v, ((0, 0), (qb, qb), (0, 0)))
    kct = jnp.swapaxes(kc, 1, 2)
    return pl.pallas_call(
        functools.partial(_swa_body, n=n, hkv=hkv, g=g, qb=qb, win=SWA_WINDOW),
        grid=(b, n // qb),
        in_specs=[
            pl.BlockSpec(memory_space=pltpu.SMEM),
            pl.BlockSpec((1, qb, dq), lambda bi, i: (bi, i, 0)),
            pl.BlockSpec((1, dkv, n + 2 * qb), lambda bi, i: (bi, 0, 0)),
            pl.BlockSpec((1, n + 2 * qb, dkv), lambda bi, i: (bi, 0, 0)),
            pl.BlockSpec((1, dkv, lc), lambda bi, i: (bi, 0, 0)),
            pl.BlockSpec((1, lc, dkv), lambda bi, i: (bi, 0, 0)),
        ],
        out_specs=pl.BlockSpec((1, qb, dq), lambda bi, i: (bi, i, 0)),
        out_shape=jax.ShapeDtypeStruct((b, n, dq), BF16),
        compiler_params=_params("parallel", "parallel"),
        name="swa",
    )(sink.astype(F32), q, kt, vp, kct, vc)


def _full_attn_body(sink_ref, q_ref, kt_ref, v_ref, o_ref, *, hkv, g, qb, has_sink):
    q = q_ref[0]
    for j in range(hkv):
        hs = slice(j * HEAD_DIM, (j + 1) * HEAD_DIM)
        q4 = _stack_heads(q, j, g)
        s = jnp.dot(q4, kt_ref[0, hs, :], preferred_element_type=F32)
        m = jnp.max(s, axis=-1, keepdims=True)
        if has_sink:
            s_sink = _sink_column(sink_ref, j, g, qb)
            m = jnp.maximum(m, s_sink)
        e = jnp.exp(s - m)
        den = jnp.sum(e, axis=-1, keepdims=True)
        if has_sink:
            den = den + jnp.exp(s_sink - m)
        o = jnp.dot(e.astype(BF16), v_ref[0, :, hs], preferred_element_type=F32) / den
        for gg in range(g):
            c0 = (j * g + gg) * HEAD_DIM
            o_ref[0, :, c0:c0 + HEAD_DIM] = o[gg * qb:(gg + 1) * qb].astype(BF16)


def _full_attention(q, k, v, hkv, sink=None):
    b, n, dq = q.shape
    g = dq // HEAD_DIM // hkv
    qb = Q_BLOCK
    nk = k.shape[1]
    dkv = hkv * HEAD_DIM
    kt = jnp.swapaxes(k, 1, 2)
    has_sink = sink is not None
    sink = jnp.zeros((dq // HEAD_DIM,), F32) if sink is None else sink.astype(F32)
    return pl.pallas_call(
        functools.partial(_full_attn_body, hkv=hkv, g=g, qb=qb, has_sink=has_sink),
        grid=(b, n // qb),
        in_specs=[
            pl.BlockSpec(memory_space=pltpu.SMEM),
            pl.BlockSpec((1, qb, dq), lambda bi, i: (bi, i, 0)),
            pl.BlockSpec((1, dkv, nk), lambda bi, i: (bi, 0, 0)),
            pl.BlockSpec((1, nk, dkv), lambda bi, i: (bi, 0, 0)),
        ],
        out_specs=pl.BlockSpec((1, qb, dq), lambda bi, i: (bi, i, 0)),
        out_shape=jax.ShapeDtypeStruct((b, n, dq), BF16),
        compiler_params=_params("parallel", "parallel"),
        name="full_attn",
    )(sink, q, kt, v)


def _short_conv_body(u_ref, w_ref, b_ref, o_ref, *, n):
    x = u_ref[0]
    row = lax.broadcasted_iota(jnp.int32, x.shape, 0)
    prev = jnp.where(row == 0, 0.0, pltpu.roll(x, 1, 0))
    nxt = jnp.where(row == n - 1, 0.0, pltpu.roll(x, n - 1, 0))
    o_ref[0, 0] = prev * w_ref[0:1, :] + x * w_ref[1:2, :] + nxt * w_ref[2:3, :] + b_ref[...]


def _short_conv(u, w, bias, cb=256):
    b, n, c3 = u.shape
    c = c3 // 3
    per = c // cb
    return pl.pallas_call(
        functools.partial(_short_conv_body, n=n),
        grid=(b, c3 // cb),
        in_specs=[
            pl.BlockSpec((1, n, cb), lambda bi, j: (bi, 0, j)),
            pl.BlockSpec((3, cb), lambda bi, j: (0, j)),
            pl.BlockSpec((1, cb), lambda bi, j: (0, j)),
        ],
        out_specs=pl.BlockSpec((1, 1, n, cb), lambda bi, j: (j // per, bi, 0, j % per)),
        out_shape=jax.ShapeDtypeStruct((3, b, n, c), F32),
        compiler_params=_params("parallel", "parallel"),
        name="short_conv",
    )(u, w, bias.reshape(1, c3))


def _filter_body(band_ref, w1_ref, b1_ref, w2_ref, b2_ref, w3_ref, fr_ref, dl_ref, o_ref, s_ref, *, n, rt, c):
    i = pl.program_id(0)
    hp = lax.Precision.HIGHEST
    m = i * rt + lax.broadcasted_iota(jnp.int32, (rt, 1), 0)
    pos = jnp.where(m < n, m, 2 * n - m).astype(F32)
    t_norm = pos / max(n - 1, 1)
    ang = (2.0 * math.pi / n) * pos * band_ref[...]
    lane = lax.broadcasted_iota(jnp.int32, (rt, LANES), 1)
    z = jnp.where(lane == 0, t_norm,
                  jnp.where(lane <= HY_BANDS, jnp.cos(ang),
                            jnp.where(lane <= 2 * HY_BANDS, -jnp.sin(ang), 0.0)))
    fr = fr_ref[...]
    hdn = jnp.sin(fr * (jnp.dot(z, w1_ref[...], precision=hp, preferred_element_type=F32) + b1_ref[...]))
    hdn = jnp.sin(fr * (jnp.dot(hdn, w2_ref[...], precision=hp, preferred_element_type=F32) + b2_ref[...]))
    h = jnp.dot(hdn, w3_ref[...], precision=hp, preferred_element_type=F32)
    h = h * jnp.exp(-t_norm * dl_ref[...])
    half = HY_ORDER * c
    sel = jnp.where(m < n, h[:, :half], jnp.where(m > n, -h[:, half:], 0.0))
    for o in range(HY_ORDER):
        o_ref[o] = sel[:, o * c:(o + 1) * c]

    @pl.when(i == 0)
    def _():
        s_ref[...] = jnp.zeros_like(s_ref)

    s_ref[...] += jnp.sum(jnp.abs(sel), axis=0, keepdims=True)


def _hyena_filters(n, w1, b1, w2, b2, w3, freq, c):
    rt = min(1024, n)
    hid = w1.shape[1]
    bands = jnp.linspace(1e-4, HY_BANDS - 1, HY_BANDS, dtype=F32)
    band_row = jnp.zeros((1, LANES), F32).at[0, 1:1 + 2 * HY_BANDS].set(jnp.tile(bands, 2))
    w1p = jnp.zeros((LANES, hid), F32).at[:w1.shape[0]].set(w1)
    max_decay = math.log(HY_DECAY_TARGET) / HY_FAST_DECAY
    min_decay = math.log(HY_DECAY_TARGET) / HY_SLOW_DECAY
    deltas = jnp.abs(jnp.linspace(min_decay, max_decay, c, dtype=F32))
    dl = jnp.tile(deltas, HY_DIRS * HY_ORDER).reshape(1, -1)
    full = lambda a: pl.BlockSpec(a.shape, lambda i: (0,) * a.ndim)
    args = (band_row, w1p, b1.reshape(1, hid), w2, b2.reshape(1, hid), w3, freq.reshape(1, hid), dl)
    return pl.pallas_call(
        functools.partial(_filter_body, n=n, rt=rt, c=c),
        grid=(2 * n // rt,),
        in_specs=[full(a) for a in args],
        out_specs=[pl.BlockSpec((HY_ORDER, rt, c), lambda i: (0, i, 0)),
                   pl.BlockSpec((1, HY_ORDER * c), lambda i: (0, 0))],
        out_shape=[jax.ShapeDtypeStruct((HY_ORDER, 2 * n, c), F32),
                   jax.ShapeDtypeStruct((1, HY_ORDER * c), F32)],
        compiler_params=_params("arbitrary"),
        name="hyena_filter",
    )(*args)


def _dft_split(n):
    l2 = 32 if n >= 2048 else 16
    return 2 * n // l2, l2


def _dft_constants(n):
    l1, l2 = _dft_split(n)
    h1 = l1 // 2
    nn = 2 * n
    k1 = np.arange(h1)[:, None]
    a = 2 * np.pi * (k1 + 0.5) * np.arange(l1)[None, :] / l1
    w1 = np.concatenate([np.cos(a), -np.sin(a)], axis=0)
    t = 2 * np.pi * (k1 + 0.5) * np.arange(l2)[None, :] / nn
    tw_cos, tw_sin = np.cos(t), np.sin(t)
    p = 2 * np.pi * np.arange(l2)[:, None] * np.arange(l2)[None, :] / l2
    w2 = np.block([[np.cos(p), np.sin(p)], [-np.sin(p), np.cos(p)]])
    w2i = np.block([[np.cos(p), -np.sin(p)], [np.sin(p), np.cos(p)]])
    ai = 2 * np.pi * np.arange(h1)[:, None] * (np.arange(h1)[None, :] + 0.5) / l1
    w1i = (2.0 / nn) * np.concatenate([np.cos(ai), -np.sin(ai)], axis=1)
    c = lambda m, dt: jnp.asarray(m, dtype=dt)
    return dict(
        l1=l1, l2=l2, h1=h1,
        w1=c(w1, BF16), w2=c(w2, BF16), w2i=c(w2i, BF16), w1i=c(w1i, BF16),
        tw_cos_fwd=c(tw_cos.T[:, :, None], F32), tw_sin_fwd=c(tw_sin.T[:, :, None], F32),
        tw_cos_inv=c(tw_cos[:, :, None], F32), tw_sin_inv=c(tw_sin[:, :, None], F32),
    )


def _ct_fwd1_body(x_ref, w_ref, tc_ref, ts_ref, nrm_ref, o_ref, *, lb, c, h1, normalise):
    for q in range(lb):
        x = x_ref[0, 0, :, q * c:(q + 1) * c]
        if normalise:
            x = x / (nrm_ref[0] + HY_FILTER_EPS)
        a = jnp.dot(w_ref[...], x.astype(BF16), preferred_element_type=F32)
        ar, ai = a[:h1], a[h1:]
        tc, ts = tc_ref[q], ts_ref[q]
        o_ref[0, 0, q] = (ar * tc + ai * ts).astype(BF16)
        o_ref[0, 1, q] = (ai * tc - ar * ts).astype(BF16)


def _ct_fwd1(xs, idx, consts, c, norms=None, lb=8):
    _, b, k1n, _ = xs.shape
    l2, h1 = consts["l2"], consts["h1"]
    lb = min(lb, l2)
    w = consts["w1"][:, :k1n]
    normalise = norms is not None
    if norms is None:
        norms = jnp.zeros((b, 1, c), F32)
    return pl.pallas_call(
        functools.partial(_ct_fwd1_body, lb=lb, c=c, h1=h1, normalise=normalise),
        grid=(b, l2 // lb),
        in_specs=[
            pl.BlockSpec((1, 1, k1n, lb * c), lambda bi, i: (idx, bi, 0, i)),
            pl.BlockSpec(w.shape, lambda bi, i: (0, 0)),
            pl.BlockSpec((lb, h1, 1), lambda bi, i: (i, 0, 0)),
            pl.BlockSpec((lb, h1, 1), lambda bi, i: (i, 0, 0)),
            pl.BlockSpec((1, 1, c), lambda bi, i: (bi, 0, 0)),
        ],
        out_specs=pl.BlockSpec((1, 2, lb, h1, c), lambda bi, i: (bi, 0, i, 0, 0)),
        out_shape=jax.ShapeDtypeStruct((b, 2, l2, h1, c), BF16),
        compiler_params=_params("parallel", "parallel"),
        name="ct_fwd1",
    )(xs, w, consts["tw_cos_fwd"], consts["tw_sin_fwd"], norms)


def _ct_spec_body(a_ref, w2_ref, o_ref, *, l2):
    a = jnp.concatenate([a_ref[0, 0], a_ref[0, 1]], axis=0)
    x = jnp.dot(w2_ref[...], a, preferred_element_type=F32)
    o_ref[0, 0] = x[:l2]
    o_ref[0, 1] = x[l2:]


def _ct_spectrum(a, consts, c, kb=8):
    b = a.shape[0]
    l2, h1 = consts["l2"], consts["h1"]
    kb = min(kb, h1)
    return pl.pallas_call(
        functools.partial(_ct_spec_body, l2=l2),
        grid=(b, h1 // kb),
        in_specs=[pl.BlockSpec((1, 2, l2, kb * c), lambda bi, i: (bi, 0, 0, i)),
                  pl.BlockSpec((2 * l2, 2 * l2), lambda bi, i: (0, 0))],
        out_specs=pl.BlockSpec((1, 2, l2, kb * c), lambda bi, i: (bi, 0, 0, i)),
        out_shape=jax.ShapeDtypeStruct((b, 2, l2, h1 * c), F32),
        compiler_params=_params("parallel", "parallel"),
        name="ct_spectrum",
    )(a, consts["w2"])


def _ct_mid_body(a_ref, h_ref, w2_ref, w2i_ref, tc_ref, ts_ref, o_ref, *, l2, kb, c):
    a = jnp.concatenate([a_ref[0, 0], a_ref[0, 1]], axis=0)
    x = jnp.dot(w2_ref[...], a, preferred_element_type=F32)
    xr, xi = x[:l2], x[l2:]
    hr, hi = h_ref[0, 0], h_ref[0, 1]
    y = jnp.concatenate([xr * hr - xi * hi, xr * hi + xi * hr], axis=0).astype(BF16)
    bm = jnp.dot(w2i_ref[...], y, preferred_element_type=F32)
    br, bi = bm[:l2], bm[l2:]
    for q in range(kb):
        cs = slice(q * c, (q + 1) * c)
        tc, ts = tc_ref[q], ts_ref[q]
        o_ref[0, 0, :, cs] = (br[:, cs] * tc - bi[:, cs] * ts).astype(BF16)
        o_ref[0, 1, :, cs] = (br[:, cs] * ts + bi[:, cs] * tc).astype(BF16)


def _ct_mid(a, hspec, order, consts, c, kb=8):
    b = a.shape[0]
    l2, h1 = consts["l2"], consts["h1"]
    kb = min(kb, h1)
    return pl.pallas_call(
        functools.partial(_ct_mid_body, l2=l2, kb=kb, c=c),
        grid=(b, h1 // kb),
        in_specs=[
            pl.BlockSpec((1, 2, l2, kb * c), lambda bi, i: (bi, 0, 0, i)),
            pl.BlockSpec((1, 2, l2, kb * c), lambda bi, i: (order, 0, 0, i)),
            pl.BlockSpec((2 * l2, 2 * l2), lambda bi, i: (0, 0)),
            pl.BlockSpec((2 * l2, 2 * l2), lambda bi, i: (0, 0)),
            pl.BlockSpec((kb, l2, 1), lambda bi, i: (i, 0, 0)),
            pl.BlockSpec((kb, l2, 1), lambda bi, i: (i, 0, 0)),
        ],
        out_specs=pl.BlockSpec((1, 2, l2, kb * c), lambda bi, i: (bi, 0, 0, i)),
        out_shape=jax.ShapeDtypeStruct((b, 2, l2, h1 * c), BF16),
        compiler_params=_params("parallel", "parallel"),
        name="ct_mid",
    )(a, hspec, consts["w2"], consts["w2i"], consts["tw_cos_inv"], consts["tw_sin_inv"])


def _ct_inv1_body(b_ref, w_ref, u_ref, gate_ref, skip_ref, o_ref, *, lb, c):
    for q in range(lb):
        bb = jnp.concatenate([b_ref[0, 0, q], b_ref[0, 1, q]], axis=0)
        y = jnp.dot(w_ref[...], bb, preferred_element_type=F32)
        cs = slice(q * c, (q + 1) * c)
        o_ref[0, :, cs] = gate_ref[0, 0, :, cs] * (y + u_ref[0, 0, :, cs] * skip_ref[...])


def _ct_inv1(bsp, u, u_idx, gate, gate_idx, skip, consts, c, lb=8):
    b = bsp.shape[0]
    l2, h1 = consts["l2"], consts["h1"]
    lb = min(lb, l2)
    return pl.pallas_call(
        functools.partial(_ct_inv1_body, lb=lb, c=c),
        grid=(b, l2 // lb),
        in_specs=[
            pl.BlockSpec((1, 2, lb, h1, c), lambda bi, i: (bi, 0, i, 0, 0)),
            pl.BlockSpec((h1, 2 * h1), lambda bi, i: (0, 0)),
            pl.BlockSpec((1, 1, h1, lb * c), lambda bi, i: (u_idx, bi, 0, i)),
            pl.BlockSpec((1, 1, h1, lb * c), lambda bi, i: (gate_idx, bi, 0, i)),
            pl.BlockSpec((1, c), lambda bi, i: (0, 0)),
        ],
        out_specs=pl.BlockSpec((1, h1, lb * c), lambda bi, i: (bi, 0, i)),
        out_shape=jax.ShapeDtypeStruct((b, h1, l2 * c), F32),
        compiler_params=_params("parallel", "parallel"),
        name="ct_inv1",
    )(bsp, consts["w1i"], u, gate, skip.reshape(1, c))


def _hyena(u, conv_w, conv_b, f_w1, f_b1, f_w2, f_b2, f_w3, f_freq, skip):
    b, n, c3 = u.shape
    c = c3 // 3
    consts = _dft_constants(n)
    l1, l2, h1 = consts["l1"], consts["l2"], consts["h1"]
    filt, norms = _hyena_filters(n, f_w1, f_b1, f_w2, f_b2, f_w3, f_freq, c)
    fa = _ct_fwd1(filt.reshape(1, HY_ORDER, l1, l2 * c), 0, consts, c, norms=norms.reshape(HY_ORDER, 1, c))
    hspec = _ct_spectrum(fa.reshape(HY_ORDER, 2, l2, h1 * c), consts, c)
    parts = _short_conv(u, conv_w, conv_b).reshape(3, b, h1, l2 * c)

    def long_conv_gated(x_stack, x_idx, gate_idx, order):
        a = _ct_fwd1(x_stack, x_idx, consts, c)
        bsp = _ct_mid(a.reshape(b, 2, l2, h1 * c), hspec, order, consts, c)
        return _ct_inv1(bsp.reshape(b, 2, l2, h1, c), x_stack, x_idx, parts, gate_idx, skip[order], consts, c)

    z = long_conv_gated(parts, 0, 1, 0)
    y = long_conv_gated(z[None], 0, 2, 1)
    return y.reshape(b, n, c)


def _outproj_body(*refs, n_in):
    ins = refs[:n_in]
    ws = refs[n_in:2 * n_in]
    x_ref, gate_ref, g_ref, sc_ref, sh_ref, rhi_ref, rlo_ref, rb_ref, xo_ref, h_ref, lg_ref = refs[2 * n_in:]
    y = None
    for a_ref, w_ref in zip(ins, ws):
        t = jnp.dot(a_ref[0].astype(BF16), w_ref[...], preferred_element_type=F32)
        y = t if y is None else y + t
    x = x_ref[0] + gate_ref[0] * y
    xo_ref[0] = x
    h = x * lax.rsqrt(jnp.mean(x * x, axis=-1, keepdims=True) + NORM_EPS) * g_ref[...]
    h = h * (1.0 + sc_ref[0]) + sh_ref[0]
    h_ref[0] = h
    hi = h.astype(BF16)
    lo = (h - hi.astype(F32)).astype(BF16)
    lg_ref[0] = (jnp.dot(hi, rhi_ref[...], preferred_element_type=F32)
                 + jnp.dot(lo, rhi_ref[...], preferred_element_type=F32)
                 + jnp.dot(hi, rlo_ref[...], preferred_element_type=F32) + rb_ref[...])


def _outproj(ins, ws, x, gate, g, scale, shift, r_hi, r_lo, r_b, tm=512):
    b, n, d = x.shape
    tm = min(tm, n)
    bm = gate.shape[0]
    mod_map = (lambda bi, i: (bi, 0, 0)) if bm > 1 else (lambda bi, i: (0, 0, 0))
    row = lambda wd: pl.BlockSpec((1, tm, wd), lambda bi, i: (bi, i, 0))
    full = lambda a: pl.BlockSpec(a.shape, lambda bi, i: (0,) * a.ndim)
    mod = pl.BlockSpec((1, 1, d), mod_map)
    return pl.pallas_call(
        functools.partial(_outproj_body, n_in=len(ins)),
        grid=(b, n // tm),
        in_specs=([row(a.shape[-1]) for a in ins] + [full(w) for w in ws]
                  + [row(d), mod, pl.BlockSpec((1, d), lambda bi, i: (0, 0)), mod, mod,
                     full(r_hi), full(r_lo), full(r_b)]),
        out_specs=[row(d), row(d), row(LANES)],
        out_shape=[jax.ShapeDtypeStruct((b, n, d), F32), jax.ShapeDtypeStruct((b, n, d), F32),
                   jax.ShapeDtypeStruct((b, n, LANES), F32)],
        compiler_params=_params("parallel", "parallel"),
        name="outproj",
    )(*ins, *ws, x, gate, g.reshape(1, d), scale, shift, r_hi, r_lo, r_b)


def _expert_body(bexp_ref, nused_ref, tok_hbm, h_hbm, gate_ref, wg_ref, wu_ref, wd_ref, o_ref,
                 idx_smem, xbuf, sem_idx, sem_rows):
    i = pl.program_id(0)

    @pl.when(i < nused_ref[0])
    def _():
        idx_copy = pltpu.make_async_copy(tok_hbm.at[i], idx_smem, sem_idx)
        idx_copy.start()
        idx_copy.wait()

        def row_copy(r):
            return pltpu.make_async_copy(h_hbm.at[pl.ds(idx_smem[r], 1)], xbuf.at[pl.ds(r, 1)], sem_rows)

        def issue(r, carry):
            row_copy(r).start()
            return carry

        def drain(r, carry):
            row_copy(r).wait()
            return carry

        lax.fori_loop(0, MOE_BLOCK, issue, 0)
        lax.fori_loop(0, MOE_BLOCK, drain, 0)
        xb = xbuf[...].astype(BF16)
        gt = jnp.dot(xb, wg_ref[0], preferred_element_type=F32)
        up = jnp.dot(xb, wu_ref[0], preferred_element_type=F32)
        hid = (gt * jax.nn.sigmoid(gt) * up).astype(BF16)
        o_ref[...] = jnp.dot(hid, wd_ref[0], preferred_element_type=F32) * gate_ref[...]

    @pl.when(i >= nused_ref[0])
    def _():
        o_ref[...] = jnp.zeros_like(o_ref)


def _experts(h, block_exp, n_used, slot_tok, slot_gate, w_gate, w_up, w_down):
    t, d = h.shape
    n_blocks = block_exp.shape[0]
    de = w_gate.shape[-1]
    grid_spec = pltpu.PrefetchScalarGridSpec(
        num_scalar_prefetch=2,
        grid=(n_blocks,),
        in_specs=[
            pl.BlockSpec(memory_space=pl.ANY),
            pl.BlockSpec(memory_space=pl.ANY),
            pl.BlockSpec((MOE_BLOCK, 1), lambda i, be, nu: (i, 0)),
            pl.BlockSpec((1, d, de), lambda i, be, nu: (be[i], 0, 0)),
            pl.BlockSpec((1, d, de), lambda i, be, nu: (be[i], 0, 0)),
            pl.BlockSpec((1, de, d), lambda i, be, nu: (be[i], 0, 0)),
        ],
        out_specs=pl.BlockSpec((MOE_BLOCK, d), lambda i, be, nu: (i, 0)),
        scratch_shapes=[
            pltpu.SMEM((MOE_BLOCK,), jnp.int32),
            pltpu.VMEM((MOE_BLOCK, d), F32),
            pltpu.SemaphoreType.DMA,
            pltpu.SemaphoreType.DMA,
        ],
    )
    return pl.pallas_call(
        _expert_body,
        grid_spec=grid_spec,
        out_shape=jax.ShapeDtypeStruct((n_blocks * MOE_BLOCK, d), F32),
        compiler_params=_params("arbitrary"),
        name="experts",
    )(block_exp, n_used, slot_tok.reshape(n_blocks, MOE_BLOCK), h,
      slot_gate.reshape(n_blocks * MOE_BLOCK, 1), w_gate, w_up, w_down)


def _combine_body(slots_hbm, ys_hbm, x_ref, gate_ref, o_ref, idx_smem, ybuf, sem_idx, sem_rows, *, tm, n_tiles):
    i = pl.program_id(0) * n_tiles + pl.program_id(1)
    idx_copy = pltpu.make_async_copy(slots_hbm.at[i], idx_smem, sem_idx)
    idx_copy.start()
    idx_copy.wait()

    def row_copy(r):
        return pltpu.make_async_copy(ys_hbm.at[pl.ds(idx_smem[r], 1)], ybuf.at[pl.ds(r, 1)], sem_rows)

    def issue(r, carry):
        row_copy(r).start()
        return carry

    def drain(r, carry):
        row_copy(r).wait()
        return carry

    lax.fori_loop(0, TOP_K * tm, issue, 0)
    lax.fori_loop(0, TOP_K * tm, drain, 0)
    o_ref[0] = x_ref[0] + gate_ref[0] * (ybuf[0:tm, :] + ybuf[tm:2 * tm, :])


def _combine(ys, tok_slots, x, gate, tm=256):
    b, n, d = x.shape
    tm = min(tm, n)
    n_tiles = n // tm
    bm = gate.shape[0]
    mod_map = (lambda bi, i: (bi, 0, 0)) if bm > 1 else (lambda bi, i: (0, 0, 0))
    slots = jnp.swapaxes(tok_slots.reshape(b * n_tiles, tm, TOP_K), 1, 2).reshape(b * n_tiles, TOP_K * tm)
    return pl.pallas_call(
        functools.partial(_combine_body, tm=tm, n_tiles=n_tiles),
        grid=(b, n_tiles),
        in_specs=[
            pl.BlockSpec(memory_space=pl.ANY),
            pl.BlockSpec(memory_space=pl.ANY),
            pl.BlockSpec((1, tm, d), lambda bi, i: (bi, i, 0)),
            pl.BlockSpec((1, 1, d), mod_map),
        ],
        out_specs=pl.BlockSpec((1, tm, d), lambda bi, i: (bi, i, 0)),
        out_shape=jax.ShapeDtypeStruct((b, n, d), F32),
        scratch_shapes=[
            pltpu.SMEM((TOP_K * tm,), jnp.int32),
            pltpu.VMEM((TOP_K * tm, d), F32),
            pltpu.SemaphoreType.DMA,
            pltpu.SemaphoreType.DMA,
        ],
        compiler_params=_params("arbitrary", "arbitrary"),
        name="moe_combine",
    )(slots, ys, x, gate)


def _route(logits):
    g_logits = logits[:, :N_GROUPS]
    _, grp = lax.top_k(g_logits, 1)
    p_grp = jnp.take_along_axis(jax.nn.softmax(g_logits, axis=-1), grp, axis=-1)
    e_logits = logits[:, N_GROUPS:N_GROUPS + N_EXPERTS].reshape(-1, N_GROUPS, EXP_PER_GROUP)
    e_logits = jnp.take_along_axis(e_logits, grp[:, :, None], axis=1)[:, 0]
    top_l, top_i = lax.top_k(e_logits, TOP_K)
    return grp * EXP_PER_GROUP + top_i, p_grp * jax.nn.softmax(top_l, axis=-1)


def _dispatch(experts):
    n_tok = experts.shape[0]
    n_assign = n_tok * TOP_K
    e_flat = experts.reshape(n_assign).astype(jnp.int32)
    order = jnp.argsort(e_flat)
    e_s = e_flat[order]
    counts = jnp.bincount(e_flat, length=N_EXPERTS)
    padded = ((counts + MOE_BLOCK - 1) // MOE_BLOCK) * MOE_BLOCK
    ends = jnp.cumsum(padded)
    pstart = ends - padded
    start = jnp.cumsum(counts) - counts
    slot_sorted = (pstart[e_s] + (jnp.arange(n_assign) - start[e_s])).astype(jnp.int32)
    n_blocks = -(-n_assign // MOE_BLOCK) + N_EXPERTS
    assign_slot = jnp.zeros((n_assign,), jnp.int32).at[order].set(slot_sorted)
    block_exp = jnp.minimum(jnp.searchsorted(ends, jnp.arange(n_blocks) * MOE_BLOCK, side="right"),
                            N_EXPERTS - 1).astype(jnp.int32)
    n_used = (ends[-1] // MOE_BLOCK).astype(jnp.int32).reshape(1)
    return assign_slot.reshape(n_tok, TOP_K), block_exp, n_used, n_blocks


def _moe_experts(h2_flat, logits_flat, w_gate, w_up, w_down):
    experts, gates = _route(logits_flat)
    tok_slots, block_exp, n_used, n_blocks = _dispatch(experts)
    n_tok = h2_flat.shape[0]
    n_slots = n_blocks * MOE_BLOCK
    tok_ids = jnp.broadcast_to(jnp.arange(n_tok, dtype=jnp.int32)[:, None], (n_tok, TOP_K))
    slot_tok = jnp.zeros((n_slots,), jnp.int32).at[tok_slots.reshape(-1)].set(tok_ids.reshape(-1))
    slot_gate = jnp.zeros((n_slots,), F32).at[tok_slots.reshape(-1)].set(gates.reshape(-1))
    ys = _experts(h2_flat, block_exp, n_used, slot_tok, slot_gate, w_gate, w_up, w_down)
    return ys, tok_slots


def _final_norm_body(x_ref, g_ref, o_ref):
    x = x_ref[...]
    o_ref[...] = x * lax.rsqrt(jnp.mean(x * x, axis=-1, keepdims=True) + NORM_EPS) * g_ref[...]


def _final_norm(x, g, tm=1024):
    t, d = x.shape
    return pl.pallas_call(
        _final_norm_body,
        grid=(t // tm,),
        in_specs=[pl.BlockSpec((tm, d), lambda i: (i, 0)), pl.BlockSpec((1, d), lambda i: (0, 0))],
        out_specs=pl.BlockSpec((tm, d), lambda i: (i, 0)),
        out_shape=jax.ShapeDtypeStruct((t, d), F32),
        compiler_params=_params("parallel"),
        name="final_norm",
    )(x, g.reshape(1, d))


def _router_weights(wg, bg, we, be):
    d = wg.shape[0]
    w = jnp.zeros((d, LANES), F32).at[:, :N_GROUPS].set(wg).at[:, N_GROUPS:N_GROUPS + N_EXPERTS].set(we)
    bias = jnp.zeros((1, LANES), F32).at[0, :N_GROUPS].set(bg).at[0, N_GROUPS:N_GROUPS + N_EXPERTS].set(be)
    hi = w.astype(BF16)
    lo = (w - hi.astype(F32)).astype(BF16)
    return hi, lo, bias


def kernel(x, c, ctx, c_ctx, ada_w, ada_b, norm1_g, norm2_g, ev_w_in, ev_w_out, hy_conv_w, hy_conv_b, hy_f_w1, hy_f_b1, hy_f_w2, hy_f_b2, hy_f_w3, hy_f_freq, hy_skip, swa_sink, od_w_qkv, od_w_out, od_q_norm_g, od_k_norm_g, rt_group_w, rt_group_b, rt_exp_w, rt_exp_b, moe_w_gate, moe_w_up, moe_w_down, final_norm_g):
    b, n, d = x.shape
    lc = ctx.shape[1]
    depth = ada_w.shape[0]
    rope = _rope_tables(n)
    xc = ctx
    sc = jax.nn.silu(c)
    scc = jax.nn.silu(c_ctx)
    q_scale = HEAD_DIM ** -0.5
    for layer in range(depth):
        with_ctx = layer < depth - 1
        mod = (sc @ ada_w[layer] + ada_b[layer]).reshape(b, N_MOD, 1, d)
        modc = (scc @ ada_w[layer] + ada_b[layer]).reshape(1, N_MOD, 1, d)
        m = [mod[:, k] for k in range(N_MOD)]
        mc = [modc[:, k] for k in range(N_MOD)]
        r_hi, r_lo, r_b = _router_weights(rt_group_w[layer], rt_group_b[layer], rt_exp_w[layer], rt_exp_b[layer])
        wgt, wup, wdn = (moe_w_gate[layer].astype(BF16), moe_w_up[layer].astype(BF16),
                         moe_w_down[layer].astype(BF16))
        if layer % 2 == 0:
            e = layer // 2
            c_hy = hy_conv_w.shape[-1] // 3
            d_hy = 3 * c_hy
            hq = swa_sink.shape[-1]
            d_q = hq * HEAD_DIM
            hkv = hq // 4
            d_kv = hkv * HEAD_DIM
            w_in = ev_w_in[e].astype(BF16)
            w_out = ev_w_out[e].astype(BF16)
            hy_args = (hy_conv_w[e], hy_conv_b[e], hy_f_w1[e], hy_f_b1[e], hy_f_w2[e], hy_f_b2[e],
                       hy_f_w3[e], hy_f_freq[e], hy_skip[e])
            u, q, k, v = _proj(x, norm1_g[layer], m[1], m[0], w_in, [
                (0, d_hy, "f32", None, False, 1.0),
                (d_hy, d_q, "qk", None, True, q_scale),
                (d_hy + d_q, d_kv, "qk", None, True, 1.0),
                (d_hy + d_q + d_kv, d_kv, "bf16", None, False, 1.0)], rope_tabs=rope)
            if with_ctx:
                uc, qc, kc, vc = _proj(xc, norm1_g[layer], mc[1], mc[0], w_in, [
                    (0, d_hy, "f32", None, False, 1.0),
                    (d_hy, d_q, "qk", None, False, q_scale),
                    (d_hy + d_q, d_kv, "bf16", None, False, 1.0),
                    (d_hy + d_q + d_kv, d_kv, "bf16", None, False, 1.0)])
            else:
                kc, vc = _proj(xc, norm1_g[layer], mc[1], mc[0], w_in, [
                    (d_hy + d_q, d_kv, "bf16", None, False, 1.0),
                    (d_hy + d_q + d_kv, d_kv, "bf16", None, False, 1.0)])
            y_hy = _hyena(u, *hy_args)
            y_att = _windowed_attention(q, k, v, kc, vc, swa_sink[e], hkv)
            mix_in, mix_w = [y_hy, y_att], [w_out[:c_hy], w_out[c_hy:]]
            if with_ctx:
                yc_hy = _hyena(uc, *hy_args)
                yc_att = _full_attention(qc, kc, vc, hkv, sink=swa_sink[e])
                mixc_in = [yc_hy, yc_att]
        else:
            o = layer // 2
            hkv = od_w_qkv.shape[-1] // HEAD_DIM // 6
            hq = 4 * hkv
            d_q = hq * HEAD_DIM
            d_kv = hkv * HEAD_DIM
            w_qkv = od_w_qkv[o].astype(BF16)
            w_out = od_w_out[o].astype(BF16)
            norm_g = jnp.zeros((8, LANES), F32).at[0].set(jnp.tile(od_q_norm_g[o], 2)).at[1].set(
                jnp.tile(od_k_norm_g[o], 2))
            q, k, v = _proj(x, norm1_g[layer], m[1], m[0], w_qkv, [
                (0, d_q, "qk", 0, True, q_scale),
                (d_q, d_kv, "qk", 1, True, 1.0),
                (d_q + d_kv, d_kv, "bf16", None, False, 1.0)], rope_tabs=rope, norm_g=norm_g)
            if with_ctx:
                qc, kc, vc = _proj(xc, norm1_g[layer], mc[1], mc[0], w_qkv, [
                    (0, d_q, "qk", 0, False, q_scale),
                    (d_q, d_kv, "qk", 1, False, 1.0),
                    (d_q + d_kv, d_kv, "bf16", None, False, 1.0)], norm_g=norm_g)
            else:
                kc, vc = _proj(xc, norm1_g[layer], mc[1], mc[0], w_qkv, [
                    (d_q, d_kv, "qk", 1, False, 1.0),
                    (d_q + d_kv, d_kv, "bf16", None, False, 1.0)], norm_g=norm_g)
            y_att = _full_attention(q, jnp.concatenate([kc, k], axis=1), jnp.concatenate([vc, v], axis=1), hkv)
            mix_in, mix_w = [y_att], [w_out]
            if with_ctx:
                mixc_in = [_full_attention(qc, kc, vc, hkv)]
        x, h2, lg = _outproj(mix_in, mix_w, x, m[2], norm2_g[layer], m[4], m[3], r_hi, r_lo, r_b)
        if with_ctx:
            xc, h2c, lgc = _outproj(mixc_in, mix_w, xc, mc[2], norm2_g[layer], mc[4], mc[3], r_hi, r_lo, r_b)
            tokens = jnp.concatenate([h2.reshape(b * n, d), h2c.reshape(b * lc, d)], axis=0)
            logits = jnp.concatenate([lg.reshape(b * n, LANES), lgc.reshape(b * lc, LANES)], axis=0)
        else:
            tokens = h2.reshape(b * n, d)
            logits = lg.reshape(b * n, LANES)
        ys, tok_slots = _moe_experts(tokens, logits, wgt, wup, wdn)
        x = _combine(ys, tok_slots[:b * n], x, m[5])
        if with_ctx:
            xc = _combine(ys, tok_slots[b * n:], xc, mc[5])
    return _final_norm(x.reshape(b * n, d), final_norm_g).reshape(b, n, d)
```

```python
import functools
import math

import numpy as np
import jax
import jax.numpy as jnp
from jax import lax
from jax.experimental import pallas as pl
from jax.experimental.pallas import tpu as pltpu

F32 = jnp.float32
BF16 = jnp.bfloat16

HEAD_DIM = 64
GRID_W = 64
ROPE_BASE = 10000.0
Q_BLOCK = 128
NORM_EPS = 1e-6
N_MOD = 6
HY_ORDER = 2
HY_BANDS = 16
HY_DIRS = 2
HY_DECAY_TARGET = 1e-2
HY_FAST_DECAY = 0.3
HY_SLOW_DECAY = 1.5
HY_FILTER_EPS = 1e-6
SWA_WINDOW = 128
N_GROUPS = 4
EXP_PER_GROUP = 8
N_EXPERTS = N_GROUPS * EXP_PER_GROUP
TOP_K = 2
EXPERT_ROWS = 512

LANES = 128
VMEM_LIMIT_BYTES = 56 * 1024 * 1024


def _params(*sem):
    return pltpu.CompilerParams(dimension_semantics=sem, vmem_limit_bytes=VMEM_LIMIT_BYTES)


def _rope_tables(n):
    d_axis = HEAD_DIM // 2
    t = jnp.arange(n)
    inv = ROPE_BASE ** (-jnp.arange(0, d_axis, 2, dtype=F32) / d_axis)
    ang_r = (t // GRID_W).astype(F32)[:, None] * inv[None, :]
    ang_c = (t % GRID_W).astype(F32)[:, None] * inv[None, :]
    cos = jnp.concatenate([jnp.cos(ang_r)] * 2 + [jnp.cos(ang_c)] * 2, axis=-1)
    sin = jnp.concatenate([-jnp.sin(ang_r), jnp.sin(ang_r), -jnp.sin(ang_c), jnp.sin(ang_c)], axis=-1)
    return jnp.tile(cos, (1, 2)), jnp.tile(sin, (1, 2))


def _head_mean_matrix():
    i = np.arange(LANES)
    return jnp.asarray((i[:, None] // HEAD_DIM == i[None, :] // HEAD_DIM) / HEAD_DIM, dtype=BF16)


def _proj_body(x_ref, g_ref, sc_ref, sh_ref, w_ref, cos_ref, sin_ref, ng_ref, bd_ref, *out_refs, segs):
    x = x_ref[0]
    h = x * lax.rsqrt(jnp.mean(x * x, axis=-1, keepdims=True) + NORM_EPS) * g_ref[...]
    hb = (h * (1.0 + sc_ref[0]) + sh_ref[0]).astype(BF16)
    for o_ref, (c0, width, kind, norm_row, rope, out_scale) in zip(out_refs, segs):
        seg = jnp.dot(hb, w_ref[:, c0:c0 + width], preferred_element_type=F32)
        if kind == "f32":
            o_ref[0] = seg
            continue
        if kind == "bf16":
            o_ref[0] = seg.astype(BF16)
            continue
        for j in range(width // LANES):
            ch = seg[:, j * LANES:(j + 1) * LANES]
            if norm_row is not None:
                sq = ch * ch
                hi = sq.astype(BF16)
                lo = (sq - hi.astype(F32)).astype(BF16)
                ms = (jnp.dot(hi, bd_ref[...], preferred_element_type=F32)
                      + jnp.dot(lo, bd_ref[...], preferred_element_type=F32))
                ch = ch * lax.rsqrt(ms + NORM_EPS) * ng_ref[norm_row:norm_row + 1, :]
            if rope:
                lane = lax.broadcasted_iota(jnp.int32, ch.shape, 1)
                partner = jnp.where(lane % 32 < 16, pltpu.roll(ch, LANES - 16, 1), pltpu.roll(ch, 16, 1))
                ch = ch * cos_ref[...] + partner * sin_ref[...]
            if out_scale != 1.0:
                ch = ch * out_scale
            o_ref[0, :, j * LANES:(j + 1) * LANES] = ch.astype(BF16)


def _proj(x, g, scale, shift, w, segs, rope_tabs=None, norm_g=None, tm=512):
    b, n, d = x.shape
    tm = min(tm, n)
    bm = scale.shape[0]
    mod_map = (lambda bi, i: (bi, 0, 0)) if bm > 1 else (lambda bi, i: (0, 0, 0))
    if rope_tabs is None:
        cos = sin = jnp.zeros((8, LANES), F32)
        tab_spec = pl.BlockSpec((8, LANES), lambda bi, i: (0, 0))
    else:
        cos, sin = rope_tabs
        tab_spec = pl.BlockSpec((tm, LANES), lambda bi, i: (i, 0))
    if norm_g is None:
        norm_g = jnp.ones((8, LANES), F32)
    out_shape = [jax.ShapeDtypeStruct((b, n, s[1]), F32 if s[2] == "f32" else BF16) for s in segs]
    out_specs = [pl.BlockSpec((1, tm, s[1]), lambda bi, i: (bi, i, 0)) for s in segs]
    return pl.pallas_call(
        functools.partial(_proj_body, segs=tuple(segs)),
        grid=(b, n // tm),
        in_specs=[
            pl.BlockSpec((1, tm, d), lambda bi, i: (bi, i, 0)),
            pl.BlockSpec((1, d), lambda bi, i: (0, 0)),
            pl.BlockSpec((1, 1, d), mod_map),
            pl.BlockSpec((1, 1, d), mod_map),
            pl.BlockSpec(w.shape, lambda bi, i: (0, 0)),
            tab_spec,
            tab_spec,
            pl.BlockSpec(norm_g.shape, lambda bi, i: (0, 0)),
            pl.BlockSpec((LANES, LANES), lambda bi, i: (0, 0)),
        ],
        out_specs=out_specs,
        out_shape=out_shape,
        compiler_params=_params("parallel", "parallel"),
        name="proj",
    )(x, g.reshape(1, d), scale, shift, w, cos, sin, norm_g, _head_mean_matrix())


def _stack_heads(q, j, g):
    return jnp.concatenate(
        [q[:, (j * g + gg) * HEAD_DIM:(j * g + gg + 1) * HEAD_DIM] for gg in range(g)], axis=0)


def _sink_column(sink_ref, j, g, qb):
    return jnp.concatenate([jnp.full((qb, 1), sink_ref[j * g + gg], F32) for gg in range(g)], axis=0)


def _swa_body(sink_ref, q_ref, kt_ref, v_ref, kct_ref, vc_ref, o_ref, *, n, hkv, g, qb, win):
    i = pl.program_id(1)
    start = pl.multiple_of(i * qb, qb)
    q = q_ref[0]
    rows = lax.broadcasted_iota(jnp.int32, (g * qb, 3 * qb), 0) % qb
    cols = lax.broadcasted_iota(jnp.int32, (g * qb, 3 * qb), 1)
    key_pos = cols + (i - 1) * qb
    valid = (jnp.abs(rows + qb - cols) <= win) & (key_pos >= 0) & (key_pos < n)
    for j in range(hkv):
        hs = slice(j * HEAD_DIM, (j + 1) * HEAD_DIM)
        q4 = _stack_heads(q, j, g)
        s_lat = jnp.dot(q4, kt_ref[0, hs, pl.ds(start, 3 * qb)], preferred_element_type=F32)
        s_lat = jnp.where(valid, s_lat, -jnp.inf)
        s_ctx = jnp.dot(q4, kct_ref[0, hs, :], preferred_element_type=F32)
        s_sink = _sink_column(sink_ref, j, g, qb)
        m = jnp.maximum(jnp.maximum(jnp.max(s_lat, axis=-1, keepdims=True),
                                    jnp.max(s_ctx, axis=-1, keepdims=True)), s_sink)
        e_lat = jnp.exp(s_lat - m)
        e_ctx = jnp.exp(s_ctx - m)
        den = (jnp.sum(e_lat, axis=-1, keepdims=True) + jnp.sum(e_ctx, axis=-1, keepdims=True)
               + jnp.exp(s_sink - m))
        o = (jnp.dot(e_ctx.astype(BF16), vc_ref[0, :, hs], preferred_element_type=F32)
             + jnp.dot(e_lat.astype(BF16), v_ref[0, pl.ds(start, 3 * qb), hs], preferred_element_type=F32))
        o = o / den
        for gg in range(g):
            c0 = (j * g + gg) * HEAD_DIM
            o_ref[0, :, c0:c0 + HEAD_DIM] = o[gg * qb:(gg + 1) * qb].astype(BF16)


def _windowed_attention(q, k, v, kc, vc, sink, hkv):
    b, n, dq = q.shape
    g = dq // HEAD_DIM // hkv
    qb = Q_BLOCK
    lc = kc.shape[1]
    dkv = hkv * HEAD_DIM
    kt = jnp.swapaxes(jnp.pad(k, ((0, 0), (qb, qb), (0, 0))), 1, 2)
    vp = jnp.pad(v, ((0, 0), (qb, qb), (0, 0)))
    kct = jnp.swapaxes(kc, 1, 2)
    return pl.pallas_call(
        functools.partial(_swa_body, n=n, hkv=hkv, g=g, qb=qb, win=SWA_WINDOW),
        grid=(b, n // qb),
        in_specs=[
            pl.BlockSpec(memory_space=pltpu.SMEM),
            pl.BlockSpec((1, qb, dq), lambda bi, i: (bi, i, 0)),
            pl.BlockSpec((1, dkv, n + 2 * qb), lambda bi, i: (bi, 0, 0)),
            pl.BlockSpec((1, n + 2 * qb, dkv), lambda bi, i: (bi, 0, 0)),
            pl.BlockSpec((1, dkv, lc), lambda bi, i: (bi, 0, 0)),
            pl.BlockSpec((1, lc, dkv), lambda bi, i: (bi, 0, 0)),
        ],
        out_specs=pl.BlockSpec((1, qb, dq), lambda bi, i: (bi, i, 0)),
        out_shape=jax.ShapeDtypeStruct((b, n, dq), BF16),
        compiler_params=_params("parallel", "parallel"),
        name="swa",
    )(sink.astype(F32), q, kt, vp, kct, vc)


def _full_attn_body(sink_ref, q_ref, kt_ref, v_ref, o_ref, *, hkv, g, qb, has_sink):
    q = q_ref[0]
    for j in range(hkv):
        hs = slice(j * HEAD_DIM, (j + 1) * HEAD_DIM)
        q4 = _stack_heads(q, j, g)
        s = jnp.dot(q4, kt_ref[0, hs, :], preferred_element_type=F32)
        m = jnp.max(s, axis=-1, keepdims=True)
        if has_sink:
            s_sink = _sink_column(sink_ref, j, g, qb)
            m = jnp.maximum(m, s_sink)
        e = jnp.exp(s - m)
        den = jnp.sum(e, axis=-1, keepdims=True)
        if has_sink:
            den = den + jnp.exp(s_sink - m)
        o = jnp.dot(e.astype(BF16), v_ref[0, :, hs], preferred_element_type=F32) / den
        for gg in range(g):
            c0 = (j * g + gg) * HEAD_DIM
            o_ref[0, :, c0:c0 + HEAD_DIM] = o[gg * qb:(gg + 1) * qb].astype(BF16)


def _full_attention(q, k, v, hkv, sink=None):
    b, n, dq = q.shape
    g = dq // HEAD_DIM // hkv
    qb = Q_BLOCK
    nk = k.shape[1]
    dkv = hkv * HEAD_DIM
    kt = jnp.swapaxes(k, 1, 2)
    has_sink = sink is not None
    sink = jnp.zeros((dq // HEAD_DIM,), F32) if sink is None else sink.astype(F32)
    return pl.pallas_call(
        functools.partial(_full_attn_body, hkv=hkv, g=g, qb=qb, has_sink=has_sink),
        grid=(b, n // qb),
        in_specs=[
            pl.BlockSpec(memory_space=pltpu.SMEM),
            pl.BlockSpec((1, qb, dq), lambda bi, i: (bi, i, 0)),
            pl.BlockSpec((1, dkv, nk), lambda bi, i: (bi, 0, 0)),
            pl.BlockSpec((1, nk, dkv), lambda bi, i: (bi, 0, 0)),
        ],
        out_specs=pl.BlockSpec((1, qb, dq), lambda bi, i: (bi, i, 0)),
        out_shape=jax.ShapeDtypeStruct((b, n, dq), BF16),
        compiler_params=_params("parallel", "parallel"),
        name="full_attn",
    )(sink, q, kt, v)


def _short_conv_body(u_ref, w_ref, b_ref, o_ref, *, n):
    x = u_ref[0]
    row = lax.broadcasted_iota(jnp.int32, x.shape, 0)
    prev = jnp.where(row == 0, 0.0, pltpu.roll(x, 1, 0))
    nxt = jnp.where(row == n - 1, 0.0, pltpu.roll(x, n - 1, 0))
    o_ref[0, 0] = prev * w_ref[0:1, :] + x * w_ref[1:2, :] + nxt * w_ref[2:3, :] + b_ref[...]


def _short_conv(u, w, bias, cb=256):
    b, n, c3 = u.shape
    c = c3 // 3
    per = c // cb
    return pl.pallas_call(
        functools.partial(_short_conv_body, n=n),
        grid=(b, c3 // cb),
        in_specs=[
            pl.BlockSpec((1, n, cb), lambda bi, j: (bi, 0, j)),
            pl.BlockSpec((3, cb), lambda bi, j: (0, j)),
            pl.BlockSpec((1, cb), lambda bi, j: (0, j)),
        ],
        out_specs=pl.BlockSpec((1, 1, n, cb), lambda bi, j: (j // per, bi, 0, j % per)),
        out_shape=jax.ShapeDtypeStruct((3, b, n, c), F32),
        compiler_params=_params("parallel", "parallel"),
        name="short_conv",
    )(u, w, bias.reshape(1, c3))


def _filter_body(band_ref, w1_ref, b1_ref, w2_ref, b2_ref, w3_ref, fr_ref, dl_ref, o_ref, s_ref, *, n, rt, c):
    i = pl.program_id(0)
    hp = lax.Precision.HIGHEST
    m = i * rt + lax.broadcasted_iota(jnp.int32, (rt, 1), 0)
    pos = jnp.where(m < n, m, 2 * n - m).astype(F32)
    t_norm = pos / max(n - 1, 1)
    ang = (2.0 * math.pi / n) * pos * band_ref[...]
    lane = lax.broadcasted_iota(jnp.int32, (rt, LANES), 1)
    z = jnp.where(lane == 0, t_norm,
                  jnp.where(lane <= HY_BANDS, jnp.cos(ang),
                            jnp.where(lane <= 2 * HY_BANDS, -jnp.sin(ang), 0.0)))
    fr = fr_ref[...]
    hdn = jnp.sin(fr * (jnp.dot(z, w1_ref[...], precision=hp, preferred_element_type=F32) + b1_ref[...]))
    hdn = jnp.sin(fr * (jnp.dot(hdn, w2_ref[...], precision=hp, preferred_element_type=F32) + b2_ref[...]))
    h = jnp.dot(hdn, w3_ref[...], precision=hp, preferred_element_type=F32)
    h = h * jnp.exp(-t_norm * dl_ref[...])
    half = HY_ORDER * c
    sel = jnp.where(m < n, h[:, :half], jnp.where(m > n, -h[:, half:], 0.0))
    for o in range(HY_ORDER):
        o_ref[o] = sel[:, o * c:(o + 1) * c]

    @pl.when(i == 0)
    def _():
        s_ref[...] = jnp.zeros_like(s_ref)

    s_ref[...] += jnp.sum(jnp.abs(sel), axis=0, keepdims=True)


def _hyena_filters(n, w1, b1, w2, b2, w3, freq, c):
    rt = min(1024, n)
    hid = w1.shape[1]
    bands = jnp.linspace(1e-4, HY_BANDS - 1, HY_BANDS, dtype=F32)
    band_row = jnp.zeros((1, LANES), F32).at[0, 1:1 + 2 * HY_BANDS].set(jnp.tile(bands, 2))
    w1p = jnp.zeros((LANES, hid), F32).at[:w1.shape[0]].set(w1)
    max_decay = math.log(HY_DECAY_TARGET) / HY_FAST_DECAY
    min_decay = math.log(HY_DECAY_TARGET) / HY_SLOW_DECAY
    deltas = jnp.abs(jnp.linspace(min_decay, max_decay, c, dtype=F32))
    dl = jnp.tile(deltas, HY_DIRS * HY_ORDER).reshape(1, -1)
    full = lambda a: pl.BlockSpec(a.shape, lambda i: (0,) * a.ndim)
    args = (band_row, w1p, b1.reshape(1, hid), w2, b2.reshape(1, hid), w3, freq.reshape(1, hid), dl)
    return pl.pallas_call(
        functools.partial(_filter_body, n=n, rt=rt, c=c),
        grid=(2 * n // rt,),
        in_specs=[full(a) for a in args],
        out_specs=[pl.BlockSpec((HY_ORDER, rt, c), lambda i: (0, i, 0)),
                   pl.BlockSpec((1, HY_ORDER * c), lambda i: (0, 0))],
        out_shape=[jax.ShapeDtypeStruct((HY_ORDER, 2 * n, c), F32),
                   jax.ShapeDtypeStruct((1, HY_ORDER * c), F32)],
        compiler_params=_params("arbitrary"),
        name="hyena_filter",
    )(*args)


def _dft_split(n):
    l2 = 32 if n >= 2048 else 16
    return 2 * n // l2, l2


def _dft_constants(n):
    l1, l2 = _dft_split(n)
    h1 = l1 // 2
    nn = 2 * n
    k1 = np.arange(h1)[:, None]
    a = 2 * np.pi * (k1 + 0.5) * np.arange(l1)[None, :] / l1
    w1 = np.concatenate([np.cos(a), -np.sin(a)], axis=0)
    t = 2 * np.pi * (k1 + 0.5) * np.arange(l2)[None, :] / nn
    tw_cos, tw_sin = np.cos(t), np.sin(t)
    p = 2 * np.pi * np.arange(l2)[:, None] * np.arange(l2)[None, :] / l2
    w2 = np.block([[np.cos(p), np.sin(p)], [-np.sin(p), np.cos(p)]])
    w2i = np.block([[np.cos(p), -np.sin(p)], [np.sin(p), np.cos(p)]])
    ai = 2 * np.pi * np.arange(h1)[:, None] * (np.arange(h1)[None, :] + 0.5) / l1
    w1i = (2.0 / nn) * np.concatenate([np.cos(ai), -np.sin(ai)], axis=1)
    c = lambda m, dt: jnp.asarray(m, dtype=dt)
    return dict(
        l1=l1, l2=l2, h1=h1,
        w1=c(w1, BF16), w2=c(w2, BF16), w2i=c(w2i, BF16), w1i=c(w1i, BF16),
        tw_cos_fwd=c(tw_cos.T[:, :, None], F32), tw_sin_fwd=c(tw_sin.T[:, :, None], F32),
        tw_cos_inv=c(tw_cos[:, :, None], F32), tw_sin_inv=c(tw_sin[:, :, None], F32),
    )


def _ct_fwd1_body(x_ref, w_ref, tc_ref, ts_ref, nrm_ref, o_ref, *, lb, c, h1, normalise):
    for q in range(lb):
        x = x_ref[0, 0, :, q * c:(q + 1) * c]
        if normalise:
            x = x / (nrm_ref[0] + HY_FILTER_EPS)
        a = jnp.dot(w_ref[...], x.astype(BF16), preferred_element_type=F32)
        ar, ai = a[:h1], a[h1:]
        tc, ts = tc_ref[q], ts_ref[q]
        o_ref[0, 0, q] = (ar * tc + ai * ts).astype(BF16)
        o_ref[0, 1, q] = (ai * tc - ar * ts).astype(BF16)


def _ct_fwd1(xs, idx, consts, c, norms=None, lb=8):
    _, b, k1n, _ = xs.shape
    l2, h1 = consts["l2"], consts["h1"]
    lb = min(lb, l2)
    w = consts["w1"][:, :k1n]
    normalise = norms is not None
    if norms is None:
        norms = jnp.zeros((b, 1, c), F32)
    return pl.pallas_call(
        functools.partial(_ct_fwd1_body, lb=lb, c=c, h1=h1, normalise=normalise),
        grid=(b, l2 // lb),
        in_specs=[
            pl.BlockSpec((1, 1, k1n, lb * c), lambda bi, i: (idx, bi, 0, i)),
            pl.BlockSpec(w.shape, lambda bi, i: (0, 0)),
            pl.BlockSpec((lb, h1, 1), lambda bi, i: (i, 0, 0)),
            pl.BlockSpec((lb, h1, 1), lambda bi, i: (i, 0, 0)),
            pl.BlockSpec((1, 1, c), lambda bi, i: (bi, 0, 0)),
        ],
        out_specs=pl.BlockSpec((1, 2, lb, h1, c), lambda bi, i: (bi, 0, i, 0, 0)),
        out_shape=jax.ShapeDtypeStruct((b, 2, l2, h1, c), BF16),
        compiler_params=_params("parallel", "parallel"),
        name="ct_fwd1",
    )(xs, w, consts["tw_cos_fwd"], consts["tw_sin_fwd"], norms)


def _ct_spec_body(a_ref, w2_ref, o_ref, *, l2):
    a = jnp.concatenate([a_ref[0, 0], a_ref[0, 1]], axis=0)
    x = jnp.dot(w2_ref[...], a, preferred_element_type=F32)
    o_ref[0, 0] = x[:l2]
    o_ref[0, 1] = x[l2:]


def _ct_spectrum(a, consts, c, kb=8):
    b = a.shape[0]
    l2, h1 = consts["l2"], consts["h1"]
    kb = min(kb, h1)
    return pl.pallas_call(
        functools.partial(_ct_spec_body, l2=l2),
        grid=(b, h1 // kb),
        in_specs=[pl.BlockSpec((1, 2, l2, kb * c), lambda bi, i: (bi, 0, 0, i)),
                  pl.BlockSpec((2 * l2, 2 * l2), lambda bi, i: (0, 0))],
        out_specs=pl.BlockSpec((1, 2, l2, kb * c), lambda bi, i: (bi, 0, 0, i)),
        out_shape=jax.ShapeDtypeStruct((b, 2, l2, h1 * c), F32),
        compiler_params=_params("parallel", "parallel"),
        name="ct_spectrum",
    )(a, consts["w2"])


def _ct_mid_body(a_ref, h_ref, w2_ref, w2i_ref, tc_ref, ts_ref, o_ref, *, l2, kb, c):
    a = jnp.concatenate([a_ref[0, 0], a_ref[0, 1]], axis=0)
    x = jnp.dot(w2_ref[...], a, preferred_element_type=F32)
    xr, xi = x[:l2], x[l2:]
    hr, hi = h_ref[0, 0], h_ref[0, 1]
    y = jnp.concatenate([xr * hr - xi * hi, xr * hi + xi * hr], axis=0).astype(BF16)
    bm = jnp.dot(w2i_ref[...], y, preferred_element_type=F32)
    br, bi = bm[:l2], bm[l2:]
    for q in range(kb):
        cs = slice(q * c, (q + 1) * c)
        tc, ts = tc_ref[q], ts_ref[q]
        o_ref[0, 0, :, cs] = (br[:, cs] * tc - bi[:, cs] * ts).astype(BF16)
        o_ref[0, 1, :, cs] = (br[:, cs] * ts + bi[:, cs] * tc).astype(BF16)


def _ct_mid(a, hspec, order, consts, c, kb=8):
    b = a.shape[0]
    l2, h1 = consts["l2"], consts["h1"]
    kb = min(kb, h1)
    return pl.pallas_call(
        functools.partial(_ct_mid_body, l2=l2, kb=kb, c=c),
        grid=(b, h1 // kb),
        in_specs=[
            pl.BlockSpec((1, 2, l2, kb * c), lambda bi, i: (bi, 0, 0, i)),
            pl.BlockSpec((1, 2, l2, kb * c), lambda bi, i: (order, 0, 0, i)),
            pl.BlockSpec((2 * l2, 2 * l2), lambda bi, i: (0, 0)),
            pl.BlockSpec((2 * l2, 2 * l2), lambda bi, i: (0, 0)),
            pl.BlockSpec((kb, l2, 1), lambda bi, i: (i, 0, 0)),
            pl.BlockSpec((kb, l2, 1), lambda bi, i: (i, 0, 0)),
        ],
        out_specs=pl.BlockSpec((1, 2, l2, kb * c), lambda bi, i: (bi, 0, 0, i)),
        out_shape=jax.ShapeDtypeStruct((b, 2, l2, h1 * c), BF16),
        compiler_params=_params("parallel", "parallel"),
        name="ct_mid",
    )(a, hspec, consts["w2"], consts["w2i"], consts["tw_cos_inv"], consts["tw_sin_inv"])


def _ct_inv1_body(b_ref, w_ref, u_ref, gate_ref, skip_ref, o_ref, *, lb, c):
    for q in range(lb):
        bb = jnp.concatenate([b_ref[0, 0, q], b_ref[0, 1, q]], axis=0)
        y = jnp.dot(w_ref[...], bb, preferred_element_type=F32)
        cs = slice(q * c, (q + 1) * c)
        o_ref[0, :, cs] = gate_ref[0, 0, :, cs] * (y + u_ref[0, 0, :, cs] * skip_ref[...])


def _ct_inv1(bsp, u, u_idx, gate, gate_idx, skip, consts, c, lb=8):
    b = bsp.shape[0]
    l2, h1 = consts["l2"], consts["h1"]
    lb = min(lb, l2)
    return pl.pallas_call(
        functools.partial(_ct_inv1_body, lb=lb, c=c),
        grid=(b, l2 // lb),
        in_specs=[
            pl.BlockSpec((1, 2, lb, h1, c), lambda bi, i: (bi, 0, i, 0, 0)),
            pl.BlockSpec((h1, 2 * h1), lambda bi, i: (0, 0)),
            pl.BlockSpec((1, 1, h1, lb * c), lambda bi, i: (u_idx, bi, 0, i)),
            pl.BlockSpec((1, 1, h1, lb * c), lambda bi, i: (gate_idx, bi, 0, i)),
            pl.BlockSpec((1, c), lambda bi, i: (0, 0)),
        ],
        out_specs=pl.BlockSpec((1, h1, lb * c), lambda bi, i: (bi, 0, i)),
        out_shape=jax.ShapeDtypeStruct((b, h1, l2 * c), F32),
        compiler_params=_params("parallel", "parallel"),
        name="ct_inv1",
    )(bsp, consts["w1i"], u, gate, skip.reshape(1, c))


def _hyena(u, conv_w, conv_b, f_w1, f_b1, f_w2, f_b2, f_w3, f_freq, skip):
    b, n, c3 = u.shape
    c = c3 // 3
    consts = _dft_constants(n)
    l1, l2, h1 = consts["l1"], consts["l2"], consts["h1"]
    filt, norms = _hyena_filters(n, f_w1, f_b1, f_w2, f_b2, f_w3, f_freq, c)
    fa = _ct_fwd1(filt.reshape(1, HY_ORDER, l1, l2 * c), 0, consts, c, norms=norms.reshape(HY_ORDER, 1, c))
    hspec = _ct_spectrum(fa.reshape(HY_ORDER, 2, l2, h1 * c), consts, c)
    parts = _short_conv(u, conv_w, conv_b).reshape(3, b, h1, l2 * c)

    def long_conv_gated(x_stack, x_idx, gate_idx, order):
        a = _ct_fwd1(x_stack, x_idx, consts, c)
        bsp = _ct_mid(a.reshape(b, 2, l2, h1 * c), hspec, order, consts, c)
        return _ct_inv1(bsp.reshape(b, 2, l2, h1, c), x_stack, x_idx, parts, gate_idx, skip[order], consts, c)

    z = long_conv_gated(parts, 0, 1, 0)
    y = long_conv_gated(z[None], 0, 2, 1)
    return y.reshape(b, n, c)


def _outproj_body(*refs, n_in):
    ins = refs[:n_in]
    ws = refs[n_in:2 * n_in]
    x_ref, gate_ref, g_ref, sc_ref, sh_ref, rhi_ref, rlo_ref, rb_ref, xo_ref, h_ref, rt_ref = refs[2 * n_in:]
    y = None
    for a_ref, w_ref in zip(ins, ws):
        t = jnp.dot(a_ref[0].astype(BF16), w_ref[...], preferred_element_type=F32)
        y = t if y is None else y + t
    x = x_ref[0] + gate_ref[0] * y
    xo_ref[0] = x
    h = x * lax.rsqrt(jnp.mean(x * x, axis=-1, keepdims=True) + NORM_EPS) * g_ref[...]
    h = h * (1.0 + sc_ref[0]) + sh_ref[0]
    for s in range(h.shape[-1] // LANES):
        h_ref[0, :, s, :] = h[:, s * LANES:(s + 1) * LANES]
    hi = h.astype(BF16)
    lo = (h - hi.astype(F32)).astype(BF16)
    lg = (jnp.dot(hi, rhi_ref[...], preferred_element_type=F32)
          + jnp.dot(lo, rhi_ref[...], preferred_element_type=F32)
          + jnp.dot(hi, rlo_ref[...], preferred_element_type=F32) + rb_ref[...])
    rt_ref[0] = _route(lg)


def _route(lg):
    lane = lax.broadcasted_iota(jnp.int32, lg.shape, 1)
    lane_f = lane.astype(F32)
    neg = -jnp.inf

    def top(v):
        m = jnp.max(v, axis=-1, keepdims=True)
        return m, jnp.min(jnp.where(v == m, lane_f, float(LANES)), axis=-1, keepdims=True)

    gl = jnp.where(lane < N_GROUPS, lg, neg)
    gmax, grp = top(gl)
    p_grp = 1.0 / jnp.sum(jnp.exp(gl - gmax), axis=-1, keepdims=True)
    first = N_GROUPS + grp * EXP_PER_GROUP
    el = jnp.where((lane_f >= first) & (lane_f < first + EXP_PER_GROUP), lg, neg)
    m1, i1 = top(el)
    m2, i2 = top(jnp.where(lane_f == i1, neg, el))
    e2 = jnp.exp(m2 - m1)
    den = 1.0 + e2
    vals = (i1 - N_GROUPS, i2 - N_GROUPS, p_grp * (1.0 / den), p_grp * (e2 / den))
    out = jnp.zeros(lg.shape, F32)
    for k, v in enumerate(vals):
        out = jnp.where(lane == k, v, out)
    return out


def _outproj(ins, ws, x, gate, g, scale, shift, r_hi, r_lo, r_b, tm=512):
    b, n, d = x.shape
    tm = min(tm, n)
    bm = gate.shape[0]
    mod_map = (lambda bi, i: (bi, 0, 0)) if bm > 1 else (lambda bi, i: (0, 0, 0))
    row = lambda wd: pl.BlockSpec((1, tm, wd), lambda bi, i: (bi, i, 0))
    full = lambda a: pl.BlockSpec(a.shape, lambda bi, i: (0,) * a.ndim)
    mod = pl.BlockSpec((1, 1, d), mod_map)
    return pl.pallas_call(
        functools.partial(_outproj_body, n_in=len(ins)),
        grid=(b, n // tm),
        in_specs=([row(a.shape[-1]) for a in ins] + [full(w) for w in ws]
                  + [row(d), mod, pl.BlockSpec((1, d), lambda bi, i: (0, 0)), mod, mod,
                     full(r_hi), full(r_lo), full(r_b)]),
        out_specs=[row(d), pl.BlockSpec((1, tm, d // LANES, LANES), lambda bi, i: (bi, i, 0, 0)), row(LANES)],
        out_shape=[jax.ShapeDtypeStruct((b, n, d), F32), jax.ShapeDtypeStruct((b, n, d // LANES, LANES), F32),
                   jax.ShapeDtypeStruct((b, n, LANES), F32)],
        compiler_params=_params("parallel", "parallel"),
        name="outproj",
    )(*ins, *ws, x, gate, g.reshape(1, d), scale, shift, r_hi, r_lo, r_b)


def _rank_body(rt_ref, tri_ref, rank_ref, cnt_ref, run_ref):
    i = pl.program_id(0)

    @pl.when(i == 0)
    def _():
        run_ref[...] = jnp.zeros_like(run_ref)

    rt = rt_ref[...]
    lane = lax.broadcasted_iota(jnp.int32, rt.shape, 1).astype(F32)
    oh_a = lane == rt[:, 0:1]
    oh_b = lane == rt[:, 1:2]
    one_a = jnp.where(oh_a, 1.0, 0.0)
    one_b = jnp.where(oh_b, 1.0, 0.0)
    before_a = jnp.dot(tri_ref[...], one_a.astype(BF16), preferred_element_type=F32)
    before_b = jnp.dot(tri_ref[...], one_b.astype(BF16), preferred_element_type=F32)
    tot_a = jnp.sum(one_a, axis=0, keepdims=True)
    run = run_ref[...]
    rank_a = jnp.sum(jnp.where(oh_a, before_a + run, 0.0), axis=-1, keepdims=True)
    rank_b = jnp.sum(jnp.where(oh_b, before_b + run + tot_a, 0.0), axis=-1, keepdims=True)
    lane_i = lax.broadcasted_iota(jnp.int32, rt.shape, 1)
    rank_ref[...] = jnp.where(lane_i == 0, rank_a, jnp.where(lane_i == 1, rank_b, 0.0))
    run = run + tot_a + jnp.sum(one_b, axis=0, keepdims=True)
    run_ref[...] = run
    cnt_ref[...] = run


def _rank(route, tm=512):
    t = route.shape[0]
    tri = jnp.asarray(np.tril(np.ones((tm, tm)), -1), dtype=BF16)
    return pl.pallas_call(
        _rank_body,
        grid=(t // tm,),
        in_specs=[pl.BlockSpec((tm, LANES), lambda i: (i, 0)), pl.BlockSpec((tm, tm), lambda i: (0, 0))],
        out_specs=[pl.BlockSpec((tm, LANES), lambda i: (i, 0)), pl.BlockSpec((1, LANES), lambda i: (0, 0))],
        out_shape=[jax.ShapeDtypeStruct((t, LANES), F32), jax.ShapeDtypeStruct((1, LANES), F32)],
        scratch_shapes=[pltpu.VMEM((1, LANES), F32)],
        compiler_params=_params("arbitrary"),
        name="moe_rank",
    )(route, tri)


def _scatter_rows_body(slots_hbm, h_ref, xs_in, xs_out, idx_smem, sem_idx, sem_rows, *, tm):
    del xs_in
    i = pl.program_id(0)
    idx_copy = pltpu.make_async_copy(slots_hbm.at[i], idx_smem, sem_idx)
    idx_copy.start()
    idx_copy.wait()

    def row_copy(j):
        return pltpu.make_async_copy(h_ref.at[j % tm], xs_out.at[idx_smem[j]], sem_rows)

    def issue(j, carry):
        row_copy(j).start()
        return carry

    def drain(j, carry):
        row_copy(j).wait()
        return carry

    lax.fori_loop(0, TOP_K * tm, issue, 0)
    lax.fori_loop(0, TOP_K * tm, drain, 0)


def _tile_slots(tok_slots, tm):
    t = tok_slots.shape[0]
    return jnp.swapaxes(tok_slots.reshape(t // tm, tm, TOP_K), 1, 2).reshape(t // tm, TOP_K * tm)


def _scatter_rows(h_tiles, tok_slots, xs, tm=512):
    t = h_tiles.shape[0]
    tm = min(tm, t)
    tile = h_tiles.shape[1:]
    return pl.pallas_call(
        functools.partial(_scatter_rows_body, tm=tm),
        grid=(t // tm,),
        in_specs=[
            pl.BlockSpec(memory_space=pl.ANY),
            pl.BlockSpec((tm,) + tile, lambda i: (i, 0, 0)),
            pl.BlockSpec(memory_space=pl.ANY),
        ],
        out_specs=pl.BlockSpec(memory_space=pl.ANY),
        out_shape=jax.ShapeDtypeStruct(xs.shape, xs.dtype),
        input_output_aliases={2: 0},
        scratch_shapes=[
            pltpu.SMEM((TOP_K * tm,), jnp.int32),
            pltpu.SemaphoreType.DMA,
            pltpu.SemaphoreType.DMA,
        ],
        compiler_params=_params("arbitrary"),
        name="moe_scatter_rows",
    )(_tile_slots(tok_slots, tm), h_tiles, xs)


def _expert_body(bexp_ref, nused_ref, x_ref, wg_ref, wu_ref, wd_ref, o_ref):
    i = pl.program_id(0)
    n_sub = x_ref.shape[1]

    @pl.when(i < nused_ref[0])
    def _():
        xb = jnp.concatenate([x_ref[:, s, :] for s in range(n_sub)], axis=1).astype(BF16)
        gt = jnp.dot(xb, wg_ref[0], preferred_element_type=F32)
        up = jnp.dot(xb, wu_ref[0], preferred_element_type=F32)
        hid = (gt * jax.nn.sigmoid(gt) * up).astype(BF16)
        y = jnp.dot(hid, wd_ref[0], preferred_element_type=F32)
        for s in range(n_sub):
            o_ref[:, s, :] = y[:, s * LANES:(s + 1) * LANES]

    @pl.when(i >= nused_ref[0])
    def _():
        o_ref[...] = jnp.zeros_like(o_ref)


def _experts(xs, block_exp, n_used, w_gate, w_up, w_down):
    n_slots = xs.shape[0]
    tile = xs.shape[1:]
    d, de = w_gate.shape[1:]
    n_blocks = n_slots // EXPERT_ROWS
    grid_spec = pltpu.PrefetchScalarGridSpec(
        num_scalar_prefetch=2,
        grid=(n_blocks,),
        in_specs=[
            pl.BlockSpec((EXPERT_ROWS,) + tile, lambda i, be, nu: (i, 0, 0)),
            pl.BlockSpec((1, d, de), lambda i, be, nu: (be[i], 0, 0)),
            pl.BlockSpec((1, d, de), lambda i, be, nu: (be[i], 0, 0)),
            pl.BlockSpec((1, de, d), lambda i, be, nu: (be[i], 0, 0)),
        ],
        out_specs=pl.BlockSpec((EXPERT_ROWS,) + tile, lambda i, be, nu: (i, 0, 0)),
    )
    return pl.pallas_call(
        _expert_body,
        grid_spec=grid_spec,
        out_shape=jax.ShapeDtypeStruct(xs.shape, F32),
        compiler_params=_params("arbitrary"),
        name="experts",
    )(block_exp, n_used, xs, w_gate, w_up, w_down)


def _combine_body(slots_hbm, ys_hbm, x_ref, gate_ref, rt_ref, o_ref, idx_smem, ybuf, sem_idx, sem_rows,
                  *, tm, n_tiles):
    i = pl.program_id(0) * n_tiles + pl.program_id(1)
    idx_copy = pltpu.make_async_copy(slots_hbm.at[i], idx_smem, sem_idx)
    idx_copy.start()
    idx_copy.wait()

    def row_copy(j):
        return pltpu.make_async_copy(ys_hbm.at[idx_smem[j]], ybuf.at[j], sem_rows)

    def issue(j, carry):
        row_copy(j).start()
        return carry

    def drain(j, carry):
        row_copy(j).wait()
        return carry

    lax.fori_loop(0, TOP_K * tm, issue, 0)
    lax.fori_loop(0, TOP_K * tm, drain, 0)
    g_a = rt_ref[0, :, TOP_K:TOP_K + 1]
    g_b = rt_ref[0, :, TOP_K + 1:TOP_K + 2]
    for s in range(ybuf.shape[1]):
        cs = slice(s * LANES, (s + 1) * LANES)
        y = g_a * ybuf[0:tm, s, :] + g_b * ybuf[tm:2 * tm, s, :]
        o_ref[0, :, cs] = x_ref[0, :, cs] + gate_ref[0, :, cs] * y


def _combine(ys, tok_slots, route, x, gate, tm=256):
    b, n, d = x.shape
    tm = min(tm, n)
    n_tiles = n // tm
    bm = gate.shape[0]
    mod_map = (lambda bi, i: (bi, 0, 0)) if bm > 1 else (lambda bi, i: (0, 0, 0))
    return pl.pallas_call(
        functools.partial(_combine_body, tm=tm, n_tiles=n_tiles),
        grid=(b, n_tiles),
        in_specs=[
            pl.BlockSpec(memory_space=pl.ANY),
            pl.BlockSpec(memory_space=pl.ANY),
            pl.BlockSpec((1, tm, d), lambda bi, i: (bi, i, 0)),
            pl.BlockSpec((1, 1, d), mod_map),
            pl.BlockSpec((1, tm, LANES), lambda bi, i: (bi, i, 0)),
        ],
        out_specs=pl.BlockSpec((1, tm, d), lambda bi, i: (bi, i, 0)),
        out_shape=jax.ShapeDtypeStruct((b, n, d), F32),
        scratch_shapes=[
            pltpu.SMEM((TOP_K * tm,), jnp.int32),
            pltpu.VMEM((TOP_K * tm,) + ys.shape[1:], F32),
            pltpu.SemaphoreType.DMA,
            pltpu.SemaphoreType.DMA,
        ],
        compiler_params=_params("arbitrary", "arbitrary"),
        name="moe_combine",
    )(_tile_slots(tok_slots, tm), ys, x, gate, route)


def _slot_layout(route_flat, n_blocks):
    rank, counts = _rank(route_flat)
    counts = counts[0, :N_EXPERTS].astype(jnp.int32)
    padded = ((counts + EXPERT_ROWS - 1) // EXPERT_ROWS) * EXPERT_ROWS
    ends = jnp.cumsum(padded)
    pstart = ends - padded
    experts = route_flat[:, :TOP_K].astype(jnp.int32)
    onehot = experts[:, :, None] == jnp.arange(N_EXPERTS, dtype=jnp.int32)[None, None, :]
    first_row = jnp.sum(jnp.where(onehot, pstart[None, None, :], 0), axis=-1)
    tok_slots = first_row + rank[:, :TOP_K].astype(jnp.int32)
    block_row0 = jnp.arange(n_blocks, dtype=jnp.int32) * EXPERT_ROWS
    block_exp = jnp.minimum(jnp.sum(ends[None, :] <= block_row0[:, None], axis=1), N_EXPERTS - 1)
    n_used = (ends[-1] // EXPERT_ROWS).astype(jnp.int32).reshape(1)
    return tok_slots, block_exp.astype(jnp.int32), n_used


def _final_norm_body(x_ref, g_ref, o_ref):
    x = x_ref[...]
    o_ref[...] = x * lax.rsqrt(jnp.mean(x * x, axis=-1, keepdims=True) + NORM_EPS) * g_ref[...]


def _final_norm(x, g, tm=1024):
    t, d = x.shape
    return pl.pallas_call(
        _final_norm_body,
        grid=(t // tm,),
        in_specs=[pl.BlockSpec((tm, d), lambda i: (i, 0)), pl.BlockSpec((1, d), lambda i: (0, 0))],
        out_specs=pl.BlockSpec((tm, d), lambda i: (i, 0)),
        out_shape=jax.ShapeDtypeStruct((t, d), F32),
        compiler_params=_params("parallel"),
        name="final_norm",
    )(x, g.reshape(1, d))


def _router_weights(wg, bg, we, be):
    d = wg.shape[0]
    w = jnp.zeros((d, LANES), F32).at[:, :N_GROUPS].set(wg).at[:, N_GROUPS:N_GROUPS + N_EXPERTS].set(we)
    bias = jnp.zeros((1, LANES), F32).at[0, :N_GROUPS].set(bg).at[0, N_GROUPS:N_GROUPS + N_EXPERTS].set(be)
    hi = w.astype(BF16)
    lo = (w - hi.astype(F32)).astype(BF16)
    return hi, lo, bias


def kernel(x, c, ctx, c_ctx, ada_w, ada_b, norm1_g, norm2_g, ev_w_in, ev_w_out, hy_conv_w, hy_conv_b, hy_f_w1, hy_f_b1, hy_f_w2, hy_f_b2, hy_f_w3, hy_f_freq, hy_skip, swa_sink, od_w_qkv, od_w_out, od_q_norm_g, od_k_norm_g, rt_group_w, rt_group_b, rt_exp_w, rt_exp_b, moe_w_gate, moe_w_up, moe_w_down, final_norm_g):
    b, n, d = x.shape
    lc = ctx.shape[1]
    depth = ada_w.shape[0]
    rope = _rope_tables(n)
    xc = ctx
    sc = jax.nn.silu(c)
    scc = jax.nn.silu(c_ctx)
    q_scale = HEAD_DIM ** -0.5
    for layer in range(depth):
        with_ctx = layer < depth - 1
        mod = (sc @ ada_w[layer] + ada_b[layer]).reshape(b, N_MOD, 1, d)
        modc = (scc @ ada_w[layer] + ada_b[layer]).reshape(1, N_MOD, 1, d)
        m = [mod[:, k] for k in range(N_MOD)]
        mc = [modc[:, k] for k in range(N_MOD)]
        r_hi, r_lo, r_b = _router_weights(rt_group_w[layer], rt_group_b[layer], rt_exp_w[layer], rt_exp_b[layer])
        wgt, wup, wdn = (moe_w_gate[layer].astype(BF16), moe_w_up[layer].astype(BF16),
                         moe_w_down[layer].astype(BF16))
        if layer % 2 == 0:
            e = layer // 2
            c_hy = hy_conv_w.shape[-1] // 3
            d_hy = 3 * c_hy
            hq = swa_sink.shape[-1]
            d_q = hq * HEAD_DIM
            hkv = hq // 4
            d_kv = hkv * HEAD_DIM
            w_in = ev_w_in[e].astype(BF16)
            w_out = ev_w_out[e].astype(BF16)
            hy_args = (hy_conv_w[e], hy_conv_b[e], hy_f_w1[e], hy_f_b1[e], hy_f_w2[e], hy_f_b2[e],
                       hy_f_w3[e], hy_f_freq[e], hy_skip[e])
            u, q, k, v = _proj(x, norm1_g[layer], m[1], m[0], w_in, [
                (0, d_hy, "f32", None, False, 1.0),
                (d_hy, d_q, "qk", None, True, q_scale),
                (d_hy + d_q, d_kv, "qk", None, True, 1.0),
                (d_hy + d_q + d_kv, d_kv, "bf16", None, False, 1.0)], rope_tabs=rope)
            if with_ctx:
                uc, qc, kc, vc = _proj(xc, norm1_g[layer], mc[1], mc[0], w_in, [
                    (0, d_hy, "f32", None, False, 1.0),
                    (d_hy, d_q, "qk", None, False, q_scale),
                    (d_hy + d_q, d_kv, "bf16", None, False, 1.0),
                    (d_hy + d_q + d_kv, d_kv, "bf16", None, False, 1.0)])
            else:
                kc, vc = _proj(xc, norm1_g[layer], mc[1], mc[0], w_in, [
                    (d_hy + d_q, d_kv, "bf16", None, False, 1.0),
                    (d_hy + d_q + d_kv, d_kv, "bf16", None, False, 1.0)])
            y_hy = _hyena(u, *hy_args)
            y_att = _windowed_attention(q, k, v, kc, vc, swa_sink[e], hkv)
            mix_in, mix_w = [y_hy, y_att], [w_out[:c_hy], w_out[c_hy:]]
            if with_ctx:
                yc_hy = _hyena(uc, *hy_args)
                yc_att = _full_attention(qc, kc, vc, hkv, sink=swa_sink[e])
                mixc_in = [yc_hy, yc_att]
        else:
            o = layer // 2
            hkv = od_w_qkv.shape[-1] // HEAD_DIM // 6
            hq = 4 * hkv
            d_q = hq * HEAD_DIM
            d_kv = hkv * HEAD_DIM
            w_qkv = od_w_qkv[o].astype(BF16)
            w_out = od_w_out[o].astype(BF16)
            norm_g = jnp.zeros((8, LANES), F32).at[0].set(jnp.tile(od_q_norm_g[o], 2)).at[1].set(
                jnp.tile(od_k_norm_g[o], 2))
            q, k, v = _proj(x, norm1_g[layer], m[1], m[0], w_qkv, [
                (0, d_q, "qk", 0, True, q_scale),
                (d_q, d_kv, "qk", 1, True, 1.0),
                (d_q + d_kv, d_kv, "bf16", None, False, 1.0)], rope_tabs=rope, norm_g=norm_g)
            if with_ctx:
                qc, kc, vc = _proj(xc, norm1_g[layer], mc[1], mc[0], w_qkv, [
                    (0, d_q, "qk", 0, False, q_scale),
                    (d_q, d_kv, "qk", 1, False, 1.0),
                    (d_q + d_kv, d_kv, "bf16", None, False, 1.0)], norm_g=norm_g)
            else:
                kc, vc = _proj(xc, norm1_g[layer], mc[1], mc[0], w_qkv, [
                    (d_q, d_kv, "qk", 1, False, 1.0),
                    (d_q + d_kv, d_kv, "bf16", None, False, 1.0)], norm_g=norm_g)
            y_att = _full_attention(q, jnp.concatenate([kc, k], axis=1), jnp.concatenate([vc, v], axis=1), hkv)
            mix_in, mix_w = [y_att], [w_out]
            if with_ctx:
                mixc_in = [_full_attention(qc, kc, vc, hkv)]
        x, h2, rt = _outproj(mix_in, mix_w, x, m[2], norm2_g[layer], m[4], m[3], r_hi, r_lo, r_b)
        route_flat = rt.reshape(b * n, LANES)
        if with_ctx:
            xc, h2c, rtc = _outproj(mixc_in, mix_w, xc, mc[2], norm2_g[layer], mc[4], mc[3], r_hi, r_lo, r_b)
            route_flat = jnp.concatenate([route_flat, rtc.reshape(b * lc, LANES)], axis=0)
        n_blocks = -(-route_flat.shape[0] * TOP_K // EXPERT_ROWS) + N_EXPERTS
        tok_slots, block_exp, n_used = _slot_layout(route_flat, n_blocks)
        tile = h2.shape[2:]
        xs = jnp.zeros((n_blocks * EXPERT_ROWS,) + tile, F32)
        xs = _scatter_rows(h2.reshape((b * n,) + tile), tok_slots[:b * n], xs)
        if with_ctx:
            xs = _scatter_rows(h2c.reshape((b * lc,) + tile), tok_slots[b * n:], xs)
        ys = _experts(xs, block_exp, n_used, wgt, wup, wdn)
        x = _combine(ys, tok_slots[:b * n], rt, x, m[5])
        if with_ctx:
            xc = _combine(ys, tok_slots[b * n:], rtc, xc, mc[5])
    return _final_norm(x.reshape(b * n, d), final_norm_g).reshape(b, n, d)
```

```python
import functools
import math

import numpy as np
import jax
import jax.numpy as jnp
from jax import lax
from jax.experimental import pallas as pl
from jax.experimental.pallas import tpu as pltpu

F32 = jnp.float32
BF16 = jnp.bfloat16

HEAD_DIM = 64
GRID_W = 64
ROPE_BASE = 10000.0
Q_BLOCK = 128
NORM_EPS = 1e-6
N_MOD = 6
HY_ORDER = 2
HY_BANDS = 16
HY_DIRS = 2
HY_DECAY_TARGET = 1e-2
HY_FAST_DECAY = 0.3
HY_SLOW_DECAY = 1.5
HY_FILTER_EPS = 1e-6
SWA_WINDOW = 128
N_GROUPS = 4
EXP_PER_GROUP = 8
N_EXPERTS = N_GROUPS * EXP_PER_GROUP
TOP_K = 2
EXPERT_ROWS = 512
TOKEN_TILE = 512
CHUNK_ROWS = 8
BUF_ROWS = 1280
BUF_CHUNKS = BUF_ROWS // CHUNK_ROWS
TABLE_WORDS = 256

LANES = 128
VMEM_LIMIT_BYTES = 56 * 1024 * 1024


def _params(*sem):
    return pltpu.CompilerParams(dimension_semantics=sem, vmem_limit_bytes=VMEM_LIMIT_BYTES)


def _rope_tables(n):
    d_axis = HEAD_DIM // 2
    t = jnp.arange(n)
    inv = ROPE_BASE ** (-jnp.arange(0, d_axis, 2, dtype=F32) / d_axis)
    ang_r = (t // GRID_W).astype(F32)[:, None] * inv[None, :]
    ang_c = (t % GRID_W).astype(F32)[:, None] * inv[None, :]
    cos = jnp.concatenate([jnp.cos(ang_r)] * 2 + [jnp.cos(ang_c)] * 2, axis=-1)
    sin = jnp.concatenate([-jnp.sin(ang_r), jnp.sin(ang_r), -jnp.sin(ang_c), jnp.sin(ang_c)], axis=-1)
    return jnp.tile(cos, (1, 2)), jnp.tile(sin, (1, 2))


def _head_mean_matrix():
    i = np.arange(LANES)
    return jnp.asarray((i[:, None] // HEAD_DIM == i[None, :] // HEAD_DIM) / HEAD_DIM, dtype=BF16)


def _proj_body(x_ref, g_ref, sc_ref, sh_ref, w_ref, cos_ref, sin_ref, ng_ref, bd_ref, *out_refs, segs):
    x = x_ref[0]
    h = x * lax.rsqrt(jnp.mean(x * x, axis=-1, keepdims=True) + NORM_EPS) * g_ref[...]
    hb = (h * (1.0 + sc_ref[0]) + sh_ref[0]).astype(BF16)
    for o_ref, (c0, width, kind, norm_row, rope, out_scale) in zip(out_refs, segs):
        seg = jnp.dot(hb, w_ref[:, c0:c0 + width], preferred_element_type=F32)
        if kind == "f32":
            o_ref[0] = seg
            continue
        if kind == "bf16":
            o_ref[0] = seg.astype(BF16)
            continue
        for j in range(width // LANES):
            ch = seg[:, j * LANES:(j + 1) * LANES]
            if norm_row is not None:
                sq = ch * ch
                hi = sq.astype(BF16)
                lo = (sq - hi.astype(F32)).astype(BF16)
                ms = (jnp.dot(hi, bd_ref[...], preferred_element_type=F32)
                      + jnp.dot(lo, bd_ref[...], preferred_element_type=F32))
                ch = ch * lax.rsqrt(ms + NORM_EPS) * ng_ref[norm_row:norm_row + 1, :]
            if rope:
                lane = lax.broadcasted_iota(jnp.int32, ch.shape, 1)
                partner = jnp.where(lane % 32 < 16, pltpu.roll(ch, LANES - 16, 1), pltpu.roll(ch, 16, 1))
                ch = ch * cos_ref[...] + partner * sin_ref[...]
            if out_scale != 1.0:
                ch = ch * out_scale
            o_ref[0, :, j * LANES:(j + 1) * LANES] = ch.astype(BF16)


def _proj(x, g, scale, shift, w, segs, rope_tabs=None, norm_g=None, tm=512):
    b, n, d = x.shape
    tm = min(tm, n)
    bm = scale.shape[0]
    mod_map = (lambda bi, i: (bi, 0, 0)) if bm > 1 else (lambda bi, i: (0, 0, 0))
    if rope_tabs is None:
        cos = sin = jnp.zeros((8, LANES), F32)
        tab_spec = pl.BlockSpec((8, LANES), lambda bi, i: (0, 0))
    else:
        cos, sin = rope_tabs
        tab_spec = pl.BlockSpec((tm, LANES), lambda bi, i: (i, 0))
    if norm_g is None:
        norm_g = jnp.ones((8, LANES), F32)
    out_shape = [jax.ShapeDtypeStruct((b, n, s[1]), F32 if s[2] == "f32" else BF16) for s in segs]
    out_specs = [pl.BlockSpec((1, tm, s[1]), lambda bi, i: (bi, i, 0)) for s in segs]
    return pl.pallas_call(
        functools.partial(_proj_body, segs=tuple(segs)),
        grid=(b, n // tm),
        in_specs=[
            pl.BlockSpec((1, tm, d), lambda bi, i: (bi, i, 0)),
            pl.BlockSpec((1, d), lambda bi, i: (0, 0)),
            pl.BlockSpec((1, 1, d), mod_map),
            pl.BlockSpec((1, 1, d), mod_map),
            pl.BlockSpec(w.shape, lambda bi, i: (0, 0)),
            tab_spec,
            tab_spec,
            pl.BlockSpec(norm_g.shape, lambda bi, i: (0, 0)),
            pl.BlockSpec((LANES, LANES), lambda bi, i: (0, 0)),
        ],
        out_specs=out_specs,
        out_shape=out_shape,
        compiler_params=_params("parallel", "parallel"),
        name="proj",
    )(x, g.reshape(1, d), scale, shift, w, cos, sin, norm_g, _head_mean_matrix())


def _stack_heads(q, j, g):
    return jnp.concatenate(
        [q[:, (j * g + gg) * HEAD_DIM:(j * g + gg + 1) * HEAD_DIM] for gg in range(g)], axis=0)


def _values_with_ones(v, hkv):
    b, nk, _ = v.shape
    ones = jnp.zeros((b, nk, hkv, HEAD_DIM), v.dtype).at[..., 0].set(1)
    return jnp.concatenate([v.reshape(b, nk, hkv, HEAD_DIM), ones], axis=-1).reshape(b, nk, 2 * hkv * HEAD_DIM)


def _sink_column(sink_ref, j, g, qb):
    return jnp.concatenate([jnp.full((qb, 1), sink_ref[j * g + gg], F32) for gg in range(g)], axis=0)


def _swa_body(sink_ref, q_ref, kt_ref, v_ref, kct_ref, vc_ref, o_ref, *, n, hkv, g, qb, win):
    i = pl.program_id(1)
    start = pl.multiple_of(i * qb, qb)
    q = q_ref[0]
    rows = lax.broadcasted_iota(jnp.int32, (g * qb, 3 * qb), 0) % qb
    cols = lax.broadcasted_iota(jnp.int32, (g * qb, 3 * qb), 1)
    key_pos = cols + (i - 1) * qb
    valid = (jnp.abs(rows + qb - cols) <= win) & (key_pos >= 0) & (key_pos < n)
    for j in range(hkv):
        hs = slice(j * HEAD_DIM, (j + 1) * HEAD_DIM)
        q4 = _stack_heads(q, j, g)
        s_lat = jnp.dot(q4, kt_ref[0, hs, pl.ds(start, 3 * qb)], preferred_element_type=F32)
        s_lat = jnp.where(valid, s_lat, -jnp.inf)
        s_ctx = jnp.dot(q4, kct_ref[0, hs, :], preferred_element_type=F32)
        s_sink = _sink_column(sink_ref, j, g, qb)
        m = jnp.maximum(jnp.maximum(jnp.max(s_lat, axis=-1, keepdims=True),
                                    jnp.max(s_ctx, axis=-1, keepdims=True)), s_sink)
        e_lat = jnp.exp((s_lat - m).astype(BF16))
        e_ctx = jnp.exp((s_ctx - m).astype(BF16))
        vs = slice(j * 2 * HEAD_DIM, (j + 1) * 2 * HEAD_DIM)
        o = (jnp.dot(e_ctx, vc_ref[0, :, vs], preferred_element_type=F32)
             + jnp.dot(e_lat, v_ref[0, pl.ds(start, 3 * qb), vs], preferred_element_type=F32))
        o = o[:, :HEAD_DIM] / (o[:, HEAD_DIM:HEAD_DIM + 1] + jnp.exp(s_sink - m))
        for gg in range(g):
            c0 = (j * g + gg) * HEAD_DIM
            o_ref[0, :, c0:c0 + HEAD_DIM] = o[gg * qb:(gg + 1) * qb].astype(BF16)


def _windowed_attention(q, k, v, kc, vc, sink, hkv):
    b, n, dq = q.shape
    g = dq // HEAD_DIM // hkv
    qb = Q_BLOCK
    lc = kc.shape[1]
    dkv = hkv * HEAD_DIM
    kt = jnp.swapaxes(jnp.pad(k, ((0, 0), (qb, qb), (0, 0))), 1, 2)
    vp = _values_with_ones(jnp.pad(v, ((0, 0), (qb, qb), (0, 0))), hkv)
    vc = _values_with_ones(vc, hkv)
    kct = jnp.swapaxes(kc, 1, 2)
    return pl.pallas_call(
        functools.partial(_swa_body, n=n, hkv=hkv, g=g, qb=qb, win=SWA_WINDOW),
        grid=(b, n // qb),
        in_specs=[
            pl.BlockSpec(memory_space=pltpu.SMEM),
            pl.BlockSpec((1, qb, dq), lambda bi, i: (bi, i, 0)),
            pl.BlockSpec((1, dkv, n + 2 * qb), lambda bi, i: (bi, 0, 0)),
            pl.BlockSpec((1, n + 2 * qb, 2 * dkv), lambda bi, i: (bi, 0, 0)),
            pl.BlockSpec((1, dkv, lc), lambda bi, i: (bi, 0, 0)),
            pl.BlockSpec((1, lc, 2 * dkv), lambda bi, i: (bi, 0, 0)),
        ],
        out_specs=pl.BlockSpec((1, qb, dq), lambda bi, i: (bi, i, 0)),
        out_shape=jax.ShapeDtypeStruct((b, n, dq), BF16),
        compiler_params=_params("parallel", "parallel"),
        name="swa",
    )(sink.astype(F32), q, kt, vp, kct, vc)


def _full_attn_body(sink_ref, q_ref, kt_ref, v_ref, o_ref, *, hkv, g, qb, has_sink):
    q = q_ref[0]
    for j in range(hkv):
        hs = slice(j * HEAD_DIM, (j + 1) * HEAD_DIM)
        q4 = _stack_heads(q, j, g)
        s = jnp.dot(q4, kt_ref[0, hs, :], preferred_element_type=F32)
        m = jnp.max(s, axis=-1, keepdims=True)
        if has_sink:
            s_sink = _sink_column(sink_ref, j, g, qb)
            m = jnp.maximum(m, s_sink)
        e = jnp.exp((s - m).astype(BF16))
        o = jnp.dot(e, v_ref[0, :, j * 2 * HEAD_DIM:(j + 1) * 2 * HEAD_DIM], preferred_element_type=F32)
        den = o[:, HEAD_DIM:HEAD_DIM + 1]
        if has_sink:
            den = den + jnp.exp(s_sink - m)
        o = o[:, :HEAD_DIM] / den
        for gg in range(g):
            c0 = (j * g + gg) * HEAD_DIM
            o_ref[0, :, c0:c0 + HEAD_DIM] = o[gg * qb:(gg + 1) * qb].astype(BF16)


def _full_attention(q, k, v, hkv, sink=None):
    b, n, dq = q.shape
    g = dq // HEAD_DIM // hkv
    qb = Q_BLOCK
    nk = k.shape[1]
    dkv = hkv * HEAD_DIM
    kt = jnp.swapaxes(k, 1, 2)
    has_sink = sink is not None
    sink = jnp.zeros((dq // HEAD_DIM,), F32) if sink is None else sink.astype(F32)
    return pl.pallas_call(
        functools.partial(_full_attn_body, hkv=hkv, g=g, qb=qb, has_sink=has_sink),
        grid=(b, n // qb),
        in_specs=[
            pl.BlockSpec(memory_space=pltpu.SMEM),
            pl.BlockSpec((1, qb, dq), lambda bi, i: (bi, i, 0)),
            pl.BlockSpec((1, dkv, nk), lambda bi, i: (bi, 0, 0)),
            pl.BlockSpec((1, nk, 2 * dkv), lambda bi, i: (bi, 0, 0)),
        ],
        out_specs=pl.BlockSpec((1, qb, dq), lambda bi, i: (bi, i, 0)),
        out_shape=jax.ShapeDtypeStruct((b, n, dq), BF16),
        compiler_params=_params("parallel", "parallel"),
        name="full_attn",
    )(sink, q, kt, _values_with_ones(v, hkv))


def _short_conv_body(u_ref, w_ref, b_ref, o_ref, *, n):
    x = u_ref[0]
    row = lax.broadcasted_iota(jnp.int32, x.shape, 0)
    prev = jnp.where(row == 0, 0.0, pltpu.roll(x, 1, 0))
    nxt = jnp.where(row == n - 1, 0.0, pltpu.roll(x, n - 1, 0))
    o_ref[0, 0] = prev * w_ref[0:1, :] + x * w_ref[1:2, :] + nxt * w_ref[2:3, :] + b_ref[...]


def _short_conv(u, w, bias, cb=256):
    b, n, c3 = u.shape
    c = c3 // 3
    per = c // cb
    return pl.pallas_call(
        functools.partial(_short_conv_body, n=n),
        grid=(b, c3 // cb),
        in_specs=[
            pl.BlockSpec((1, n, cb), lambda bi, j: (bi, 0, j)),
            pl.BlockSpec((3, cb), lambda bi, j: (0, j)),
            pl.BlockSpec((1, cb), lambda bi, j: (0, j)),
        ],
        out_specs=pl.BlockSpec((1, 1, n, cb), lambda bi, j: (j // per, bi, 0, j % per)),
        out_shape=jax.ShapeDtypeStruct((3, b, n, c), F32),
        compiler_params=_params("parallel", "parallel"),
        name="short_conv",
    )(u, w, bias.reshape(1, c3))


def _filter_body(band_ref, w1_ref, b1_ref, w2_ref, b2_ref, w3_ref, fr_ref, dl_ref, o_ref, s_ref, *, n, rt, c):
    i = pl.program_id(0)
    hp = lax.Precision.HIGHEST
    m = i * rt + lax.broadcasted_iota(jnp.int32, (rt, 1), 0)
    pos = jnp.where(m < n, m, 2 * n - m).astype(F32)
    t_norm = pos / max(n - 1, 1)
    ang = (2.0 * math.pi / n) * pos * band_ref[...]
    lane = lax.broadcasted_iota(jnp.int32, (rt, LANES), 1)
    z = jnp.where(lane == 0, t_norm,
                  jnp.where(lane <= HY_BANDS, jnp.cos(ang),
                            jnp.where(lane <= 2 * HY_BANDS, -jnp.sin(ang), 0.0)))
    fr = fr_ref[...]
    hdn = jnp.sin(fr * (jnp.dot(z, w1_ref[...], precision=hp, preferred_element_type=F32) + b1_ref[...]))
    hdn = jnp.sin(fr * (jnp.dot(hdn, w2_ref[...], precision=hp, preferred_element_type=F32) + b2_ref[...]))
    h = jnp.dot(hdn, w3_ref[...], precision=hp, preferred_element_type=F32)
    h = h * jnp.exp(-t_norm * dl_ref[...])
    half = HY_ORDER * c
    sel = jnp.where(m < n, h[:, :half], jnp.where(m > n, -h[:, half:], 0.0))
    for o in range(HY_ORDER):
        o_ref[o] = sel[:, o * c:(o + 1) * c]

    @pl.when(i == 0)
    def _():
        s_ref[...] = jnp.zeros_like(s_ref)

    s_ref[...] += jnp.sum(jnp.abs(sel), axis=0, keepdims=True)


def _hyena_filters(n, w1, b1, w2, b2, w3, freq, c):
    rt = min(1024, n)
    hid = w1.shape[1]
    bands = jnp.linspace(1e-4, HY_BANDS - 1, HY_BANDS, dtype=F32)
    band_row = jnp.zeros((1, LANES), F32).at[0, 1:1 + 2 * HY_BANDS].set(jnp.tile(bands, 2))
    w1p = jnp.zeros((LANES, hid), F32).at[:w1.shape[0]].set(w1)
    max_decay = math.log(HY_DECAY_TARGET) / HY_FAST_DECAY
    min_decay = math.log(HY_DECAY_TARGET) / HY_SLOW_DECAY
    deltas = jnp.abs(jnp.linspace(min_decay, max_decay, c, dtype=F32))
    dl = jnp.tile(deltas, HY_DIRS * HY_ORDER).reshape(1, -1)
    full = lambda a: pl.BlockSpec(a.shape, lambda i: (0,) * a.ndim)
    args = (band_row, w1p, b1.reshape(1, hid), w2, b2.reshape(1, hid), w3, freq.reshape(1, hid), dl)
    return pl.pallas_call(
        functools.partial(_filter_body, n=n, rt=rt, c=c),
        grid=(2 * n // rt,),
        in_specs=[full(a) for a in args],
        out_specs=[pl.BlockSpec((HY_ORDER, rt, c), lambda i: (0, i, 0)),
                   pl.BlockSpec((1, HY_ORDER * c), lambda i: (0, 0))],
        out_shape=[jax.ShapeDtypeStruct((HY_ORDER, 2 * n, c), F32),
                   jax.ShapeDtypeStruct((1, HY_ORDER * c), F32)],
        compiler_params=_params("arbitrary"),
        name="hyena_filter",
    )(*args)


def _dft_split(n):
    l2 = 32 if n >= 2048 else 16
    return 2 * n // l2, l2


def _dft_constants(n):
    l1, l2 = _dft_split(n)
    h1 = l1 // 2
    nn = 2 * n
    k1 = np.arange(h1)[:, None]
    a = 2 * np.pi * (k1 + 0.5) * np.arange(l1)[None, :] / l1
    w1 = np.concatenate([np.cos(a), -np.sin(a)], axis=0)
    t = 2 * np.pi * (k1 + 0.5) * np.arange(l2)[None, :] / nn
    tw_cos, tw_sin = np.cos(t), np.sin(t)
    p = 2 * np.pi * np.arange(l2)[:, None] * np.arange(l2)[None, :] / l2
    w2 = np.block([[np.cos(p), np.sin(p)], [-np.sin(p), np.cos(p)]])
    w2i = np.block([[np.cos(p), -np.sin(p)], [np.sin(p), np.cos(p)]])
    ai = 2 * np.pi * np.arange(h1)[:, None] * (np.arange(h1)[None, :] + 0.5) / l1
    w1i = (2.0 / nn) * np.concatenate([np.cos(ai), -np.sin(ai)], axis=1)
    c = lambda m, dt: jnp.asarray(m, dtype=dt)
    return dict(
        l1=l1, l2=l2, h1=h1,
        w1=c(w1, BF16), w2=c(w2, BF16), w2i=c(w2i, BF16), w1i=c(w1i, BF16),
        tw_cos_fwd=c(tw_cos.T[:, :, None], F32), tw_sin_fwd=c(tw_sin.T[:, :, None], F32),
        tw_cos_inv=c(tw_cos[:, :, None], F32), tw_sin_inv=c(tw_sin[:, :, None], F32),
    )


def _ct_fwd1_body(x_ref, w_ref, tc_ref, ts_ref, nrm_ref, o_ref, *, lb, c, h1, normalise):
    for q in range(lb):
        x = x_ref[0, 0, :, q * c:(q + 1) * c]
        if normalise:
            x = x / (nrm_ref[0] + HY_FILTER_EPS)
        a = jnp.dot(w_ref[...], x.astype(BF16), preferred_element_type=F32)
        ar, ai = a[:h1], a[h1:]
        tc, ts = tc_ref[q], ts_ref[q]
        o_ref[0, 0, q] = ar * tc + ai * ts
        o_ref[0, 1, q] = ai * tc - ar * ts


def _ct_fwd1(xs, idx, consts, c, norms=None, lb=8):
    _, b, k1n, _ = xs.shape
    l2, h1 = consts["l2"], consts["h1"]
    lb = min(lb, l2)
    w = consts["w1"][:, :k1n]
    normalise = norms is not None
    if norms is None:
        norms = jnp.zeros((b, 1, c), F32)
    return pl.pallas_call(
        functools.partial(_ct_fwd1_body, lb=lb, c=c, h1=h1, normalise=normalise),
        grid=(b, l2 // lb),
        in_specs=[
            pl.BlockSpec((1, 1, k1n, lb * c), lambda bi, i: (idx, bi, 0, i)),
            pl.BlockSpec(w.shape, lambda bi, i: (0, 0)),
            pl.BlockSpec((lb, h1, 1), lambda bi, i: (i, 0, 0)),
            pl.BlockSpec((lb, h1, 1), lambda bi, i: (i, 0, 0)),
            pl.BlockSpec((1, 1, c), lambda bi, i: (bi, 0, 0)),
        ],
        out_specs=pl.BlockSpec((1, 2, lb, h1, c), lambda bi, i: (bi, 0, i, 0, 0)),
        out_shape=jax.ShapeDtypeStruct((b, 2, l2, h1, c), F32),
        compiler_params=_params("parallel", "parallel"),
        name="ct_fwd1",
    )(xs, w, consts["tw_cos_fwd"], consts["tw_sin_fwd"], norms)


def _fine_rows(ref, lead, q):
    return jnp.concatenate([ref[lead + (0, slice(None), q, slice(None))],
                            ref[lead + (1, slice(None), q, slice(None))]], axis=0)


def _ct_spec_body(a_ref, w2_ref, o_ref, *, l2, kb):
    for q in range(kb):
        x = jnp.dot(w2_ref[...], _fine_rows(a_ref, (0,), q).astype(BF16), preferred_element_type=F32)
        o_ref[0, 0, :, q, :] = x[:l2]
        o_ref[0, 1, :, q, :] = x[l2:]


def _ct_spectrum(a, consts, c, kb=8):
    b = a.shape[0]
    l2, h1 = consts["l2"], consts["h1"]
    kb = min(kb, h1)
    blk = pl.BlockSpec((1, 2, l2, kb, c), lambda bi, i: (bi, 0, 0, i, 0))
    return pl.pallas_call(
        functools.partial(_ct_spec_body, l2=l2, kb=kb),
        grid=(b, h1 // kb),
        in_specs=[blk, pl.BlockSpec((2 * l2, 2 * l2), lambda bi, i: (0, 0))],
        out_specs=blk,
        out_shape=jax.ShapeDtypeStruct((b, 2, l2, h1, c), F32),
        compiler_params=_params("parallel", "parallel"),
        name="ct_spectrum",
    )(a, consts["w2"])


def _ct_mid_body(a_ref, h_ref, w2_ref, w2i_ref, tc_ref, ts_ref, o_ref, *, l2, kb):
    for q in range(kb):
        x = jnp.dot(w2_ref[...], _fine_rows(a_ref, (0,), q).astype(BF16), preferred_element_type=F32)
        xr, xi = x[:l2], x[l2:]
        hr, hi = h_ref[0, 0, :, q, :], h_ref[0, 1, :, q, :]
        y = jnp.concatenate([xr * hr - xi * hi, xr * hi + xi * hr], axis=0).astype(BF16)
        bm = jnp.dot(w2i_ref[...], y, preferred_element_type=F32)
        br, bi = bm[:l2], bm[l2:]
        tc, ts = tc_ref[q], ts_ref[q]
        o_ref[0, 0, :, q, :] = br * tc - bi * ts
        o_ref[0, 1, :, q, :] = br * ts + bi * tc


def _ct_mid(a, hspec, order, consts, c, kb=8):
    b = a.shape[0]
    l2, h1 = consts["l2"], consts["h1"]
    kb = min(kb, h1)
    blk = pl.BlockSpec((1, 2, l2, kb, c), lambda bi, i: (bi, 0, 0, i, 0))
    return pl.pallas_call(
        functools.partial(_ct_mid_body, l2=l2, kb=kb),
        grid=(b, h1 // kb),
        in_specs=[
            blk,
            pl.BlockSpec((1, 2, l2, kb, c), lambda bi, i: (order, 0, 0, i, 0)),
            pl.BlockSpec((2 * l2, 2 * l2), lambda bi, i: (0, 0)),
            pl.BlockSpec((2 * l2, 2 * l2), lambda bi, i: (0, 0)),
            pl.BlockSpec((kb, l2, 1), lambda bi, i: (i, 0, 0)),
            pl.BlockSpec((kb, l2, 1), lambda bi, i: (i, 0, 0)),
        ],
        out_specs=blk,
        out_shape=jax.ShapeDtypeStruct((b, 2, l2, h1, c), F32),
        compiler_params=_params("parallel", "parallel"),
        name="ct_mid",
    )(a, hspec, consts["w2"], consts["w2i"], consts["tw_cos_inv"], consts["tw_sin_inv"])


def _ct_inv1_body(b_ref, w_ref, u_ref, gate_ref, skip_ref, o_ref, *, lb, c):
    for q in range(lb):
        bb = jnp.concatenate([b_ref[0, 0, q], b_ref[0, 1, q]], axis=0).astype(BF16)
        y = jnp.dot(w_ref[...], bb, preferred_element_type=F32)
        cs = slice(q * c, (q + 1) * c)
        o_ref[0, :, cs] = gate_ref[0, 0, :, cs] * (y + u_ref[0, 0, :, cs] * skip_ref[...])


def _ct_inv1(bsp, u, u_idx, gate, gate_idx, skip, consts, c, lb=8):
    b = bsp.shape[0]
    l2, h1 = consts["l2"], consts["h1"]
    lb = min(lb, l2)
    return pl.pallas_call(
        functools.partial(_ct_inv1_body, lb=lb, c=c),
        grid=(b, l2 // lb),
        in_specs=[
            pl.BlockSpec((1, 2, lb, h1, c), lambda bi, i: (bi, 0, i, 0, 0)),
            pl.BlockSpec((h1, 2 * h1), lambda bi, i: (0, 0)),
            pl.BlockSpec((1, 1, h1, lb * c), lambda bi, i: (u_idx, bi, 0, i)),
            pl.BlockSpec((1, 1, h1, lb * c), lambda bi, i: (gate_idx, bi, 0, i)),
            pl.BlockSpec((1, c), lambda bi, i: (0, 0)),
        ],
        out_specs=pl.BlockSpec((1, h1, lb * c), lambda bi, i: (bi, 0, i)),
        out_shape=jax.ShapeDtypeStruct((b, h1, l2 * c), F32),
        compiler_params=_params("parallel", "parallel"),
        name="ct_inv1",
    )(bsp, consts["w1i"], u, gate, skip.reshape(1, c))


def _hyena(u, conv_w, conv_b, f_w1, f_b1, f_w2, f_b2, f_w3, f_freq, skip):
    b, n, c3 = u.shape
    c = c3 // 3
    consts = _dft_constants(n)
    l1, l2, h1 = consts["l1"], consts["l2"], consts["h1"]
    filt, norms = _hyena_filters(n, f_w1, f_b1, f_w2, f_b2, f_w3, f_freq, c)
    fa = _ct_fwd1(filt.reshape(1, HY_ORDER, l1, l2 * c), 0, consts, c, norms=norms.reshape(HY_ORDER, 1, c))
    hspec = _ct_spectrum(fa, consts, c)
    parts = _short_conv(u, conv_w, conv_b).reshape(3, b, h1, l2 * c)

    def long_conv_gated(x_stack, x_idx, gate_idx, order):
        a = _ct_fwd1(x_stack, x_idx, consts, c)
        bsp = _ct_mid(a, hspec, order, consts, c)
        return _ct_inv1(bsp, x_stack, x_idx, parts, gate_idx, skip[order], consts, c)

    z = long_conv_gated(parts, 0, 1, 0)
    y = long_conv_gated(z[None], 0, 2, 1)
    return y.reshape(b, n, c)


def _outproj_body(*refs, n_in):
    ins = refs[:n_in]
    ws = refs[n_in:2 * n_in]
    x_ref, gate_ref, g_ref, sc_ref, sh_ref, rhi_ref, rlo_ref, rb_ref, xo_ref, h_ref, rt_ref = refs[2 * n_in:]
    y = None
    for a_ref, w_ref in zip(ins, ws):
        t = jnp.dot(a_ref[0].astype(BF16), w_ref[...], preferred_element_type=F32)
        y = t if y is None else y + t
    x = x_ref[0] + gate_ref[0] * y
    xo_ref[0] = x
    h = x * lax.rsqrt(jnp.mean(x * x, axis=-1, keepdims=True) + NORM_EPS) * g_ref[...]
    h = h * (1.0 + sc_ref[0]) + sh_ref[0]
    hi = h.astype(BF16)
    h_ref[0] = hi
    lo = (h - hi.astype(F32)).astype(BF16)
    lg = (jnp.dot(hi, rhi_ref[...], preferred_element_type=F32)
          + jnp.dot(lo, rhi_ref[...], preferred_element_type=F32)
          + jnp.dot(hi, rlo_ref[...], preferred_element_type=F32) + rb_ref[...])
    rt_ref[0] = _route(lg)


def _route(lg):
    lane = lax.broadcasted_iota(jnp.int32, lg.shape, 1)
    lane_f = lane.astype(F32)
    neg = -jnp.inf

    def top(v):
        m = jnp.max(v, axis=-1, keepdims=True)
        return m, jnp.min(jnp.where(v == m, lane_f, float(LANES)), axis=-1, keepdims=True)

    gl = jnp.where(lane < N_GROUPS, lg, neg)
    gmax, grp = top(gl)
    p_grp = 1.0 / jnp.sum(jnp.exp(gl - gmax), axis=-1, keepdims=True)
    first = N_GROUPS + grp * EXP_PER_GROUP
    el = jnp.where((lane_f >= first) & (lane_f < first + EXP_PER_GROUP), lg, neg)
    m1, i1 = top(el)
    m2, i2 = top(jnp.where(lane_f == i1, neg, el))
    e2 = jnp.exp(m2 - m1)
    den = 1.0 + e2
    vals = (i1 - N_GROUPS, i2 - N_GROUPS, p_grp * (1.0 / den), p_grp * (e2 / den))
    out = jnp.zeros(lg.shape, F32)
    for k, v in enumerate(vals):
        out = jnp.where(lane == k, v, out)
    return out


def _outproj(ins, ws, x, gate, g, scale, shift, r_hi, r_lo, r_b, tm=512):
    b, n, d = x.shape
    tm = min(tm, n)
    bm = gate.shape[0]
    mod_map = (lambda bi, i: (bi, 0, 0)) if bm > 1 else (lambda bi, i: (0, 0, 0))
    row = lambda wd: pl.BlockSpec((1, tm, wd), lambda bi, i: (bi, i, 0))
    full = lambda a: pl.BlockSpec(a.shape, lambda bi, i: (0,) * a.ndim)
    mod = pl.BlockSpec((1, 1, d), mod_map)
    return pl.pallas_call(
        functools.partial(_outproj_body, n_in=len(ins)),
        grid=(b, n // tm),
        in_specs=([row(a.shape[-1]) for a in ins] + [full(w) for w in ws]
                  + [row(d), mod, pl.BlockSpec((1, d), lambda bi, i: (0, 0)), mod, mod,
                     full(r_hi), full(r_lo), full(r_b)]),
        out_specs=[row(d), row(d), row(LANES)],
        out_shape=[jax.ShapeDtypeStruct((b, n, d), F32), jax.ShapeDtypeStruct((b, n, d), BF16),
                   jax.ShapeDtypeStruct((b, n, LANES), F32)],
        compiler_params=_params("parallel", "parallel"),
        name="outproj",
    )(*ins, *ws, x, gate, g.reshape(1, d), scale, shift, r_hi, r_lo, r_b)


def _rank_body(rt_ref, tri_ref, upper_ref, pos_ref, cnt_ref):
    rt = rt_ref[...]
    lane_i = lax.broadcasted_iota(jnp.int32, rt.shape, 1)
    lane = lane_i.astype(F32)
    oh_a = lane == rt[:, 0:1]
    oh_b = lane == rt[:, 1:2]
    one_a = jnp.where(oh_a, 1.0, 0.0)
    one_b = jnp.where(oh_b, 1.0, 0.0)
    before_a = jnp.dot(tri_ref[...], one_a.astype(BF16), preferred_element_type=F32)
    before_b = jnp.dot(tri_ref[...], one_b.astype(BF16), preferred_element_type=F32)
    tot_a = jnp.sum(one_a, axis=0, keepdims=True)
    cnt = tot_a + jnp.sum(one_b, axis=0, keepdims=True)
    padded = jnp.floor((cnt + (CHUNK_ROWS - 1)) * (1.0 / CHUNK_ROWS)) * CHUNK_ROWS
    first = jnp.dot(jnp.broadcast_to(padded, (8, LANES)).astype(BF16), upper_ref[...],
                    preferred_element_type=F32)[0:1]
    pos_a = jnp.sum(jnp.where(oh_a, before_a + first, 0.0), axis=-1, keepdims=True)
    pos_b = jnp.sum(jnp.where(oh_b, before_b + first + tot_a, 0.0), axis=-1, keepdims=True)
    pos_ref[...] = jnp.where(lane_i == 0, pos_a, jnp.where(lane_i == 1, pos_b, 0.0))
    cnt_ref[0] = jnp.broadcast_to(cnt, (8, LANES))


def _rank(route):
    t = route.shape[0]
    tm = TOKEN_TILE
    tri = jnp.asarray(np.tril(np.ones((tm, tm)), -1), dtype=BF16)
    upper = jnp.asarray(np.triu(np.ones((LANES, LANES)), 1), dtype=BF16)
    return pl.pallas_call(
        _rank_body,
        grid=(t // tm,),
        in_specs=[pl.BlockSpec((tm, LANES), lambda i: (i, 0)), pl.BlockSpec((tm, tm), lambda i: (0, 0)),
                  pl.BlockSpec((LANES, LANES), lambda i: (0, 0))],
        out_specs=[pl.BlockSpec((tm, LANES), lambda i: (i, 0)), pl.BlockSpec((1, 8, LANES), lambda i: (i, 0, 0))],
        out_shape=[jax.ShapeDtypeStruct((t, LANES), F32), jax.ShapeDtypeStruct((t // tm, 8, LANES), F32)],
        compiler_params=_params("parallel"),
        name="moe_rank",
    )(route, tri, upper)


def _chunk_tables(cnt, n_blocks):
    padded = (cnt + CHUNK_ROWS - 1) // CHUNK_ROWS * CHUNK_ROWS
    run_end = jnp.cumsum(padded, axis=1)
    run_start = run_end - padded
    seg_rows = jnp.sum(padded, axis=0)
    seg_rows = (seg_rows + EXPERT_ROWS - 1) // EXPERT_ROWS * EXPERT_ROWS
    seg_end = jnp.cumsum(seg_rows)
    dst_start = (seg_end - seg_rows)[None, :] + jnp.cumsum(padded, axis=0) - padded
    row0 = jnp.arange(BUF_CHUNKS, dtype=jnp.int32) * CHUNK_ROWS
    chunk_exp = jnp.minimum(jnp.sum(run_end[:, None, :] <= row0[None, :, None], axis=-1), N_EXPERTS - 1)
    onehot = chunk_exp[:, :, None] == jnp.arange(N_EXPERTS, dtype=jnp.int32)[None, None, :]
    dst = jnp.sum(jnp.where(onehot, (dst_start - run_start)[:, None, :], 0), axis=-1) + row0[None, :]
    n_chunks = run_end[:, -1:] // CHUNK_ROWS
    table = jnp.concatenate(
        [dst, n_chunks, jnp.zeros((cnt.shape[0], TABLE_WORDS - BUF_CHUNKS - 1), jnp.int32)], axis=1)
    block_row0 = jnp.arange(n_blocks, dtype=jnp.int32) * EXPERT_ROWS
    block_exp = jnp.minimum(jnp.sum(seg_end[None, :] <= block_row0[:, None], axis=1), N_EXPERTS - 1)
    n_used = (seg_end[-1] // EXPERT_ROWS).reshape(1)
    return table.astype(jnp.int32), block_exp.astype(jnp.int32), n_used.astype(jnp.int32)


def _chunk_copies(tab_smem, make_copy):
    n = tab_smem[BUF_CHUNKS]

    def issue(c, carry):
        make_copy(c, pl.multiple_of(tab_smem[c], CHUNK_ROWS)).start()
        return carry

    def drain(c, carry):
        make_copy(c, 0).wait()
        return carry

    lax.fori_loop(0, n, issue, 0)
    lax.fori_loop(0, n, drain, 0)


def _dispatch_body(tab_hbm, h_ref, pos_ref, xs_in, xs_out, tab_smem, buf, sem_tab, sem_rows, *, tile0):
    del xs_in
    tab_copy = pltpu.make_async_copy(tab_hbm.at[tile0 + pl.program_id(0)], tab_smem, sem_tab)
    tab_copy.start()
    tab_copy.wait()
    tm = h_ref.shape[0]
    row = lax.broadcasted_iota(jnp.int32, (BUF_ROWS, tm), 0).astype(F32)
    onehot = (row == pos_ref[0, 0:1, :]) | (row == pos_ref[0, 1:2, :])
    buf[...] = jnp.dot(jnp.where(onehot, 1.0, 0.0).astype(BF16), h_ref[...], preferred_element_type=F32)
    _chunk_copies(tab_smem, lambda c, dst: pltpu.make_async_copy(
        buf.at[pl.ds(pl.multiple_of(c * CHUNK_ROWS, CHUNK_ROWS), CHUNK_ROWS)],
        xs_out.at[pl.ds(dst, CHUNK_ROWS)], sem_rows))


def _dispatch(h, pos_t, table, xs, tile0):
    t, d = h.shape
    tm = TOKEN_TILE
    return pl.pallas_call(
        functools.partial(_dispatch_body, tile0=tile0),
        grid=(t // tm,),
        in_specs=[
            pl.BlockSpec(memory_space=pl.ANY),
            pl.BlockSpec((tm, d), lambda i: (i, 0)),
            pl.BlockSpec((1, 8, tm), lambda i: (tile0 + i, 0, 0)),
            pl.BlockSpec(memory_space=pl.ANY),
        ],
        out_specs=pl.BlockSpec(memory_space=pl.ANY),
        out_shape=jax.ShapeDtypeStruct(xs.shape, xs.dtype),
        input_output_aliases={3: 0},
        scratch_shapes=[
            pltpu.SMEM((TABLE_WORDS,), jnp.int32),
            pltpu.VMEM((BUF_ROWS, d), F32),
            pltpu.SemaphoreType.DMA,
            pltpu.SemaphoreType.DMA,
        ],
        compiler_params=_params("arbitrary"),
        name="moe_dispatch",
    )(table, h, pos_t, xs)


def _expert_body(bexp_ref, nused_ref, x_ref, wg_ref, wu_ref, wd_ref, o_ref):
    @pl.when(pl.program_id(0) < nused_ref[0])
    def _():
        xb = x_ref[...].astype(BF16)
        gt = jnp.dot(xb, wg_ref[0], preferred_element_type=F32)
        up = jnp.dot(xb, wu_ref[0], preferred_element_type=F32)
        hid = (gt * jax.nn.sigmoid(gt) * up).astype(BF16)
        o_ref[...] = jnp.dot(hid, wd_ref[0], preferred_element_type=F32)

    @pl.when(pl.program_id(0) >= nused_ref[0])
    def _():
        o_ref[...] = jnp.zeros_like(o_ref)


def _experts(xs, block_exp, n_used, w_gate, w_up, w_down):
    rows, d = xs.shape
    de = w_gate.shape[-1]
    used = lambda i, be, nu: (jnp.minimum(i, nu[0] - 1), 0)
    grid_spec = pltpu.PrefetchScalarGridSpec(
        num_scalar_prefetch=2,
        grid=(rows // EXPERT_ROWS,),
        in_specs=[
            pl.BlockSpec((EXPERT_ROWS, d), used),
            pl.BlockSpec((1, d, de), lambda i, be, nu: (be[i], 0, 0)),
            pl.BlockSpec((1, d, de), lambda i, be, nu: (be[i], 0, 0)),
            pl.BlockSpec((1, de, d), lambda i, be, nu: (be[i], 0, 0)),
        ],
        out_specs=pl.BlockSpec((EXPERT_ROWS, d), lambda i, be, nu: (i, 0)),
    )
    return pl.pallas_call(
        _expert_body,
        grid_spec=grid_spec,
        out_shape=jax.ShapeDtypeStruct(xs.shape, F32),
        compiler_params=_params("arbitrary"),
        name="experts",
    )(block_exp, n_used, xs, w_gate, w_up, w_down)


def _combine_body(tab_hbm, ys_hbm, x_ref, gate_ref, rt_ref, pos_ref, o_ref, tab_smem, ybuf, sem_tab, sem_rows,
                  *, tile0, n_tiles):
    step = pl.program_id(0) * n_tiles + pl.program_id(1)

    @pl.when(step == 0)
    def _():
        ybuf[...] = jnp.zeros_like(ybuf)

    tab_copy = pltpu.make_async_copy(tab_hbm.at[tile0 + step], tab_smem, sem_tab)
    tab_copy.start()
    tab_copy.wait()
    _chunk_copies(tab_smem, lambda c, src: pltpu.make_async_copy(
        ys_hbm.at[pl.ds(src, CHUNK_ROWS)],
        ybuf.at[pl.ds(pl.multiple_of(c * CHUNK_ROWS, CHUNK_ROWS), CHUNK_ROWS)], sem_rows))
    tm = x_ref.shape[1]
    col = lax.broadcasted_iota(jnp.int32, (tm, BUF_ROWS), 1).astype(F32)
    yb = ybuf[...].astype(BF16)
    y = None
    for k in range(TOP_K):
        pick = jnp.where(col == pos_ref[:, k:k + 1], 1.0, 0.0).astype(BF16)
        t = rt_ref[0, :, TOP_K + k:TOP_K + k + 1] * jnp.dot(pick, yb, preferred_element_type=F32)
        y = t if y is None else y + t
    o_ref[0] = x_ref[0] + gate_ref[0] * y


def _combine(ys, table, pos, route, x, gate, tile0):
    b, n, d = x.shape
    tm = min(TOKEN_TILE, n)
    n_tiles = n // tm
    bm = gate.shape[0]
    mod_map = (lambda bi, i: (bi, 0, 0)) if bm > 1 else (lambda bi, i: (0, 0, 0))
    return pl.pallas_call(
        functools.partial(_combine_body, tile0=tile0, n_tiles=n_tiles),
        grid=(b, n_tiles),
        in_specs=[
            pl.BlockSpec(memory_space=pl.ANY),
            pl.BlockSpec(memory_space=pl.ANY),
            pl.BlockSpec((1, tm, d), lambda bi, i: (bi, i, 0)),
            pl.BlockSpec((1, 1, d), mod_map),
            pl.BlockSpec((1, tm, LANES), lambda bi, i: (bi, i, 0)),
            pl.BlockSpec((tm, LANES), lambda bi, i: (tile0 + bi * n_tiles + i, 0)),
        ],
        out_specs=pl.BlockSpec((1, tm, d), lambda bi, i: (bi, i, 0)),
        out_shape=jax.ShapeDtypeStruct((b, n, d), F32),
        scratch_shapes=[
            pltpu.SMEM((TABLE_WORDS,), jnp.int32),
            pltpu.VMEM((BUF_ROWS, d), F32),
            pltpu.SemaphoreType.DMA,
            pltpu.SemaphoreType.DMA,
        ],
        compiler_params=_params("arbitrary", "arbitrary"),
        name="moe_combine",
    )(table, ys, x, gate, route, pos)


def _final_norm_body(x_ref, g_ref, o_ref):
    x = x_ref[...]
    o_ref[...] = x * lax.rsqrt(jnp.mean(x * x, axis=-1, keepdims=True) + NORM_EPS) * g_ref[...]


def _final_norm(x, g, tm=1024):
    t, d = x.shape
    return pl.pallas_call(
        _final_norm_body,
        grid=(t // tm,),
        in_specs=[pl.BlockSpec((tm, d), lambda i: (i, 0)), pl.BlockSpec((1, d), lambda i: (0, 0))],
        out_specs=pl.BlockSpec((tm, d), lambda i: (i, 0)),
        out_shape=jax.ShapeDtypeStruct((t, d), F32),
        compiler_params=_params("parallel"),
        name="final_norm",
    )(x, g.reshape(1, d))


def _router_weights(wg, bg, we, be):
    d = wg.shape[0]
    w = jnp.zeros((d, LANES), F32).at[:, :N_GROUPS].set(wg).at[:, N_GROUPS:N_GROUPS + N_EXPERTS].set(we)
    bias = jnp.zeros((1, LANES), F32).at[0, :N_GROUPS].set(bg).at[0, N_GROUPS:N_GROUPS + N_EXPERTS].set(be)
    hi = w.astype(BF16)
    lo = (w - hi.astype(F32)).astype(BF16)
    return hi, lo, bias


def kernel(x, c, ctx, c_ctx, ada_w, ada_b, norm1_g, norm2_g, ev_w_in, ev_w_out, hy_conv_w, hy_conv_b, hy_f_w1, hy_f_b1, hy_f_w2, hy_f_b2, hy_f_w3, hy_f_freq, hy_skip, swa_sink, od_w_qkv, od_w_out, od_q_norm_g, od_k_norm_g, rt_group_w, rt_group_b, rt_exp_w, rt_exp_b, moe_w_gate, moe_w_up, moe_w_down, final_norm_g):
    b, n, d = x.shape
    lc = ctx.shape[1]
    depth = ada_w.shape[0]
    rope = _rope_tables(n)
    xc = ctx
    sc = jax.nn.silu(c)
    scc = jax.nn.silu(c_ctx)
    q_scale = HEAD_DIM ** -0.5
    for layer in range(depth):
        with_ctx = layer < depth - 1
        mod = (sc @ ada_w[layer] + ada_b[layer]).reshape(b, N_MOD, 1, d)
        modc = (scc @ ada_w[layer] + ada_b[layer]).reshape(1, N_MOD, 1, d)
        m = [mod[:, k] for k in range(N_MOD)]
        mc = [modc[:, k] for k in range(N_MOD)]
        r_hi, r_lo, r_b = _router_weights(rt_group_w[layer], rt_group_b[layer], rt_exp_w[layer], rt_exp_b[layer])
        wgt, wup, wdn = (moe_w_gate[layer].astype(BF16), moe_w_up[layer].astype(BF16),
                         moe_w_down[layer].astype(BF16))
        if layer % 2 == 0:
            e = layer // 2
            c_hy = hy_conv_w.shape[-1] // 3
            d_hy = 3 * c_hy
            hq = swa_sink.shape[-1]
            d_q = hq * HEAD_DIM
            hkv = hq // 4
            d_kv = hkv * HEAD_DIM
            w_in = ev_w_in[e].astype(BF16)
            w_out = ev_w_out[e].astype(BF16)
            hy_args = (hy_conv_w[e], hy_conv_b[e], hy_f_w1[e], hy_f_b1[e], hy_f_w2[e], hy_f_b2[e],
                       hy_f_w3[e], hy_f_freq[e], hy_skip[e])
            u, q, k, v = _proj(x, norm1_g[layer], m[1], m[0], w_in, [
                (0, d_hy, "f32", None, False, 1.0),
                (d_hy, d_q, "qk", None, True, q_scale),
                (d_hy + d_q, d_kv, "qk", None, True, 1.0),
                (d_hy + d_q + d_kv, d_kv, "bf16", None, False, 1.0)], rope_tabs=rope)
            if with_ctx:
                uc, qc, kc, vc = _proj(xc, norm1_g[layer], mc[1], mc[0], w_in, [
                    (0, d_hy, "f32", None, False, 1.0),
                    (d_hy, d_q, "qk", None, False, q_scale),
                    (d_hy + d_q, d_kv, "bf16", None, False, 1.0),
                    (d_hy + d_q + d_kv, d_kv, "bf16", None, False, 1.0)])
            else:
                kc, vc = _proj(xc, norm1_g[layer], mc[1], mc[0], w_in, [
                    (d_hy + d_q, d_kv, "bf16", None, False, 1.0),
                    (d_hy + d_q + d_kv, d_kv, "bf16", None, False, 1.0)])
            y_hy = _hyena(u, *hy_args)
            y_att = _windowed_attention(q, k, v, kc, vc, swa_sink[e], hkv)
            mix_in, mix_w = [y_hy, y_att], [w_out[:c_hy], w_out[c_hy:]]
            if with_ctx:
                yc_hy = _hyena(uc, *hy_args)
                yc_att = _full_attention(qc, kc, vc, hkv, sink=swa_sink[e])
                mixc_in = [yc_hy, yc_att]
        else:
            o = layer // 2
            hkv = od_w_qkv.shape[-1] // HEAD_DIM // 6
            hq = 4 * hkv
            d_q = hq * HEAD_DIM
            d_kv = hkv * HEAD_DIM
            w_qkv = od_w_qkv[o].astype(BF16)
            w_out = od_w_out[o].astype(BF16)
            norm_g = jnp.zeros((8, LANES), F32).at[0].set(jnp.tile(od_q_norm_g[o], 2)).at[1].set(
                jnp.tile(od_k_norm_g[o], 2))
            q, k, v = _proj(x, norm1_g[layer], m[1], m[0], w_qkv, [
                (0, d_q, "qk", 0, True, q_scale),
                (d_q, d_kv, "qk", 1, True, 1.0),
                (d_q + d_kv, d_kv, "bf16", None, False, 1.0)], rope_tabs=rope, norm_g=norm_g)
            if with_ctx:
                qc, kc, vc = _proj(xc, norm1_g[layer], mc[1], mc[0], w_qkv, [
                    (0, d_q, "qk", 0, False, q_scale),
                    (d_q, d_kv, "qk", 1, False, 1.0),
                    (d_q + d_kv, d_kv, "bf16", None, False, 1.0)], norm_g=norm_g)
            else:
                kc, vc = _proj(xc, norm1_g[layer], mc[1], mc[0], w_qkv, [
                    (d_q, d_kv, "qk", 1, False, 1.0),
                    (d_q + d_kv, d_kv, "bf16", None, False, 1.0)], norm_g=norm_g)
            y_att = _full_attention(q, jnp.concatenate([kc, k], axis=1), jnp.concatenate([vc, v], axis=1), hkv)
            mix_in, mix_w = [y_att], [w_out]
            if with_ctx:
                mixc_in = [_full_attention(qc, kc, vc, hkv)]
        x, h2, rt = _outproj(mix_in, mix_w, x, m[2], norm2_g[layer], m[4], m[3], r_hi, r_lo, r_b)
        route_flat = rt.reshape(b * n, LANES)
        if with_ctx:
            xc, h2c, rtc = _outproj(mixc_in, mix_w, xc, mc[2], norm2_g[layer], mc[4], mc[3], r_hi, r_lo, r_b)
            route_flat = jnp.concatenate([route_flat, rtc.reshape(b * lc, LANES)], axis=0)
        n_tok = route_flat.shape[0]
        n_tiles = n_tok // TOKEN_TILE
        lat_tiles = b * n // TOKEN_TILE
        max_rows = n_tok * TOP_K + n_tiles * N_EXPERTS * (CHUNK_ROWS - 1)
        n_blocks = -(-max_rows // EXPERT_ROWS) + N_EXPERTS
        pos, cnt = _rank(route_flat)
        table, block_exp, n_used = _chunk_tables(cnt[:, 0, :N_EXPERTS].astype(jnp.int32), n_blocks)
        pos_t = jnp.swapaxes(pos[:, :8].reshape(n_tiles, TOKEN_TILE, 8), 1, 2)
        xs = jnp.zeros((n_blocks * EXPERT_ROWS, d), F32)
        xs = _dispatch(h2.reshape(b * n, d), pos_t, table, xs, 0)
        if with_ctx:
            xs = _dispatch(h2c.reshape(b * lc, d), pos_t, table, xs, lat_tiles)
        ys = _experts(xs, block_exp, n_used, wgt, wup, wdn)
        x = _combine(ys, table, pos, rt, x, m[5], 0)
        if with_ctx:
            ctx_tiled = lambda a: a.reshape((b * lc // TOKEN_TILE, TOKEN_TILE) + a.shape[2:])
            xc = _combine(ys, table, pos, ctx_tiled(rtc), ctx_tiled(xc), mc[5], lat_tiles).reshape(b, lc, d)
    return _final_norm(x.reshape(b * n, d), final_norm_g).reshape(b, n, d)
```

```python
import functools
import math

import numpy as np
import jax
import jax.numpy as jnp
from jax import lax
from jax.experimental import pallas as pl
from jax.experimental.pallas import tpu as pltpu

F32 = jnp.float32
BF16 = jnp.bfloat16

HEAD_DIM = 64
GRID_W = 64
ROPE_BASE = 10000.0
Q_BLOCK = 128
NORM_EPS = 1e-6
N_MOD = 6
HY_ORDER = 2
HY_BANDS = 16
HY_DIRS = 2
HY_DECAY_TARGET = 1e-2
HY_FAST_DECAY = 0.3
HY_SLOW_DECAY = 1.5
HY_FILTER_EPS = 1e-6
SWA_WINDOW = 128
N_GROUPS = 4
EXP_PER_GROUP = 8
N_EXPERTS = N_GROUPS * EXP_PER_GROUP
TOP_K = 2
EXPERT_ROWS = 512
TOKEN_TILE = 512
CHUNK_ROWS = 8
BUF_ROWS = 1280
BUF_CHUNKS = BUF_ROWS // CHUNK_ROWS
TABLE_WORDS = 256

LANES = 128
VMEM_LIMIT_BYTES = 56 * 1024 * 1024


def _params(*sem):
    return pltpu.CompilerParams(dimension_semantics=sem, vmem_limit_bytes=VMEM_LIMIT_BYTES)


def _rope_tables(n):
    d_axis = HEAD_DIM // 2
    t = jnp.arange(n)
    inv = ROPE_BASE ** (-jnp.arange(0, d_axis, 2, dtype=F32) / d_axis)
    ang_r = (t // GRID_W).astype(F32)[:, None] * inv[None, :]
    ang_c = (t % GRID_W).astype(F32)[:, None] * inv[None, :]
    cos = jnp.concatenate([jnp.cos(ang_r)] * 2 + [jnp.cos(ang_c)] * 2, axis=-1)
    sin = jnp.concatenate([-jnp.sin(ang_r), jnp.sin(ang_r), -jnp.sin(ang_c), jnp.sin(ang_c)], axis=-1)
    return jnp.tile(cos, (1, 2)), jnp.tile(sin, (1, 2))


def _head_mean_matrix():
    i = np.arange(LANES)
    return jnp.asarray((i[:, None] // HEAD_DIM == i[None, :] // HEAD_DIM) / HEAD_DIM, dtype=BF16)


def _proj_body(x_ref, g_ref, sc_ref, sh_ref, w_ref, cos_ref, sin_ref, ng_ref, bd_ref, *out_refs, segs):
    x = x_ref[0]
    h = x * lax.rsqrt(jnp.mean(x * x, axis=-1, keepdims=True) + NORM_EPS) * g_ref[...]
    hb = (h * (1.0 + sc_ref[0]) + sh_ref[0]).astype(BF16)
    for o_ref, (c0, width, kind, norm_row, rope, out_scale) in zip(out_refs, segs):
        seg = jnp.dot(hb, w_ref[:, c0:c0 + width], preferred_element_type=F32)
        if kind == "f32":
            o_ref[0] = seg
            continue
        if kind == "bf16":
            o_ref[0] = seg.astype(BF16)
            continue
        for j in range(width // LANES):
            ch = seg[:, j * LANES:(j + 1) * LANES]
            if norm_row is not None:
                sq = ch * ch
                hi = sq.astype(BF16)
                lo = (sq - hi.astype(F32)).astype(BF16)
                ms = (jnp.dot(hi, bd_ref[...], preferred_element_type=F32)
                      + jnp.dot(lo, bd_ref[...], preferred_element_type=F32))
                ch = ch * lax.rsqrt(ms + NORM_EPS) * ng_ref[norm_row:norm_row + 1, :]
            if rope:
                lane = lax.broadcasted_iota(jnp.int32, ch.shape, 1)
                partner = jnp.where(lane % 32 < 16, pltpu.roll(ch, LANES - 16, 1), pltpu.roll(ch, 16, 1))
                ch = ch * cos_ref[...] + partner * sin_ref[...]
            if out_scale != 1.0:
                ch = ch * out_scale
            o_ref[0, :, j * LANES:(j + 1) * LANES] = ch.astype(BF16)


def _proj(x, g, scale, shift, w, segs, rope_tabs=None, norm_g=None, tm=512):
    b, n, d = x.shape
    tm = min(tm, n)
    bm = scale.shape[0]
    mod_map = (lambda bi, i: (bi, 0, 0)) if bm > 1 else (lambda bi, i: (0, 0, 0))
    if rope_tabs is None:
        cos = sin = jnp.zeros((8, LANES), F32)
        tab_spec = pl.BlockSpec((8, LANES), lambda bi, i: (0, 0))
    else:
        cos, sin = rope_tabs
        tab_spec = pl.BlockSpec((tm, LANES), lambda bi, i: (i, 0))
    if norm_g is None:
        norm_g = jnp.ones((8, LANES), F32)
    out_shape = [jax.ShapeDtypeStruct((b, n, s[1]), F32 if s[2] == "f32" else BF16) for s in segs]
    out_specs = [pl.BlockSpec((1, tm, s[1]), lambda bi, i: (bi, i, 0)) for s in segs]
    return pl.pallas_call(
        functools.partial(_proj_body, segs=tuple(segs)),
        grid=(b, n // tm),
        in_specs=[
            pl.BlockSpec((1, tm, d), lambda bi, i: (bi, i, 0)),
            pl.BlockSpec((1, d), lambda bi, i: (0, 0)),
            pl.BlockSpec((1, 1, d), mod_map),
            pl.BlockSpec((1, 1, d), mod_map),
            pl.BlockSpec(w.shape, lambda bi, i: (0, 0)),
            tab_spec,
            tab_spec,
            pl.BlockSpec(norm_g.shape, lambda bi, i: (0, 0)),
            pl.BlockSpec((LANES, LANES), lambda bi, i: (0, 0)),
        ],
        out_specs=out_specs,
        out_shape=out_shape,
        compiler_params=_params("parallel", "parallel"),
        name="proj",
    )(x, g.reshape(1, d), scale, shift, w, cos, sin, norm_g, _head_mean_matrix())


def _stack_heads(q, j, g):
    return jnp.concatenate(
        [q[:, (j * g + gg) * HEAD_DIM:(j * g + gg + 1) * HEAD_DIM] for gg in range(g)], axis=0)


def _values_with_ones(v, hkv):
    b, nk, _ = v.shape
    ones = jnp.zeros((b, nk, hkv, HEAD_DIM), v.dtype).at[..., 0].set(1)
    return jnp.concatenate([v.reshape(b, nk, hkv, HEAD_DIM), ones], axis=-1).reshape(b, nk, 2 * hkv * HEAD_DIM)


def _sink_column(sink_ref, j, g, qb):
    return jnp.concatenate([jnp.full((qb, 1), sink_ref[j * g + gg], F32) for gg in range(g)], axis=0)


def _swa_body(sink_ref, q_ref, kt_ref, v_ref, kct_ref, vc_ref, o_ref, *, n, hkv, g, qb, win):
    i = pl.program_id(1)
    start = pl.multiple_of(i * qb, qb)
    q = q_ref[0]
    rows = lax.broadcasted_iota(jnp.int32, (g * qb, 3 * qb), 0) % qb
    cols = lax.broadcasted_iota(jnp.int32, (g * qb, 3 * qb), 1)
    key_pos = cols + (i - 1) * qb
    valid = (jnp.abs(rows + qb - cols) <= win) & (key_pos >= 0) & (key_pos < n)
    for j in range(hkv):
        hs = slice(j * HEAD_DIM, (j + 1) * HEAD_DIM)
        q4 = _stack_heads(q, j, g)
        s_lat = jnp.dot(q4, kt_ref[0, hs, pl.ds(start, 3 * qb)], preferred_element_type=F32)
        s_lat = jnp.where(valid, s_lat, -jnp.inf)
        s_ctx = jnp.dot(q4, kct_ref[0, hs, :], preferred_element_type=F32)
        s_sink = _sink_column(sink_ref, j, g, qb)
        m = jnp.maximum(jnp.maximum(jnp.max(s_lat, axis=-1, keepdims=True),
                                    jnp.max(s_ctx, axis=-1, keepdims=True)), s_sink)
        e_lat = jnp.exp((s_lat - m).astype(BF16))
        e_ctx = jnp.exp((s_ctx - m).astype(BF16))
        vs = slice(j * 2 * HEAD_DIM, (j + 1) * 2 * HEAD_DIM)
        o = (jnp.dot(e_ctx, vc_ref[0, :, vs], preferred_element_type=F32)
             + jnp.dot(e_lat, v_ref[0, pl.ds(start, 3 * qb), vs], preferred_element_type=F32))
        o = o[:, :HEAD_DIM] / (o[:, HEAD_DIM:HEAD_DIM + 1] + jnp.exp(s_sink - m))
        for gg in range(g):
            c0 = (j * g + gg) * HEAD_DIM
            o_ref[0, :, c0:c0 + HEAD_DIM] = o[gg * qb:(gg + 1) * qb].astype(BF16)


def _windowed_attention(q, k, v, kc, vc, sink, hkv):
    b, n, dq = q.shape
    g = dq // HEAD_DIM // hkv
    qb = Q_BLOCK
    lc = kc.shape[1]
    dkv = hkv * HEAD_DIM
    kt = jnp.swapaxes(jnp.pad(k, ((0, 0), (qb, qb), (0, 0))), 1, 2)
    vp = _values_with_ones(jnp.pad(v, ((0, 0), (qb, qb), (0, 0))), hkv)
    vc = _values_with_ones(vc, hkv)
    kct = jnp.swapaxes(kc, 1, 2)
    return pl.pallas_call(
        functools.partial(_swa_body, n=n, hkv=hkv, g=g, qb=qb, win=SWA_WINDOW),
        grid=(b, n // qb),
        in_specs=[
            pl.BlockSpec(memory_space=pltpu.SMEM),
            pl.BlockSpec((1, qb, dq), lambda bi, i: (bi, i, 0)),
            pl.BlockSpec((1, dkv, n + 2 * qb), lambda bi, i: (bi, 0, 0)),
            pl.BlockSpec((1, n + 2 * qb, 2 * dkv), lambda bi, i: (bi, 0, 0)),
            pl.BlockSpec((1, dkv, lc), lambda bi, i: (bi, 0, 0)),
            pl.BlockSpec((1, lc, 2 * dkv), lambda bi, i: (bi, 0, 0)),
        ],
        out_specs=pl.BlockSpec((1, qb, dq), lambda bi, i: (bi, i, 0)),
        out_shape=jax.ShapeDtypeStruct((b, n, dq), BF16),
        compiler_params=_params("parallel", "parallel"),
        name="swa",
    )(sink.astype(F32), q, kt, vp, kct, vc)


def _full_attn_body(sink_ref, q_ref, kt_ref, v_ref, o_ref, *, hkv, g, qb, has_sink):
    q = q_ref[0]
    for j in range(hkv):
        hs = slice(j * HEAD_DIM, (j + 1) * HEAD_DIM)
        q4 = _stack_heads(q, j, g)
        s = jnp.dot(q4, kt_ref[0, hs, :], preferred_element_type=F32)
        m = jnp.max(s, axis=-1, keepdims=True)
        if has_sink:
            s_sink = _sink_column(sink_ref, j, g, qb)
            m = jnp.maximum(m, s_sink)
        e = jnp.exp((s - m).astype(BF16))
        o = jnp.dot(e, v_ref[0, :, j * 2 * HEAD_DIM:(j + 1) * 2 * HEAD_DIM], preferred_element_type=F32)
        den = o[:, HEAD_DIM:HEAD_DIM + 1]
        if has_sink:
            den = den + jnp.exp(s_sink - m)
        o = o[:, :HEAD_DIM] / den
        for gg in range(g):
            c0 = (j * g + gg) * HEAD_DIM
            o_ref[0, :, c0:c0 + HEAD_DIM] = o[gg * qb:(gg + 1) * qb].astype(BF16)


def _full_attention(q, k, v, hkv, sink=None):
    b, n, dq = q.shape
    g = dq // HEAD_DIM // hkv
    qb = Q_BLOCK
    nk = k.shape[1]
    dkv = hkv * HEAD_DIM
    kt = jnp.swapaxes(k, 1, 2)
    has_sink = sink is not None
    sink = jnp.zeros((dq // HEAD_DIM,), F32) if sink is None else sink.astype(F32)
    return pl.pallas_call(
        functools.partial(_full_attn_body, hkv=hkv, g=g, qb=qb, has_sink=has_sink),
        grid=(b, n // qb),
        in_specs=[
            pl.BlockSpec(memory_space=pltpu.SMEM),
            pl.BlockSpec((1, qb, dq), lambda bi, i: (bi, i, 0)),
            pl.BlockSpec((1, dkv, nk), lambda bi, i: (bi, 0, 0)),
            pl.BlockSpec((1, nk, 2 * dkv), lambda bi, i: (bi, 0, 0)),
        ],
        out_specs=pl.BlockSpec((1, qb, dq), lambda bi, i: (bi, i, 0)),
        out_shape=jax.ShapeDtypeStruct((b, n, dq), BF16),
        compiler_params=_params("parallel", "parallel"),
        name="full_attn",
    )(sink, q, kt, _values_with_ones(v, hkv))


def _short_conv_body(u_ref, w_ref, b_ref, o_ref, *, h1, l2, c):
    slab = lambda f: u_ref[0, :, f, :]
    row = lax.broadcasted_iota(jnp.int32, (h1, c), 0)
    for f in range(l2):
        prev = slab(f - 1) if f > 0 else jnp.where(row == 0, 0.0, pltpu.roll(slab(l2 - 1), 1, 0))
        nxt = slab(f + 1) if f < l2 - 1 else jnp.where(row == h1 - 1, 0.0, pltpu.roll(slab(0), h1 - 1, 0))
        o_ref[0, 0, :, f * c:(f + 1) * c] = (prev * w_ref[0:1, :] + slab(f) * w_ref[1:2, :]
                                             + nxt * w_ref[2:3, :] + b_ref[...])


def _short_conv(u, w, bias, l2):
    b, n, c3 = u.shape
    c = c3 // 3
    h1 = n // l2
    return pl.pallas_call(
        functools.partial(_short_conv_body, h1=h1, l2=l2, c=c),
        grid=(b, 3),
        in_specs=[
            pl.BlockSpec((1, h1, l2, c), lambda bi, j: (bi, 0, 0, j)),
            pl.BlockSpec((3, c), lambda bi, j: (0, j)),
            pl.BlockSpec((1, c), lambda bi, j: (0, j)),
        ],
        out_specs=pl.BlockSpec((1, 1, h1, l2 * c), lambda bi, j: (j, bi, 0, 0)),
        out_shape=jax.ShapeDtypeStruct((3, b, h1, l2 * c), F32),
        compiler_params=_params("parallel", "parallel"),
        name="short_conv",
    )(u.reshape(b, h1, l2, c3), w, bias.reshape(1, c3))


def _filter_body(band_ref, w1_ref, b1_ref, w2_ref, b2_ref, w3_ref, fr_ref, dl_ref, o_ref, s_ref, *, n, rt, c):
    i = pl.program_id(0)
    hp = lax.Precision.HIGHEST
    m = i * rt + lax.broadcasted_iota(jnp.int32, (rt, 1), 0)
    pos = jnp.where(m < n, m, 2 * n - m).astype(F32)
    t_norm = pos / max(n - 1, 1)
    ang = (2.0 * math.pi / n) * pos * band_ref[...]
    lane = lax.broadcasted_iota(jnp.int32, (rt, LANES), 1)
    z = jnp.where(lane == 0, t_norm,
                  jnp.where(lane <= HY_BANDS, jnp.cos(ang),
                            jnp.where(lane <= 2 * HY_BANDS, -jnp.sin(ang), 0.0)))
    fr = fr_ref[...]
    hdn = jnp.sin(fr * (jnp.dot(z, w1_ref[...], precision=hp, preferred_element_type=F32) + b1_ref[...]))
    hdn = jnp.sin(fr * (jnp.dot(hdn, w2_ref[...], precision=hp, preferred_element_type=F32) + b2_ref[...]))
    h = jnp.dot(hdn, w3_ref[...], precision=hp, preferred_element_type=F32)
    h = h * jnp.exp(-t_norm * dl_ref[...])
    half = HY_ORDER * c
    sel = jnp.where(m < n, h[:, :half], jnp.where(m > n, -h[:, half:], 0.0))
    for o in range(HY_ORDER):
        o_ref[o] = sel[:, o * c:(o + 1) * c]

    @pl.when(i == 0)
    def _():
        s_ref[...] = jnp.zeros_like(s_ref)

    s_ref[...] += jnp.sum(jnp.abs(sel), axis=0, keepdims=True)


def _hyena_filters(n, w1, b1, w2, b2, w3, freq, c):
    rt = min(1024, n)
    hid = w1.shape[1]
    bands = jnp.linspace(1e-4, HY_BANDS - 1, HY_BANDS, dtype=F32)
    band_row = jnp.zeros((1, LANES), F32).at[0, 1:1 + 2 * HY_BANDS].set(jnp.tile(bands, 2))
    w1p = jnp.zeros((LANES, hid), F32).at[:w1.shape[0]].set(w1)
    max_decay = math.log(HY_DECAY_TARGET) / HY_FAST_DECAY
    min_decay = math.log(HY_DECAY_TARGET) / HY_SLOW_DECAY
    deltas = jnp.abs(jnp.linspace(min_decay, max_decay, c, dtype=F32))
    dl = jnp.tile(deltas, HY_DIRS * HY_ORDER).reshape(1, -1)
    full = lambda a: pl.BlockSpec(a.shape, lambda i: (0,) * a.ndim)
    args = (band_row, w1p, b1.reshape(1, hid), w2, b2.reshape(1, hid), w3, freq.reshape(1, hid), dl)
    return pl.pallas_call(
        functools.partial(_filter_body, n=n, rt=rt, c=c),
        grid=(2 * n // rt,),
        in_specs=[full(a) for a in args],
        out_specs=[pl.BlockSpec((HY_ORDER, rt, c), lambda i: (0, i, 0)),
                   pl.BlockSpec((1, HY_ORDER * c), lambda i: (0, 0))],
        out_shape=[jax.ShapeDtypeStruct((HY_ORDER, 2 * n, c), F32),
                   jax.ShapeDtypeStruct((1, HY_ORDER * c), F32)],
        compiler_params=_params("arbitrary"),
        name="hyena_filter",
    )(*args)


def _dft_split(n):
    l2 = 32 if n >= 2048 else 16
    return 2 * n // l2, l2


def _dft_constants(n):
    l1, l2 = _dft_split(n)
    h1 = l1 // 2
    nn = 2 * n
    k1 = np.arange(h1)[:, None]
    a = 2 * np.pi * (k1 + 0.5) * np.arange(l1)[None, :] / l1
    w1 = np.concatenate([np.cos(a), -np.sin(a)], axis=0)
    t = 2 * np.pi * (k1 + 0.5) * np.arange(l2)[None, :] / nn
    tw_cos, tw_sin = np.cos(t), np.sin(t)
    p = 2 * np.pi * np.arange(l2)[:, None] * np.arange(l2)[None, :] / l2
    w2 = np.block([[np.cos(p), np.sin(p)], [-np.sin(p), np.cos(p)]])
    w2i = np.block([[np.cos(p), -np.sin(p)], [np.sin(p), np.cos(p)]])
    ai = 2 * np.pi * np.arange(h1)[:, None] * (np.arange(h1)[None, :] + 0.5) / l1
    w1i = (2.0 / nn) * np.concatenate([np.cos(ai), -np.sin(ai)], axis=1)
    c = lambda m, dt: jnp.asarray(m, dtype=dt)
    return dict(
        l1=l1, l2=l2, h1=h1,
        w1=c(w1, BF16), w2=c(w2, BF16), w2i=c(w2i, BF16), w1i=c(w1i, BF16),
        tw_cos_fwd=c(tw_cos.T[:, :, None], F32), tw_sin_fwd=c(tw_sin.T[:, :, None], F32),
        tw_cos_inv=c(tw_cos[:, :, None], F32), tw_sin_inv=c(tw_sin[:, :, None], F32),
    )


def _ct_fwd1_body(x_ref, w_ref, tc_ref, ts_ref, nrm_ref, o_ref, *, lb, c, h1, normalise):
    for q in range(lb):
        x = x_ref[0, 0, :, q * c:(q + 1) * c]
        if normalise:
            x = x / (nrm_ref[0] + HY_FILTER_EPS)
        a = jnp.dot(w_ref[...], x.astype(BF16), preferred_element_type=F32)
        ar, ai = a[:h1], a[h1:]
        tc, ts = tc_ref[q], ts_ref[q]
        o_ref[0, 0, q] = ar * tc + ai * ts
        o_ref[0, 1, q] = ai * tc - ar * ts


def _ct_fwd1(xs, idx, consts, c, norms=None, lb=8):
    _, b, k1n, _ = xs.shape
    l2, h1 = consts["l2"], consts["h1"]
    lb = min(lb, l2)
    w = consts["w1"][:, :k1n]
    normalise = norms is not None
    if norms is None:
        norms = jnp.zeros((b, 1, c), F32)
    return pl.pallas_call(
        functools.partial(_ct_fwd1_body, lb=lb, c=c, h1=h1, normalise=normalise),
        grid=(b, l2 // lb),
        in_specs=[
            pl.BlockSpec((1, 1, k1n, lb * c), lambda bi, i: (idx, bi, 0, i)),
            pl.BlockSpec(w.shape, lambda bi, i: (0, 0)),
            pl.BlockSpec((lb, h1, 1), lambda bi, i: (i, 0, 0)),
            pl.BlockSpec((lb, h1, 1), lambda bi, i: (i, 0, 0)),
            pl.BlockSpec((1, 1, c), lambda bi, i: (bi, 0, 0)),
        ],
        out_specs=pl.BlockSpec((1, 2, lb, h1, c), lambda bi, i: (bi, 0, i, 0, 0)),
        out_shape=jax.ShapeDtypeStruct((b, 2, l2, h1, c), F32),
        compiler_params=_params("parallel", "parallel"),
        name="ct_fwd1",
    )(xs, w, consts["tw_cos_fwd"], consts["tw_sin_fwd"], norms)


def _fine_rows(ref, lead, q):
    return jnp.concatenate([ref[lead + (0, slice(None), q, slice(None))],
                            ref[lead + (1, slice(None), q, slice(None))]], axis=0)


def _ct_spec_body(a_ref, w2_ref, o_ref, *, l2, kb):
    for q in range(kb):
        x = jnp.dot(w2_ref[...], _fine_rows(a_ref, (0,), q).astype(BF16), preferred_element_type=F32)
        o_ref[0, 0, :, q, :] = x[:l2]
        o_ref[0, 1, :, q, :] = x[l2:]


def _ct_spectrum(a, consts, c, kb=8):
    b = a.shape[0]
    l2, h1 = consts["l2"], consts["h1"]
    kb = min(kb, h1)
    blk = pl.BlockSpec((1, 2, l2, kb, c), lambda bi, i: (bi, 0, 0, i, 0))
    return pl.pallas_call(
        functools.partial(_ct_spec_body, l2=l2, kb=kb),
        grid=(b, h1 // kb),
        in_specs=[blk, pl.BlockSpec((2 * l2, 2 * l2), lambda bi, i: (0, 0))],
        out_specs=blk,
        out_shape=jax.ShapeDtypeStruct((b, 2, l2, h1, c), F32),
        compiler_params=_params("parallel", "parallel"),
        name="ct_spectrum",
    )(a, consts["w2"])


def _ct_mid_body(a_ref, h_ref, w2_ref, w2i_ref, tc_ref, ts_ref, o_ref, *, l2, kb):
    for q in range(kb):
        x = jnp.dot(w2_ref[...], _fine_rows(a_ref, (0,), q).astype(BF16), preferred_element_type=F32)
        xr, xi = x[:l2], x[l2:]
        hr, hi = h_ref[0, 0, :, q, :], h_ref[0, 1, :, q, :]
        y = jnp.concatenate([xr * hr - xi * hi, xr * hi + xi * hr], axis=0).astype(BF16)
        bm = jnp.dot(w2i_ref[...], y, preferred_element_type=F32)
        br, bi = bm[:l2], bm[l2:]
        tc, ts = tc_ref[q], ts_ref[q]
        o_ref[0, 0, :, q, :] = br * tc - bi * ts
        o_ref[0, 1, :, q, :] = br * ts + bi * tc


def _ct_mid(a, hspec, order, consts, c, kb=8):
    b = a.shape[0]
    l2, h1 = consts["l2"], consts["h1"]
    kb = min(kb, h1)
    blk = pl.BlockSpec((1, 2, l2, kb, c), lambda bi, i: (bi, 0, 0, i, 0))
    return pl.pallas_call(
        functools.partial(_ct_mid_body, l2=l2, kb=kb),
        grid=(b, h1 // kb),
        in_specs=[
            blk,
            pl.BlockSpec((1, 2, l2, kb, c), lambda bi, i: (order, 0, 0, i, 0)),
            pl.BlockSpec((2 * l2, 2 * l2), lambda bi, i: (0, 0)),
            pl.BlockSpec((2 * l2, 2 * l2), lambda bi, i: (0, 0)),
            pl.BlockSpec((kb, l2, 1), lambda bi, i: (i, 0, 0)),
            pl.BlockSpec((kb, l2, 1), lambda bi, i: (i, 0, 0)),
        ],
        out_specs=blk,
        out_shape=jax.ShapeDtypeStruct((b, 2, l2, h1, c), F32),
        compiler_params=_params("parallel", "parallel"),
        name="ct_mid",
    )(a, hspec, consts["w2"], consts["w2i"], consts["tw_cos_inv"], consts["tw_sin_inv"])


def _ct_inv1_body(b_ref, w_ref, u_ref, gate_ref, skip_ref, o_ref, *, lb, c, by_position):
    for q in range(lb):
        bb = jnp.concatenate([b_ref[0, 0, q], b_ref[0, 1, q]], axis=0).astype(BF16)
        y = jnp.dot(w_ref[...], bb, preferred_element_type=F32)
        cs = slice(q * c, (q + 1) * c)
        out = gate_ref[0, 0, :, cs] * (y + u_ref[0, 0, :, cs] * skip_ref[...])
        if by_position:
            o_ref[0, :, q, :] = out
        else:
            o_ref[0, :, cs] = out


def _ct_inv1(bsp, u, u_idx, gate, gate_idx, skip, consts, c, by_position, lb=8):
    b = bsp.shape[0]
    l2, h1 = consts["l2"], consts["h1"]
    lb = min(lb, l2)
    if by_position:
        out_spec = pl.BlockSpec((1, h1, lb, c), lambda bi, i: (bi, 0, i, 0))
        out_shape = jax.ShapeDtypeStruct((b, h1, l2, c), F32)
    else:
        out_spec = pl.BlockSpec((1, h1, lb * c), lambda bi, i: (bi, 0, i))
        out_shape = jax.ShapeDtypeStruct((b, h1, l2 * c), F32)
    return pl.pallas_call(
        functools.partial(_ct_inv1_body, lb=lb, c=c, by_position=by_position),
        grid=(b, l2 // lb),
        in_specs=[
            pl.BlockSpec((1, 2, lb, h1, c), lambda bi, i: (bi, 0, i, 0, 0)),
            pl.BlockSpec((h1, 2 * h1), lambda bi, i: (0, 0)),
            pl.BlockSpec((1, 1, h1, lb * c), lambda bi, i: (u_idx, bi, 0, i)),
            pl.BlockSpec((1, 1, h1, lb * c), lambda bi, i: (gate_idx, bi, 0, i)),
            pl.BlockSpec((1, c), lambda bi, i: (0, 0)),
        ],
        out_specs=out_spec,
        out_shape=out_shape,
        compiler_params=_params("parallel", "parallel"),
        name="ct_inv1",
    )(bsp, consts["w1i"], u, gate, skip.reshape(1, c))


def _hyena(u, conv_w, conv_b, f_w1, f_b1, f_w2, f_b2, f_w3, f_freq, skip):
    b, n, c3 = u.shape
    c = c3 // 3
    consts = _dft_constants(n)
    l1, l2, h1 = consts["l1"], consts["l2"], consts["h1"]
    filt, norms = _hyena_filters(n, f_w1, f_b1, f_w2, f_b2, f_w3, f_freq, c)
    fa = _ct_fwd1(filt.reshape(1, HY_ORDER, l1, l2 * c), 0, consts, c, norms=norms.reshape(HY_ORDER, 1, c))
    hspec = _ct_spectrum(fa, consts, c)
    parts = _short_conv(u, conv_w, conv_b, l2)

    def long_conv_gated(x_stack, x_idx, gate_idx, order, by_position):
        a = _ct_fwd1(x_stack, x_idx, consts, c)
        bsp = _ct_mid(a, hspec, order, consts, c)
        return _ct_inv1(bsp, x_stack, x_idx, parts, gate_idx, skip[order], consts, c, by_position)

    z = long_conv_gated(parts, 0, 1, 0, False)
    return long_conv_gated(z[None], 0, 2, 1, True).reshape(b, n, c)


def _outproj_body(*refs, n_in):
    ins = refs[:n_in]
    ws = refs[n_in:2 * n_in]
    x_ref, gate_ref, g_ref, sc_ref, sh_ref, rhi_ref, rlo_ref, rb_ref, xo_ref, h_ref, rt_ref = refs[2 * n_in:]
    y = None
    for a_ref, w_ref in zip(ins, ws):
        t = jnp.dot(a_ref[0].astype(BF16), w_ref[...], preferred_element_type=F32)
        y = t if y is None else y + t
    x = x_ref[0] + gate_ref[0] * y
    xo_ref[0] = x
    h = x * lax.rsqrt(jnp.mean(x * x, axis=-1, keepdims=True) + NORM_EPS) * g_ref[...]
    h = h * (1.0 + sc_ref[0]) + sh_ref[0]
    hi = h.astype(BF16)
    h_ref[0] = hi
    lo = (h - hi.astype(F32)).astype(BF16)
    lg = (jnp.dot(hi, rhi_ref[...], preferred_element_type=F32)
          + jnp.dot(lo, rhi_ref[...], preferred_element_type=F32)
          + jnp.dot(hi, rlo_ref[...], preferred_element_type=F32) + rb_ref[...])
    rt_ref[0] = _route(lg)


def _route(lg):
    lane = lax.broadcasted_iota(jnp.int32, lg.shape, 1)
    lane_f = lane.astype(F32)
    neg = -jnp.inf

    def top(v):
        m = jnp.max(v, axis=-1, keepdims=True)
        return m, jnp.min(jnp.where(v == m, lane_f, float(LANES)), axis=-1, keepdims=True)

    gl = jnp.where(lane < N_GROUPS, lg, neg)
    gmax, grp = top(gl)
    p_grp = 1.0 / jnp.sum(jnp.exp(gl - gmax), axis=-1, keepdims=True)
    first = N_GROUPS + grp * EXP_PER_GROUP
    el = jnp.where((lane_f >= first) & (lane_f < first + EXP_PER_GROUP), lg, neg)
    m1, i1 = top(el)
    m2, i2 = top(jnp.where(lane_f == i1, neg, el))
    e2 = jnp.exp(m2 - m1)
    den = 1.0 + e2
    vals = (i1 - N_GROUPS, i2 - N_GROUPS, p_grp * (1.0 / den), p_grp * (e2 / den))
    out = jnp.zeros(lg.shape, F32)
    for k, v in enumerate(vals):
        out = jnp.where(lane == k, v, out)
    return out


def _outproj(ins, ws, x, gate, g, scale, shift, r_hi, r_lo, r_b, tm=512):
    b, n, d = x.shape
    tm = min(tm, n)
    bm = gate.shape[0]
    mod_map = (lambda bi, i: (bi, 0, 0)) if bm > 1 else (lambda bi, i: (0, 0, 0))
    row = lambda wd: pl.BlockSpec((1, tm, wd), lambda bi, i: (bi, i, 0))
    full = lambda a: pl.BlockSpec(a.shape, lambda bi, i: (0,) * a.ndim)
    mod = pl.BlockSpec((1, 1, d), mod_map)
    return pl.pallas_call(
        functools.partial(_outproj_body, n_in=len(ins)),
        grid=(b, n // tm),
        in_specs=([row(a.shape[-1]) for a in ins] + [full(w) for w in ws]
                  + [row(d), mod, pl.BlockSpec((1, d), lambda bi, i: (0, 0)), mod, mod,
                     full(r_hi), full(r_lo), full(r_b)]),
        out_specs=[row(d), row(d), row(LANES)],
        out_shape=[jax.ShapeDtypeStruct((b, n, d), F32), jax.ShapeDtypeStruct((b, n, d), BF16),
                   jax.ShapeDtypeStruct((b, n, LANES), F32)],
        compiler_params=_params("parallel", "parallel"),
        name="outproj",
    )(*ins, *ws, x, gate, g.reshape(1, d), scale, shift, r_hi, r_lo, r_b)


def _rank_body(rt_ref, tri_ref, upper_ref, pos_ref, cnt_ref):
    rt = rt_ref[...]
    lane_i = lax.broadcasted_iota(jnp.int32, rt.shape, 1)
    lane = lane_i.astype(F32)
    oh_a = lane == rt[:, 0:1]
    oh_b = lane == rt[:, 1:2]
    one_a = jnp.where(oh_a, 1.0, 0.0)
    one_b = jnp.where(oh_b, 1.0, 0.0)
    before_a = jnp.dot(tri_ref[...], one_a.astype(BF16), preferred_element_type=F32)
    before_b = jnp.dot(tri_ref[...], one_b.astype(BF16), preferred_element_type=F32)
    tot_a = jnp.sum(one_a, axis=0, keepdims=True)
    cnt = tot_a + jnp.sum(one_b, axis=0, keepdims=True)
    padded = jnp.floor((cnt + (CHUNK_ROWS - 1)) * (1.0 / CHUNK_ROWS)) * CHUNK_ROWS
    first = jnp.dot(jnp.broadcast_to(padded, (8, LANES)).astype(BF16), upper_ref[...],
                    preferred_element_type=F32)[0:1]
    pos_a = jnp.sum(jnp.where(oh_a, before_a + first, 0.0), axis=-1, keepdims=True)
    pos_b = jnp.sum(jnp.where(oh_b, before_b + first + tot_a, 0.0), axis=-1, keepdims=True)
    is_gate = (lane_i >= TOP_K) & (lane_i < 2 * TOP_K)
    pos_ref[...] = jnp.where(lane_i == 0, pos_a, jnp.where(lane_i == 1, pos_b, jnp.where(is_gate, rt, 0.0)))
    cnt_ref[0] = jnp.broadcast_to(cnt, (8, LANES))


def _rank(route):
    t = route.shape[0]
    tm = TOKEN_TILE
    tri = jnp.asarray(np.tril(np.ones((tm, tm)), -1), dtype=BF16)
    upper = jnp.asarray(np.triu(np.ones((LANES, LANES)), 1), dtype=BF16)
    return pl.pallas_call(
        _rank_body,
        grid=(t // tm,),
        in_specs=[pl.BlockSpec((tm, LANES), lambda i: (i, 0)), pl.BlockSpec((tm, tm), lambda i: (0, 0)),
                  pl.BlockSpec((LANES, LANES), lambda i: (0, 0))],
        out_specs=[pl.BlockSpec((tm, LANES), lambda i: (i, 0)), pl.BlockSpec((1, 8, LANES), lambda i: (i, 0, 0))],
        out_shape=[jax.ShapeDtypeStruct((t, LANES), F32), jax.ShapeDtypeStruct((t // tm, 8, LANES), F32)],
        compiler_params=_params("parallel"),
        name="moe_rank",
    )(route, tri, upper)


def _chunk_tables(cnt, n_blocks):
    padded = (cnt + CHUNK_ROWS - 1) // CHUNK_ROWS * CHUNK_ROWS
    run_end = jnp.cumsum(padded, axis=1)
    run_start = run_end - padded
    seg_rows = jnp.sum(padded, axis=0)
    seg_rows = (seg_rows + EXPERT_ROWS - 1) // EXPERT_ROWS * EXPERT_ROWS
    seg_end = jnp.cumsum(seg_rows)
    dst_start = (seg_end - seg_rows)[None, :] + jnp.cumsum(padded, axis=0) - padded
    row0 = jnp.arange(BUF_CHUNKS, dtype=jnp.int32) * CHUNK_ROWS
    chunk_exp = jnp.minimum(jnp.sum(run_end[:, None, :] <= row0[None, :, None], axis=-1), N_EXPERTS - 1)
    onehot = chunk_exp[:, :, None] == jnp.arange(N_EXPERTS, dtype=jnp.int32)[None, None, :]
    dst = jnp.sum(jnp.where(onehot, (dst_start - run_start)[:, None, :], 0), axis=-1) + row0[None, :]
    n_chunks = run_end[:, -1:] // CHUNK_ROWS
    table = jnp.concatenate(
        [dst, n_chunks, jnp.zeros((cnt.shape[0], TABLE_WORDS - BUF_CHUNKS - 1), jnp.int32)], axis=1)
    block_row0 = jnp.arange(n_blocks, dtype=jnp.int32) * EXPERT_ROWS
    block_exp = jnp.minimum(jnp.sum(seg_end[None, :] <= block_row0[:, None], axis=1), N_EXPERTS - 1)
    n_used = (seg_end[-1] // EXPERT_ROWS).reshape(1)
    return table.astype(jnp.int32), block_exp.astype(jnp.int32), n_used.astype(jnp.int32)


def _chunk_copies(tab_smem, make_copies):
    n = tab_smem[BUF_CHUNKS]

    def issue(c, carry):
        for cp in make_copies(pl.multiple_of(c * CHUNK_ROWS, CHUNK_ROWS), pl.multiple_of(tab_smem[c], CHUNK_ROWS)):
            cp.start()
        return carry

    def drain(c, carry):
        for cp in make_copies(0, 0):
            cp.wait()
        return carry

    lax.fori_loop(0, n, issue, 0)
    lax.fori_loop(0, n, drain, 0)


def _dispatch_body(tab_hbm, h_ref, pos_ref, xs_in, gs_in, xs_out, gs_out, tab_smem, buf, gbuf,
                   sem_tab, sem_rows, sem_gates, *, tile0):
    del xs_in, gs_in
    tab_copy = pltpu.make_async_copy(tab_hbm.at[tile0 + pl.program_id(0)], tab_smem, sem_tab)
    tab_copy.start()
    tab_copy.wait()
    tm = h_ref.shape[0]
    row = lax.broadcasted_iota(jnp.int32, (BUF_ROWS, tm), 0).astype(F32)
    oh_a = row == pos_ref[0, 0:1, :]
    oh_b = row == pos_ref[0, 1:2, :]
    buf[...] = jnp.dot(jnp.where(oh_a | oh_b, 1.0, 0.0).astype(BF16), h_ref[...], preferred_element_type=F32)
    gate = jnp.sum(jnp.where(oh_a, pos_ref[0, 2:3, :], 0.0) + jnp.where(oh_b, pos_ref[0, 3:4, :], 0.0),
                   axis=-1, keepdims=True)
    gbuf[...] = jnp.broadcast_to(gate, gbuf.shape)
    _chunk_copies(tab_smem, lambda src, dst: (
        pltpu.make_async_copy(buf.at[pl.ds(src, CHUNK_ROWS)], xs_out.at[pl.ds(dst, CHUNK_ROWS)], sem_rows),
        pltpu.make_async_copy(gbuf.at[pl.ds(src, CHUNK_ROWS)], gs_out.at[pl.ds(dst, CHUNK_ROWS)], sem_gates)))


def _dispatch(h, pos_t, table, xs, gs, tile0):
    t, d = h.shape
    tm = TOKEN_TILE
    return pl.pallas_call(
        functools.partial(_dispatch_body, tile0=tile0),
        grid=(t // tm,),
        in_specs=[
            pl.BlockSpec(memory_space=pl.ANY),
            pl.BlockSpec((tm, d), lambda i: (i, 0)),
            pl.BlockSpec((1, 8, tm), lambda i: (tile0 + i, 0, 0)),
            pl.BlockSpec(memory_space=pl.ANY),
            pl.BlockSpec(memory_space=pl.ANY),
        ],
        out_specs=[pl.BlockSpec(memory_space=pl.ANY), pl.BlockSpec(memory_space=pl.ANY)],
        out_shape=[jax.ShapeDtypeStruct(xs.shape, xs.dtype), jax.ShapeDtypeStruct(gs.shape, gs.dtype)],
        input_output_aliases={3: 0, 4: 1},
        scratch_shapes=[
            pltpu.SMEM((TABLE_WORDS,), jnp.int32),
            pltpu.VMEM((BUF_ROWS, d), F32),
            pltpu.VMEM((BUF_ROWS, LANES), F32),
            pltpu.SemaphoreType.DMA,
            pltpu.SemaphoreType.DMA,
            pltpu.SemaphoreType.DMA,
        ],
        compiler_params=_params("arbitrary"),
        name="moe_dispatch",
    )(table, h, pos_t, xs, gs)


def _expert_body(bexp_ref, nused_ref, x_ref, g_ref, wg_ref, wu_ref, wd_ref, o_ref):
    @pl.when(pl.program_id(0) < nused_ref[0])
    def _():
        xb = x_ref[...].astype(BF16)
        gt = jnp.dot(xb, wg_ref[0], preferred_element_type=F32)
        up = jnp.dot(xb, wu_ref[0], preferred_element_type=F32)
        hid = (gt * jax.nn.sigmoid(gt) * up).astype(BF16)
        o_ref[...] = jnp.dot(hid, wd_ref[0], preferred_element_type=F32) * g_ref[:, 0:1]

    @pl.when(pl.program_id(0) >= nused_ref[0])
    def _():
        o_ref[...] = jnp.zeros_like(o_ref)


def _experts(xs, gs, block_exp, n_used, w_gate, w_up, w_down):
    rows, d = xs.shape
    de = w_gate.shape[-1]
    used = lambda i, be, nu: (jnp.minimum(i, nu[0] - 1), 0)
    grid_spec = pltpu.PrefetchScalarGridSpec(
        num_scalar_prefetch=2,
        grid=(rows // EXPERT_ROWS,),
        in_specs=[
            pl.BlockSpec((EXPERT_ROWS, d), used),
            pl.BlockSpec((EXPERT_ROWS, LANES), used),
            pl.BlockSpec((1, d, de), lambda i, be, nu: (be[i], 0, 0)),
            pl.BlockSpec((1, d, de), lambda i, be, nu: (be[i], 0, 0)),
            pl.BlockSpec((1, de, d), lambda i, be, nu: (be[i], 0, 0)),
        ],
        out_specs=pl.BlockSpec((EXPERT_ROWS, d), lambda i, be, nu: (i, 0)),
    )
    return pl.pallas_call(
        _expert_body,
        grid_spec=grid_spec,
        out_shape=jax.ShapeDtypeStruct(xs.shape, F32),
        compiler_params=_params("arbitrary"),
        name="experts",
    )(block_exp, n_used, xs, gs, w_gate, w_up, w_down)


def _combine_body(tab_hbm, ys_hbm, x_ref, gate_ref, pos_ref, fg_ref, o_ref, tab_smem, ybuf, sem_tab, sem_rows,
                  *, tile0, n_tiles, final_norm):
    step = pl.program_id(0) * n_tiles + pl.program_id(1)

    @pl.when(step == 0)
    def _():
        ybuf[...] = jnp.zeros_like(ybuf)

    tab_copy = pltpu.make_async_copy(tab_hbm.at[tile0 + step], tab_smem, sem_tab)
    tab_copy.start()
    tab_copy.wait()
    _chunk_copies(tab_smem, lambda dst, src: (
        pltpu.make_async_copy(ys_hbm.at[pl.ds(src, CHUNK_ROWS)], ybuf.at[pl.ds(dst, CHUNK_ROWS)], sem_rows),))
    tm = x_ref.shape[1]
    col = lax.broadcasted_iota(jnp.int32, (tm, BUF_ROWS), 1).astype(F32)
    pick = jnp.where((col == pos_ref[:, 0:1]) | (col == pos_ref[:, 1:2]), 1.0, 0.0).astype(BF16)
    out = x_ref[0] + gate_ref[0] * jnp.dot(pick, ybuf[...].astype(BF16), preferred_element_type=F32)
    if final_norm:
        out = out * lax.rsqrt(jnp.mean(out * out, axis=-1, keepdims=True) + NORM_EPS) * fg_ref[...]
    o_ref[0] = out


def _combine(ys, table, pos, x, gate, tile0, final_g=None):
    b, n, d = x.shape
    tm = min(TOKEN_TILE, n)
    n_tiles = n // tm
    bm = gate.shape[0]
    mod_map = (lambda bi, i: (bi, 0, 0)) if bm > 1 else (lambda bi, i: (0, 0, 0))
    final_norm = final_g is not None
    fg = final_g.reshape(1, d) if final_norm else jnp.ones((1, d), F32)
    return pl.pallas_call(
        functools.partial(_combine_body, tile0=tile0, n_tiles=n_tiles, final_norm=final_norm),
        grid=(b, n_tiles),
        in_specs=[
            pl.BlockSpec(memory_space=pl.ANY),
            pl.BlockSpec(memory_space=pl.ANY),
            pl.BlockSpec((1, tm, d), lambda bi, i: (bi, i, 0)),
            pl.BlockSpec((1, 1, d), mod_map),
            pl.BlockSpec((tm, LANES), lambda bi, i: (tile0 + bi * n_tiles + i, 0)),
            pl.BlockSpec((1, d), lambda bi, i: (0, 0)),
        ],
        out_specs=pl.BlockSpec((1, tm, d), lambda bi, i: (bi, i, 0)),
        out_shape=jax.ShapeDtypeStruct((b, n, d), F32),
        scratch_shapes=[
            pltpu.SMEM((TABLE_WORDS,), jnp.int32),
            pltpu.VMEM((BUF_ROWS, d), F32),
            pltpu.SemaphoreType.DMA,
            pltpu.SemaphoreType.DMA,
        ],
        compiler_params=_params("arbitrary", "arbitrary"),
        name="moe_combine",
    )(table, ys, x, gate, pos, fg)


def _router_weights(wg, bg, we, be):
    d = wg.shape[0]
    w = jnp.zeros((d, LANES), F32).at[:, :N_GROUPS].set(wg).at[:, N_GROUPS:N_GROUPS + N_EXPERTS].set(we)
    bias = jnp.zeros((1, LANES), F32).at[0, :N_GROUPS].set(bg).at[0, N_GROUPS:N_GROUPS + N_EXPERTS].set(be)
    hi = w.astype(BF16)
    lo = (w - hi.astype(F32)).astype(BF16)
    return hi, lo, bias


def kernel(x, c, ctx, c_ctx, ada_w, ada_b, norm1_g, norm2_g, ev_w_in, ev_w_out, hy_conv_w, hy_conv_b, hy_f_w1, hy_f_b1, hy_f_w2, hy_f_b2, hy_f_w3, hy_f_freq, hy_skip, swa_sink, od_w_qkv, od_w_out, od_q_norm_g, od_k_norm_g, rt_group_w, rt_group_b, rt_exp_w, rt_exp_b, moe_w_gate, moe_w_up, moe_w_down, final_norm_g):
    b, n, d = x.shape
    lc = ctx.shape[1]
    depth = ada_w.shape[0]
    rope = _rope_tables(n)
    xc = ctx
    sc = jax.nn.silu(c)
    scc = jax.nn.silu(c_ctx)
    q_scale = HEAD_DIM ** -0.5
    for layer in range(depth):
        with_ctx = layer < depth - 1
        mod = (sc @ ada_w[layer] + ada_b[layer]).reshape(b, N_MOD, 1, d)
        modc = (scc @ ada_w[layer] + ada_b[layer]).reshape(1, N_MOD, 1, d)
        m = [mod[:, k] for k in range(N_MOD)]
        mc = [modc[:, k] for k in range(N_MOD)]
        r_hi, r_lo, r_b = _router_weights(rt_group_w[layer], rt_group_b[layer], rt_exp_w[layer], rt_exp_b[layer])
        wgt, wup, wdn = (moe_w_gate[layer].astype(BF16), moe_w_up[layer].astype(BF16),
                         moe_w_down[layer].astype(BF16))
        if layer % 2 == 0:
            e = layer // 2
            c_hy = hy_conv_w.shape[-1] // 3
            d_hy = 3 * c_hy
            hq = swa_sink.shape[-1]
            d_q = hq * HEAD_DIM
            hkv = hq // 4
            d_kv = hkv * HEAD_DIM
            w_in = ev_w_in[e].astype(BF16)
            w_out = ev_w_out[e].astype(BF16)
            hy_args = (hy_conv_w[e], hy_conv_b[e], hy_f_w1[e], hy_f_b1[e], hy_f_w2[e], hy_f_b2[e],
                       hy_f_w3[e], hy_f_freq[e], hy_skip[e])
            u, q, k, v = _proj(x, norm1_g[layer], m[1], m[0], w_in, [
                (0, d_hy, "f32", None, False, 1.0),
                (d_hy, d_q, "qk", None, True, q_scale),
                (d_hy + d_q, d_kv, "qk", None, True, 1.0),
                (d_hy + d_q + d_kv, d_kv, "bf16", None, False, 1.0)], rope_tabs=rope)
            if with_ctx:
                uc, qc, kc, vc = _proj(xc, norm1_g[layer], mc[1], mc[0], w_in, [
                    (0, d_hy, "f32", None, False, 1.0),
                    (d_hy, d_q, "qk", None, False, q_scale),
                    (d_hy + d_q, d_kv, "bf16", None, False, 1.0),
                    (d_hy + d_q + d_kv, d_kv, "bf16", None, False, 1.0)])
            else:
                kc, vc = _proj(xc, norm1_g[layer], mc[1], mc[0], w_in, [
                    (d_hy + d_q, d_kv, "bf16", None, False, 1.0),
                    (d_hy + d_q + d_kv, d_kv, "bf16", None, False, 1.0)])
            y_hy = _hyena(u, *hy_args)
            y_att = _windowed_attention(q, k, v, kc, vc, swa_sink[e], hkv)
            mix_in, mix_w = [y_hy, y_att], [w_out[:c_hy], w_out[c_hy:]]
            if with_ctx:
                yc_hy = _hyena(uc, *hy_args)
                yc_att = _full_attention(qc, kc, vc, hkv, sink=swa_sink[e])
                mixc_in = [yc_hy, yc_att]
        else:
            o = layer // 2
            hkv = od_w_qkv.shape[-1] // HEAD_DIM // 6
            hq = 4 * hkv
            d_q = hq * HEAD_DIM
            d_kv = hkv * HEAD_DIM
            w_qkv = od_w_qkv[o].astype(BF16)
            w_out = od_w_out[o].astype(BF16)
            norm_g = jnp.zeros((8, LANES), F32).at[0].set(jnp.tile(od_q_norm_g[o], 2)).at[1].set(
                jnp.tile(od_k_norm_g[o], 2))
            q, k, v = _proj(x, norm1_g[layer], m[1], m[0], w_qkv, [
                (0, d_q, "qk", 0, True, q_scale),
                (d_q, d_kv, "qk", 1, True, 1.0),
                (d_q + d_kv, d_kv, "bf16", None, False, 1.0)], rope_tabs=rope, norm_g=norm_g)
            if with_ctx:
                qc, kc, vc = _proj(xc, norm1_g[layer], mc[1], mc[0], w_qkv, [
                    (0, d_q, "qk", 0, False, q_scale),
                    (d_q, d_kv, "qk", 1, False, 1.0),
                    (d_q + d_kv, d_kv, "bf16", None, False, 1.0)], norm_g=norm_g)
            else:
                kc, vc = _proj(xc, norm1_g[layer], mc[1], mc[0], w_qkv, [
                    (d_q, d_kv, "qk", 1, False, 1.0),
                    (d_q + d_kv, d_kv, "bf16", None, False, 1.0)], norm_g=norm_g)
            y_att = _full_attention(q, jnp.concatenate([kc, k], axis=1), jnp.concatenate([vc, v], axis=1), hkv)
            mix_in, mix_w = [y_att], [w_out]
            if with_ctx:
                mixc_in = [_full_attention(qc, kc, vc, hkv)]
        x, h2, rt = _outproj(mix_in, mix_w, x, m[2], norm2_g[layer], m[4], m[3], r_hi, r_lo, r_b)
        route_flat = rt.reshape(b * n, LANES)
        if with_ctx:
            xc, h2c, rtc = _outproj(mixc_in, mix_w, xc, mc[2], norm2_g[layer], mc[4], mc[3], r_hi, r_lo, r_b)
            route_flat = jnp.concatenate([route_flat, rtc.reshape(b * lc, LANES)], axis=0)
        n_tok = route_flat.shape[0]
        n_tiles = n_tok // TOKEN_TILE
        lat_tiles = b * n // TOKEN_TILE
        max_rows = n_tok * TOP_K + n_tiles * N_EXPERTS * (CHUNK_ROWS - 1)
        n_blocks = -(-max_rows // EXPERT_ROWS) + N_EXPERTS
        pos, cnt = _rank(route_flat)
        table, block_exp, n_used = _chunk_tables(cnt[:, 0, :N_EXPERTS].astype(jnp.int32), n_blocks)
        pos_t = jnp.swapaxes(pos[:, :8].reshape(n_tiles, TOKEN_TILE, 8), 1, 2)
        xs = jnp.zeros((n_blocks * EXPERT_ROWS, d), F32)
        gs = jnp.zeros((n_blocks * EXPERT_ROWS, LANES), F32)
        xs, gs = _dispatch(h2.reshape(b * n, d), pos_t, table, xs, gs, 0)
        if with_ctx:
            xs, gs = _dispatch(h2c.reshape(b * lc, d), pos_t, table, xs, gs, lat_tiles)
        ys = _experts(xs, gs, block_exp, n_used, wgt, wup, wdn)
        x = _combine(ys, table, pos, x, m[5], 0, final_g=None if with_ctx else final_norm_g)
        if with_ctx:
            xc = _combine(ys, table, pos, xc.reshape(b * lc // TOKEN_TILE, TOKEN_TILE, d), mc[5],
                          lat_tiles).reshape(b, lc, d)
    return x
```

```python
import functools
import math

import numpy as np
import jax
import jax.numpy as jnp
from jax import lax
from jax.experimental import pallas as pl
from jax.experimental.pallas import tpu as pltpu

F32 = jnp.float32
BF16 = jnp.bfloat16

HEAD_DIM = 64
GRID_W = 64
ROPE_BASE = 10000.0
Q_BLOCK = 128
NORM_EPS = 1e-6
N_MOD = 6
HY_ORDER = 2
HY_BANDS = 16
HY_DIRS = 2
HY_DECAY_TARGET = 1e-2
HY_FAST_DECAY = 0.3
HY_SLOW_DECAY = 1.5
HY_FILTER_EPS = 1e-6
SWA_WINDOW = 128
N_GROUPS = 4
EXP_PER_GROUP = 8
N_EXPERTS = N_GROUPS * EXP_PER_GROUP
TOP_K = 2
EXPERT_ROWS = 512
FULL_ATTN_Q_ROWS = 256
FULL_ATTN_UNIT_ROWS = 512
TOKEN_TILE = 512
CHUNK_ROWS = 8
BUF_ROWS = 1280
BUF_CHUNKS = BUF_ROWS // CHUNK_ROWS
TABLE_WORDS = 256

LANES = 128
VMEM_LIMIT_BYTES = 56 * 1024 * 1024


def _params(*sem):
    return pltpu.CompilerParams(dimension_semantics=sem, vmem_limit_bytes=VMEM_LIMIT_BYTES)


def _rope_tables(n):
    d_axis = HEAD_DIM // 2
    t = jnp.arange(n)
    inv = ROPE_BASE ** (-jnp.arange(0, d_axis, 2, dtype=F32) / d_axis)
    ang_r = (t // GRID_W).astype(F32)[:, None] * inv[None, :]
    ang_c = (t % GRID_W).astype(F32)[:, None] * inv[None, :]
    cos = jnp.concatenate([jnp.cos(ang_r)] * 2 + [jnp.cos(ang_c)] * 2, axis=-1)
    sin = jnp.concatenate([-jnp.sin(ang_r), jnp.sin(ang_r), -jnp.sin(ang_c), jnp.sin(ang_c)], axis=-1)
    return jnp.tile(cos, (1, 2)), jnp.tile(sin, (1, 2))


def _head_mean_matrix():
    i = np.arange(LANES)
    return jnp.asarray((i[:, None] // HEAD_DIM == i[None, :] // HEAD_DIM) / HEAD_DIM, dtype=BF16)


def _proj_body(x_ref, g_ref, sc_ref, sh_ref, w_ref, cos_ref, sin_ref, ng_ref, bd_ref, *out_refs, segs):
    x = x_ref[0]
    h = x * lax.rsqrt(jnp.mean(x * x, axis=-1, keepdims=True) + NORM_EPS) * g_ref[...]
    hb = (h * (1.0 + sc_ref[0]) + sh_ref[0]).astype(BF16)
    for o_ref, (c0, width, kind, norm_row, rope, out_scale) in zip(out_refs, segs):
        seg = jnp.dot(hb, w_ref[:, c0:c0 + width], preferred_element_type=F32)
        if kind == "f32":
            o_ref[0] = seg
            continue
        if kind == "bf16":
            o_ref[0] = seg.astype(BF16)
            continue
        for j in range(width // LANES):
            ch = seg[:, j * LANES:(j + 1) * LANES]
            if norm_row is not None:
                sq = ch * ch
                hi = sq.astype(BF16)
                lo = (sq - hi.astype(F32)).astype(BF16)
                ms = (jnp.dot(hi, bd_ref[...], preferred_element_type=F32)
                      + jnp.dot(lo, bd_ref[...], preferred_element_type=F32))
                ch = ch * lax.rsqrt(ms + NORM_EPS) * ng_ref[norm_row:norm_row + 1, :]
            if rope:
                lane = lax.broadcasted_iota(jnp.int32, ch.shape, 1)
                partner = jnp.where(lane % 32 < 16, pltpu.roll(ch, LANES - 16, 1), pltpu.roll(ch, 16, 1))
                ch = ch * cos_ref[...] + partner * sin_ref[...]
            if out_scale != 1.0:
                ch = ch * out_scale
            o_ref[0, :, j * LANES:(j + 1) * LANES] = ch.astype(BF16)


def _proj(x, g, scale, shift, w, segs, rope_tabs=None, norm_g=None, tm=512):
    b, n, d = x.shape
    tm = min(tm, n)
    bm = scale.shape[0]
    mod_map = (lambda bi, i: (bi, 0, 0)) if bm > 1 else (lambda bi, i: (0, 0, 0))
    if rope_tabs is None:
        cos = sin = jnp.zeros((8, LANES), F32)
        tab_spec = pl.BlockSpec((8, LANES), lambda bi, i: (0, 0))
    else:
        cos, sin = rope_tabs
        tab_spec = pl.BlockSpec((tm, LANES), lambda bi, i: (i, 0))
    if norm_g is None:
        norm_g = jnp.ones((8, LANES), F32)
    out_shape = [jax.ShapeDtypeStruct((b, n, s[1]), F32 if s[2] == "f32" else BF16) for s in segs]
    out_specs = [pl.BlockSpec((1, tm, s[1]), lambda bi, i: (bi, i, 0)) for s in segs]
    return pl.pallas_call(
        functools.partial(_proj_body, segs=tuple(segs)),
        grid=(b, n // tm),
        in_specs=[
            pl.BlockSpec((1, tm, d), lambda bi, i: (bi, i, 0)),
            pl.BlockSpec((1, d), lambda bi, i: (0, 0)),
            pl.BlockSpec((1, 1, d), mod_map),
            pl.BlockSpec((1, 1, d), mod_map),
            pl.BlockSpec(w.shape, lambda bi, i: (0, 0)),
            tab_spec,
            tab_spec,
            pl.BlockSpec(norm_g.shape, lambda bi, i: (0, 0)),
            pl.BlockSpec((LANES, LANES), lambda bi, i: (0, 0)),
        ],
        out_specs=out_specs,
        out_shape=out_shape,
        compiler_params=_params("parallel", "parallel"),
        name="proj",
    )(x, g.reshape(1, d), scale, shift, w, cos, sin, norm_g, _head_mean_matrix())


def _stack_heads(q, j, g):
    return jnp.concatenate(
        [q[:, (j * g + gg) * HEAD_DIM:(j * g + gg + 1) * HEAD_DIM] for gg in range(g)], axis=0)


def _values_with_ones(v, hkv):
    b, nk, _ = v.shape
    ones = jnp.zeros((b, nk, hkv, HEAD_DIM), v.dtype).at[..., 0].set(1)
    return jnp.concatenate([v.reshape(b, nk, hkv, HEAD_DIM), ones], axis=-1).reshape(b, nk, 2 * hkv * HEAD_DIM)


def _sink_column(sink_ref, j, g, qb):
    return jnp.concatenate([jnp.full((qb, 1), sink_ref[j * g + gg], F32) for gg in range(g)], axis=0)


def _swa_body(sink_ref, q_ref, kt_ref, v_ref, kct_ref, vc_ref, o_ref, *, n, hkv, g, qb, win):
    i = pl.program_id(1)
    start = pl.multiple_of(i * qb, qb)
    q = q_ref[0]
    rows = lax.broadcasted_iota(jnp.int32, (g * qb, 3 * qb), 0) % qb
    cols = lax.broadcasted_iota(jnp.int32, (g * qb, 3 * qb), 1)
    key_pos = cols + (i - 1) * qb
    valid = (jnp.abs(rows + qb - cols) <= win) & (key_pos >= 0) & (key_pos < n)
    for j in range(hkv):
        hs = slice(j * HEAD_DIM, (j + 1) * HEAD_DIM)
        q4 = _stack_heads(q, j, g)
        s_lat = jnp.dot(q4, kt_ref[0, hs, pl.ds(start, 3 * qb)], preferred_element_type=F32)
        s_lat = jnp.where(valid, s_lat, -jnp.inf)
        s_ctx = jnp.dot(q4, kct_ref[0, hs, :], preferred_element_type=F32)
        s_sink = _sink_column(sink_ref, j, g, qb)
        m = jnp.maximum(jnp.maximum(jnp.max(s_lat, axis=-1, keepdims=True),
                                    jnp.max(s_ctx, axis=-1, keepdims=True)), s_sink)
        e_lat = jnp.exp((s_lat - m).astype(BF16))
        e_ctx = jnp.exp((s_ctx - m).astype(BF16))
        vs = slice(j * 2 * HEAD_DIM, (j + 1) * 2 * HEAD_DIM)
        o = (jnp.dot(e_ctx, vc_ref[0, :, vs], preferred_element_type=F32)
             + jnp.dot(e_lat, v_ref[0, pl.ds(start, 3 * qb), vs], preferred_element_type=F32))
        o = o[:, :HEAD_DIM] / (o[:, HEAD_DIM:HEAD_DIM + 1] + jnp.exp(s_sink - m))
        for gg in range(g):
            c0 = (j * g + gg) * HEAD_DIM
            o_ref[0, :, c0:c0 + HEAD_DIM] = o[gg * qb:(gg + 1) * qb].astype(BF16)


def _windowed_attention(q, k, v, kc, vc, sink, hkv):
    b, n, dq = q.shape
    g = dq // HEAD_DIM // hkv
    qb = Q_BLOCK
    lc = kc.shape[1]
    dkv = hkv * HEAD_DIM
    kt = jnp.swapaxes(jnp.pad(k, ((0, 0), (qb, qb), (0, 0))), 1, 2)
    vp = _values_with_ones(jnp.pad(v, ((0, 0), (qb, qb), (0, 0))), hkv)
    vc = _values_with_ones(vc, hkv)
    kct = jnp.swapaxes(kc, 1, 2)
    return pl.pallas_call(
        functools.partial(_swa_body, n=n, hkv=hkv, g=g, qb=qb, win=SWA_WINDOW),
        grid=(b, n // qb),
        in_specs=[
            pl.BlockSpec(memory_space=pltpu.SMEM),
            pl.BlockSpec((1, qb, dq), lambda bi, i: (bi, i, 0)),
            pl.BlockSpec((1, dkv, n + 2 * qb), lambda bi, i: (bi, 0, 0)),
            pl.BlockSpec((1, n + 2 * qb, 2 * dkv), lambda bi, i: (bi, 0, 0)),
            pl.BlockSpec((1, dkv, lc), lambda bi, i: (bi, 0, 0)),
            pl.BlockSpec((1, lc, 2 * dkv), lambda bi, i: (bi, 0, 0)),
        ],
        out_specs=pl.BlockSpec((1, qb, dq), lambda bi, i: (bi, i, 0)),
        out_shape=jax.ShapeDtypeStruct((b, n, dq), BF16),
        compiler_params=_params("parallel", "parallel"),
        name="swa",
    )(sink.astype(F32), q, kt, vp, kct, vc)


def _full_attn_body(sink_ref, q_ref, kt_ref, v_ref, o_ref, *, hkv, g, qb, has_sink, unit):
    q = q_ref[0]
    for j in range(hkv):
        hs = slice(j * HEAD_DIM, (j + 1) * HEAD_DIM)
        for u in range(g // unit):
            heads = [j * g + u * unit + t for t in range(unit)]
            qu = jnp.concatenate([q[:, h * HEAD_DIM:(h + 1) * HEAD_DIM] for h in heads], axis=0)
            s = jnp.dot(qu, kt_ref[0, hs, :], preferred_element_type=F32)
            m = jnp.max(s, axis=-1, keepdims=True)
            if has_sink:
                s_sink = jnp.concatenate([jnp.full((qb, 1), sink_ref[h], F32) for h in heads], axis=0)
                m = jnp.maximum(m, s_sink)
            e = jnp.exp((s - m).astype(BF16))
            o = jnp.dot(e, v_ref[0, :, j * 2 * HEAD_DIM:(j + 1) * 2 * HEAD_DIM], preferred_element_type=F32)
            den = o[:, HEAD_DIM:HEAD_DIM + 1]
            if has_sink:
                den = den + jnp.exp(s_sink - m)
            o = o[:, :HEAD_DIM] / den
            for t, h in enumerate(heads):
                o_ref[0, :, h * HEAD_DIM:(h + 1) * HEAD_DIM] = o[t * qb:(t + 1) * qb].astype(BF16)


def _full_attention(q, k, v, hkv, sink=None):
    b, n, dq = q.shape
    g = dq // HEAD_DIM // hkv
    qb = min(FULL_ATTN_Q_ROWS, n)
    unit = max(1, FULL_ATTN_UNIT_ROWS // qb)
    nk = k.shape[1]
    dkv = hkv * HEAD_DIM
    kt = jnp.swapaxes(k, 1, 2)
    has_sink = sink is not None
    sink = jnp.zeros((dq // HEAD_DIM,), F32) if sink is None else sink.astype(F32)
    return pl.pallas_call(
        functools.partial(_full_attn_body, hkv=hkv, g=g, qb=qb, has_sink=has_sink, unit=min(unit, g)),
        grid=(b, n // qb),
        in_specs=[
            pl.BlockSpec(memory_space=pltpu.SMEM),
            pl.BlockSpec((1, qb, dq), lambda bi, i: (bi, i, 0)),
            pl.BlockSpec((1, dkv, nk), lambda bi, i: (bi, 0, 0)),
            pl.BlockSpec((1, nk, 2 * dkv), lambda bi, i: (bi, 0, 0)),
        ],
        out_specs=pl.BlockSpec((1, qb, dq), lambda bi, i: (bi, i, 0)),
        out_shape=jax.ShapeDtypeStruct((b, n, dq), BF16),
        compiler_params=_params("parallel", "parallel"),
        name="full_attn",
    )(sink, q, kt, _values_with_ones(v, hkv))


def _short_conv_body(u_ref, w_ref, b_ref, o_ref, *, h1, l2, c):
    slab = lambda f: u_ref[0, :, f, :]
    row = lax.broadcasted_iota(jnp.int32, (h1, c), 0)
    for f in range(l2):
        prev = slab(f - 1) if f > 0 else jnp.where(row == 0, 0.0, pltpu.roll(slab(l2 - 1), 1, 0))
        nxt = slab(f + 1) if f < l2 - 1 else jnp.where(row == h1 - 1, 0.0, pltpu.roll(slab(0), h1 - 1, 0))
        o_ref[0, 0, :, f * c:(f + 1) * c] = (prev * w_ref[0:1, :] + slab(f) * w_ref[1:2, :]
                                             + nxt * w_ref[2:3, :] + b_ref[...])


def _short_conv(u, w, bias, l2):
    b, n, c3 = u.shape
    c = c3 // 3
    h1 = n // l2
    return pl.pallas_call(
        functools.partial(_short_conv_body, h1=h1, l2=l2, c=c),
        grid=(b, 3),
        in_specs=[
            pl.BlockSpec((1, h1, l2, c), lambda bi, j: (bi, 0, 0, j)),
            pl.BlockSpec((3, c), lambda bi, j: (0, j)),
            pl.BlockSpec((1, c), lambda bi, j: (0, j)),
        ],
        out_specs=pl.BlockSpec((1, 1, h1, l2 * c), lambda bi, j: (j, bi, 0, 0)),
        out_shape=jax.ShapeDtypeStruct((3, b, h1, l2 * c), F32),
        compiler_params=_params("parallel", "parallel"),
        name="short_conv",
    )(u.reshape(b, h1, l2, c3), w, bias.reshape(1, c3))


def _filter_body(band_ref, w1_ref, b1_ref, w2_ref, b2_ref, w3_ref, fr_ref, dl_ref, o_ref, s_ref, *, n, rt, c):
    i = pl.program_id(0)
    hp = lax.Precision.HIGHEST
    m = i * rt + lax.broadcasted_iota(jnp.int32, (rt, 1), 0)
    pos = jnp.where(m < n, m, 2 * n - m).astype(F32)
    t_norm = pos / max(n - 1, 1)
    ang = (2.0 * math.pi / n) * pos * band_ref[...]
    lane = lax.broadcasted_iota(jnp.int32, (rt, LANES), 1)
    z = jnp.where(lane == 0, t_norm,
                  jnp.where(lane <= HY_BANDS, jnp.cos(ang),
                            jnp.where(lane <= 2 * HY_BANDS, -jnp.sin(ang), 0.0)))
    fr = fr_ref[...]
    hdn = jnp.sin(fr * (jnp.dot(z, w1_ref[...], precision=hp, preferred_element_type=F32) + b1_ref[...]))
    hdn = jnp.sin(fr * (jnp.dot(hdn, w2_ref[...], precision=hp, preferred_element_type=F32) + b2_ref[...]))
    h = jnp.dot(hdn, w3_ref[...], precision=hp, preferred_element_type=F32)
    h = h * jnp.exp(-t_norm * dl_ref[...])
    half = HY_ORDER * c
    sel = jnp.where(m < n, h[:, :half], jnp.where(m > n, -h[:, half:], 0.0))
    for o in range(HY_ORDER):
        o_ref[o] = sel[:, o * c:(o + 1) * c]

    @pl.when(i == 0)
    def _():
        s_ref[...] = jnp.zeros_like(s_ref)

    s_ref[...] += jnp.sum(jnp.abs(sel), axis=0, keepdims=True)


def _hyena_filters(n, w1, b1, w2, b2, w3, freq, c):
    rt = min(1024, n)
    hid = w1.shape[1]
    bands = jnp.linspace(1e-4, HY_BANDS - 1, HY_BANDS, dtype=F32)
    band_row = jnp.zeros((1, LANES), F32).at[0, 1:1 + 2 * HY_BANDS].set(jnp.tile(bands, 2))
    w1p = jnp.zeros((LANES, hid), F32).at[:w1.shape[0]].set(w1)
    max_decay = math.log(HY_DECAY_TARGET) / HY_FAST_DECAY
    min_decay = math.log(HY_DECAY_TARGET) / HY_SLOW_DECAY
    deltas = jnp.abs(jnp.linspace(min_decay, max_decay, c, dtype=F32))
    dl = jnp.tile(deltas, HY_DIRS * HY_ORDER).reshape(1, -1)
    full = lambda a: pl.BlockSpec(a.shape, lambda i: (0,) * a.ndim)
    args = (band_row, w1p, b1.reshape(1, hid), w2, b2.reshape(1, hid), w3, freq.reshape(1, hid), dl)
    return pl.pallas_call(
        functools.partial(_filter_body, n=n, rt=rt, c=c),
        grid=(2 * n // rt,),
        in_specs=[full(a) for a in args],
        out_specs=[pl.BlockSpec((HY_ORDER, rt, c), lambda i: (0, i, 0)),
                   pl.BlockSpec((1, HY_ORDER * c), lambda i: (0, 0))],
        out_shape=[jax.ShapeDtypeStruct((HY_ORDER, 2 * n, c), F32),
                   jax.ShapeDtypeStruct((1, HY_ORDER * c), F32)],
        compiler_params=_params("arbitrary"),
        name="hyena_filter",
    )(*args)


def _dft_split(n):
    l2 = 32 if n >= 2048 else 16
    return 2 * n // l2, l2


def _dft_constants(n):
    l1, l2 = _dft_split(n)
    h1 = l1 // 2
    nn = 2 * n
    k1 = np.arange(h1)[:, None]
    a = 2 * np.pi * (k1 + 0.5) * np.arange(l1)[None, :] / l1
    w1 = np.concatenate([np.cos(a), -np.sin(a)], axis=0)
    t = 2 * np.pi * (k1 + 0.5) * np.arange(l2)[None, :] / nn
    tw_cos, tw_sin = np.cos(t), np.sin(t)
    p = 2 * np.pi * np.arange(l2)[:, None] * np.arange(l2)[None, :] / l2
    w2 = np.block([[np.cos(p), np.sin(p)], [-np.sin(p), np.cos(p)]])
    w2i = np.block([[np.cos(p), -np.sin(p)], [np.sin(p), np.cos(p)]])
    ai = 2 * np.pi * np.arange(h1)[:, None] * (np.arange(h1)[None, :] + 0.5) / l1
    w1i = (2.0 / nn) * np.concatenate([np.cos(ai), -np.sin(ai)], axis=1)
    c = lambda m, dt: jnp.asarray(m, dtype=dt)
    return dict(
        l1=l1, l2=l2, h1=h1,
        w1=c(w1, BF16), w2=c(w2, BF16), w2i=c(w2i, BF16), w1i=c(w1i, BF16),
        tw_cos_fwd=c(tw_cos.T[:, :, None], F32), tw_sin_fwd=c(tw_sin.T[:, :, None], F32),
        tw_cos_inv=c(tw_cos[:, :, None], F32), tw_sin_inv=c(tw_sin[:, :, None], F32),
    )


def _ct_fwd1_body(x_ref, w_ref, tc_ref, ts_ref, nrm_ref, o_ref, *, lb, c, h1, normalise):
    for q in range(lb):
        x = x_ref[0, 0, :, q * c:(q + 1) * c]
        if normalise:
            x = x / (nrm_ref[0] + HY_FILTER_EPS)
        a = jnp.dot(w_ref[...], x.astype(BF16), preferred_element_type=F32)
        ar, ai = a[:h1], a[h1:]
        tc, ts = tc_ref[q], ts_ref[q]
        o_ref[0, 0, q] = ar * tc + ai * ts
        o_ref[0, 1, q] = ai * tc - ar * ts


def _ct_fwd1(xs, idx, consts, c, norms=None, lb=8):
    _, b, k1n, _ = xs.shape
    l2, h1 = consts["l2"], consts["h1"]
    lb = min(lb, l2)
    w = consts["w1"][:, :k1n]
    normalise = norms is not None
    if norms is None:
        norms = jnp.zeros((b, 1, c), F32)
    return pl.pallas_call(
        functools.partial(_ct_fwd1_body, lb=lb, c=c, h1=h1, normalise=normalise),
        grid=(b, l2 // lb),
        in_specs=[
            pl.BlockSpec((1, 1, k1n, lb * c), lambda bi, i: (idx, bi, 0, i)),
            pl.BlockSpec(w.shape, lambda bi, i: (0, 0)),
            pl.BlockSpec((lb, h1, 1), lambda bi, i: (i, 0, 0)),
            pl.BlockSpec((lb, h1, 1), lambda bi, i: (i, 0, 0)),
            pl.BlockSpec((1, 1, c), lambda bi, i: (bi, 0, 0)),
        ],
        out_specs=pl.BlockSpec((1, 2, lb, h1, c), lambda bi, i: (bi, 0, i, 0, 0)),
        out_shape=jax.ShapeDtypeStruct((b, 2, l2, h1, c), F32),
        compiler_params=_params("parallel", "parallel"),
        name="ct_fwd1",
    )(xs, w, consts["tw_cos_fwd"], consts["tw_sin_fwd"], norms)


def _fine_rows(ref, lead, q):
    return jnp.concatenate([ref[lead + (0, slice(None), q, slice(None))],
                            ref[lead + (1, slice(None), q, slice(None))]], axis=0)


def _ct_spec_body(a_ref, w2_ref, o_ref, *, l2, kb):
    for q in range(kb):
        x = jnp.dot(w2_ref[...], _fine_rows(a_ref, (0,), q).astype(BF16), preferred_element_type=F32)
        o_ref[0, 0, :, q, :] = x[:l2]
        o_ref[0, 1, :, q, :] = x[l2:]


def _ct_spectrum(a, consts, c, kb=8):
    b = a.shape[0]
    l2, h1 = consts["l2"], consts["h1"]
    kb = min(kb, h1)
    blk = pl.BlockSpec((1, 2, l2, kb, c), lambda bi, i: (bi, 0, 0, i, 0))
    return pl.pallas_call(
        functools.partial(_ct_spec_body, l2=l2, kb=kb),
        grid=(b, h1 // kb),
        in_specs=[blk, pl.BlockSpec((2 * l2, 2 * l2), lambda bi, i: (0, 0))],
        out_specs=blk,
        out_shape=jax.ShapeDtypeStruct((b, 2, l2, h1, c), F32),
        compiler_params=_params("parallel", "parallel"),
        name="ct_spectrum",
    )(a, consts["w2"])


def _ct_mid_body(a_ref, h_ref, w2_ref, w2i_ref, tc_ref, ts_ref, o_ref, *, l2, kb):
    for q in range(kb):
        x = jnp.dot(w2_ref[...], _fine_rows(a_ref, (0,), q).astype(BF16), preferred_element_type=F32)
        xr, xi = x[:l2], x[l2:]
        hr, hi = h_ref[0, 0, :, q, :], h_ref[0, 1, :, q, :]
        y = jnp.concatenate([xr * hr - xi * hi, xr * hi + xi * hr], axis=0).astype(BF16)
        bm = jnp.dot(w2i_ref[...], y, preferred_element_type=F32)
        br, bi = bm[:l2], bm[l2:]
        tc, ts = tc_ref[q], ts_ref[q]
        o_ref[0, 0, :, q, :] = br * tc - bi * ts
        o_ref[0, 1, :, q, :] = br * ts + bi * tc


def _ct_mid(a, hspec, order, consts, c, kb=16):
    b = a.shape[0]
    l2, h1 = consts["l2"], consts["h1"]
    kb = min(kb, h1)
    blk = pl.BlockSpec((1, 2, l2, kb, c), lambda bi, i: (bi, 0, 0, i, 0))
    return pl.pallas_call(
        functools.partial(_ct_mid_body, l2=l2, kb=kb),
        grid=(b, h1 // kb),
        in_specs=[
            blk,
            pl.BlockSpec((1, 2, l2, kb, c), lambda bi, i: (order, 0, 0, i, 0)),
            pl.BlockSpec((2 * l2, 2 * l2), lambda bi, i: (0, 0)),
            pl.BlockSpec((2 * l2, 2 * l2), lambda bi, i: (0, 0)),
            pl.BlockSpec((kb, l2, 1), lambda bi, i: (i, 0, 0)),
            pl.BlockSpec((kb, l2, 1), lambda bi, i: (i, 0, 0)),
        ],
        out_specs=blk,
        out_shape=jax.ShapeDtypeStruct((b, 2, l2, h1, c), F32),
        compiler_params=_params("parallel", "parallel"),
        name="ct_mid",
    )(a, hspec, consts["w2"], consts["w2i"], consts["tw_cos_inv"], consts["tw_sin_inv"])


def _ct_inv1_body(b_ref, w_ref, u_ref, gate_ref, skip_ref, o_ref, *, lb, c, by_position):
    for q in range(lb):
        bb = jnp.concatenate([b_ref[0, 0, q], b_ref[0, 1, q]], axis=0).astype(BF16)
        y = jnp.dot(w_ref[...], bb, preferred_element_type=F32)
        cs = slice(q * c, (q + 1) * c)
        out = gate_ref[0, 0, :, cs] * (y + u_ref[0, 0, :, cs] * skip_ref[...])
        if by_position:
            o_ref[0, :, q, :] = out
        else:
            o_ref[0, :, cs] = out


def _ct_inv1(bsp, u, u_idx, gate, gate_idx, skip, consts, c, by_position, lb=8):
    b = bsp.shape[0]
    l2, h1 = consts["l2"], consts["h1"]
    lb = min(lb, l2)
    if by_position:
        out_spec = pl.BlockSpec((1, h1, lb, c), lambda bi, i: (bi, 0, i, 0))
        out_shape = jax.ShapeDtypeStruct((b, h1, l2, c), F32)
    else:
        out_spec = pl.BlockSpec((1, h1, lb * c), lambda bi, i: (bi, 0, i))
        out_shape = jax.ShapeDtypeStruct((b, h1, l2 * c), F32)
    return pl.pallas_call(
        functools.partial(_ct_inv1_body, lb=lb, c=c, by_position=by_position),
        grid=(b, l2 // lb),
        in_specs=[
            pl.BlockSpec((1, 2, lb, h1, c), lambda bi, i: (bi, 0, i, 0, 0)),
            pl.BlockSpec((h1, 2 * h1), lambda bi, i: (0, 0)),
            pl.BlockSpec((1, 1, h1, lb * c), lambda bi, i: (u_idx, bi, 0, i)),
            pl.BlockSpec((1, 1, h1, lb * c), lambda bi, i: (gate_idx, bi, 0, i)),
            pl.BlockSpec((1, c), lambda bi, i: (0, 0)),
        ],
        out_specs=out_spec,
        out_shape=out_shape,
        compiler_params=_params("parallel", "parallel"),
        name="ct_inv1",
    )(bsp, consts["w1i"], u, gate, skip.reshape(1, c))


def _hyena(u, conv_w, conv_b, f_w1, f_b1, f_w2, f_b2, f_w3, f_freq, skip):
    b, n, c3 = u.shape
    c = c3 // 3
    consts = _dft_constants(n)
    l1, l2, h1 = consts["l1"], consts["l2"], consts["h1"]
    filt, norms = _hyena_filters(n, f_w1, f_b1, f_w2, f_b2, f_w3, f_freq, c)
    fa = _ct_fwd1(filt.reshape(1, HY_ORDER, l1, l2 * c), 0, consts, c, norms=norms.reshape(HY_ORDER, 1, c))
    hspec = _ct_spectrum(fa, consts, c)
    parts = _short_conv(u, conv_w, conv_b, l2)

    def long_conv_gated(x_stack, x_idx, gate_idx, order, by_position):
        a = _ct_fwd1(x_stack, x_idx, consts, c)
        bsp = _ct_mid(a, hspec, order, consts, c)
        return _ct_inv1(bsp, x_stack, x_idx, parts, gate_idx, skip[order], consts, c, by_position)

    z = long_conv_gated(parts, 0, 1, 0, False)
    return long_conv_gated(z[None], 0, 2, 1, True).reshape(b, n, c)


def _outproj_body(*refs, n_in):
    ins = refs[:n_in]
    ws = refs[n_in:2 * n_in]
    x_ref, gate_ref, g_ref, sc_ref, sh_ref, rhi_ref, rlo_ref, rb_ref, xo_ref, h_ref, rt_ref = refs[2 * n_in:]
    y = None
    for a_ref, w_ref in zip(ins, ws):
        t = jnp.dot(a_ref[0].astype(BF16), w_ref[...], preferred_element_type=F32)
        y = t if y is None else y + t
    x = x_ref[0] + gate_ref[0] * y
    xo_ref[0] = x
    h = x * lax.rsqrt(jnp.mean(x * x, axis=-1, keepdims=True) + NORM_EPS) * g_ref[...]
    h = h * (1.0 + sc_ref[0]) + sh_ref[0]
    hi = h.astype(BF16)
    h_ref[0] = hi
    lo = (h - hi.astype(F32)).astype(BF16)
    lg = (jnp.dot(hi, rhi_ref[...], preferred_element_type=F32)
          + jnp.dot(lo, rhi_ref[...], preferred_element_type=F32)
          + jnp.dot(hi, rlo_ref[...], preferred_element_type=F32) + rb_ref[...])
    rt_ref[0] = _route(lg)


def _route(lg):
    lane = lax.broadcasted_iota(jnp.int32, lg.shape, 1)
    lane_f = lane.astype(F32)
    neg = -jnp.inf

    def top(v):
        m = jnp.max(v, axis=-1, keepdims=True)
        return m, jnp.min(jnp.where(v == m, lane_f, float(LANES)), axis=-1, keepdims=True)

    gl = jnp.where(lane < N_GROUPS, lg, neg)
    gmax, grp = top(gl)
    p_grp = 1.0 / jnp.sum(jnp.exp(gl - gmax), axis=-1, keepdims=True)
    first = N_GROUPS + grp * EXP_PER_GROUP
    el = jnp.where((lane_f >= first) & (lane_f < first + EXP_PER_GROUP), lg, neg)
    m1, i1 = top(el)
    m2, i2 = top(jnp.where(lane_f == i1, neg, el))
    e2 = jnp.exp(m2 - m1)
    den = 1.0 + e2
    vals = (i1 - N_GROUPS, i2 - N_GROUPS, p_grp * (1.0 / den), p_grp * (e2 / den))
    out = jnp.zeros(lg.shape, F32)
    for k, v in enumerate(vals):
        out = jnp.where(lane == k, v, out)
    return out


def _outproj(ins, ws, x, gate, g, scale, shift, r_hi, r_lo, r_b, tm=512):
    b, n, d = x.shape
    tm = min(tm, n)
    bm = gate.shape[0]
    mod_map = (lambda bi, i: (bi, 0, 0)) if bm > 1 else (lambda bi, i: (0, 0, 0))
    row = lambda wd: pl.BlockSpec((1, tm, wd), lambda bi, i: (bi, i, 0))
    full = lambda a: pl.BlockSpec(a.shape, lambda bi, i: (0,) * a.ndim)
    mod = pl.BlockSpec((1, 1, d), mod_map)
    return pl.pallas_call(
        functools.partial(_outproj_body, n_in=len(ins)),
        grid=(b, n // tm),
        in_specs=([row(a.shape[-1]) for a in ins] + [full(w) for w in ws]
                  + [row(d), mod, pl.BlockSpec((1, d), lambda bi, i: (0, 0)), mod, mod,
                     full(r_hi), full(r_lo), full(r_b)]),
        out_specs=[row(d), row(d), row(LANES)],
        out_shape=[jax.ShapeDtypeStruct((b, n, d), F32), jax.ShapeDtypeStruct((b, n, d), BF16),
                   jax.ShapeDtypeStruct((b, n, LANES), F32)],
        compiler_params=_params("parallel", "parallel"),
        name="outproj",
    )(*ins, *ws, x, gate, g.reshape(1, d), scale, shift, r_hi, r_lo, r_b)


def _rank_body(rt_ref, tri_ref, upper_ref, pos_ref, cnt_ref):
    rt = rt_ref[...]
    lane_i = lax.broadcasted_iota(jnp.int32, rt.shape, 1)
    lane = lane_i.astype(F32)
    oh_a = lane == rt[:, 0:1]
    oh_b = lane == rt[:, 1:2]
    one_a = jnp.where(oh_a, 1.0, 0.0)
    one_b = jnp.where(oh_b, 1.0, 0.0)
    before_a = jnp.dot(tri_ref[...], one_a.astype(BF16), preferred_element_type=F32)
    before_b = jnp.dot(tri_ref[...], one_b.astype(BF16), preferred_element_type=F32)
    tot_a = jnp.sum(one_a, axis=0, keepdims=True)
    cnt = tot_a + jnp.sum(one_b, axis=0, keepdims=True)
    padded = jnp.floor((cnt + (CHUNK_ROWS - 1)) * (1.0 / CHUNK_ROWS)) * CHUNK_ROWS
    first = jnp.dot(jnp.broadcast_to(padded, (8, LANES)).astype(BF16), upper_ref[...],
                    preferred_element_type=F32)[0:1]
    pos_a = jnp.sum(jnp.where(oh_a, before_a + first, 0.0), axis=-1, keepdims=True)
    pos_b = jnp.sum(jnp.where(oh_b, before_b + first + tot_a, 0.0), axis=-1, keepdims=True)
    is_gate = (lane_i >= TOP_K) & (lane_i < 2 * TOP_K)
    pos_ref[...] = jnp.where(lane_i == 0, pos_a, jnp.where(lane_i == 1, pos_b, jnp.where(is_gate, rt, 0.0)))
    cnt_ref[0] = jnp.broadcast_to(cnt, (8, LANES))


def _rank(route):
    t = route.shape[0]
    tm = TOKEN_TILE
    tri = jnp.asarray(np.tril(np.ones((tm, tm)), -1), dtype=BF16)
    upper = jnp.asarray(np.triu(np.ones((LANES, LANES)), 1), dtype=BF16)
    return pl.pallas_call(
        _rank_body,
        grid=(t // tm,),
        in_specs=[pl.BlockSpec((tm, LANES), lambda i: (i, 0)), pl.BlockSpec((tm, tm), lambda i: (0, 0)),
                  pl.BlockSpec((LANES, LANES), lambda i: (0, 0))],
        out_specs=[pl.BlockSpec((tm, LANES), lambda i: (i, 0)), pl.BlockSpec((1, 8, LANES), lambda i: (i, 0, 0))],
        out_shape=[jax.ShapeDtypeStruct((t, LANES), F32), jax.ShapeDtypeStruct((t // tm, 8, LANES), F32)],
        compiler_params=_params("parallel"),
        name="moe_rank",
    )(route, tri, upper)


def _chunk_tables(cnt, n_blocks):
    padded = (cnt + CHUNK_ROWS - 1) // CHUNK_ROWS * CHUNK_ROWS
    run_end = jnp.cumsum(padded, axis=1)
    run_start = run_end - padded
    seg_rows = jnp.sum(padded, axis=0)
    seg_rows = (seg_rows + EXPERT_ROWS - 1) // EXPERT_ROWS * EXPERT_ROWS
    seg_end = jnp.cumsum(seg_rows)
    dst_start = (seg_end - seg_rows)[None, :] + jnp.cumsum(padded, axis=0) - padded
    row0 = jnp.arange(BUF_CHUNKS, dtype=jnp.int32) * CHUNK_ROWS
    chunk_exp = jnp.minimum(jnp.sum(run_end[:, None, :] <= row0[None, :, None], axis=-1), N_EXPERTS - 1)
    onehot = chunk_exp[:, :, None] == jnp.arange(N_EXPERTS, dtype=jnp.int32)[None, None, :]
    dst = jnp.sum(jnp.where(onehot, (dst_start - run_start)[:, None, :], 0), axis=-1) + row0[None, :]
    n_chunks = run_end[:, -1:] // CHUNK_ROWS
    table = jnp.concatenate(
        [dst, n_chunks, jnp.zeros((cnt.shape[0], TABLE_WORDS - BUF_CHUNKS - 1), jnp.int32)], axis=1)
    block_row0 = jnp.arange(n_blocks, dtype=jnp.int32) * EXPERT_ROWS
    block_exp = jnp.minimum(jnp.sum(seg_end[None, :] <= block_row0[:, None], axis=1), N_EXPERTS - 1)
    n_used = (seg_end[-1] // EXPERT_ROWS).reshape(1)
    return table.astype(jnp.int32), block_exp.astype(jnp.int32), n_used.astype(jnp.int32)


def _start_chunks(tab_ref, tile, make_copies):
    base = tile * TABLE_WORDS

    def issue(c, carry):
        for cp in make_copies(pl.multiple_of(c * CHUNK_ROWS, CHUNK_ROWS),
                              pl.multiple_of(tab_ref[base + c], CHUNK_ROWS)):
            cp.start()
        return carry

    lax.fori_loop(0, tab_ref[base + BUF_CHUNKS], issue, 0)


def _wait_chunks(tab_ref, tile, make_copies):
    def drain(c, carry):
        for cp in make_copies(0, 0):
            cp.wait()
        return carry

    lax.fori_loop(0, tab_ref[tile * TABLE_WORDS + BUF_CHUNKS], drain, 0)


def _dispatch_body(tab_ref, h_ref, pos_ref, xs_in, gs_in, xs_out, gs_out, buf, gbuf, sem_rows, sem_gates,
                   *, tile0, n_steps):
    del xs_in, gs_in
    step = pl.program_id(0)

    def copies(at_step):
        slot = at_step % 2
        return lambda src, dst: (
            pltpu.make_async_copy(buf.at[slot, pl.ds(src, CHUNK_ROWS)], xs_out.at[pl.ds(dst, CHUNK_ROWS)],
                                  sem_rows.at[slot]),
            pltpu.make_async_copy(gbuf.at[slot, pl.ds(src, CHUNK_ROWS)], gs_out.at[pl.ds(dst, CHUNK_ROWS)],
                                  sem_gates.at[slot]))

    @pl.when(step >= 2)
    def _():
        _wait_chunks(tab_ref, tile0 + step - 2, copies(step - 2))

    tm = h_ref.shape[0]
    row = lax.broadcasted_iota(jnp.int32, (BUF_ROWS, tm), 0).astype(F32)
    oh_a = row == pos_ref[0, 0:1, :]
    oh_b = row == pos_ref[0, 1:2, :]
    buf[step % 2] = jnp.dot(jnp.where(oh_a | oh_b, 1.0, 0.0).astype(BF16), h_ref[...],
                            preferred_element_type=F32)
    gate = jnp.sum(jnp.where(oh_a, pos_ref[0, 2:3, :], 0.0) + jnp.where(oh_b, pos_ref[0, 3:4, :], 0.0),
                   axis=-1, keepdims=True)
    gbuf[step % 2] = jnp.broadcast_to(gate, gbuf.shape[1:])
    _start_chunks(tab_ref, tile0 + step, copies(step))

    @pl.when(step == n_steps - 1)
    def _():
        @pl.when(step >= 1)
        def _():
            _wait_chunks(tab_ref, tile0 + step - 1, copies(step - 1))

        _wait_chunks(tab_ref, tile0 + step, copies(step))


def _dispatch(h, pos_t, table, xs, gs, tile0):
    t, d = h.shape
    tm = TOKEN_TILE
    n_steps = t // tm
    grid_spec = pltpu.PrefetchScalarGridSpec(
        num_scalar_prefetch=1,
        grid=(n_steps,),
        in_specs=[
            pl.BlockSpec((tm, d), lambda i, tab: (i, 0)),
            pl.BlockSpec((1, 8, tm), lambda i, tab: (tile0 + i, 0, 0)),
            pl.BlockSpec(memory_space=pl.ANY),
            pl.BlockSpec(memory_space=pl.ANY),
        ],
        out_specs=[pl.BlockSpec(memory_space=pl.ANY), pl.BlockSpec(memory_space=pl.ANY)],
        scratch_shapes=[
            pltpu.VMEM((2, BUF_ROWS, d), F32),
            pltpu.VMEM((2, BUF_ROWS, LANES), F32),
            pltpu.SemaphoreType.DMA((2,)),
            pltpu.SemaphoreType.DMA((2,)),
        ],
    )
    return pl.pallas_call(
        functools.partial(_dispatch_body, tile0=tile0, n_steps=n_steps),
        grid_spec=grid_spec,
        out_shape=[jax.ShapeDtypeStruct(xs.shape, xs.dtype), jax.ShapeDtypeStruct(gs.shape, gs.dtype)],
        input_output_aliases={3: 0, 4: 1},
        compiler_params=_params("arbitrary"),
        name="moe_dispatch",
    )(table.reshape(-1), h, pos_t, xs, gs)


def _expert_body(bexp_ref, nused_ref, x_ref, g_ref, wg_ref, wu_ref, wd_ref, o_ref):
    @pl.when(pl.program_id(0) < nused_ref[0])
    def _():
        xb = x_ref[...].astype(BF16)
        gt = jnp.dot(xb, wg_ref[0], preferred_element_type=F32)
        up = jnp.dot(xb, wu_ref[0], preferred_element_type=F32)
        hid = (gt * jax.nn.sigmoid(gt) * up).astype(BF16)
        o_ref[...] = jnp.dot(hid, wd_ref[0], preferred_element_type=F32) * g_ref[:, 0:1]

    @pl.when(pl.program_id(0) >= nused_ref[0])
    def _():
        o_ref[...] = jnp.zeros_like(o_ref)


def _experts(xs, gs, block_exp, n_used, w_gate, w_up, w_down):
    rows, d = xs.shape
    de = w_gate.shape[-1]
    used = lambda i, be, nu: (jnp.minimum(i, nu[0] - 1), 0)
    grid_spec = pltpu.PrefetchScalarGridSpec(
        num_scalar_prefetch=2,
        grid=(rows // EXPERT_ROWS,),
        in_specs=[
            pl.BlockSpec((EXPERT_ROWS, d), used),
            pl.BlockSpec((EXPERT_ROWS, LANES), used),
            pl.BlockSpec((1, d, de), lambda i, be, nu: (be[i], 0, 0)),
            pl.BlockSpec((1, d, de), lambda i, be, nu: (be[i], 0, 0)),
            pl.BlockSpec((1, de, d), lambda i, be, nu: (be[i], 0, 0)),
        ],
        out_specs=pl.BlockSpec((EXPERT_ROWS, d), lambda i, be, nu: (i, 0)),
    )
    return pl.pallas_call(
        _expert_body,
        grid_spec=grid_spec,
        out_shape=jax.ShapeDtypeStruct(xs.shape, F32),
        compiler_params=_params("arbitrary"),
        name="experts",
    )(block_exp, n_used, xs, gs, w_gate, w_up, w_down)


def _combine_body(tab_ref, ys_hbm, x_ref, gate_ref, pos_ref, fg_ref, o_ref, ybuf, sem_rows,
                  *, tile0, n_tiles, n_steps, final_norm):
    step = pl.program_id(0) * n_tiles + pl.program_id(1)

    def copies(at_step):
        slot = at_step % 2
        return lambda dst, src: (
            pltpu.make_async_copy(ys_hbm.at[pl.ds(src, CHUNK_ROWS)], ybuf.at[slot, pl.ds(dst, CHUNK_ROWS)],
                                  sem_rows.at[slot]),)

    @pl.when(step == 0)
    def _():
        ybuf[...] = jnp.zeros_like(ybuf)
        _start_chunks(tab_ref, tile0, copies(0))

    @pl.when(step + 1 < n_steps)
    def _():
        _start_chunks(tab_ref, tile0 + step + 1, copies(step + 1))

    _wait_chunks(tab_ref, tile0 + step, copies(step))
    tm = x_ref.shape[1]
    col = lax.broadcasted_iota(jnp.int32, (tm, BUF_ROWS), 1).astype(F32)
    pick = jnp.where((col == pos_ref[:, 0:1]) | (col == pos_ref[:, 1:2]), 1.0, 0.0).astype(BF16)
    out = x_ref[0] + gate_ref[0] * jnp.dot(pick, ybuf[step % 2].astype(BF16), preferred_element_type=F32)
    if final_norm:
        out = out * lax.rsqrt(jnp.mean(out * out, axis=-1, keepdims=True) + NORM_EPS) * fg_ref[...]
    o_ref[0] = out


def _combine(ys, table, pos, x, gate, tile0, final_g=None):
    b, n, d = x.shape
    tm = min(TOKEN_TILE, n)
    n_tiles = n // tm
    bm = gate.shape[0]
    mod_map = (lambda bi, i, tab: (bi, 0, 0)) if bm > 1 else (lambda bi, i, tab: (0, 0, 0))
    final_norm = final_g is not None
    fg = final_g.reshape(1, d) if final_norm else jnp.ones((1, d), F32)
    grid_spec = pltpu.PrefetchScalarGridSpec(
        num_scalar_prefetch=1,
        grid=(b, n_tiles),
        in_specs=[
            pl.BlockSpec(memory_space=pl.ANY),
            pl.BlockSpec((1, tm, d), lambda bi, i, tab: (bi, i, 0)),
            pl.BlockSpec((1, 1, d), mod_map),
            pl.BlockSpec((tm, LANES), lambda bi, i, tab: (tile0 + bi * n_tiles + i, 0)),
            pl.BlockSpec((1, d), lambda bi, i, tab: (0, 0)),
        ],
        out_specs=pl.BlockSpec((1, tm, d), lambda bi, i, tab: (bi, i, 0)),
        scratch_shapes=[
            pltpu.VMEM((2, BUF_ROWS, d), F32),
            pltpu.SemaphoreType.DMA((2,)),
        ],
    )
    return pl.pallas_call(
        functools.partial(_combine_body, tile0=tile0, n_tiles=n_tiles, n_steps=b * n_tiles,
                          final_norm=final_norm),
        grid_spec=grid_spec,
        out_shape=jax.ShapeDtypeStruct((b, n, d), F32),
        compiler_params=_params("arbitrary", "arbitrary"),
        name="moe_combine",
    )(table.reshape(-1), ys, x, gate, pos, fg)


def _router_weights(wg, bg, we, be):
    d = wg.shape[0]
    w = jnp.zeros((d, LANES), F32).at[:, :N_GROUPS].set(wg).at[:, N_GROUPS:N_GROUPS + N_EXPERTS].set(we)
    bias = jnp.zeros((1, LANES), F32).at[0, :N_GROUPS].set(bg).at[0, N_GROUPS:N_GROUPS + N_EXPERTS].set(be)
    hi = w.astype(BF16)
    lo = (w - hi.astype(F32)).astype(BF16)
    return hi, lo, bias


def kernel(x, c, ctx, c_ctx, ada_w, ada_b, norm1_g, norm2_g, ev_w_in, ev_w_out, hy_conv_w, hy_conv_b, hy_f_w1, hy_f_b1, hy_f_w2, hy_f_b2, hy_f_w3, hy_f_freq, hy_skip, swa_sink, od_w_qkv, od_w_out, od_q_norm_g, od_k_norm_g, rt_group_w, rt_group_b, rt_exp_w, rt_exp_b, moe_w_gate, moe_w_up, moe_w_down, final_norm_g):
    b, n, d = x.shape
    lc = ctx.shape[1]
    depth = ada_w.shape[0]
    rope = _rope_tables(n)
    xc = ctx
    sc = jax.nn.silu(c)
    scc = jax.nn.silu(c_ctx)
    q_scale = HEAD_DIM ** -0.5
    for layer in range(depth):
        with_ctx = layer < depth - 1
        mod = (sc @ ada_w[layer] + ada_b[layer]).reshape(b, N_MOD, 1, d)
        modc = (scc @ ada_w[layer] + ada_b[layer]).reshape(1, N_MOD, 1, d)
        m = [mod[:, k] for k in range(N_MOD)]
        mc = [modc[:, k] for k in range(N_MOD)]
        r_hi, r_lo, r_b = _router_weights(rt_group_w[layer], rt_group_b[layer], rt_exp_w[layer], rt_exp_b[layer])
        wgt, wup, wdn = (moe_w_gate[layer].astype(BF16), moe_w_up[layer].astype(BF16),
                         moe_w_down[layer].astype(BF16))
        if layer % 2 == 0:
            e = layer // 2
            c_hy = hy_conv_w.shape[-1] // 3
            d_hy = 3 * c_hy
            hq = swa_sink.shape[-1]
            d_q = hq * HEAD_DIM
            hkv = hq // 4
            d_kv = hkv * HEAD_DIM
            w_in = ev_w_in[e].astype(BF16)
            w_out = ev_w_out[e].astype(BF16)
            hy_args = (hy_conv_w[e], hy_conv_b[e], hy_f_w1[e], hy_f_b1[e], hy_f_w2[e], hy_f_b2[e],
                       hy_f_w3[e], hy_f_freq[e], hy_skip[e])
            u, q, k, v = _proj(x, norm1_g[layer], m[1], m[0], w_in, [
                (0, d_hy, "f32", None, False, 1.0),
                (d_hy, d_q, "qk", None, True, q_scale),
                (d_hy + d_q, d_kv, "qk", None, True, 1.0),
                (d_hy + d_q + d_kv, d_kv, "bf16", None, False, 1.0)], rope_tabs=rope)
            if with_ctx:
                uc, qc, kc, vc = _proj(xc, norm1_g[layer], mc[1], mc[0], w_in, [
                    (0, d_hy, "f32", None, False, 1.0),
                    (d_hy, d_q, "qk", None, False, q_scale),
                    (d_hy + d_q, d_kv, "bf16", None, False, 1.0),
                    (d_hy + d_q + d_kv, d_kv, "bf16", None, False, 1.0)])
            else:
                kc, vc = _proj(xc, norm1_g[layer], mc[1], mc[0], w_in, [
                    (d_hy + d_q, d_kv, "bf16", None, False, 1.0),
                    (d_hy + d_q + d_kv, d_kv, "bf16", None, False, 1.0)])
            y_hy = _hyena(u, *hy_args)
            y_att = _windowed_attention(q, k, v, kc, vc, swa_sink[e], hkv)
            mix_in, mix_w = [y_hy, y_att], [w_out[:c_hy], w_out[c_hy:]]
            if with_ctx:
                yc_hy = _hyena(uc, *hy_args)
                yc_att = _full_attention(qc, kc, vc, hkv, sink=swa_sink[e])
                mixc_in = [yc_hy, yc_att]
        else:
            o = layer // 2
            hkv = od_w_qkv.shape[-1] // HEAD_DIM // 6
            hq = 4 * hkv
            d_q = hq * HEAD_DIM
            d_kv = hkv * HEAD_DIM
            w_qkv = od_w_qkv[o].astype(BF16)
            w_out = od_w_out[o].astype(BF16)
            norm_g = jnp.zeros((8, LANES), F32).at[0].set(jnp.tile(od_q_norm_g[o], 2)).at[1].set(
                jnp.tile(od_k_norm_g[o], 2))
            q, k, v = _proj(x, norm1_g[layer], m[1], m[0], w_qkv, [
                (0, d_q, "qk", 0, True, q_scale),
                (d_q, d_kv, "qk", 1, True, 1.0),
                (d_q + d_kv, d_kv, "bf16", None, False, 1.0)], rope_tabs=rope, norm_g=norm_g)
            if with_ctx:
                qc, kc, vc = _proj(xc, norm1_g[layer], mc[1], mc[0], w_qkv, [
                    (0, d_q, "qk", 0, False, q_scale),
                    (d_q, d_kv, "qk", 1, False, 1.0),
                    (d_q + d_kv, d_kv, "bf16", None, False, 1.0)], norm_g=norm_g)
            else:
                kc, vc = _proj(xc, norm1_g[layer], mc[1], mc[0], w_qkv, [
                    (d_q, d_kv, "qk", 1, False, 1.0),
                    (d_q + d_kv, d_kv, "bf16", None, False, 1.0)], norm_g=norm_g)
            y_att = _full_attention(q, jnp.concatenate([kc, k], axis=1), jnp.concatenate([vc, v], axis=1), hkv)
            mix_in, mix_w = [y_att], [w_out]
            if with_ctx:
                mixc_in = [_full_attention(qc, kc, vc, hkv)]
        x, h2, rt = _outproj(mix_in, mix_w, x, m[2], norm2_g[layer], m[4], m[3], r_hi, r_lo, r_b)
        route_flat = rt.reshape(b * n, LANES)
        if with_ctx:
            xc, h2c, rtc = _outproj(mixc_in, mix_w, xc, mc[2], norm2_g[layer], mc[4], mc[3], r_hi, r_lo, r_b)
            route_flat = jnp.concatenate([route_flat, rtc.reshape(b * lc, LANES)], axis=0)
        n_tok = route_flat.shape[0]
        n_tiles = n_tok // TOKEN_TILE
        lat_tiles = b * n // TOKEN_TILE
        max_rows = n_tok * TOP_K + n_tiles * N_EXPERTS * (CHUNK_ROWS - 1)
        n_blocks = -(-max_rows // EXPERT_ROWS) + N_EXPERTS
        pos, cnt = _rank(route_flat)
        table, block_exp, n_used = _chunk_tables(cnt[:, 0, :N_EXPERTS].astype(jnp.int32), n_blocks)
        pos_t = jnp.swapaxes(pos[:, :8].reshape(n_tiles, TOKEN_TILE, 8), 1, 2)
        xs = jnp.zeros((n_blocks * EXPERT_ROWS, d), F32)
        gs = jnp.zeros((n_blocks * EXPERT_ROWS, LANES), F32)
        xs, gs = _dispatch(h2.reshape(b * n, d), pos_t, table, xs, gs, 0)
        if with_ctx:
            xs, gs = _dispatch(h2c.reshape(b * lc, d), pos_t, table, xs, gs, lat_tiles)
        ys = _experts(xs, gs, block_exp, n_used, wgt, wup, wdn)
        x = _combine(ys, table, pos, x, m[5], 0, final_g=None if with_ctx else final_norm_g)
        if with_ctx:
            xc = _combine(ys, table, pos, xc.reshape(b * lc // TOKEN_TILE, TOKEN_TILE, d), mc[5],
                          lat_tiles).reshape(b, lc, d)
    return x
```

```python
import functools
import math

import numpy as np
import jax
import jax.numpy as jnp
from jax import lax
from jax.experimental import pallas as pl
from jax.experimental.pallas import tpu as pltpu

F32 = jnp.float32
BF16 = jnp.bfloat16

HEAD_DIM = 64
GRID_W = 64
ROPE_BASE = 10000.0
Q_BLOCK = 128
NORM_EPS = 1e-6
N_MOD = 6
HY_ORDER = 2
HY_BANDS = 16
HY_DIRS = 2
HY_DECAY_TARGET = 1e-2
HY_FAST_DECAY = 0.3
HY_SLOW_DECAY = 1.5
HY_FILTER_EPS = 1e-6
SWA_WINDOW = 128
N_GROUPS = 4
EXP_PER_GROUP = 8
N_EXPERTS = N_GROUPS * EXP_PER_GROUP
TOP_K = 2
EXPERT_ROWS = 512
SWA_BLOCKS_PER_STEP = 4
FULL_ATTN_Q_ROWS = 256
FULL_ATTN_UNIT_ROWS = 512
TOKEN_TILE = 512
CHUNK_ROWS = 8
BUF_ROWS = 1280
BUF_CHUNKS = BUF_ROWS // CHUNK_ROWS
TABLE_WORDS = 256

LANES = 128
VMEM_LIMIT_BYTES = 56 * 1024 * 1024


def _params(*sem):
    return pltpu.CompilerParams(dimension_semantics=sem, vmem_limit_bytes=VMEM_LIMIT_BYTES)


def _rope_tables(n):
    d_axis = HEAD_DIM // 2
    t = jnp.arange(n)
    inv = ROPE_BASE ** (-jnp.arange(0, d_axis, 2, dtype=F32) / d_axis)
    ang_r = (t // GRID_W).astype(F32)[:, None] * inv[None, :]
    ang_c = (t % GRID_W).astype(F32)[:, None] * inv[None, :]
    cos = jnp.concatenate([jnp.cos(ang_r)] * 2 + [jnp.cos(ang_c)] * 2, axis=-1)
    sin = jnp.concatenate([-jnp.sin(ang_r), jnp.sin(ang_r), -jnp.sin(ang_c), jnp.sin(ang_c)], axis=-1)
    return jnp.tile(cos, (1, 2)), jnp.tile(sin, (1, 2))


def _head_mean_matrix():
    i = np.arange(LANES)
    return jnp.asarray((i[:, None] // HEAD_DIM == i[None, :] // HEAD_DIM) / HEAD_DIM, dtype=BF16)


def _proj_body(x_ref, g_ref, sc_ref, sh_ref, w_ref, cos_ref, sin_ref, ng_ref, bd_ref, *out_refs, segs):
    x = x_ref[0]
    h = x * lax.rsqrt(jnp.mean(x * x, axis=-1, keepdims=True) + NORM_EPS) * g_ref[...]
    hb = (h * (1.0 + sc_ref[0]) + sh_ref[0]).astype(BF16)
    for o_ref, (c0, width, kind, norm_row, rope, out_scale) in zip(out_refs, segs):
        seg = jnp.dot(hb, w_ref[:, c0:c0 + width], preferred_element_type=F32)
        if kind == "f32":
            o_ref[0] = seg
            continue
        if kind == "bf16":
            o_ref[0] = seg.astype(BF16)
            continue
        for j in range(width // LANES):
            ch = seg[:, j * LANES:(j + 1) * LANES]
            if norm_row is not None:
                sq = ch * ch
                hi = sq.astype(BF16)
                lo = (sq - hi.astype(F32)).astype(BF16)
                ms = (jnp.dot(hi, bd_ref[...], preferred_element_type=F32)
                      + jnp.dot(lo, bd_ref[...], preferred_element_type=F32))
                ch = ch * lax.rsqrt(ms + NORM_EPS) * ng_ref[norm_row:norm_row + 1, :]
            if rope:
                lane = lax.broadcasted_iota(jnp.int32, ch.shape, 1)
                partner = jnp.where(lane % 32 < 16, pltpu.roll(ch, LANES - 16, 1), pltpu.roll(ch, 16, 1))
                ch = ch * cos_ref[...] + partner * sin_ref[...]
            if out_scale != 1.0:
                ch = ch * out_scale
            o_ref[0, :, j * LANES:(j + 1) * LANES] = ch.astype(BF16)


def _proj(x, g, scale, shift, w, segs, rope_tabs=None, norm_g=None, tm=512):
    b, n, d = x.shape
    tm = min(tm, n)
    bm = scale.shape[0]
    mod_map = (lambda bi, i: (bi, 0, 0)) if bm > 1 else (lambda bi, i: (0, 0, 0))
    if rope_tabs is None:
        cos = sin = jnp.zeros((8, LANES), F32)
        tab_spec = pl.BlockSpec((8, LANES), lambda bi, i: (0, 0))
    else:
        cos, sin = rope_tabs
        tab_spec = pl.BlockSpec((tm, LANES), lambda bi, i: (i, 0))
    if norm_g is None:
        norm_g = jnp.ones((8, LANES), F32)
    out_shape = [jax.ShapeDtypeStruct((b, n, s[1]), F32 if s[2] == "f32" else BF16) for s in segs]
    out_specs = [pl.BlockSpec((1, tm, s[1]), lambda bi, i: (bi, i, 0)) for s in segs]
    return pl.pallas_call(
        functools.partial(_proj_body, segs=tuple(segs)),
        grid=(b, n // tm),
        in_specs=[
            pl.BlockSpec((1, tm, d), lambda bi, i: (bi, i, 0)),
            pl.BlockSpec((1, d), lambda bi, i: (0, 0)),
            pl.BlockSpec((1, 1, d), mod_map),
            pl.BlockSpec((1, 1, d), mod_map),
            pl.BlockSpec(w.shape, lambda bi, i: (0, 0)),
            tab_spec,
            tab_spec,
            pl.BlockSpec(norm_g.shape, lambda bi, i: (0, 0)),
            pl.BlockSpec((LANES, LANES), lambda bi, i: (0, 0)),
        ],
        out_specs=out_specs,
        out_shape=out_shape,
        compiler_params=_params("parallel", "parallel"),
        name="proj",
    )(x, g.reshape(1, d), scale, shift, w, cos, sin, norm_g, _head_mean_matrix())


def _stack_heads(q, j, g):
    return jnp.concatenate(
        [q[:, (j * g + gg) * HEAD_DIM:(j * g + gg + 1) * HEAD_DIM] for gg in range(g)], axis=0)


def _values_with_ones(v, hkv):
    b, nk, _ = v.shape
    ones = jnp.zeros((b, nk, hkv, HEAD_DIM), v.dtype).at[..., 0].set(1)
    return jnp.concatenate([v.reshape(b, nk, hkv, HEAD_DIM), ones], axis=-1).reshape(b, nk, 2 * hkv * HEAD_DIM)


def _sink_column(sink_ref, j, g, qb):
    return jnp.concatenate([jnp.full((qb, 1), sink_ref[j * g + gg], F32) for gg in range(g)], axis=0)


def _swa_body(sink_ref, q_ref, kt_ref, v_ref, kct_ref, vc_ref, o_ref, *, n, hkv, g, qb, win, sub):
    rows = lax.broadcasted_iota(jnp.int32, (g * qb, 3 * qb), 0) % qb
    cols = lax.broadcasted_iota(jnp.int32, (g * qb, 3 * qb), 1)
    in_band = jnp.abs(rows + qb - cols) <= win
    for sb in range(sub):
        blk = pl.program_id(1) * sub + sb
        start = pl.multiple_of(blk * qb, qb)
        qs = slice(sb * qb, (sb + 1) * qb)
        q = q_ref[0, qs, :]
        key_pos = cols + (blk - 1) * qb
        valid = in_band & (key_pos >= 0) & (key_pos < n)
        for j in range(hkv):
            hs = slice(j * HEAD_DIM, (j + 1) * HEAD_DIM)
            q4 = _stack_heads(q, j, g)
            s_lat = jnp.dot(q4, kt_ref[0, hs, pl.ds(start, 3 * qb)], preferred_element_type=F32)
            s_lat = jnp.where(valid, s_lat, -jnp.inf)
            s_ctx = jnp.dot(q4, kct_ref[0, hs, :], preferred_element_type=F32)
            s_sink = _sink_column(sink_ref, j, g, qb)
            m = jnp.maximum(jnp.maximum(jnp.max(s_lat, axis=-1, keepdims=True),
                                        jnp.max(s_ctx, axis=-1, keepdims=True)), s_sink)
            e_lat = jnp.exp((s_lat - m).astype(BF16))
            e_ctx = jnp.exp((s_ctx - m).astype(BF16))
            vs = slice(j * 2 * HEAD_DIM, (j + 1) * 2 * HEAD_DIM)
            o = (jnp.dot(e_ctx, vc_ref[0, :, vs], preferred_element_type=F32)
                 + jnp.dot(e_lat, v_ref[0, pl.ds(start, 3 * qb), vs], preferred_element_type=F32))
            o = o[:, :HEAD_DIM] / (o[:, HEAD_DIM:HEAD_DIM + 1] + jnp.exp(s_sink - m))
            for gg in range(g):
                c0 = (j * g + gg) * HEAD_DIM
                o_ref[0, qs, c0:c0 + HEAD_DIM] = o[gg * qb:(gg + 1) * qb].astype(BF16)


def _windowed_attention(q, k, v, kc, vc, sink, hkv):
    b, n, dq = q.shape
    g = dq // HEAD_DIM // hkv
    qb = Q_BLOCK
    lc = kc.shape[1]
    dkv = hkv * HEAD_DIM
    kt = jnp.swapaxes(jnp.pad(k, ((0, 0), (qb, qb), (0, 0))), 1, 2)
    vp = _values_with_ones(jnp.pad(v, ((0, 0), (qb, qb), (0, 0))), hkv)
    vc = _values_with_ones(vc, hkv)
    kct = jnp.swapaxes(kc, 1, 2)
    sub = min(SWA_BLOCKS_PER_STEP, n // qb)
    return pl.pallas_call(
        functools.partial(_swa_body, n=n, hkv=hkv, g=g, qb=qb, win=SWA_WINDOW, sub=sub),
        grid=(b, n // (sub * qb)),
        in_specs=[
            pl.BlockSpec(memory_space=pltpu.SMEM),
            pl.BlockSpec((1, sub * qb, dq), lambda bi, i: (bi, i, 0)),
            pl.BlockSpec((1, dkv, n + 2 * qb), lambda bi, i: (bi, 0, 0)),
            pl.BlockSpec((1, n + 2 * qb, 2 * dkv), lambda bi, i: (bi, 0, 0)),
            pl.BlockSpec((1, dkv, lc), lambda bi, i: (bi, 0, 0)),
            pl.BlockSpec((1, lc, 2 * dkv), lambda bi, i: (bi, 0, 0)),
        ],
        out_specs=pl.BlockSpec((1, sub * qb, dq), lambda bi, i: (bi, i, 0)),
        out_shape=jax.ShapeDtypeStruct((b, n, dq), BF16),
        compiler_params=_params("parallel", "parallel"),
        name="swa",
    )(sink.astype(F32), q, kt, vp, kct, vc)


def _full_attn_body(sink_ref, q_ref, kt_ref, v_ref, o_ref, *, hkv, g, qb, has_sink, unit):
    q = q_ref[0]
    for j in range(hkv):
        hs = slice(j * HEAD_DIM, (j + 1) * HEAD_DIM)
        for u in range(g // unit):
            heads = [j * g + u * unit + t for t in range(unit)]
            qu = jnp.concatenate([q[:, h * HEAD_DIM:(h + 1) * HEAD_DIM] for h in heads], axis=0)
            s = jnp.dot(qu, kt_ref[0, hs, :], preferred_element_type=F32)
            m = jnp.max(s, axis=-1, keepdims=True)
            if has_sink:
                s_sink = jnp.concatenate([jnp.full((qb, 1), sink_ref[h], F32) for h in heads], axis=0)
                m = jnp.maximum(m, s_sink)
            e = jnp.exp((s - m).astype(BF16))
            o = jnp.dot(e, v_ref[0, :, j * 2 * HEAD_DIM:(j + 1) * 2 * HEAD_DIM], preferred_element_type=F32)
            den = o[:, HEAD_DIM:HEAD_DIM + 1]
            if has_sink:
                den = den + jnp.exp(s_sink - m)
            o = o[:, :HEAD_DIM] / den
            for t, h in enumerate(heads):
                o_ref[0, :, h * HEAD_DIM:(h + 1) * HEAD_DIM] = o[t * qb:(t + 1) * qb].astype(BF16)


def _full_attention(q, k, v, hkv, sink=None):
    b, n, dq = q.shape
    g = dq // HEAD_DIM // hkv
    qb = min(FULL_ATTN_Q_ROWS, n)
    unit = max(1, FULL_ATTN_UNIT_ROWS // qb)
    nk = k.shape[1]
    dkv = hkv * HEAD_DIM
    kt = jnp.swapaxes(k, 1, 2)
    has_sink = sink is not None
    sink = jnp.zeros((dq // HEAD_DIM,), F32) if sink is None else sink.astype(F32)
    return pl.pallas_call(
        functools.partial(_full_attn_body, hkv=hkv, g=g, qb=qb, has_sink=has_sink, unit=min(unit, g)),
        grid=(b, n // qb),
        in_specs=[
            pl.BlockSpec(memory_space=pltpu.SMEM),
            pl.BlockSpec((1, qb, dq), lambda bi, i: (bi, i, 0)),
            pl.BlockSpec((1, dkv, nk), lambda bi, i: (bi, 0, 0)),
            pl.BlockSpec((1, nk, 2 * dkv), lambda bi, i: (bi, 0, 0)),
        ],
        out_specs=pl.BlockSpec((1, qb, dq), lambda bi, i: (bi, i, 0)),
        out_shape=jax.ShapeDtypeStruct((b, n, dq), BF16),
        compiler_params=_params("parallel", "parallel"),
        name="full_attn",
    )(sink, q, kt, _values_with_ones(v, hkv))


def _short_conv_body(u_ref, w_ref, b_ref, o_ref, *, h1, l2, c):
    slab = lambda f: u_ref[0, :, f, :]
    row = lax.broadcasted_iota(jnp.int32, (h1, c), 0)
    for f in range(l2):
        prev = slab(f - 1) if f > 0 else jnp.where(row == 0, 0.0, pltpu.roll(slab(l2 - 1), 1, 0))
        nxt = slab(f + 1) if f < l2 - 1 else jnp.where(row == h1 - 1, 0.0, pltpu.roll(slab(0), h1 - 1, 0))
        o_ref[0, 0, :, f * c:(f + 1) * c] = (prev * w_ref[0:1, :] + slab(f) * w_ref[1:2, :]
                                             + nxt * w_ref[2:3, :] + b_ref[...])


def _short_conv(u, w, bias, l2):
    b, n, c3 = u.shape
    c = c3 // 3
    h1 = n // l2
    return pl.pallas_call(
        functools.partial(_short_conv_body, h1=h1, l2=l2, c=c),
        grid=(b, 3),
        in_specs=[
            pl.BlockSpec((1, h1, l2, c), lambda bi, j: (bi, 0, 0, j)),
            pl.BlockSpec((3, c), lambda bi, j: (0, j)),
            pl.BlockSpec((1, c), lambda bi, j: (0, j)),
        ],
        out_specs=pl.BlockSpec((1, 1, h1, l2 * c), lambda bi, j: (j, bi, 0, 0)),
        out_shape=jax.ShapeDtypeStruct((3, b, h1, l2 * c), F32),
        compiler_params=_params("parallel", "parallel"),
        name="short_conv",
    )(u.reshape(b, h1, l2, c3), w, bias.reshape(1, c3))


def _filter_body(band_ref, w1_ref, b1_ref, w2_ref, b2_ref, w3_ref, fr_ref, dl_ref, o_ref, s_ref, *, n, rt, c):
    i = pl.program_id(0)
    hp = lax.Precision.HIGHEST
    m = i * rt + lax.broadcasted_iota(jnp.int32, (rt, 1), 0)
    pos = jnp.where(m < n, m, 2 * n - m).astype(F32)
    t_norm = pos / max(n - 1, 1)
    ang = (2.0 * math.pi / n) * pos * band_ref[...]
    lane = lax.broadcasted_iota(jnp.int32, (rt, LANES), 1)
    z = jnp.where(lane == 0, t_norm,
                  jnp.where(lane <= HY_BANDS, jnp.cos(ang),
                            jnp.where(lane <= 2 * HY_BANDS, -jnp.sin(ang), 0.0)))
    fr = fr_ref[...]
    hdn = jnp.sin(fr * (jnp.dot(z, w1_ref[...], precision=hp, preferred_element_type=F32) + b1_ref[...]))
    hdn = jnp.sin(fr * (jnp.dot(hdn, w2_ref[...], precision=hp, preferred_element_type=F32) + b2_ref[...]))
    h = jnp.dot(hdn, w3_ref[...], precision=hp, preferred_element_type=F32)
    h = h * jnp.exp(-t_norm * dl_ref[...])
    half = HY_ORDER * c
    sel = jnp.where(m < n, h[:, :half], jnp.where(m > n, -h[:, half:], 0.0))
    for o in range(HY_ORDER):
        o_ref[o] = sel[:, o * c:(o + 1) * c]

    @pl.when(i == 0)
    def _():
        s_ref[...] = jnp.zeros_like(s_ref)

    s_ref[...] += jnp.sum(jnp.abs(sel), axis=0, keepdims=True)


def _hyena_filters(n, w1, b1, w2, b2, w3, freq, c):
    rt = min(1024, n)
    hid = w1.shape[1]
    bands = jnp.linspace(1e-4, HY_BANDS - 1, HY_BANDS, dtype=F32)
    band_row = jnp.zeros((1, LANES), F32).at[0, 1:1 + 2 * HY_BANDS].set(jnp.tile(bands, 2))
    w1p = jnp.zeros((LANES, hid), F32).at[:w1.shape[0]].set(w1)
    max_decay = math.log(HY_DECAY_TARGET) / HY_FAST_DECAY
    min_decay = math.log(HY_DECAY_TARGET) / HY_SLOW_DECAY
    deltas = jnp.abs(jnp.linspace(min_decay, max_decay, c, dtype=F32))
    dl = jnp.tile(deltas, HY_DIRS * HY_ORDER).reshape(1, -1)
    full = lambda a: pl.BlockSpec(a.shape, lambda i: (0,) * a.ndim)
    args = (band_row, w1p, b1.reshape(1, hid), w2, b2.reshape(1, hid), w3, freq.reshape(1, hid), dl)
    return pl.pallas_call(
        functools.partial(_filter_body, n=n, rt=rt, c=c),
        grid=(2 * n // rt,),
        in_specs=[full(a) for a in args],
        out_specs=[pl.BlockSpec((HY_ORDER, rt, c), lambda i: (0, i, 0)),
                   pl.BlockSpec((1, HY_ORDER * c), lambda i: (0, 0))],
        out_shape=[jax.ShapeDtypeStruct((HY_ORDER, 2 * n, c), F32),
                   jax.ShapeDtypeStruct((1, HY_ORDER * c), F32)],
        compiler_params=_params("arbitrary"),
        name="hyena_filter",
    )(*args)


def _dft_split(n):
    l2 = 32 if n >= 2048 else 16
    return 2 * n // l2, l2


def _dft_constants(n):
    l1, l2 = _dft_split(n)
    h1 = l1 // 2
    nn = 2 * n
    k1 = np.arange(h1)[:, None]
    a = 2 * np.pi * (k1 + 0.5) * np.arange(l1)[None, :] / l1
    w1 = np.concatenate([np.cos(a), -np.sin(a)], axis=0)
    t = 2 * np.pi * (k1 + 0.5) * np.arange(l2)[None, :] / nn
    tw_cos, tw_sin = np.cos(t), np.sin(t)
    p = 2 * np.pi * np.arange(l2)[:, None] * np.arange(l2)[None, :] / l2
    w2 = np.block([[np.cos(p), np.sin(p)], [-np.sin(p), np.cos(p)]])
    w2i = np.block([[np.cos(p), -np.sin(p)], [np.sin(p), np.cos(p)]])
    ai = 2 * np.pi * np.arange(h1)[:, None] * (np.arange(h1)[None, :] + 0.5) / l1
    w1i = (2.0 / nn) * np.concatenate([np.cos(ai), -np.sin(ai)], axis=1)
    c = lambda m, dt: jnp.asarray(m, dtype=dt)
    return dict(
        l1=l1, l2=l2, h1=h1,
        w1=c(w1, BF16), w2=c(w2, BF16), w2i=c(w2i, BF16), w1i=c(w1i, BF16),
        tw_cos_fwd=c(tw_cos.T[:, :, None], F32), tw_sin_fwd=c(tw_sin.T[:, :, None], F32),
        tw_cos_inv=c(tw_cos[:, :, None], F32), tw_sin_inv=c(tw_sin[:, :, None], F32),
    )


def _ct_fwd1_body(x_ref, w_ref, tc_ref, ts_ref, nrm_ref, o_ref, *, lb, c, h1, normalise):
    for q in range(lb):
        x = x_ref[0, 0, :, q * c:(q + 1) * c]
        if normalise:
            x = x / (nrm_ref[0] + HY_FILTER_EPS)
        a = jnp.dot(w_ref[...], x.astype(BF16), preferred_element_type=F32)
        ar, ai = a[:h1], a[h1:]
        tc, ts = tc_ref[q], ts_ref[q]
        o_ref[0, 0, q] = ar * tc + ai * ts
        o_ref[0, 1, q] = ai * tc - ar * ts


def _ct_fwd1(xs, idx, consts, c, norms=None, lb=8):
    _, b, k1n, _ = xs.shape
    l2, h1 = consts["l2"], consts["h1"]
    lb = min(lb, l2)
    w = consts["w1"][:, :k1n]
    normalise = norms is not None
    if norms is None:
        norms = jnp.zeros((b, 1, c), F32)
    return pl.pallas_call(
        functools.partial(_ct_fwd1_body, lb=lb, c=c, h1=h1, normalise=normalise),
        grid=(b, l2 // lb),
        in_specs=[
            pl.BlockSpec((1, 1, k1n, lb * c), lambda bi, i: (idx, bi, 0, i)),
            pl.BlockSpec(w.shape, lambda bi, i: (0, 0)),
            pl.BlockSpec((lb, h1, 1), lambda bi, i: (i, 0, 0)),
            pl.BlockSpec((lb, h1, 1), lambda bi, i: (i, 0, 0)),
            pl.BlockSpec((1, 1, c), lambda bi, i: (bi, 0, 0)),
        ],
        out_specs=pl.BlockSpec((1, 2, lb, h1, c), lambda bi, i: (bi, 0, i, 0, 0)),
        out_shape=jax.ShapeDtypeStruct((b, 2, l2, h1, c), F32),
        compiler_params=_params("parallel", "parallel"),
        name="ct_fwd1",
    )(xs, w, consts["tw_cos_fwd"], consts["tw_sin_fwd"], norms)


def _fine_rows(ref, lead, q):
    return jnp.concatenate([ref[lead + (0, slice(None), q, slice(None))],
                            ref[lead + (1, slice(None), q, slice(None))]], axis=0)


def _ct_spec_body(a_ref, w2_ref, o_ref, *, l2, kb):
    for q in range(kb):
        x = jnp.dot(w2_ref[...], _fine_rows(a_ref, (0,), q).astype(BF16), preferred_element_type=F32)
        o_ref[0, 0, :, q, :] = x[:l2]
        o_ref[0, 1, :, q, :] = x[l2:]


def _ct_spectrum(a, consts, c, kb=8):
    b = a.shape[0]
    l2, h1 = consts["l2"], consts["h1"]
    kb = min(kb, h1)
    blk = pl.BlockSpec((1, 2, l2, kb, c), lambda bi, i: (bi, 0, 0, i, 0))
    return pl.pallas_call(
        functools.partial(_ct_spec_body, l2=l2, kb=kb),
        grid=(b, h1 // kb),
        in_specs=[blk, pl.BlockSpec((2 * l2, 2 * l2), lambda bi, i: (0, 0))],
        out_specs=blk,
        out_shape=jax.ShapeDtypeStruct((b, 2, l2, h1, c), F32),
        compiler_params=_params("parallel", "parallel"),
        name="ct_spectrum",
    )(a, consts["w2"])


def _ct_mid_body(a_ref, h_ref, w2_ref, w2i_ref, tc_ref, ts_ref, o_ref, *, l2, kb):
    for q in range(kb):
        x = jnp.dot(w2_ref[...], _fine_rows(a_ref, (0,), q).astype(BF16), preferred_element_type=F32)
        xr, xi = x[:l2], x[l2:]
        hr, hi = h_ref[0, 0, :, q, :], h_ref[0, 1, :, q, :]
        y = jnp.concatenate([xr * hr - xi * hi, xr * hi + xi * hr], axis=0).astype(BF16)
        bm = jnp.dot(w2i_ref[...], y, preferred_element_type=F32)
        br, bi = bm[:l2], bm[l2:]
        tc, ts = tc_ref[q], ts_ref[q]
        o_ref[0, 0, :, q, :] = br * tc - bi * ts
        o_ref[0, 1, :, q, :] = br * ts + bi * tc


def _ct_mid(a, hspec, order, consts, c, kb=16):
    b = a.shape[0]
    l2, h1 = consts["l2"], consts["h1"]
    kb = min(kb, h1)
    blk = pl.BlockSpec((1, 2, l2, kb, c), lambda bi, i: (bi, 0, 0, i, 0))
    return pl.pallas_call(
        functools.partial(_ct_mid_body, l2=l2, kb=kb),
        grid=(b, h1 // kb),
        in_specs=[
            blk,
            pl.BlockSpec((1, 2, l2, kb, c), lambda bi, i: (order, 0, 0, i, 0)),
            pl.BlockSpec((2 * l2, 2 * l2), lambda bi, i: (0, 0)),
            pl.BlockSpec((2 * l2, 2 * l2), lambda bi, i: (0, 0)),
            pl.BlockSpec((kb, l2, 1), lambda bi, i: (i, 0, 0)),
            pl.BlockSpec((kb, l2, 1), lambda bi, i: (i, 0, 0)),
        ],
        out_specs=blk,
        out_shape=jax.ShapeDtypeStruct((b, 2, l2, h1, c), F32),
        compiler_params=_params("parallel", "parallel"),
        name="ct_mid",
    )(a, hspec, consts["w2"], consts["w2i"], consts["tw_cos_inv"], consts["tw_sin_inv"])


def _ct_inv1_body(b_ref, w_ref, u_ref, gate_ref, skip_ref, o_ref, *, lb, c, by_position):
    for q in range(lb):
        bb = jnp.concatenate([b_ref[0, 0, q], b_ref[0, 1, q]], axis=0).astype(BF16)
        y = jnp.dot(w_ref[...], bb, preferred_element_type=F32)
        cs = slice(q * c, (q + 1) * c)
        out = gate_ref[0, 0, :, cs] * (y + u_ref[0, 0, :, cs] * skip_ref[...])
        if by_position:
            o_ref[0, :, q, :] = out
        else:
            o_ref[0, :, cs] = out


def _ct_inv1(bsp, u, u_idx, gate, gate_idx, skip, consts, c, by_position, lb=8):
    b = bsp.shape[0]
    l2, h1 = consts["l2"], consts["h1"]
    lb = min(lb, l2)
    if by_position:
        out_spec = pl.BlockSpec((1, h1, lb, c), lambda bi, i: (bi, 0, i, 0))
        out_shape = jax.ShapeDtypeStruct((b, h1, l2, c), F32)
    else:
        out_spec = pl.BlockSpec((1, h1, lb * c), lambda bi, i: (bi, 0, i))
        out_shape = jax.ShapeDtypeStruct((b, h1, l2 * c), F32)
    return pl.pallas_call(
        functools.partial(_ct_inv1_body, lb=lb, c=c, by_position=by_position),
        grid=(b, l2 // lb),
        in_specs=[
            pl.BlockSpec((1, 2, lb, h1, c), lambda bi, i: (bi, 0, i, 0, 0)),
            pl.BlockSpec((h1, 2 * h1), lambda bi, i: (0, 0)),
            pl.BlockSpec((1, 1, h1, lb * c), lambda bi, i: (u_idx, bi, 0, i)),
            pl.BlockSpec((1, 1, h1, lb * c), lambda bi, i: (gate_idx, bi, 0, i)),
            pl.BlockSpec((1, c), lambda bi, i: (0, 0)),
        ],
        out_specs=out_spec,
        out_shape=out_shape,
        compiler_params=_params("parallel", "parallel"),
        name="ct_inv1",
    )(bsp, consts["w1i"], u, gate, skip.reshape(1, c))


def _hyena(u, conv_w, conv_b, f_w1, f_b1, f_w2, f_b2, f_w3, f_freq, skip):
    b, n, c3 = u.shape
    c = c3 // 3
    consts = _dft_constants(n)
    l1, l2, h1 = consts["l1"], consts["l2"], consts["h1"]
    filt, norms = _hyena_filters(n, f_w1, f_b1, f_w2, f_b2, f_w3, f_freq, c)
    fa = _ct_fwd1(filt.reshape(1, HY_ORDER, l1, l2 * c), 0, consts, c, norms=norms.reshape(HY_ORDER, 1, c))
    hspec = _ct_spectrum(fa, consts, c)
    parts = _short_conv(u, conv_w, conv_b, l2)

    def long_conv_gated(x_stack, x_idx, gate_idx, order, by_position):
        a = _ct_fwd1(x_stack, x_idx, consts, c)
        bsp = _ct_mid(a, hspec, order, consts, c)
        return _ct_inv1(bsp, x_stack, x_idx, parts, gate_idx, skip[order], consts, c, by_position)

    z = long_conv_gated(parts, 0, 1, 0, False)
    return long_conv_gated(z[None], 0, 2, 1, True).reshape(b, n, c)


def _outproj_body(*refs, n_in):
    ins = refs[:n_in]
    ws = refs[n_in:2 * n_in]
    x_ref, gate_ref, g_ref, sc_ref, sh_ref, rhi_ref, rlo_ref, rb_ref, xo_ref, h_ref, rt_ref = refs[2 * n_in:]
    y = None
    for a_ref, w_ref in zip(ins, ws):
        t = jnp.dot(a_ref[0].astype(BF16), w_ref[...], preferred_element_type=F32)
        y = t if y is None else y + t
    x = x_ref[0] + gate_ref[0] * y
    xo_ref[0] = x
    h = x * lax.rsqrt(jnp.mean(x * x, axis=-1, keepdims=True) + NORM_EPS) * g_ref[...]
    h = h * (1.0 + sc_ref[0]) + sh_ref[0]
    hi = h.astype(BF16)
    h_ref[0] = hi
    lo = (h - hi.astype(F32)).astype(BF16)
    lg = (jnp.dot(hi, rhi_ref[...], preferred_element_type=F32)
          + jnp.dot(lo, rhi_ref[...], preferred_element_type=F32)
          + jnp.dot(hi, rlo_ref[...], preferred_element_type=F32) + rb_ref[...])
    rt_ref[0] = _route(lg)


def _route(lg):
    lane = lax.broadcasted_iota(jnp.int32, lg.shape, 1)
    lane_f = lane.astype(F32)
    neg = -jnp.inf

    def top(v):
        m = jnp.max(v, axis=-1, keepdims=True)
        return m, jnp.min(jnp.where(v == m, lane_f, float(LANES)), axis=-1, keepdims=True)

    gl = jnp.where(lane < N_GROUPS, lg, neg)
    gmax, grp = top(gl)
    p_grp = 1.0 / jnp.sum(jnp.exp(gl - gmax), axis=-1, keepdims=True)
    first = N_GROUPS + grp * EXP_PER_GROUP
    el = jnp.where((lane_f >= first) & (lane_f < first + EXP_PER_GROUP), lg, neg)
    m1, i1 = top(el)
    m2, i2 = top(jnp.where(lane_f == i1, neg, el))
    e2 = jnp.exp(m2 - m1)
    den = 1.0 + e2
    vals = (i1 - N_GROUPS, i2 - N_GROUPS, p_grp * (1.0 / den), p_grp * (e2 / den))
    out = jnp.zeros(lg.shape, F32)
    for k, v in enumerate(vals):
        out = jnp.where(lane == k, v, out)
    return out


def _outproj(ins, ws, x, gate, g, scale, shift, r_hi, r_lo, r_b, tm=512):
    b, n, d = x.shape
    tm = min(tm, n)
    bm = gate.shape[0]
    mod_map = (lambda bi, i: (bi, 0, 0)) if bm > 1 else (lambda bi, i: (0, 0, 0))
    row = lambda wd: pl.BlockSpec((1, tm, wd), lambda bi, i: (bi, i, 0))
    full = lambda a: pl.BlockSpec(a.shape, lambda bi, i: (0,) * a.ndim)
    mod = pl.BlockSpec((1, 1, d), mod_map)
    return pl.pallas_call(
        functools.partial(_outproj_body, n_in=len(ins)),
        grid=(b, n // tm),
        in_specs=([row(a.shape[-1]) for a in ins] + [full(w) for w in ws]
                  + [row(d), mod, pl.BlockSpec((1, d), lambda bi, i: (0, 0)), mod, mod,
                     full(r_hi), full(r_lo), full(r_b)]),
        out_specs=[row(d), row(d), row(LANES)],
        out_shape=[jax.ShapeDtypeStruct((b, n, d), F32), jax.ShapeDtypeStruct((b, n, d), BF16),
                   jax.ShapeDtypeStruct((b, n, LANES), F32)],
        compiler_params=_params("parallel", "parallel"),
        name="outproj",
    )(*ins, *ws, x, gate, g.reshape(1, d), scale, shift, r_hi, r_lo, r_b)


def _rank_body(rt_ref, tri_ref, upper_ref, pos_ref, cnt_ref):
    rt = rt_ref[...]
    lane_i = lax.broadcasted_iota(jnp.int32, rt.shape, 1)
    lane = lane_i.astype(F32)
    oh_a = lane == rt[:, 0:1]
    oh_b = lane == rt[:, 1:2]
    one_a = jnp.where(oh_a, 1.0, 0.0)
    one_b = jnp.where(oh_b, 1.0, 0.0)
    before_a = jnp.dot(tri_ref[...], one_a.astype(BF16), preferred_element_type=F32)
    before_b = jnp.dot(tri_ref[...], one_b.astype(BF16), preferred_element_type=F32)
    tot_a = jnp.sum(one_a, axis=0, keepdims=True)
    cnt = tot_a + jnp.sum(one_b, axis=0, keepdims=True)
    padded = jnp.floor((cnt + (CHUNK_ROWS - 1)) * (1.0 / CHUNK_ROWS)) * CHUNK_ROWS
    first = jnp.dot(jnp.broadcast_to(padded, (8, LANES)).astype(BF16), upper_ref[...],
                    preferred_element_type=F32)[0:1]
    pos_a = jnp.sum(jnp.where(oh_a, before_a + first, 0.0), axis=-1, keepdims=True)
    pos_b = jnp.sum(jnp.where(oh_b, before_b + first + tot_a, 0.0), axis=-1, keepdims=True)
    is_gate = (lane_i >= TOP_K) & (lane_i < 2 * TOP_K)
    pos_ref[...] = jnp.where(lane_i == 0, pos_a, jnp.where(lane_i == 1, pos_b, jnp.where(is_gate, rt, 0.0)))
    cnt_ref[0] = jnp.broadcast_to(cnt, (8, LANES))


def _rank(route):
    t = route.shape[0]
    tm = TOKEN_TILE
    tri = jnp.asarray(np.tril(np.ones((tm, tm)), -1), dtype=BF16)
    upper = jnp.asarray(np.triu(np.ones((LANES, LANES)), 1), dtype=BF16)
    return pl.pallas_call(
        _rank_body,
        grid=(t // tm,),
        in_specs=[pl.BlockSpec((tm, LANES), lambda i: (i, 0)), pl.BlockSpec((tm, tm), lambda i: (0, 0)),
                  pl.BlockSpec((LANES, LANES), lambda i: (0, 0))],
        out_specs=[pl.BlockSpec((tm, LANES), lambda i: (i, 0)), pl.BlockSpec((1, 8, LANES), lambda i: (i, 0, 0))],
        out_shape=[jax.ShapeDtypeStruct((t, LANES), F32), jax.ShapeDtypeStruct((t // tm, 8, LANES), F32)],
        compiler_params=_params("parallel"),
        name="moe_rank",
    )(route, tri, upper)


def _chunk_tables(cnt, n_blocks):
    padded = (cnt + CHUNK_ROWS - 1) // CHUNK_ROWS * CHUNK_ROWS
    run_end = jnp.cumsum(padded, axis=1)
    run_start = run_end - padded
    seg_rows = jnp.sum(padded, axis=0)
    seg_rows = (seg_rows + EXPERT_ROWS - 1) // EXPERT_ROWS * EXPERT_ROWS
    seg_end = jnp.cumsum(seg_rows)
    dst_start = (seg_end - seg_rows)[None, :] + jnp.cumsum(padded, axis=0) - padded
    row0 = jnp.arange(BUF_CHUNKS, dtype=jnp.int32) * CHUNK_ROWS
    chunk_exp = jnp.minimum(jnp.sum(run_end[:, None, :] <= row0[None, :, None], axis=-1), N_EXPERTS - 1)
    onehot = chunk_exp[:, :, None] == jnp.arange(N_EXPERTS, dtype=jnp.int32)[None, None, :]
    dst = jnp.sum(jnp.where(onehot, (dst_start - run_start)[:, None, :], 0), axis=-1) + row0[None, :]
    n_chunks = run_end[:, -1:] // CHUNK_ROWS
    table = jnp.concatenate(
        [dst, n_chunks, jnp.zeros((cnt.shape[0], TABLE_WORDS - BUF_CHUNKS - 1), jnp.int32)], axis=1)
    block_row0 = jnp.arange(n_blocks, dtype=jnp.int32) * EXPERT_ROWS
    block_exp = jnp.minimum(jnp.sum(seg_end[None, :] <= block_row0[:, None], axis=1), N_EXPERTS - 1)
    n_used = (seg_end[-1] // EXPERT_ROWS).reshape(1)
    return table.astype(jnp.int32), block_exp.astype(jnp.int32), n_used.astype(jnp.int32)


def _pack_halves(x):
    w = x.shape[-1] // 2
    lo = lax.bitcast_convert_type(x[:, :w].astype(BF16).astype(F32), jnp.int32)
    hi = lax.bitcast_convert_type(x[:, w:].astype(BF16).astype(F32), jnp.int32)
    return lax.shift_right_logical(lo, jnp.int32(16)) | (hi & jnp.int32(-65536))


def _unpack_halves(p):
    lo = lax.bitcast_convert_type(lax.shift_left(p, jnp.int32(16)), F32)
    hi = lax.bitcast_convert_type(p & jnp.int32(-65536), F32)
    return jnp.concatenate([lo, hi], axis=-1).astype(BF16)


def _start_chunks(tab_ref, tile, make_copies):
    base = tile * TABLE_WORDS

    def issue(c, carry):
        for cp in make_copies(pl.multiple_of(c * CHUNK_ROWS, CHUNK_ROWS),
                              pl.multiple_of(tab_ref[base + c], CHUNK_ROWS)):
            cp.start()
        return carry

    lax.fori_loop(0, tab_ref[base + BUF_CHUNKS], issue, 0)


def _wait_chunks(tab_ref, tile, make_copies):
    def drain(c, carry):
        for cp in make_copies(0, 0):
            cp.wait()
        return carry

    lax.fori_loop(0, tab_ref[tile * TABLE_WORDS + BUF_CHUNKS], drain, 0)


def _dispatch_body(tab_ref, h_ref, pos_ref, xs_in, gs_in, xs_out, gs_out, buf, gbuf, sem_rows, sem_gates,
                   *, tile0, n_steps):
    del xs_in, gs_in
    step = pl.program_id(0)

    def copies(at_step):
        slot = at_step % 2
        return lambda src, dst: (
            pltpu.make_async_copy(buf.at[slot, pl.ds(src, CHUNK_ROWS)], xs_out.at[pl.ds(dst, CHUNK_ROWS)],
                                  sem_rows.at[slot]),
            pltpu.make_async_copy(gbuf.at[slot, pl.ds(src, CHUNK_ROWS)], gs_out.at[pl.ds(dst, CHUNK_ROWS)],
                                  sem_gates.at[slot]))

    @pl.when(step >= 2)
    def _():
        _wait_chunks(tab_ref, tile0 + step - 2, copies(step - 2))

    tm = h_ref.shape[0]
    row = lax.broadcasted_iota(jnp.int32, (BUF_ROWS, tm), 0).astype(F32)
    oh_a = row == pos_ref[0, 0:1, :]
    oh_b = row == pos_ref[0, 1:2, :]
    buf[step % 2] = _pack_halves(jnp.dot(jnp.where(oh_a | oh_b, 1.0, 0.0).astype(BF16), h_ref[...],
                                         preferred_element_type=F32))
    gate = jnp.sum(jnp.where(oh_a, pos_ref[0, 2:3, :], 0.0) + jnp.where(oh_b, pos_ref[0, 3:4, :], 0.0),
                   axis=-1, keepdims=True)
    gbuf[step % 2] = jnp.broadcast_to(gate, gbuf.shape[1:])
    _start_chunks(tab_ref, tile0 + step, copies(step))

    @pl.when(step == n_steps - 1)
    def _():
        @pl.when(step >= 1)
        def _():
            _wait_chunks(tab_ref, tile0 + step - 1, copies(step - 1))

        _wait_chunks(tab_ref, tile0 + step, copies(step))


def _dispatch(h, pos_t, table, xs, gs, tile0):
    t, d = h.shape
    tm = TOKEN_TILE
    n_steps = t // tm
    grid_spec = pltpu.PrefetchScalarGridSpec(
        num_scalar_prefetch=1,
        grid=(n_steps,),
        in_specs=[
            pl.BlockSpec((tm, d), lambda i, tab: (i, 0)),
            pl.BlockSpec((1, 8, tm), lambda i, tab: (tile0 + i, 0, 0)),
            pl.BlockSpec(memory_space=pl.ANY),
            pl.BlockSpec(memory_space=pl.ANY),
        ],
        out_specs=[pl.BlockSpec(memory_space=pl.ANY), pl.BlockSpec(memory_space=pl.ANY)],
        scratch_shapes=[
            pltpu.VMEM((2, BUF_ROWS, d // 2), jnp.int32),
            pltpu.VMEM((2, BUF_ROWS, LANES), F32),
            pltpu.SemaphoreType.DMA((2,)),
            pltpu.SemaphoreType.DMA((2,)),
        ],
    )
    return pl.pallas_call(
        functools.partial(_dispatch_body, tile0=tile0, n_steps=n_steps),
        grid_spec=grid_spec,
        out_shape=[jax.ShapeDtypeStruct(xs.shape, xs.dtype), jax.ShapeDtypeStruct(gs.shape, gs.dtype)],
        input_output_aliases={3: 0, 4: 1},
        compiler_params=_params("arbitrary"),
        name="moe_dispatch",
    )(table.reshape(-1), h, pos_t, xs, gs)


def _expert_body(bexp_ref, nused_ref, x_ref, g_ref, wg_ref, wu_ref, wd_ref, o_ref):
    @pl.when(pl.program_id(0) < nused_ref[0])
    def _():
        xb = _unpack_halves(x_ref[...])
        gt = jnp.dot(xb, wg_ref[0], preferred_element_type=F32)
        up = jnp.dot(xb, wu_ref[0], preferred_element_type=F32)
        hid = (gt * jax.nn.sigmoid(gt) * up).astype(BF16)
        o_ref[...] = _pack_halves(jnp.dot(hid, wd_ref[0], preferred_element_type=F32) * g_ref[:, 0:1])

    @pl.when(pl.program_id(0) >= nused_ref[0])
    def _():
        o_ref[...] = jnp.zeros_like(o_ref)


def _experts(xs, gs, block_exp, n_used, w_gate, w_up, w_down):
    rows, dp = xs.shape
    d, de = w_gate.shape[1:]
    used = lambda i, be, nu: (jnp.minimum(i, nu[0] - 1), 0)
    grid_spec = pltpu.PrefetchScalarGridSpec(
        num_scalar_prefetch=2,
        grid=(rows // EXPERT_ROWS,),
        in_specs=[
            pl.BlockSpec((EXPERT_ROWS, dp), used),
            pl.BlockSpec((EXPERT_ROWS, LANES), used),
            pl.BlockSpec((1, d, de), lambda i, be, nu: (be[i], 0, 0)),
            pl.BlockSpec((1, d, de), lambda i, be, nu: (be[i], 0, 0)),
            pl.BlockSpec((1, de, d), lambda i, be, nu: (be[i], 0, 0)),
        ],
        out_specs=pl.BlockSpec((EXPERT_ROWS, dp), lambda i, be, nu: (i, 0)),
    )
    return pl.pallas_call(
        _expert_body,
        grid_spec=grid_spec,
        out_shape=jax.ShapeDtypeStruct(xs.shape, xs.dtype),
        compiler_params=_params("arbitrary"),
        name="experts",
    )(block_exp, n_used, xs, gs, w_gate, w_up, w_down)


def _combine_body(tab_ref, ys_hbm, x_ref, gate_ref, pos_ref, fg_ref, o_ref, ybuf, sem_rows,
                  *, tile0, n_tiles, n_steps, final_norm):
    step = pl.program_id(0) * n_tiles + pl.program_id(1)

    def copies(at_step):
        slot = at_step % 2
        return lambda dst, src: (
            pltpu.make_async_copy(ys_hbm.at[pl.ds(src, CHUNK_ROWS)], ybuf.at[slot, pl.ds(dst, CHUNK_ROWS)],
                                  sem_rows.at[slot]),)

    @pl.when(step == 0)
    def _():
        ybuf[...] = jnp.zeros_like(ybuf)
        _start_chunks(tab_ref, tile0, copies(0))

    @pl.when(step + 1 < n_steps)
    def _():
        _start_chunks(tab_ref, tile0 + step + 1, copies(step + 1))

    _wait_chunks(tab_ref, tile0 + step, copies(step))
    tm = x_ref.shape[1]
    col = lax.broadcasted_iota(jnp.int32, (tm, BUF_ROWS), 1).astype(F32)
    pick = jnp.where((col == pos_ref[:, 0:1]) | (col == pos_ref[:, 1:2]), 1.0, 0.0).astype(BF16)
    out = x_ref[0] + gate_ref[0] * jnp.dot(pick, _unpack_halves(ybuf[step % 2]), preferred_element_type=F32)
    if final_norm:
        out = out * lax.rsqrt(jnp.mean(out * out, axis=-1, keepdims=True) + NORM_EPS) * fg_ref[...]
    o_ref[0] = out


def _combine(ys, table, pos, x, gate, tile0, final_g=None):
    b, n, d = x.shape
    tm = min(TOKEN_TILE, n)
    n_tiles = n // tm
    bm = gate.shape[0]
    mod_map = (lambda bi, i, tab: (bi, 0, 0)) if bm > 1 else (lambda bi, i, tab: (0, 0, 0))
    final_norm = final_g is not None
    fg = final_g.reshape(1, d) if final_norm else jnp.ones((1, d), F32)
    grid_spec = pltpu.PrefetchScalarGridSpec(
        num_scalar_prefetch=1,
        grid=(b, n_tiles),
        in_specs=[
            pl.BlockSpec(memory_space=pl.ANY),
            pl.BlockSpec((1, tm, d), lambda bi, i, tab: (bi, i, 0)),
            pl.BlockSpec((1, 1, d), mod_map),
            pl.BlockSpec((tm, LANES), lambda bi, i, tab: (tile0 + bi * n_tiles + i, 0)),
            pl.BlockSpec((1, d), lambda bi, i, tab: (0, 0)),
        ],
        out_specs=pl.BlockSpec((1, tm, d), lambda bi, i, tab: (bi, i, 0)),
        scratch_shapes=[
            pltpu.VMEM((2, BUF_ROWS, d // 2), jnp.int32),
            pltpu.SemaphoreType.DMA((2,)),
        ],
    )
    return pl.pallas_call(
        functools.partial(_combine_body, tile0=tile0, n_tiles=n_tiles, n_steps=b * n_tiles,
                          final_norm=final_norm),
        grid_spec=grid_spec,
        out_shape=jax.ShapeDtypeStruct((b, n, d), F32),
        compiler_params=_params("arbitrary", "arbitrary"),
        name="moe_combine",
    )(table.reshape(-1), ys, x, gate, pos, fg)


def _router_weights(wg, bg, we, be):
    d = wg.shape[0]
    w = jnp.zeros((d, LANES), F32).at[:, :N_GROUPS].set(wg).at[:, N_GROUPS:N_GROUPS + N_EXPERTS].set(we)
    bias = jnp.zeros((1, LANES), F32).at[0, :N_GROUPS].set(bg).at[0, N_GROUPS:N_GROUPS + N_EXPERTS].set(be)
    hi = w.astype(BF16)
    lo = (w - hi.astype(F32)).astype(BF16)
    return hi, lo, bias


def kernel(x, c, ctx, c_ctx, ada_w, ada_b, norm1_g, norm2_g, ev_w_in, ev_w_out, hy_conv_w, hy_conv_b, hy_f_w1, hy_f_b1, hy_f_w2, hy_f_b2, hy_f_w3, hy_f_freq, hy_skip, swa_sink, od_w_qkv, od_w_out, od_q_norm_g, od_k_norm_g, rt_group_w, rt_group_b, rt_exp_w, rt_exp_b, moe_w_gate, moe_w_up, moe_w_down, final_norm_g):
    b, n, d = x.shape
    lc = ctx.shape[1]
    depth = ada_w.shape[0]
    rope = _rope_tables(n)
    xc = ctx
    sc = jax.nn.silu(c)
    scc = jax.nn.silu(c_ctx)
    q_scale = HEAD_DIM ** -0.5
    for layer in range(depth):
        with_ctx = layer < depth - 1
        mod = (sc @ ada_w[layer] + ada_b[layer]).reshape(b, N_MOD, 1, d)
        modc = (scc @ ada_w[layer] + ada_b[layer]).reshape(1, N_MOD, 1, d)
        m = [mod[:, k] for k in range(N_MOD)]
        mc = [modc[:, k] for k in range(N_MOD)]
        r_hi, r_lo, r_b = _router_weights(rt_group_w[layer], rt_group_b[layer], rt_exp_w[layer], rt_exp_b[layer])
        wgt, wup, wdn = (moe_w_gate[layer].astype(BF16), moe_w_up[layer].astype(BF16),
                         moe_w_down[layer].astype(BF16))
        if layer % 2 == 0:
            e = layer // 2
            c_hy = hy_conv_w.shape[-1] // 3
            d_hy = 3 * c_hy
            hq = swa_sink.shape[-1]
            d_q = hq * HEAD_DIM
            hkv = hq // 4
            d_kv = hkv * HEAD_DIM
            w_in = ev_w_in[e].astype(BF16)
            w_out = ev_w_out[e].astype(BF16)
            hy_args = (hy_conv_w[e], hy_conv_b[e], hy_f_w1[e], hy_f_b1[e], hy_f_w2[e], hy_f_b2[e],
                       hy_f_w3[e], hy_f_freq[e], hy_skip[e])
            u, q, k, v = _proj(x, norm1_g[layer], m[1], m[0], w_in, [
                (0, d_hy, "f32", None, False, 1.0),
                (d_hy, d_q, "qk", None, True, q_scale),
                (d_hy + d_q, d_kv, "qk", None, True, 1.0),
                (d_hy + d_q + d_kv, d_kv, "bf16", None, False, 1.0)], rope_tabs=rope)
            if with_ctx:
                uc, qc, kc, vc = _proj(xc, norm1_g[layer], mc[1], mc[0], w_in, [
                    (0, d_hy, "f32", None, False, 1.0),
                    (d_hy, d_q, "qk", None, False, q_scale),
                    (d_hy + d_q, d_kv, "bf16", None, False, 1.0),
                    (d_hy + d_q + d_kv, d_kv, "bf16", None, False, 1.0)])
            else:
                kc, vc = _proj(xc, norm1_g[layer], mc[1], mc[0], w_in, [
                    (d_hy + d_q, d_kv, "bf16", None, False, 1.0),
                    (d_hy + d_q + d_kv, d_kv, "bf16", None, False, 1.0)])
            y_hy = _hyena(u, *hy_args)
            y_att = _windowed_attention(q, k, v, kc, vc, swa_sink[e], hkv)
            mix_in, mix_w = [y_hy, y_att], [w_out[:c_hy], w_out[c_hy:]]
            if with_ctx:
                yc_hy = _hyena(uc, *hy_args)
                yc_att = _full_attention(qc, kc, vc, hkv, sink=swa_sink[e])
                mixc_in = [yc_hy, yc_att]
        else:
            o = layer // 2
            hkv = od_w_qkv.shape[-1] // HEAD_DIM // 6
            hq = 4 * hkv
            d_q = hq * HEAD_DIM
            d_kv = hkv * HEAD_DIM
            w_qkv = od_w_qkv[o].astype(BF16)
            w_out = od_w_out[o].astype(BF16)
            norm_g = jnp.zeros((8, LANES), F32).at[0].set(jnp.tile(od_q_norm_g[o], 2)).at[1].set(
                jnp.tile(od_k_norm_g[o], 2))
            q, k, v = _proj(x, norm1_g[layer], m[1], m[0], w_qkv, [
                (0, d_q, "qk", 0, True, q_scale),
                (d_q, d_kv, "qk", 1, True, 1.0),
                (d_q + d_kv, d_kv, "bf16", None, False, 1.0)], rope_tabs=rope, norm_g=norm_g)
            if with_ctx:
                qc, kc, vc = _proj(xc, norm1_g[layer], mc[1], mc[0], w_qkv, [
                    (0, d_q, "qk", 0, False, q_scale),
                    (d_q, d_kv, "qk", 1, False, 1.0),
                    (d_q + d_kv, d_kv, "bf16", None, False, 1.0)], norm_g=norm_g)
            else:
                kc, vc = _proj(xc, norm1_g[layer], mc[1], mc[0], w_qkv, [
                    (d_q, d_kv, "qk", 1, False, 1.0),
                    (d_q + d_kv, d_kv, "bf16", None, False, 1.0)], norm_g=norm_g)
            y_att = _full_attention(q, jnp.concatenate([kc, k], axis=1), jnp.concatenate([vc, v], axis=1), hkv)
            mix_in, mix_w = [y_att], [w_out]
            if with_ctx:
                mixc_in = [_full_attention(qc, kc, vc, hkv)]
        x, h2, rt = _outproj(mix_in, mix_w, x, m[2], norm2_g[layer], m[4], m[3], r_hi, r_lo, r_b)
        route_flat = rt.reshape(b * n, LANES)
        if with_ctx:
            xc, h2c, rtc = _outproj(mixc_in, mix_w, xc, mc[2], norm2_g[layer], mc[4], mc[3], r_hi, r_lo, r_b)
            route_flat = jnp.concatenate([route_flat, rtc.reshape(b * lc, LANES)], axis=0)
        n_tok = route_flat.shape[0]
        n_tiles = n_tok // TOKEN_TILE
        lat_tiles = b * n // TOKEN_TILE
        max_rows = n_tok * TOP_K + n_tiles * N_EXPERTS * (CHUNK_ROWS - 1)
        n_blocks = -(-max_rows // EXPERT_ROWS) + N_EXPERTS
        pos, cnt = _rank(route_flat)
        table, block_exp, n_used = _chunk_tables(cnt[:, 0, :N_EXPERTS].astype(jnp.int32), n_blocks)
        pos_t = jnp.swapaxes(pos[:, :8].reshape(n_tiles, TOKEN_TILE, 8), 1, 2)
        xs = jnp.zeros((n_blocks * EXPERT_ROWS, d // 2), jnp.int32)
        gs = jnp.zeros((n_blocks * EXPERT_ROWS, LANES), F32)
        xs, gs = _dispatch(h2.reshape(b * n, d), pos_t, table, xs, gs, 0)
        if with_ctx:
            xs, gs = _dispatch(h2c.reshape(b * lc, d), pos_t, table, xs, gs, lat_tiles)
        ys = _experts(xs, gs, block_exp, n_used, wgt, wup, wdn)
        x = _combine(ys, table, pos, x, m[5], 0, final_g=None if with_ctx else final_norm_g)
        if with_ctx:
            xc = _combine(ys, table, pos, xc.reshape(b * lc // TOKEN_TILE, TOKEN_TILE, d), mc[5],
                          lat_tiles).reshape(b, lc, d)
    return x
```

```python
import functools
import math

import numpy as np
import jax
import jax.numpy as jnp
from jax import lax
from jax.experimental import pallas as pl
from jax.experimental.pallas import tpu as pltpu

F32 = jnp.float32
BF16 = jnp.bfloat16

HEAD_DIM = 64
GRID_W = 64
ROPE_BASE = 10000.0
Q_BLOCK = 128
NORM_EPS = 1e-6
N_MOD = 6
HY_ORDER = 2
HY_BANDS = 16
HY_DIRS = 2
HY_DECAY_TARGET = 1e-2
HY_FAST_DECAY = 0.3
HY_SLOW_DECAY = 1.5
HY_FILTER_EPS = 1e-6
SWA_WINDOW = 128
N_GROUPS = 4
EXP_PER_GROUP = 8
N_EXPERTS = N_GROUPS * EXP_PER_GROUP
TOP_K = 2
EXPERT_ROWS = 512
SWA_BLOCKS_PER_STEP = 1
LOOP_GROUP = 4
FULL_ATTN_Q_ROWS = 256
FULL_ATTN_UNIT_ROWS = 512
TOKEN_TILE = 512
CHUNK_ROWS = 8
BUF_ROWS = 1280
BUF_CHUNKS = BUF_ROWS // CHUNK_ROWS
TABLE_WORDS = 256

LANES = 128
VMEM_LIMIT_BYTES = 56 * 1024 * 1024


def _params(*sem):
    return pltpu.CompilerParams(dimension_semantics=sem, vmem_limit_bytes=VMEM_LIMIT_BYTES)


def _rope_tables(n):
    d_axis = HEAD_DIM // 2
    t = jnp.arange(n)
    inv = ROPE_BASE ** (-jnp.arange(0, d_axis, 2, dtype=F32) / d_axis)
    ang_r = (t // GRID_W).astype(F32)[:, None] * inv[None, :]
    ang_c = (t % GRID_W).astype(F32)[:, None] * inv[None, :]
    cos = jnp.concatenate([jnp.cos(ang_r)] * 2 + [jnp.cos(ang_c)] * 2, axis=-1)
    sin = jnp.concatenate([-jnp.sin(ang_r), jnp.sin(ang_r), -jnp.sin(ang_c), jnp.sin(ang_c)], axis=-1)
    return jnp.tile(cos, (1, 2)), jnp.tile(sin, (1, 2))


def _head_mean_matrix():
    i = np.arange(LANES)
    return jnp.asarray((i[:, None] // HEAD_DIM == i[None, :] // HEAD_DIM) / HEAD_DIM, dtype=BF16)


def _proj_body(x_ref, g_ref, sc_ref, sh_ref, w_ref, cos_ref, sin_ref, ng_ref, bd_ref, *out_refs, segs):
    x = x_ref[0]
    h = x * lax.rsqrt(jnp.mean(x * x, axis=-1, keepdims=True) + NORM_EPS) * g_ref[...]
    hb = (h * (1.0 + sc_ref[0]) + sh_ref[0]).astype(BF16)
    for o_ref, (c0, width, kind, norm_row, rope, out_scale) in zip(out_refs, segs):
        seg = jnp.dot(hb, w_ref[:, c0:c0 + width], preferred_element_type=F32)
        if kind == "f32":
            o_ref[0] = seg
            continue
        if kind == "bf16":
            o_ref[0] = seg.astype(BF16)
            continue
        for j in range(width // LANES):
            ch = seg[:, j * LANES:(j + 1) * LANES]
            if norm_row is not None:
                sq = ch * ch
                hi = sq.astype(BF16)
                lo = (sq - hi.astype(F32)).astype(BF16)
                ms = (jnp.dot(hi, bd_ref[...], preferred_element_type=F32)
                      + jnp.dot(lo, bd_ref[...], preferred_element_type=F32))
                ch = ch * lax.rsqrt(ms + NORM_EPS) * ng_ref[norm_row:norm_row + 1, :]
            if rope:
                lane = lax.broadcasted_iota(jnp.int32, ch.shape, 1)
                partner = jnp.where(lane % 32 < 16, pltpu.roll(ch, LANES - 16, 1), pltpu.roll(ch, 16, 1))
                ch = ch * cos_ref[...] + partner * sin_ref[...]
            if out_scale != 1.0:
                ch = ch * out_scale
            o_ref[0, :, j * LANES:(j + 1) * LANES] = ch.astype(BF16)


def _proj(x, g, scale, shift, w, segs, rope_tabs=None, norm_g=None, tm=512):
    b, n, d = x.shape
    tm = min(tm, n)
    bm = scale.shape[0]
    mod_map = (lambda bi, i: (bi, 0, 0)) if bm > 1 else (lambda bi, i: (0, 0, 0))
    if rope_tabs is None:
        cos = sin = jnp.zeros((8, LANES), F32)
        tab_spec = pl.BlockSpec((8, LANES), lambda bi, i: (0, 0))
    else:
        cos, sin = rope_tabs
        tab_spec = pl.BlockSpec((tm, LANES), lambda bi, i: (i, 0))
    if norm_g is None:
        norm_g = jnp.ones((8, LANES), F32)
    out_shape = [jax.ShapeDtypeStruct((b, n, s[1]), F32 if s[2] == "f32" else BF16) for s in segs]
    out_specs = [pl.BlockSpec((1, tm, s[1]), lambda bi, i: (bi, i, 0)) for s in segs]
    return pl.pallas_call(
        functools.partial(_proj_body, segs=tuple(segs)),
        grid=(b, n // tm),
        in_specs=[
            pl.BlockSpec((1, tm, d), lambda bi, i: (bi, i, 0)),
            pl.BlockSpec((1, d), lambda bi, i: (0, 0)),
            pl.BlockSpec((1, 1, d), mod_map),
            pl.BlockSpec((1, 1, d), mod_map),
            pl.BlockSpec(w.shape, lambda bi, i: (0, 0)),
            tab_spec,
            tab_spec,
            pl.BlockSpec(norm_g.shape, lambda bi, i: (0, 0)),
            pl.BlockSpec((LANES, LANES), lambda bi, i: (0, 0)),
        ],
        out_specs=out_specs,
        out_shape=out_shape,
        compiler_params=_params("parallel", "parallel"),
        name="proj",
    )(x, g.reshape(1, d), scale, shift, w, cos, sin, norm_g, _head_mean_matrix())


def _stack_heads(q, j, g):
    return jnp.concatenate(
        [q[:, (j * g + gg) * HEAD_DIM:(j * g + gg + 1) * HEAD_DIM] for gg in range(g)], axis=0)


def _values_with_ones(v, hkv):
    b, nk, _ = v.shape
    ones = jnp.zeros((b, nk, hkv, HEAD_DIM), v.dtype).at[..., 0].set(1)
    return jnp.concatenate([v.reshape(b, nk, hkv, HEAD_DIM), ones], axis=-1).reshape(b, nk, 2 * hkv * HEAD_DIM)


def _sink_column(sink_ref, j, g, qb):
    return jnp.concatenate([jnp.full((qb, 1), sink_ref[j * g + gg], F32) for gg in range(g)], axis=0)


def _swa_body(sink_ref, q_ref, kt_ref, v_ref, kct_ref, vc_ref, o_ref, *, n, hkv, g, qb, win, sub):
    rows = lax.broadcasted_iota(jnp.int32, (g * qb, 3 * qb), 0) % qb
    cols = lax.broadcasted_iota(jnp.int32, (g * qb, 3 * qb), 1)
    in_band = jnp.abs(rows + qb - cols) <= win
    for sb in range(sub):
        blk = pl.program_id(1) * sub + sb
        start = pl.multiple_of(blk * qb, qb)
        qs = slice(sb * qb, (sb + 1) * qb)
        q = q_ref[0, qs, :]
        key_pos = cols + (blk - 1) * qb
        valid = in_band & (key_pos >= 0) & (key_pos < n)
        for j in range(hkv):
            hs = slice(j * HEAD_DIM, (j + 1) * HEAD_DIM)
            q4 = _stack_heads(q, j, g)
            s_lat = jnp.dot(q4, kt_ref[0, hs, pl.ds(start, 3 * qb)], preferred_element_type=F32)
            s_lat = jnp.where(valid, s_lat, -jnp.inf)
            s_ctx = jnp.dot(q4, kct_ref[0, hs, :], preferred_element_type=F32)
            s_sink = _sink_column(sink_ref, j, g, qb)
            m = jnp.maximum(jnp.maximum(jnp.max(s_lat, axis=-1, keepdims=True),
                                        jnp.max(s_ctx, axis=-1, keepdims=True)), s_sink)
            e_lat = jnp.exp((s_lat - m).astype(BF16))
            e_ctx = jnp.exp((s_ctx - m).astype(BF16))
            vs = slice(j * 2 * HEAD_DIM, (j + 1) * 2 * HEAD_DIM)
            o = (jnp.dot(e_ctx, vc_ref[0, :, vs], preferred_element_type=F32)
                 + jnp.dot(e_lat, v_ref[0, pl.ds(start, 3 * qb), vs], preferred_element_type=F32))
            o = o[:, :HEAD_DIM] / (o[:, HEAD_DIM:HEAD_DIM + 1] + jnp.exp(s_sink - m))
            for gg in range(g):
                c0 = (j * g + gg) * HEAD_DIM
                o_ref[0, qs, c0:c0 + HEAD_DIM] = o[gg * qb:(gg + 1) * qb].astype(BF16)


def _windowed_attention(q, k, v, kc, vc, sink, hkv):
    b, n, dq = q.shape
    g = dq // HEAD_DIM // hkv
    qb = Q_BLOCK
    lc = kc.shape[1]
    dkv = hkv * HEAD_DIM
    kt = jnp.swapaxes(jnp.pad(k, ((0, 0), (qb, qb), (0, 0))), 1, 2)
    vp = _values_with_ones(jnp.pad(v, ((0, 0), (qb, qb), (0, 0))), hkv)
    vc = _values_with_ones(vc, hkv)
    kct = jnp.swapaxes(kc, 1, 2)
    sub = min(SWA_BLOCKS_PER_STEP, n // qb)
    return pl.pallas_call(
        functools.partial(_swa_body, n=n, hkv=hkv, g=g, qb=qb, win=SWA_WINDOW, sub=sub),
        grid=(b, n // (sub * qb)),
        in_specs=[
            pl.BlockSpec(memory_space=pltpu.SMEM),
            pl.BlockSpec((1, sub * qb, dq), lambda bi, i: (bi, i, 0)),
            pl.BlockSpec((1, dkv, n + 2 * qb), lambda bi, i: (bi, 0, 0)),
            pl.BlockSpec((1, n + 2 * qb, 2 * dkv), lambda bi, i: (bi, 0, 0)),
            pl.BlockSpec((1, dkv, lc), lambda bi, i: (bi, 0, 0)),
            pl.BlockSpec((1, lc, 2 * dkv), lambda bi, i: (bi, 0, 0)),
        ],
        out_specs=pl.BlockSpec((1, sub * qb, dq), lambda bi, i: (bi, i, 0)),
        out_shape=jax.ShapeDtypeStruct((b, n, dq), BF16),
        compiler_params=_params("parallel", "parallel"),
        name="swa",
    )(sink.astype(F32), q, kt, vp, kct, vc)


def _full_attn_body(sink_ref, q_ref, kt_ref, v_ref, o_ref, *, hkv, g, qb, has_sink, unit):
    q = q_ref[0]
    for j in range(hkv):
        hs = slice(j * HEAD_DIM, (j + 1) * HEAD_DIM)
        for u in range(g // unit):
            heads = [j * g + u * unit + t for t in range(unit)]
            qu = jnp.concatenate([q[:, h * HEAD_DIM:(h + 1) * HEAD_DIM] for h in heads], axis=0)
            s = jnp.dot(qu, kt_ref[0, hs, :], preferred_element_type=F32)
            m = jnp.max(s, axis=-1, keepdims=True)
            if has_sink:
                s_sink = jnp.concatenate([jnp.full((qb, 1), sink_ref[h], F32) for h in heads], axis=0)
                m = jnp.maximum(m, s_sink)
            e = jnp.exp((s - m).astype(BF16))
            o = jnp.dot(e, v_ref[0, :, j * 2 * HEAD_DIM:(j + 1) * 2 * HEAD_DIM], preferred_element_type=F32)
            den = o[:, HEAD_DIM:HEAD_DIM + 1]
            if has_sink:
                den = den + jnp.exp(s_sink - m)
            o = o[:, :HEAD_DIM] / den
            for t, h in enumerate(heads):
                o_ref[0, :, h * HEAD_DIM:(h + 1) * HEAD_DIM] = o[t * qb:(t + 1) * qb].astype(BF16)


def _full_attention(q, k, v, hkv, sink=None):
    b, n, dq = q.shape
    g = dq // HEAD_DIM // hkv
    qb = min(FULL_ATTN_Q_ROWS, n)
    unit = max(1, FULL_ATTN_UNIT_ROWS // qb)
    nk = k.shape[1]
    dkv = hkv * HEAD_DIM
    kt = jnp.swapaxes(k, 1, 2)
    has_sink = sink is not None
    sink = jnp.zeros((dq // HEAD_DIM,), F32) if sink is None else sink.astype(F32)
    return pl.pallas_call(
        functools.partial(_full_attn_body, hkv=hkv, g=g, qb=qb, has_sink=has_sink, unit=min(unit, g)),
        grid=(b, n // qb),
        in_specs=[
            pl.BlockSpec(memory_space=pltpu.SMEM),
            pl.BlockSpec((1, qb, dq), lambda bi, i: (bi, i, 0)),
            pl.BlockSpec((1, dkv, nk), lambda bi, i: (bi, 0, 0)),
            pl.BlockSpec((1, nk, 2 * dkv), lambda bi, i: (bi, 0, 0)),
        ],
        out_specs=pl.BlockSpec((1, qb, dq), lambda bi, i: (bi, i, 0)),
        out_shape=jax.ShapeDtypeStruct((b, n, dq), BF16),
        compiler_params=_params("parallel", "parallel"),
        name="full_attn",
    )(sink, q, kt, _values_with_ones(v, hkv))


def _short_conv_body(u_ref, w_ref, b_ref, o_ref, *, h1, l2, c):
    slab = lambda f: u_ref[0, :, f, :]
    row = lax.broadcasted_iota(jnp.int32, (h1, c), 0)
    for f in range(l2):
        prev = slab(f - 1) if f > 0 else jnp.where(row == 0, 0.0, pltpu.roll(slab(l2 - 1), 1, 0))
        nxt = slab(f + 1) if f < l2 - 1 else jnp.where(row == h1 - 1, 0.0, pltpu.roll(slab(0), h1 - 1, 0))
        o_ref[0, 0, :, f * c:(f + 1) * c] = (prev * w_ref[0:1, :] + slab(f) * w_ref[1:2, :]
                                             + nxt * w_ref[2:3, :] + b_ref[...])


def _short_conv(u, w, bias, l2):
    b, n, c3 = u.shape
    c = c3 // 3
    h1 = n // l2
    return pl.pallas_call(
        functools.partial(_short_conv_body, h1=h1, l2=l2, c=c),
        grid=(b, 3),
        in_specs=[
            pl.BlockSpec((1, h1, l2, c), lambda bi, j: (bi, 0, 0, j)),
            pl.BlockSpec((3, c), lambda bi, j: (0, j)),
            pl.BlockSpec((1, c), lambda bi, j: (0, j)),
        ],
        out_specs=pl.BlockSpec((1, 1, h1, l2 * c), lambda bi, j: (j, bi, 0, 0)),
        out_shape=jax.ShapeDtypeStruct((3, b, h1, l2 * c), F32),
        compiler_params=_params("parallel", "parallel"),
        name="short_conv",
    )(u.reshape(b, h1, l2, c3), w, bias.reshape(1, c3))


def _filter_body(band_ref, w1_ref, b1_ref, w2_ref, b2_ref, w3_ref, fr_ref, dl_ref, o_ref, s_ref, *, n, rt, c):
    i = pl.program_id(0)
    hp = lax.Precision.HIGHEST
    m = i * rt + lax.broadcasted_iota(jnp.int32, (rt, 1), 0)
    pos = jnp.where(m < n, m, 2 * n - m).astype(F32)
    t_norm = pos / max(n - 1, 1)
    ang = (2.0 * math.pi / n) * pos * band_ref[...]
    lane = lax.broadcasted_iota(jnp.int32, (rt, LANES), 1)
    z = jnp.where(lane == 0, t_norm,
                  jnp.where(lane <= HY_BANDS, jnp.cos(ang),
                            jnp.where(lane <= 2 * HY_BANDS, -jnp.sin(ang), 0.0)))
    fr = fr_ref[...]
    hdn = jnp.sin(fr * (jnp.dot(z, w1_ref[...], precision=hp, preferred_element_type=F32) + b1_ref[...]))
    hdn = jnp.sin(fr * (jnp.dot(hdn, w2_ref[...], precision=hp, preferred_element_type=F32) + b2_ref[...]))
    h = jnp.dot(hdn, w3_ref[...], precision=hp, preferred_element_type=F32)
    h = h * jnp.exp(-t_norm * dl_ref[...])
    half = HY_ORDER * c
    sel = jnp.where(m < n, h[:, :half], jnp.where(m > n, -h[:, half:], 0.0))
    for o in range(HY_ORDER):
        o_ref[o] = sel[:, o * c:(o + 1) * c]

    @pl.when(i == 0)
    def _():
        s_ref[...] = jnp.zeros_like(s_ref)

    s_ref[...] += jnp.sum(jnp.abs(sel), axis=0, keepdims=True)


def _hyena_filters(n, w1, b1, w2, b2, w3, freq, c):
    rt = min(1024, n)
    hid = w1.shape[1]
    bands = jnp.linspace(1e-4, HY_BANDS - 1, HY_BANDS, dtype=F32)
    band_row = jnp.zeros((1, LANES), F32).at[0, 1:1 + 2 * HY_BANDS].set(jnp.tile(bands, 2))
    w1p = jnp.zeros((LANES, hid), F32).at[:w1.shape[0]].set(w1)
    max_decay = math.log(HY_DECAY_TARGET) / HY_FAST_DECAY
    min_decay = math.log(HY_DECAY_TARGET) / HY_SLOW_DECAY
    deltas = jnp.abs(jnp.linspace(min_decay, max_decay, c, dtype=F32))
    dl = jnp.tile(deltas, HY_DIRS * HY_ORDER).reshape(1, -1)
    full = lambda a: pl.BlockSpec(a.shape, lambda i: (0,) * a.ndim)
    args = (band_row, w1p, b1.reshape(1, hid), w2, b2.reshape(1, hid), w3, freq.reshape(1, hid), dl)
    return pl.pallas_call(
        functools.partial(_filter_body, n=n, rt=rt, c=c),
        grid=(2 * n // rt,),
        in_specs=[full(a) for a in args],
        out_specs=[pl.BlockSpec((HY_ORDER, rt, c), lambda i: (0, i, 0)),
                   pl.BlockSpec((1, HY_ORDER * c), lambda i: (0, 0))],
        out_shape=[jax.ShapeDtypeStruct((HY_ORDER, 2 * n, c), F32),
                   jax.ShapeDtypeStruct((1, HY_ORDER * c), F32)],
        compiler_params=_params("arbitrary"),
        name="hyena_filter",
    )(*args)


def _dft_split(n):
    l2 = 32 if n >= 2048 else 16
    return 2 * n // l2, l2


def _dft_constants(n):
    l1, l2 = _dft_split(n)
    h1 = l1 // 2
    nn = 2 * n
    k1 = np.arange(h1)[:, None]
    a = 2 * np.pi * (k1 + 0.5) * np.arange(l1)[None, :] / l1
    w1 = np.concatenate([np.cos(a), -np.sin(a)], axis=0)
    t = 2 * np.pi * (k1 + 0.5) * np.arange(l2)[None, :] / nn
    tw_cos, tw_sin = np.cos(t), np.sin(t)
    p = 2 * np.pi * np.arange(l2)[:, None] * np.arange(l2)[None, :] / l2
    w2 = np.block([[np.cos(p), np.sin(p)], [-np.sin(p), np.cos(p)]])
    w2i = np.block([[np.cos(p), -np.sin(p)], [np.sin(p), np.cos(p)]])
    ai = 2 * np.pi * np.arange(h1)[:, None] * (np.arange(h1)[None, :] + 0.5) / l1
    w1i = (2.0 / nn) * np.concatenate([np.cos(ai), -np.sin(ai)], axis=1)
    c = lambda m, dt: jnp.asarray(m, dtype=dt)
    return dict(
        l1=l1, l2=l2, h1=h1,
        w1=c(w1, BF16), w2=c(w2, BF16), w2i=c(w2i, BF16), w1i=c(w1i, BF16),
        tw_cos_fwd=c(tw_cos.T[:, :, None], F32), tw_sin_fwd=c(tw_sin.T[:, :, None], F32),
        tw_cos_inv=c(tw_cos[:, :, None], F32), tw_sin_inv=c(tw_sin[:, :, None], F32),
    )


def _ct_fwd1_body(x_ref, w_ref, tc_ref, ts_ref, nrm_ref, o_ref, *, lb, c, h1, normalise):
    for q in range(lb):
        x = x_ref[0, 0, :, q * c:(q + 1) * c]
        if normalise:
            x = x / (nrm_ref[0] + HY_FILTER_EPS)
        a = jnp.dot(w_ref[...], x.astype(BF16), preferred_element_type=F32)
        ar, ai = a[:h1], a[h1:]
        tc, ts = tc_ref[q], ts_ref[q]
        o_ref[0, 0, q] = ar * tc + ai * ts
        o_ref[0, 1, q] = ai * tc - ar * ts


def _ct_fwd1(xs, idx, consts, c, norms=None, lb=8):
    _, b, k1n, _ = xs.shape
    l2, h1 = consts["l2"], consts["h1"]
    lb = min(lb, l2)
    w = consts["w1"][:, :k1n]
    normalise = norms is not None
    if norms is None:
        norms = jnp.zeros((b, 1, c), F32)
    return pl.pallas_call(
        functools.partial(_ct_fwd1_body, lb=lb, c=c, h1=h1, normalise=normalise),
        grid=(b, l2 // lb),
        in_specs=[
            pl.BlockSpec((1, 1, k1n, lb * c), lambda bi, i: (idx, bi, 0, i)),
            pl.BlockSpec(w.shape, lambda bi, i: (0, 0)),
            pl.BlockSpec((lb, h1, 1), lambda bi, i: (i, 0, 0)),
            pl.BlockSpec((lb, h1, 1), lambda bi, i: (i, 0, 0)),
            pl.BlockSpec((1, 1, c), lambda bi, i: (bi, 0, 0)),
        ],
        out_specs=pl.BlockSpec((1, 2, lb, h1, c), lambda bi, i: (bi, 0, i, 0, 0)),
        out_shape=jax.ShapeDtypeStruct((b, 2, l2, h1, c), F32),
        compiler_params=_params("parallel", "parallel"),
        name="ct_fwd1",
    )(xs, w, consts["tw_cos_fwd"], consts["tw_sin_fwd"], norms)


def _fine_rows(ref, lead, q):
    return jnp.concatenate([ref[lead + (0, slice(None), q, slice(None))],
                            ref[lead + (1, slice(None), q, slice(None))]], axis=0)


def _ct_spec_body(a_ref, w2_ref, o_ref, *, l2, kb):
    for q in range(kb):
        x = jnp.dot(w2_ref[...], _fine_rows(a_ref, (0,), q).astype(BF16), preferred_element_type=F32)
        o_ref[0, 0, :, q, :] = x[:l2]
        o_ref[0, 1, :, q, :] = x[l2:]


def _ct_spectrum(a, consts, c, kb=8):
    b = a.shape[0]
    l2, h1 = consts["l2"], consts["h1"]
    kb = min(kb, h1)
    blk = pl.BlockSpec((1, 2, l2, kb, c), lambda bi, i: (bi, 0, 0, i, 0))
    return pl.pallas_call(
        functools.partial(_ct_spec_body, l2=l2, kb=kb),
        grid=(b, h1 // kb),
        in_specs=[blk, pl.BlockSpec((2 * l2, 2 * l2), lambda bi, i: (0, 0))],
        out_specs=blk,
        out_shape=jax.ShapeDtypeStruct((b, 2, l2, h1, c), F32),
        compiler_params=_params("parallel", "parallel"),
        name="ct_spectrum",
    )(a, consts["w2"])


def _ct_mid_body(a_ref, h_ref, w2_ref, w2i_ref, tc_ref, ts_ref, o_ref, *, l2, kb):
    for q in range(kb):
        x = jnp.dot(w2_ref[...], _fine_rows(a_ref, (0,), q).astype(BF16), preferred_element_type=F32)
        xr, xi = x[:l2], x[l2:]
        hr, hi = h_ref[0, 0, :, q, :], h_ref[0, 1, :, q, :]
        y = jnp.concatenate([xr * hr - xi * hi, xr * hi + xi * hr], axis=0).astype(BF16)
        bm = jnp.dot(w2i_ref[...], y, preferred_element_type=F32)
        br, bi = bm[:l2], bm[l2:]
        tc, ts = tc_ref[q], ts_ref[q]
        o_ref[0, 0, :, q, :] = br * tc - bi * ts
        o_ref[0, 1, :, q, :] = br * ts + bi * tc


def _ct_mid(a, hspec, order, consts, c, kb=16):
    b = a.shape[0]
    l2, h1 = consts["l2"], consts["h1"]
    kb = min(kb, h1)
    blk = pl.BlockSpec((1, 2, l2, kb, c), lambda bi, i: (bi, 0, 0, i, 0))
    return pl.pallas_call(
        functools.partial(_ct_mid_body, l2=l2, kb=kb),
        grid=(b, h1 // kb),
        in_specs=[
            blk,
            pl.BlockSpec((1, 2, l2, kb, c), lambda bi, i: (order, 0, 0, i, 0)),
            pl.BlockSpec((2 * l2, 2 * l2), lambda bi, i: (0, 0)),
            pl.BlockSpec((2 * l2, 2 * l2), lambda bi, i: (0, 0)),
            pl.BlockSpec((kb, l2, 1), lambda bi, i: (i, 0, 0)),
            pl.BlockSpec((kb, l2, 1), lambda bi, i: (i, 0, 0)),
        ],
        out_specs=blk,
        out_shape=jax.ShapeDtypeStruct((b, 2, l2, h1, c), F32),
        compiler_params=_params("parallel", "parallel"),
        name="ct_mid",
    )(a, hspec, consts["w2"], consts["w2i"], consts["tw_cos_inv"], consts["tw_sin_inv"])


def _ct_inv1_body(b_ref, w_ref, u_ref, gate_ref, skip_ref, o_ref, *, lb, c, by_position):
    for q in range(lb):
        bb = jnp.concatenate([b_ref[0, 0, q], b_ref[0, 1, q]], axis=0).astype(BF16)
        y = jnp.dot(w_ref[...], bb, preferred_element_type=F32)
        cs = slice(q * c, (q + 1) * c)
        out = gate_ref[0, 0, :, cs] * (y + u_ref[0, 0, :, cs] * skip_ref[...])
        if by_position:
            o_ref[0, :, q, :] = out
        else:
            o_ref[0, :, cs] = out


def _ct_inv1(bsp, u, u_idx, gate, gate_idx, skip, consts, c, by_position, lb=8):
    b = bsp.shape[0]
    l2, h1 = consts["l2"], consts["h1"]
    lb = min(lb, l2)
    if by_position:
        out_spec = pl.BlockSpec((1, h1, lb, c), lambda bi, i: (bi, 0, i, 0))
        out_shape = jax.ShapeDtypeStruct((b, h1, l2, c), F32)
    else:
        out_spec = pl.BlockSpec((1, h1, lb * c), lambda bi, i: (bi, 0, i))
        out_shape = jax.ShapeDtypeStruct((b, h1, l2 * c), F32)
    return pl.pallas_call(
        functools.partial(_ct_inv1_body, lb=lb, c=c, by_position=by_position),
        grid=(b, l2 // lb),
        in_specs=[
            pl.BlockSpec((1, 2, lb, h1, c), lambda bi, i: (bi, 0, i, 0, 0)),
            pl.BlockSpec((h1, 2 * h1), lambda bi, i: (0, 0)),
            pl.BlockSpec((1, 1, h1, lb * c), lambda bi, i: (u_idx, bi, 0, i)),
            pl.BlockSpec((1, 1, h1, lb * c), lambda bi, i: (gate_idx, bi, 0, i)),
            pl.BlockSpec((1, c), lambda bi, i: (0, 0)),
        ],
        out_specs=out_spec,
        out_shape=out_shape,
        compiler_params=_params("parallel", "parallel"),
        name="ct_inv1",
    )(bsp, consts["w1i"], u, gate, skip.reshape(1, c))


def _hyena(u, conv_w, conv_b, f_w1, f_b1, f_w2, f_b2, f_w3, f_freq, skip):
    b, n, c3 = u.shape
    c = c3 // 3
    consts = _dft_constants(n)
    l1, l2, h1 = consts["l1"], consts["l2"], consts["h1"]
    filt, norms = _hyena_filters(n, f_w1, f_b1, f_w2, f_b2, f_w3, f_freq, c)
    fa = _ct_fwd1(filt.reshape(1, HY_ORDER, l1, l2 * c), 0, consts, c, norms=norms.reshape(HY_ORDER, 1, c))
    hspec = _ct_spectrum(fa, consts, c)
    parts = _short_conv(u, conv_w, conv_b, l2)

    def long_conv_gated(x_stack, x_idx, gate_idx, order, by_position):
        a = _ct_fwd1(x_stack, x_idx, consts, c)
        bsp = _ct_mid(a, hspec, order, consts, c)
        return _ct_inv1(bsp, x_stack, x_idx, parts, gate_idx, skip[order], consts, c, by_position)

    z = long_conv_gated(parts, 0, 1, 0, False)
    return long_conv_gated(z[None], 0, 2, 1, True).reshape(b, n, c)


def _outproj_body(*refs, n_in):
    ins = refs[:n_in]
    ws = refs[n_in:2 * n_in]
    x_ref, gate_ref, g_ref, sc_ref, sh_ref, rhi_ref, rlo_ref, rb_ref, xo_ref, h_ref, rt_ref = refs[2 * n_in:]
    y = None
    for a_ref, w_ref in zip(ins, ws):
        t = jnp.dot(a_ref[0].astype(BF16), w_ref[...], preferred_element_type=F32)
        y = t if y is None else y + t
    x = x_ref[0] + gate_ref[0] * y
    xo_ref[0] = x
    h = x * lax.rsqrt(jnp.mean(x * x, axis=-1, keepdims=True) + NORM_EPS) * g_ref[...]
    h = h * (1.0 + sc_ref[0]) + sh_ref[0]
    hi = h.astype(BF16)
    h_ref[0] = hi
    lo = (h - hi.astype(F32)).astype(BF16)
    lg = (jnp.dot(hi, rhi_ref[...], preferred_element_type=F32)
          + jnp.dot(lo, rhi_ref[...], preferred_element_type=F32)
          + jnp.dot(hi, rlo_ref[...], preferred_element_type=F32) + rb_ref[...])
    rt_ref[0] = _route(lg)


def _route(lg):
    lane = lax.broadcasted_iota(jnp.int32, lg.shape, 1)
    lane_f = lane.astype(F32)
    neg = -jnp.inf

    def top(v):
        m = jnp.max(v, axis=-1, keepdims=True)
        return m, jnp.min(jnp.where(v == m, lane_f, float(LANES)), axis=-1, keepdims=True)

    gl = jnp.where(lane < N_GROUPS, lg, neg)
    gmax, grp = top(gl)
    p_grp = 1.0 / jnp.sum(jnp.exp(gl - gmax), axis=-1, keepdims=True)
    first = N_GROUPS + grp * EXP_PER_GROUP
    el = jnp.where((lane_f >= first) & (lane_f < first + EXP_PER_GROUP), lg, neg)
    m1, i1 = top(el)
    m2, i2 = top(jnp.where(lane_f == i1, neg, el))
    e2 = jnp.exp(m2 - m1)
    den = 1.0 + e2
    vals = (i1 - N_GROUPS, i2 - N_GROUPS, p_grp * (1.0 / den), p_grp * (e2 / den))
    out = jnp.zeros(lg.shape, F32)
    for k, v in enumerate(vals):
        out = jnp.where(lane == k, v, out)
    return out


def _outproj(ins, ws, x, gate, g, scale, shift, r_hi, r_lo, r_b, tm=512):
    b, n, d = x.shape
    tm = min(tm, n)
    bm = gate.shape[0]
    mod_map = (lambda bi, i: (bi, 0, 0)) if bm > 1 else (lambda bi, i: (0, 0, 0))
    row = lambda wd: pl.BlockSpec((1, tm, wd), lambda bi, i: (bi, i, 0))
    full = lambda a: pl.BlockSpec(a.shape, lambda bi, i: (0,) * a.ndim)
    mod = pl.BlockSpec((1, 1, d), mod_map)
    return pl.pallas_call(
        functools.partial(_outproj_body, n_in=len(ins)),
        grid=(b, n // tm),
        in_specs=([row(a.shape[-1]) for a in ins] + [full(w) for w in ws]
                  + [row(d), mod, pl.BlockSpec((1, d), lambda bi, i: (0, 0)), mod, mod,
                     full(r_hi), full(r_lo), full(r_b)]),
        out_specs=[row(d), row(d), row(LANES)],
        out_shape=[jax.ShapeDtypeStruct((b, n, d), F32), jax.ShapeDtypeStruct((b, n, d), BF16),
                   jax.ShapeDtypeStruct((b, n, LANES), F32)],
        compiler_params=_params("parallel", "parallel"),
        name="outproj",
    )(*ins, *ws, x, gate, g.reshape(1, d), scale, shift, r_hi, r_lo, r_b)


def _rank_body(rt_ref, tri_ref, upper_ref, pos_ref, cnt_ref):
    rt = rt_ref[...]
    lane_i = lax.broadcasted_iota(jnp.int32, rt.shape, 1)
    lane = lane_i.astype(F32)
    oh_a = lane == rt[:, 0:1]
    oh_b = lane == rt[:, 1:2]
    one_a = jnp.where(oh_a, 1.0, 0.0)
    one_b = jnp.where(oh_b, 1.0, 0.0)
    before_a = jnp.dot(tri_ref[...], one_a.astype(BF16), preferred_element_type=F32)
    before_b = jnp.dot(tri_ref[...], one_b.astype(BF16), preferred_element_type=F32)
    tot_a = jnp.sum(one_a, axis=0, keepdims=True)
    cnt = tot_a + jnp.sum(one_b, axis=0, keepdims=True)
    padded = jnp.floor((cnt + (CHUNK_ROWS - 1)) * (1.0 / CHUNK_ROWS)) * CHUNK_ROWS
    first = jnp.dot(jnp.broadcast_to(padded, (8, LANES)).astype(BF16), upper_ref[...],
                    preferred_element_type=F32)[0:1]
    pos_a = jnp.sum(jnp.where(oh_a, before_a + first, 0.0), axis=-1, keepdims=True)
    pos_b = jnp.sum(jnp.where(oh_b, before_b + first + tot_a, 0.0), axis=-1, keepdims=True)
    is_gate = (lane_i >= TOP_K) & (lane_i < 2 * TOP_K)
    pos_ref[...] = jnp.where(lane_i == 0, pos_a, jnp.where(lane_i == 1, pos_b, jnp.where(is_gate, rt, 0.0)))
    cnt_ref[0] = jnp.broadcast_to(cnt, (8, LANES))


def _rank(route):
    t = route.shape[0]
    tm = TOKEN_TILE
    tri = jnp.asarray(np.tril(np.ones((tm, tm)), -1), dtype=BF16)
    upper = jnp.asarray(np.triu(np.ones((LANES, LANES)), 1), dtype=BF16)
    return pl.pallas_call(
        _rank_body,
        grid=(t // tm,),
        in_specs=[pl.BlockSpec((tm, LANES), lambda i: (i, 0)), pl.BlockSpec((tm, tm), lambda i: (0, 0)),
                  pl.BlockSpec((LANES, LANES), lambda i: (0, 0))],
        out_specs=[pl.BlockSpec((tm, LANES), lambda i: (i, 0)), pl.BlockSpec((1, 8, LANES), lambda i: (i, 0, 0))],
        out_shape=[jax.ShapeDtypeStruct((t, LANES), F32), jax.ShapeDtypeStruct((t // tm, 8, LANES), F32)],
        compiler_params=_params("parallel"),
        name="moe_rank",
    )(route, tri, upper)


def _chunk_tables(cnt, n_blocks):
    padded = (cnt + CHUNK_ROWS - 1) // CHUNK_ROWS * CHUNK_ROWS
    run_end = jnp.cumsum(padded, axis=1)
    run_start = run_end - padded
    seg_rows = jnp.sum(padded, axis=0)
    seg_rows = (seg_rows + EXPERT_ROWS - 1) // EXPERT_ROWS * EXPERT_ROWS
    seg_end = jnp.cumsum(seg_rows)
    dst_start = (seg_end - seg_rows)[None, :] + jnp.cumsum(padded, axis=0) - padded
    row0 = jnp.arange(BUF_CHUNKS, dtype=jnp.int32) * CHUNK_ROWS
    chunk_exp = jnp.minimum(jnp.sum(run_end[:, None, :] <= row0[None, :, None], axis=-1), N_EXPERTS - 1)
    onehot = chunk_exp[:, :, None] == jnp.arange(N_EXPERTS, dtype=jnp.int32)[None, None, :]
    dst = jnp.sum(jnp.where(onehot, (dst_start - run_start)[:, None, :], 0), axis=-1) + row0[None, :]
    n_chunks = run_end[:, -1:] // CHUNK_ROWS
    table = jnp.concatenate(
        [dst, n_chunks, jnp.zeros((cnt.shape[0], TABLE_WORDS - BUF_CHUNKS - 1), jnp.int32)], axis=1)
    block_row0 = jnp.arange(n_blocks, dtype=jnp.int32) * EXPERT_ROWS
    block_exp = jnp.minimum(jnp.sum(seg_end[None, :] <= block_row0[:, None], axis=1), N_EXPERTS - 1)
    n_used = (seg_end[-1] // EXPERT_ROWS).reshape(1)
    return table.astype(jnp.int32), block_exp.astype(jnp.int32), n_used.astype(jnp.int32)


def _pack_halves(x):
    w = x.shape[-1] // 2
    lo = lax.bitcast_convert_type(x[:, :w].astype(BF16).astype(F32), jnp.int32)
    hi = lax.bitcast_convert_type(x[:, w:].astype(BF16).astype(F32), jnp.int32)
    return lax.shift_right_logical(lo, jnp.int32(16)) | (hi & jnp.int32(-65536))


def _unpack_halves(p):
    lo = lax.bitcast_convert_type(lax.shift_left(p, jnp.int32(16)), F32)
    hi = lax.bitcast_convert_type(p & jnp.int32(-65536), F32)
    return jnp.concatenate([lo, hi], axis=-1).astype(BF16)


def _start_chunks(tab_ref, tile, make_copy):
    base = tile * TABLE_WORDS

    def issue(c):
        make_copy(pl.multiple_of(c * CHUNK_ROWS, CHUNK_ROWS),
                  pl.multiple_of(tab_ref[base + c], CHUNK_ROWS)).start()

    _for_each_chunk(tab_ref[base + BUF_CHUNKS], issue)


def _wait_chunks(tab_ref, tile, make_copy):
    _for_each_chunk(tab_ref[tile * TABLE_WORDS + BUF_CHUNKS], lambda c: make_copy(0, 0).wait())


def _for_each_chunk(n, fn):
    groups = n // LOOP_GROUP

    def grouped(i, carry):
        for k in range(LOOP_GROUP):
            fn(i * LOOP_GROUP + k)
        return carry

    def single(c, carry):
        fn(c)
        return carry

    lax.fori_loop(0, groups, grouped, 0)
    lax.fori_loop(groups * LOOP_GROUP, n, single, 0)


def _dispatch_body(tab_ref, h_ref, pos_ref, xs_in, xs_out, buf, sem_rows, *, tile0, n_steps):
    del xs_in
    step = pl.program_id(0)
    dp = h_ref.shape[1] // 2

    def copies(at_step):
        slot = at_step % 2
        return lambda src, dst: pltpu.make_async_copy(
            buf.at[slot, pl.ds(src, CHUNK_ROWS)], xs_out.at[pl.ds(dst, CHUNK_ROWS)], sem_rows.at[slot])

    @pl.when(step >= 2)
    def _():
        _wait_chunks(tab_ref, tile0 + step - 2, copies(step - 2))

    tm = h_ref.shape[0]
    row = lax.broadcasted_iota(jnp.int32, (BUF_ROWS, tm), 0).astype(F32)
    oh_a = row == pos_ref[0, 0:1, :]
    oh_b = row == pos_ref[0, 1:2, :]
    buf[step % 2, :, :dp] = _pack_halves(jnp.dot(jnp.where(oh_a | oh_b, 1.0, 0.0).astype(BF16), h_ref[...],
                                                 preferred_element_type=F32))
    gate = jnp.sum(jnp.where(oh_a, pos_ref[0, 2:3, :], 0.0) + jnp.where(oh_b, pos_ref[0, 3:4, :], 0.0),
                   axis=-1, keepdims=True)
    buf[step % 2, :, dp:] = lax.bitcast_convert_type(jnp.broadcast_to(gate, (BUF_ROWS, LANES)), jnp.int32)
    _start_chunks(tab_ref, tile0 + step, copies(step))

    @pl.when(step == n_steps - 1)
    def _():
        @pl.when(step >= 1)
        def _():
            _wait_chunks(tab_ref, tile0 + step - 1, copies(step - 1))

        _wait_chunks(tab_ref, tile0 + step, copies(step))


def _dispatch(h, pos_t, table, xs, tile0):
    t, d = h.shape
    tm = TOKEN_TILE
    n_steps = t // tm
    grid_spec = pltpu.PrefetchScalarGridSpec(
        num_scalar_prefetch=1,
        grid=(n_steps,),
        in_specs=[
            pl.BlockSpec((tm, d), lambda i, tab: (i, 0)),
            pl.BlockSpec((1, 8, tm), lambda i, tab: (tile0 + i, 0, 0)),
            pl.BlockSpec(memory_space=pl.ANY),
        ],
        out_specs=pl.BlockSpec(memory_space=pl.ANY),
        scratch_shapes=[
            pltpu.VMEM((2, BUF_ROWS, d // 2 + LANES), jnp.int32),
            pltpu.SemaphoreType.DMA((2,)),
        ],
    )
    return pl.pallas_call(
        functools.partial(_dispatch_body, tile0=tile0, n_steps=n_steps),
        grid_spec=grid_spec,
        out_shape=jax.ShapeDtypeStruct(xs.shape, xs.dtype),
        input_output_aliases={3: 0},
        compiler_params=_params("arbitrary"),
        name="moe_dispatch",
    )(table.reshape(-1), h, pos_t, xs)


def _expert_body(bexp_ref, nused_ref, x_ref, wg_ref, wu_ref, wd_ref, o_ref):
    @pl.when(pl.program_id(0) < nused_ref[0])
    def _():
        dp = o_ref.shape[1]
        xb = _unpack_halves(x_ref[:, :dp])
        gate = lax.bitcast_convert_type(x_ref[:, dp:dp + 1], F32)
        gt = jnp.dot(xb, wg_ref[0], preferred_element_type=F32)
        up = jnp.dot(xb, wu_ref[0], preferred_element_type=F32)
        hid = (gt * jax.nn.sigmoid(gt) * up).astype(BF16)
        o_ref[...] = _pack_halves(jnp.dot(hid, wd_ref[0], preferred_element_type=F32) * gate)

    @pl.when(pl.program_id(0) >= nused_ref[0])
    def _():
        o_ref[...] = jnp.zeros_like(o_ref)


def _experts(xs, block_exp, n_used, w_gate, w_up, w_down):
    rows, width = xs.shape
    d, de = w_gate.shape[1:]
    used = lambda i, be, nu: (jnp.minimum(i, nu[0] - 1), 0)
    grid_spec = pltpu.PrefetchScalarGridSpec(
        num_scalar_prefetch=2,
        grid=(rows // EXPERT_ROWS,),
        in_specs=[
            pl.BlockSpec((EXPERT_ROWS, width), used),
            pl.BlockSpec((1, d, de), lambda i, be, nu: (be[i], 0, 0)),
            pl.BlockSpec((1, d, de), lambda i, be, nu: (be[i], 0, 0)),
            pl.BlockSpec((1, de, d), lambda i, be, nu: (be[i], 0, 0)),
        ],
        out_specs=pl.BlockSpec((EXPERT_ROWS, d // 2), lambda i, be, nu: (i, 0)),
    )
    return pl.pallas_call(
        _expert_body,
        grid_spec=grid_spec,
        out_shape=jax.ShapeDtypeStruct((rows, d // 2), jnp.int32),
        compiler_params=_params("arbitrary"),
        name="experts",
    )(block_exp, n_used, xs, w_gate, w_up, w_down)


def _combine_body(tab_ref, ys_hbm, x_ref, gate_ref, pos_ref, fg_ref, o_ref, ybuf, sem_rows,
                  *, tile0, n_tiles, n_steps, final_norm):
    step = pl.program_id(0) * n_tiles + pl.program_id(1)

    def copies(at_step):
        slot = at_step % 2
        return lambda dst, src: pltpu.make_async_copy(
            ys_hbm.at[pl.ds(src, CHUNK_ROWS)], ybuf.at[slot, pl.ds(dst, CHUNK_ROWS)], sem_rows.at[slot])

    @pl.when(step == 0)
    def _():
        ybuf[...] = jnp.zeros_like(ybuf)
        _start_chunks(tab_ref, tile0, copies(0))

    @pl.when(step + 1 < n_steps)
    def _():
        _start_chunks(tab_ref, tile0 + step + 1, copies(step + 1))

    _wait_chunks(tab_ref, tile0 + step, copies(step))
    tm = x_ref.shape[1]
    col = lax.broadcasted_iota(jnp.int32, (tm, BUF_ROWS), 1).astype(F32)
    pick = jnp.where((col == pos_ref[:, 0:1]) | (col == pos_ref[:, 1:2]), 1.0, 0.0).astype(BF16)
    out = x_ref[0] + gate_ref[0] * jnp.dot(pick, _unpack_halves(ybuf[step % 2]), preferred_element_type=F32)
    if final_norm:
        out = out * lax.rsqrt(jnp.mean(out * out, axis=-1, keepdims=True) + NORM_EPS) * fg_ref[...]
    o_ref[0] = out


def _combine(ys, table, pos, x, gate, tile0, final_g=None):
    b, n, d = x.shape
    tm = min(TOKEN_TILE, n)
    n_tiles = n // tm
    bm = gate.shape[0]
    mod_map = (lambda bi, i, tab: (bi, 0, 0)) if bm > 1 else (lambda bi, i, tab: (0, 0, 0))
    final_norm = final_g is not None
    fg = final_g.reshape(1, d) if final_norm else jnp.ones((1, d), F32)
    grid_spec = pltpu.PrefetchScalarGridSpec(
        num_scalar_prefetch=1,
        grid=(b, n_tiles),
        in_specs=[
            pl.BlockSpec(memory_space=pl.ANY),
            pl.BlockSpec((1, tm, d), lambda bi, i, tab: (bi, i, 0)),
            pl.BlockSpec((1, 1, d), mod_map),
            pl.BlockSpec((tm, LANES), lambda bi, i, tab: (tile0 + bi * n_tiles + i, 0)),
            pl.BlockSpec((1, d), lambda bi, i, tab: (0, 0)),
        ],
        out_specs=pl.BlockSpec((1, tm, d), lambda bi, i, tab: (bi, i, 0)),
        scratch_shapes=[
            pltpu.VMEM((2, BUF_ROWS, d // 2), jnp.int32),
            pltpu.SemaphoreType.DMA((2,)),
        ],
    )
    return pl.pallas_call(
        functools.partial(_combine_body, tile0=tile0, n_tiles=n_tiles, n_steps=b * n_tiles,
                          final_norm=final_norm),
        grid_spec=grid_spec,
        out_shape=jax.ShapeDtypeStruct((b, n, d), F32),
        compiler_params=_params("arbitrary", "arbitrary"),
        name="moe_combine",
    )(table.reshape(-1), ys, x, gate, pos, fg)


def _router_weights(wg, bg, we, be):
    d = wg.shape[0]
    w = jnp.zeros((d, LANES), F32).at[:, :N_GROUPS].set(wg).at[:, N_GROUPS:N_GROUPS + N_EXPERTS].set(we)
    bias = jnp.zeros((1, LANES), F32).at[0, :N_GROUPS].set(bg).at[0, N_GROUPS:N_GROUPS + N_EXPERTS].set(be)
    hi = w.astype(BF16)
    lo = (w - hi.astype(F32)).astype(BF16)
    return hi, lo, bias


def kernel(x, c, ctx, c_ctx, ada_w, ada_b, norm1_g, norm2_g, ev_w_in, ev_w_out, hy_conv_w, hy_conv_b, hy_f_w1, hy_f_b1, hy_f_w2, hy_f_b2, hy_f_w3, hy_f_freq, hy_skip, swa_sink, od_w_qkv, od_w_out, od_q_norm_g, od_k_norm_g, rt_group_w, rt_group_b, rt_exp_w, rt_exp_b, moe_w_gate, moe_w_up, moe_w_down, final_norm_g):
    b, n, d = x.shape
    lc = ctx.shape[1]
    depth = ada_w.shape[0]
    rope = _rope_tables(n)
    xc = ctx
    sc = jax.nn.silu(c)
    scc = jax.nn.silu(c_ctx)
    q_scale = HEAD_DIM ** -0.5
    for layer in range(depth):
        with_ctx = layer < depth - 1
        mod = (sc @ ada_w[layer] + ada_b[layer]).reshape(b, N_MOD, 1, d)
        modc = (scc @ ada_w[layer] + ada_b[layer]).reshape(1, N_MOD, 1, d)
        m = [mod[:, k] for k in range(N_MOD)]
        mc = [modc[:, k] for k in range(N_MOD)]
        r_hi, r_lo, r_b = _router_weights(rt_group_w[layer], rt_group_b[layer], rt_exp_w[layer], rt_exp_b[layer])
        wgt, wup, wdn = (moe_w_gate[layer].astype(BF16), moe_w_up[layer].astype(BF16),
                         moe_w_down[layer].astype(BF16))
        if layer % 2 == 0:
            e = layer // 2
            c_hy = hy_conv_w.shape[-1] // 3
            d_hy = 3 * c_hy
            hq = swa_sink.shape[-1]
            d_q = hq * HEAD_DIM
            hkv = hq // 4
            d_kv = hkv * HEAD_DIM
            w_in = ev_w_in[e].astype(BF16)
            w_out = ev_w_out[e].astype(BF16)
            hy_args = (hy_conv_w[e], hy_conv_b[e], hy_f_w1[e], hy_f_b1[e], hy_f_w2[e], hy_f_b2[e],
                       hy_f_w3[e], hy_f_freq[e], hy_skip[e])
            u, q, k, v = _proj(x, norm1_g[layer], m[1], m[0], w_in, [
                (0, d_hy, "f32", None, False, 1.0),
                (d_hy, d_q, "qk", None, True, q_scale),
                (d_hy + d_q, d_kv, "qk", None, True, 1.0),
                (d_hy + d_q + d_kv, d_kv, "bf16", None, False, 1.0)], rope_tabs=rope)
            if with_ctx:
                uc, qc, kc, vc = _proj(xc, norm1_g[layer], mc[1], mc[0], w_in, [
                    (0, d_hy, "f32", None, False, 1.0),
                    (d_hy, d_q, "qk", None, False, q_scale),
                    (d_hy + d_q, d_kv, "bf16", None, False, 1.0),
                    (d_hy + d_q + d_kv, d_kv, "bf16", None, False, 1.0)])
            else:
                kc, vc = _proj(xc, norm1_g[layer], mc[1], mc[0], w_in, [
                    (d_hy + d_q, d_kv, "bf16", None, False, 1.0),
                    (d_hy + d_q + d_kv, d_kv, "bf16", None, False, 1.0)])
            y_hy = _hyena(u, *hy_args)
            y_att = _windowed_attention(q, k, v, kc, vc, swa_sink[e], hkv)
            mix_in, mix_w = [y_hy, y_att], [w_out[:c_hy], w_out[c_hy:]]
            if with_ctx:
                yc_hy = _hyena(uc, *hy_args)
                yc_att = _full_attention(qc, kc, vc, hkv, sink=swa_sink[e])
                mixc_in = [yc_hy, yc_att]
        else:
            o = layer // 2
            hkv = od_w_qkv.shape[-1] // HEAD_DIM // 6
            hq = 4 * hkv
            d_q = hq * HEAD_DIM
            d_kv = hkv * HEAD_DIM
            w_qkv = od_w_qkv[o].astype(BF16)
            w_out = od_w_out[o].astype(BF16)
            norm_g = jnp.zeros((8, LANES), F32).at[0].set(jnp.tile(od_q_norm_g[o], 2)).at[1].set(
                jnp.tile(od_k_norm_g[o], 2))
            q, k, v = _proj(x, norm1_g[layer], m[1], m[0], w_qkv, [
                (0, d_q, "qk", 0, True, q_scale),
                (d_q, d_kv, "qk", 1, True, 1.0),
                (d_q + d_kv, d_kv, "bf16", None, False, 1.0)], rope_tabs=rope, norm_g=norm_g)
            if with_ctx:
                qc, kc, vc = _proj(xc, norm1_g[layer], mc[1], mc[0], w_qkv, [
                    (0, d_q, "qk", 0, False, q_scale),
                    (d_q, d_kv, "qk", 1, False, 1.0),
                    (d_q + d_kv, d_kv, "bf16", None, False, 1.0)], norm_g=norm_g)
            else:
                kc, vc = _proj(xc, norm1_g[layer], mc[1], mc[0], w_qkv, [
                    (d_q, d_kv, "qk", 1, False, 1.0),
                    (d_q + d_kv, d_kv, "bf16", None, False, 1.0)], norm_g=norm_g)
            y_att = _full_attention(q, jnp.concatenate([kc, k], axis=1), jnp.concatenate([vc, v], axis=1), hkv)
            mix_in, mix_w = [y_att], [w_out]
            if with_ctx:
                mixc_in = [_full_attention(qc, kc, vc, hkv)]
        x, h2, rt = _outproj(mix_in, mix_w, x, m[2], norm2_g[layer], m[4], m[3], r_hi, r_lo, r_b)
        route_flat = rt.reshape(b * n, LANES)
        if with_ctx:
            xc, h2c, rtc = _outproj(mixc_in, mix_w, xc, mc[2], norm2_g[layer], mc[4], mc[3], r_hi, r_lo, r_b)
            route_flat = jnp.concatenate([route_flat, rtc.reshape(b * lc, LANES)], axis=0)
        n_tok = route_flat.shape[0]
        n_tiles = n_tok // TOKEN_TILE
        lat_tiles = b * n // TOKEN_TILE
        max_rows = n_tok * TOP_K + n_tiles * N_EXPERTS * (CHUNK_ROWS - 1)
        n_blocks = -(-max_rows // EXPERT_ROWS) + N_EXPERTS
        pos, cnt = _rank(route_flat)
        table, block_exp, n_used = _chunk_tables(cnt[:, 0, :N_EXPERTS].astype(jnp.int32), n_blocks)
        pos_t = jnp.swapaxes(pos[:, :8].reshape(n_tiles, TOKEN_TILE, 8), 1, 2)
        xs = jnp.zeros((n_blocks * EXPERT_ROWS, d // 2 + LANES), jnp.int32)
        xs = _dispatch(h2.reshape(b * n, d), pos_t, table, xs, 0)
        if with_ctx:
            xs = _dispatch(h2c.reshape(b * lc, d), pos_t, table, xs, lat_tiles)
        ys = _experts(xs, block_exp, n_used, wgt, wup, wdn)
        x = _combine(ys, table, pos, x, m[5], 0, final_g=None if with_ctx else final_norm_g)
        if with_ctx:
            xc = _combine(ys, table, pos, xc.reshape(b * lc // TOKEN_TILE, TOKEN_TILE, d), mc[5],
                          lat_tiles).reshape(b, lc, d)
    return x
```

```python
import functools
import math

import numpy as np
import jax
import jax.numpy as jnp
from jax import lax
from jax.experimental import pallas as pl
from jax.experimental.pallas import tpu as pltpu

F32 = jnp.float32
BF16 = jnp.bfloat16

HEAD_DIM = 64
GRID_W = 64
ROPE_BASE = 10000.0
Q_BLOCK = 128
NORM_EPS = 1e-6
N_MOD = 6
HY_ORDER = 2
HY_BANDS = 16
HY_DIRS = 2
HY_DECAY_TARGET = 1e-2
HY_FAST_DECAY = 0.3
HY_SLOW_DECAY = 1.5
HY_FILTER_EPS = 1e-6
SWA_WINDOW = 128
N_GROUPS = 4
EXP_PER_GROUP = 8
N_EXPERTS = N_GROUPS * EXP_PER_GROUP
TOP_K = 2
EXPERT_ROWS = 512
SWA_BLOCKS_PER_STEP = 1
LOOP_GROUP = 4
FULL_ATTN_Q_ROWS = 256
FULL_ATTN_UNIT_ROWS = 512
FULL_ATTN_LOOKAHEAD = 1
TOKEN_TILE = 512
CHUNK_ROWS = 8
BUF_ROWS = 1280
BUF_CHUNKS = BUF_ROWS // CHUNK_ROWS
TABLE_WORDS = 256

LANES = 128
VMEM_LIMIT_BYTES = 56 * 1024 * 1024


def _params(*sem):
    return pltpu.CompilerParams(dimension_semantics=sem, vmem_limit_bytes=VMEM_LIMIT_BYTES)


def _rope_tables(n):
    d_axis = HEAD_DIM // 2
    t = jnp.arange(n)
    inv = ROPE_BASE ** (-jnp.arange(0, d_axis, 2, dtype=F32) / d_axis)
    ang_r = (t // GRID_W).astype(F32)[:, None] * inv[None, :]
    ang_c = (t % GRID_W).astype(F32)[:, None] * inv[None, :]
    cos = jnp.concatenate([jnp.cos(ang_r)] * 2 + [jnp.cos(ang_c)] * 2, axis=-1)
    sin = jnp.concatenate([-jnp.sin(ang_r), jnp.sin(ang_r), -jnp.sin(ang_c), jnp.sin(ang_c)], axis=-1)
    return jnp.tile(cos, (1, 2)), jnp.tile(sin, (1, 2))


def _head_mean_matrix():
    i = np.arange(LANES)
    return jnp.asarray((i[:, None] // HEAD_DIM == i[None, :] // HEAD_DIM) / HEAD_DIM, dtype=BF16)


def _proj_body(x_ref, g_ref, sc_ref, sh_ref, w_ref, cos_ref, sin_ref, ng_ref, bd_ref, *out_refs, segs):
    x = x_ref[0]
    h = x * lax.rsqrt(jnp.mean(x * x, axis=-1, keepdims=True) + NORM_EPS) * g_ref[...]
    hb = (h * (1.0 + sc_ref[0]) + sh_ref[0]).astype(BF16)
    for o_ref, (c0, width, kind, norm_row, rope, out_scale) in zip(out_refs, segs):
        seg = jnp.dot(hb, w_ref[:, c0:c0 + width], preferred_element_type=F32)
        if kind == "f32":
            o_ref[0] = seg
            continue
        if kind == "bf16":
            o_ref[0] = seg.astype(BF16)
            continue
        for j in range(width // LANES):
            ch = seg[:, j * LANES:(j + 1) * LANES]
            if norm_row is not None:
                sq = ch * ch
                hi = sq.astype(BF16)
                lo = (sq - hi.astype(F32)).astype(BF16)
                ms = (jnp.dot(hi, bd_ref[...], preferred_element_type=F32)
                      + jnp.dot(lo, bd_ref[...], preferred_element_type=F32))
                ch = ch * lax.rsqrt(ms + NORM_EPS) * ng_ref[norm_row:norm_row + 1, :]
            if rope:
                lane = lax.broadcasted_iota(jnp.int32, ch.shape, 1)
                partner = jnp.where(lane % 32 < 16, pltpu.roll(ch, LANES - 16, 1), pltpu.roll(ch, 16, 1))
                ch = ch * cos_ref[...] + partner * sin_ref[...]
            if out_scale != 1.0:
                ch = ch * out_scale
            o_ref[0, :, j * LANES:(j + 1) * LANES] = ch.astype(BF16)


def _proj(x, g, scale, shift, w, segs, rope_tabs=None, norm_g=None, tm=512):
    b, n, d = x.shape
    tm = min(tm, n)
    bm = scale.shape[0]
    mod_map = (lambda bi, i: (bi, 0, 0)) if bm > 1 else (lambda bi, i: (0, 0, 0))
    if rope_tabs is None:
        cos = sin = jnp.zeros((8, LANES), F32)
        tab_spec = pl.BlockSpec((8, LANES), lambda bi, i: (0, 0))
    else:
        cos, sin = rope_tabs
        tab_spec = pl.BlockSpec((tm, LANES), lambda bi, i: (i, 0))
    if norm_g is None:
        norm_g = jnp.ones((8, LANES), F32)
    out_shape = [jax.ShapeDtypeStruct((b, n, s[1]), F32 if s[2] == "f32" else BF16) for s in segs]
    out_specs = [pl.BlockSpec((1, tm, s[1]), lambda bi, i: (bi, i, 0)) for s in segs]
    return pl.pallas_call(
        functools.partial(_proj_body, segs=tuple(segs)),
        grid=(b, n // tm),
        in_specs=[
            pl.BlockSpec((1, tm, d), lambda bi, i: (bi, i, 0)),
            pl.BlockSpec((1, d), lambda bi, i: (0, 0)),
            pl.BlockSpec((1, 1, d), mod_map),
            pl.BlockSpec((1, 1, d), mod_map),
            pl.BlockSpec(w.shape, lambda bi, i: (0, 0)),
            tab_spec,
            tab_spec,
            pl.BlockSpec(norm_g.shape, lambda bi, i: (0, 0)),
            pl.BlockSpec((LANES, LANES), lambda bi, i: (0, 0)),
        ],
        out_specs=out_specs,
        out_shape=out_shape,
        compiler_params=_params("parallel", "parallel"),
        name="proj",
    )(x, g.reshape(1, d), scale, shift, w, cos, sin, norm_g, _head_mean_matrix())


def _stack_heads(q, j, g):
    return jnp.concatenate(
        [q[:, (j * g + gg) * HEAD_DIM:(j * g + gg + 1) * HEAD_DIM] for gg in range(g)], axis=0)


def _values_with_ones(v, hkv):
    b, nk, _ = v.shape
    ones = jnp.zeros((b, nk, hkv, HEAD_DIM), v.dtype).at[..., 0].set(1)
    return jnp.concatenate([v.reshape(b, nk, hkv, HEAD_DIM), ones], axis=-1).reshape(b, nk, 2 * hkv * HEAD_DIM)


def _sink_column(sink_ref, j, g, qb):
    return jnp.concatenate([jnp.full((qb, 1), sink_ref[j * g + gg], F32) for gg in range(g)], axis=0)


def _swa_body(sink_ref, q_ref, kt_ref, v_ref, kct_ref, vc_ref, o_ref, *, n, hkv, g, qb, win, sub):
    rows = lax.broadcasted_iota(jnp.int32, (g * qb, 3 * qb), 0) % qb
    cols = lax.broadcasted_iota(jnp.int32, (g * qb, 3 * qb), 1)
    in_band = jnp.abs(rows + qb - cols) <= win
    for sb in range(sub):
        blk = pl.program_id(1) * sub + sb
        start = pl.multiple_of(blk * qb, qb)
        qs = slice(sb * qb, (sb + 1) * qb)
        q = q_ref[0, qs, :]
        key_pos = cols + (blk - 1) * qb
        valid = in_band & (key_pos >= 0) & (key_pos < n)

        def scores(j):
            hs = slice(j * HEAD_DIM, (j + 1) * HEAD_DIM)
            q4 = _stack_heads(q, j, g)
            return (jnp.dot(q4, kt_ref[0, hs, pl.ds(start, 3 * qb)], preferred_element_type=F32),
                    jnp.dot(q4, kct_ref[0, hs, :], preferred_element_type=F32))

        s_next = scores(0)
        for j in range(hkv):
            s_lat, s_ctx = s_next
            if j + 1 < hkv:
                s_next = scores(j + 1)
            s_lat = jnp.where(valid, s_lat, -jnp.inf)
            s_sink = _sink_column(sink_ref, j, g, qb)
            m = jnp.maximum(jnp.maximum(jnp.max(s_lat, axis=-1, keepdims=True),
                                        jnp.max(s_ctx, axis=-1, keepdims=True)), s_sink)
            e_lat = jnp.exp((s_lat - m).astype(BF16))
            e_ctx = jnp.exp((s_ctx - m).astype(BF16))
            vs = slice(j * 2 * HEAD_DIM, (j + 1) * 2 * HEAD_DIM)
            o = (jnp.dot(e_ctx, vc_ref[0, :, vs], preferred_element_type=F32)
                 + jnp.dot(e_lat, v_ref[0, pl.ds(start, 3 * qb), vs], preferred_element_type=F32))
            o = o[:, :HEAD_DIM] / (o[:, HEAD_DIM:HEAD_DIM + 1] + jnp.exp(s_sink - m))
            for gg in range(g):
                c0 = (j * g + gg) * HEAD_DIM
                o_ref[0, qs, c0:c0 + HEAD_DIM] = o[gg * qb:(gg + 1) * qb].astype(BF16)


def _windowed_attention(q, k, v, kc, vc, sink, hkv):
    b, n, dq = q.shape
    g = dq // HEAD_DIM // hkv
    qb = Q_BLOCK
    lc = kc.shape[1]
    dkv = hkv * HEAD_DIM
    kt = jnp.swapaxes(jnp.pad(k, ((0, 0), (qb, qb), (0, 0))), 1, 2)
    vp = _values_with_ones(jnp.pad(v, ((0, 0), (qb, qb), (0, 0))), hkv)
    vc = _values_with_ones(vc, hkv)
    kct = jnp.swapaxes(kc, 1, 2)
    sub = min(SWA_BLOCKS_PER_STEP, n // qb)
    return pl.pallas_call(
        functools.partial(_swa_body, n=n, hkv=hkv, g=g, qb=qb, win=SWA_WINDOW, sub=sub),
        grid=(b, n // (sub * qb)),
        in_specs=[
            pl.BlockSpec(memory_space=pltpu.SMEM),
            pl.BlockSpec((1, sub * qb, dq), lambda bi, i: (bi, i, 0)),
            pl.BlockSpec((1, dkv, n + 2 * qb), lambda bi, i: (bi, 0, 0)),
            pl.BlockSpec((1, n + 2 * qb, 2 * dkv), lambda bi, i: (bi, 0, 0)),
            pl.BlockSpec((1, dkv, lc), lambda bi, i: (bi, 0, 0)),
            pl.BlockSpec((1, lc, 2 * dkv), lambda bi, i: (bi, 0, 0)),
        ],
        out_specs=pl.BlockSpec((1, sub * qb, dq), lambda bi, i: (bi, i, 0)),
        out_shape=jax.ShapeDtypeStruct((b, n, dq), BF16),
        compiler_params=_params("parallel", "parallel"),
        name="swa",
    )(sink.astype(F32), q, kt, vp, kct, vc)


def _full_attn_body(sink_ref, q_ref, kt_ref, v_ref, o_ref, *, hkv, g, qb, has_sink, unit):
    q = q_ref[0]
    units = [(j, [j * g + u * unit + t for t in range(unit)]) for j in range(hkv) for u in range(g // unit)]

    def scores(j, heads):
        qu = jnp.concatenate([q[:, h * HEAD_DIM:(h + 1) * HEAD_DIM] for h in heads], axis=0)
        return jnp.dot(qu, kt_ref[0, j * HEAD_DIM:(j + 1) * HEAD_DIM, :], preferred_element_type=F32)

    pending = [scores(*u) for u in units[:FULL_ATTN_LOOKAHEAD]]
    for idx, (j, heads) in enumerate(units):
        s = pending.pop(0)
        if idx + FULL_ATTN_LOOKAHEAD < len(units):
            pending.append(scores(*units[idx + FULL_ATTN_LOOKAHEAD]))
        m = jnp.max(s, axis=-1, keepdims=True)
        if has_sink:
            s_sink = jnp.concatenate([jnp.full((qb, 1), sink_ref[h], F32) for h in heads], axis=0)
            m = jnp.maximum(m, s_sink)
        e = jnp.exp((s - m).astype(BF16))
        o = jnp.dot(e, v_ref[0, :, j * 2 * HEAD_DIM:(j + 1) * 2 * HEAD_DIM], preferred_element_type=F32)
        den = o[:, HEAD_DIM:HEAD_DIM + 1]
        if has_sink:
            den = den + jnp.exp(s_sink - m)
        o = o[:, :HEAD_DIM] / den
        for t, h in enumerate(heads):
            o_ref[0, :, h * HEAD_DIM:(h + 1) * HEAD_DIM] = o[t * qb:(t + 1) * qb].astype(BF16)


def _full_attention(q, k, v, hkv, sink=None):
    b, n, dq = q.shape
    g = dq // HEAD_DIM // hkv
    qb = min(FULL_ATTN_Q_ROWS, n)
    unit = max(1, FULL_ATTN_UNIT_ROWS // qb)
    nk = k.shape[1]
    dkv = hkv * HEAD_DIM
    kt = jnp.swapaxes(k, 1, 2)
    has_sink = sink is not None
    sink = jnp.zeros((dq // HEAD_DIM,), F32) if sink is None else sink.astype(F32)
    return pl.pallas_call(
        functools.partial(_full_attn_body, hkv=hkv, g=g, qb=qb, has_sink=has_sink, unit=min(unit, g)),
        grid=(b, n // qb),
        in_specs=[
            pl.BlockSpec(memory_space=pltpu.SMEM),
            pl.BlockSpec((1, qb, dq), lambda bi, i: (bi, i, 0)),
            pl.BlockSpec((1, dkv, nk), lambda bi, i: (bi, 0, 0)),
            pl.BlockSpec((1, nk, 2 * dkv), lambda bi, i: (bi, 0, 0)),
        ],
        out_specs=pl.BlockSpec((1, qb, dq), lambda bi, i: (bi, i, 0)),
        out_shape=jax.ShapeDtypeStruct((b, n, dq), BF16),
        compiler_params=_params("parallel", "parallel"),
        name="full_attn",
    )(sink, q, kt, _values_with_ones(v, hkv))


def _short_conv_body(u_ref, w_ref, b_ref, o_ref, *, h1, l2, c):
    slab = lambda f: u_ref[0, :, f, :]
    row = lax.broadcasted_iota(jnp.int32, (h1, c), 0)
    for f in range(l2):
        prev = slab(f - 1) if f > 0 else jnp.where(row == 0, 0.0, pltpu.roll(slab(l2 - 1), 1, 0))
        nxt = slab(f + 1) if f < l2 - 1 else jnp.where(row == h1 - 1, 0.0, pltpu.roll(slab(0), h1 - 1, 0))
        o_ref[0, 0, :, f * c:(f + 1) * c] = (prev * w_ref[0:1, :] + slab(f) * w_ref[1:2, :]
                                             + nxt * w_ref[2:3, :] + b_ref[...])


def _short_conv(u, w, bias, l2):
    b, n, c3 = u.shape
    c = c3 // 3
    h1 = n // l2
    return pl.pallas_call(
        functools.partial(_short_conv_body, h1=h1, l2=l2, c=c),
        grid=(b, 3),
        in_specs=[
            pl.BlockSpec((1, h1, l2, c), lambda bi, j: (bi, 0, 0, j)),
            pl.BlockSpec((3, c), lambda bi, j: (0, j)),
            pl.BlockSpec((1, c), lambda bi, j: (0, j)),
        ],
        out_specs=pl.BlockSpec((1, 1, h1, l2 * c), lambda bi, j: (j, bi, 0, 0)),
        out_shape=jax.ShapeDtypeStruct((3, b, h1, l2 * c), F32),
        compiler_params=_params("parallel", "parallel"),
        name="short_conv",
    )(u.reshape(b, h1, l2, c3), w, bias.reshape(1, c3))


def _filter_body(band_ref, w1_ref, b1_ref, w2_ref, b2_ref, w3_ref, fr_ref, dl_ref, o_ref, s_ref, *, n, rt, c):
    i = pl.program_id(0)
    hp = lax.Precision.HIGHEST
    m = i * rt + lax.broadcasted_iota(jnp.int32, (rt, 1), 0)
    pos = jnp.where(m < n, m, 2 * n - m).astype(F32)
    t_norm = pos / max(n - 1, 1)
    ang = (2.0 * math.pi / n) * pos * band_ref[...]
    lane = lax.broadcasted_iota(jnp.int32, (rt, LANES), 1)
    z = jnp.where(lane == 0, t_norm,
                  jnp.where(lane <= HY_BANDS, jnp.cos(ang),
                            jnp.where(lane <= 2 * HY_BANDS, -jnp.sin(ang), 0.0)))
    fr = fr_ref[...]
    hdn = jnp.sin(fr * (jnp.dot(z, w1_ref[...], precision=hp, preferred_element_type=F32) + b1_ref[...]))
    hdn = jnp.sin(fr * (jnp.dot(hdn, w2_ref[...], precision=hp, preferred_element_type=F32) + b2_ref[...]))
    h = jnp.dot(hdn, w3_ref[...], precision=hp, preferred_element_type=F32)
    h = h * jnp.exp(-t_norm * dl_ref[...])
    half = HY_ORDER * c
    sel = jnp.where(m < n, h[:, :half], jnp.where(m > n, -h[:, half:], 0.0))
    for o in range(HY_ORDER):
        o_ref[o] = sel[:, o * c:(o + 1) * c]

    @pl.when(i == 0)
    def _():
        s_ref[...] = jnp.zeros_like(s_ref)

    s_ref[...] += jnp.sum(jnp.abs(sel), axis=0, keepdims=True)


def _hyena_filters(n, w1, b1, w2, b2, w3, freq, c):
    rt = min(1024, n)
    hid = w1.shape[1]
    bands = jnp.linspace(1e-4, HY_BANDS - 1, HY_BANDS, dtype=F32)
    band_row = jnp.zeros((1, LANES), F32).at[0, 1:1 + 2 * HY_BANDS].set(jnp.tile(bands, 2))
    w1p = jnp.zeros((LANES, hid), F32).at[:w1.shape[0]].set(w1)
    max_decay = math.log(HY_DECAY_TARGET) / HY_FAST_DECAY
    min_decay = math.log(HY_DECAY_TARGET) / HY_SLOW_DECAY
    deltas = jnp.abs(jnp.linspace(min_decay, max_decay, c, dtype=F32))
    dl = jnp.tile(deltas, HY_DIRS * HY_ORDER).reshape(1, -1)
    full = lambda a: pl.BlockSpec(a.shape, lambda i: (0,) * a.ndim)
    args = (band_row, w1p, b1.reshape(1, hid), w2, b2.reshape(1, hid), w3, freq.reshape(1, hid), dl)
    return pl.pallas_call(
        functools.partial(_filter_body, n=n, rt=rt, c=c),
        grid=(2 * n // rt,),
        in_specs=[full(a) for a in args],
        out_specs=[pl.BlockSpec((HY_ORDER, rt, c), lambda i: (0, i, 0)),
                   pl.BlockSpec((1, HY_ORDER * c), lambda i: (0, 0))],
        out_shape=[jax.ShapeDtypeStruct((HY_ORDER, 2 * n, c), F32),
                   jax.ShapeDtypeStruct((1, HY_ORDER * c), F32)],
        compiler_params=_params("arbitrary"),
        name="hyena_filter",
    )(*args)


def _dft_split(n):
    l2 = 32 if n >= 2048 else 16
    return 2 * n // l2, l2


def _dft_constants(n):
    l1, l2 = _dft_split(n)
    h1 = l1 // 2
    nn = 2 * n
    k1 = np.arange(h1)[:, None]
    a = 2 * np.pi * (k1 + 0.5) * np.arange(l1)[None, :] / l1
    w1 = np.concatenate([np.cos(a), -np.sin(a)], axis=0)
    t = 2 * np.pi * (k1 + 0.5) * np.arange(l2)[None, :] / nn
    tw_cos, tw_sin = np.cos(t), np.sin(t)
    p = 2 * np.pi * np.arange(l2)[:, None] * np.arange(l2)[None, :] / l2
    w2 = np.block([[np.cos(p), np.sin(p)], [-np.sin(p), np.cos(p)]])
    w2i = np.block([[np.cos(p), -np.sin(p)], [np.sin(p), np.cos(p)]])
    ai = 2 * np.pi * np.arange(h1)[:, None] * (np.arange(h1)[None, :] + 0.5) / l1
    w1i = (2.0 / nn) * np.concatenate([np.cos(ai), -np.sin(ai)], axis=1)
    c = lambda m, dt: jnp.asarray(m, dtype=dt)
    return dict(
        l1=l1, l2=l2, h1=h1,
        w1=c(w1, BF16), w2=c(w2, BF16), w2i=c(w2i, BF16), w1i=c(w1i, BF16),
        tw_cos_fwd=c(tw_cos.T[:, :, None], F32), tw_sin_fwd=c(tw_sin.T[:, :, None], F32),
        tw_cos_inv=c(tw_cos[:, :, None], F32), tw_sin_inv=c(tw_sin[:, :, None], F32),
    )


def _ct_fwd1_body(x_ref, w_ref, tc_ref, ts_ref, nrm_ref, o_ref, *, lb, c, h1, normalise):
    for q in range(lb):
        x = x_ref[0, 0, :, q * c:(q + 1) * c]
        if normalise:
            x = x / (nrm_ref[0] + HY_FILTER_EPS)
        a = jnp.dot(w_ref[...], x.astype(BF16), preferred_element_type=F32)
        ar, ai = a[:h1], a[h1:]
        tc, ts = tc_ref[q], ts_ref[q]
        o_ref[0, 0, q] = ar * tc + ai * ts
        o_ref[0, 1, q] = ai * tc - ar * ts


def _ct_fwd1(xs, idx, consts, c, norms=None, lb=8):
    _, b, k1n, _ = xs.shape
    l2, h1 = consts["l2"], consts["h1"]
    lb = min(lb, l2)
    w = consts["w1"][:, :k1n]
    normalise = norms is not None
    if norms is None:
        norms = jnp.zeros((b, 1, c), F32)
    return pl.pallas_call(
        functools.partial(_ct_fwd1_body, lb=lb, c=c, h1=h1, normalise=normalise),
        grid=(b, l2 // lb),
        in_specs=[
            pl.BlockSpec((1, 1, k1n, lb * c), lambda bi, i: (idx, bi, 0, i)),
            pl.BlockSpec(w.shape, lambda bi, i: (0, 0)),
            pl.BlockSpec((lb, h1, 1), lambda bi, i: (i, 0, 0)),
            pl.BlockSpec((lb, h1, 1), lambda bi, i: (i, 0, 0)),
            pl.BlockSpec((1, 1, c), lambda bi, i: (bi, 0, 0)),
        ],
        out_specs=pl.BlockSpec((1, 2, lb, h1, c), lambda bi, i: (bi, 0, i, 0, 0)),
        out_shape=jax.ShapeDtypeStruct((b, 2, l2, h1, c), F32),
        compiler_params=_params("parallel", "parallel"),
        name="ct_fwd1",
    )(xs, w, consts["tw_cos_fwd"], consts["tw_sin_fwd"], norms)


def _fine_rows(ref, lead, q):
    return jnp.concatenate([ref[lead + (0, slice(None), q, slice(None))],
                            ref[lead + (1, slice(None), q, slice(None))]], axis=0)


def _ct_spec_body(a_ref, w2_ref, o_ref, *, l2, kb):
    for q in range(kb):
        x = jnp.dot(w2_ref[...], _fine_rows(a_ref, (0,), q).astype(BF16), preferred_element_type=F32)
        o_ref[0, 0, :, q, :] = x[:l2]
        o_ref[0, 1, :, q, :] = x[l2:]


def _ct_spectrum(a, consts, c, kb=8):
    b = a.shape[0]
    l2, h1 = consts["l2"], consts["h1"]
    kb = min(kb, h1)
    blk = pl.BlockSpec((1, 2, l2, kb, c), lambda bi, i: (bi, 0, 0, i, 0))
    return pl.pallas_call(
        functools.partial(_ct_spec_body, l2=l2, kb=kb),
        grid=(b, h1 // kb),
        in_specs=[blk, pl.BlockSpec((2 * l2, 2 * l2), lambda bi, i: (0, 0))],
        out_specs=blk,
        out_shape=jax.ShapeDtypeStruct((b, 2, l2, h1, c), F32),
        compiler_params=_params("parallel", "parallel"),
        name="ct_spectrum",
    )(a, consts["w2"])


def _ct_mid_body(a_ref, h_ref, w2_ref, w2i_ref, tc_ref, ts_ref, o_ref, *, l2, kb):
    for q in range(kb):
        x = jnp.dot(w2_ref[...], _fine_rows(a_ref, (0,), q).astype(BF16), preferred_element_type=F32)
        xr, xi = x[:l2], x[l2:]
        hr, hi = h_ref[0, 0, :, q, :], h_ref[0, 1, :, q, :]
        y = jnp.concatenate([xr * hr - xi * hi, xr * hi + xi * hr], axis=0).astype(BF16)
        bm = jnp.dot(w2i_ref[...], y, preferred_element_type=F32)
        br, bi = bm[:l2], bm[l2:]
        tc, ts = tc_ref[q], ts_ref[q]
        o_ref[0, 0, :, q, :] = br * tc - bi * ts
        o_ref[0, 1, :, q, :] = br * ts + bi * tc


def _ct_mid(a, hspec, order, consts, c, kb=16):
    b = a.shape[0]
    l2, h1 = consts["l2"], consts["h1"]
    kb = min(kb, h1)
    blk = pl.BlockSpec((1, 2, l2, kb, c), lambda bi, i: (bi, 0, 0, i, 0))
    return pl.pallas_call(
        functools.partial(_ct_mid_body, l2=l2, kb=kb),
        grid=(b, h1 // kb),
        in_specs=[
            blk,
            pl.BlockSpec((1, 2, l2, kb, c), lambda bi, i: (order, 0, 0, i, 0)),
            pl.BlockSpec((2 * l2, 2 * l2), lambda bi, i: (0, 0)),
            pl.BlockSpec((2 * l2, 2 * l2), lambda bi, i: (0, 0)),
            pl.BlockSpec((kb, l2, 1), lambda bi, i: (i, 0, 0)),
            pl.BlockSpec((kb, l2, 1), lambda bi, i: (i, 0, 0)),
        ],
        out_specs=blk,
        out_shape=jax.ShapeDtypeStruct((b, 2, l2, h1, c), F32),
        compiler_params=_params("parallel", "parallel"),
        name="ct_mid",
    )(a, hspec, consts["w2"], consts["w2i"], consts["tw_cos_inv"], consts["tw_sin_inv"])


def _ct_inv1_body(b_ref, w_ref, u_ref, gate_ref, skip_ref, o_ref, *, lb, c, by_position):
    for q in range(lb):
        bb = jnp.concatenate([b_ref[0, 0, q], b_ref[0, 1, q]], axis=0).astype(BF16)
        y = jnp.dot(w_ref[...], bb, preferred_element_type=F32)
        cs = slice(q * c, (q + 1) * c)
        out = gate_ref[0, 0, :, cs] * (y + u_ref[0, 0, :, cs] * skip_ref[...])
        if by_position:
            o_ref[0, :, q, :] = out
        else:
            o_ref[0, :, cs] = out


def _ct_inv1(bsp, u, u_idx, gate, gate_idx, skip, consts, c, by_position, lb=8):
    b = bsp.shape[0]
    l2, h1 = consts["l2"], consts["h1"]
    lb = min(lb, l2)
    if by_position:
        out_spec = pl.BlockSpec((1, h1, lb, c), lambda bi, i: (bi, 0, i, 0))
        out_shape = jax.ShapeDtypeStruct((b, h1, l2, c), F32)
    else:
        out_spec = pl.BlockSpec((1, h1, lb * c), lambda bi, i: (bi, 0, i))
        out_shape = jax.ShapeDtypeStruct((b, h1, l2 * c), F32)
    return pl.pallas_call(
        functools.partial(_ct_inv1_body, lb=lb, c=c, by_position=by_position),
        grid=(b, l2 // lb),
        in_specs=[
            pl.BlockSpec((1, 2, lb, h1, c), lambda bi, i: (bi, 0, i, 0, 0)),
            pl.BlockSpec((h1, 2 * h1), lambda bi, i: (0, 0)),
            pl.BlockSpec((1, 1, h1, lb * c), lambda bi, i: (u_idx, bi, 0, i)),
            pl.BlockSpec((1, 1, h1, lb * c), lambda bi, i: (gate_idx, bi, 0, i)),
            pl.BlockSpec((1, c), lambda bi, i: (0, 0)),
        ],
        out_specs=out_spec,
        out_shape=out_shape,
        compiler_params=_params("parallel", "parallel"),
        name="ct_inv1",
    )(bsp, consts["w1i"], u, gate, skip.reshape(1, c))


def _hyena(u, conv_w, conv_b, f_w1, f_b1, f_w2, f_b2, f_w3, f_freq, skip):
    b, n, c3 = u.shape
    c = c3 // 3
    consts = _dft_constants(n)
    l1, l2, h1 = consts["l1"], consts["l2"], consts["h1"]
    filt, norms = _hyena_filters(n, f_w1, f_b1, f_w2, f_b2, f_w3, f_freq, c)
    fa = _ct_fwd1(filt.reshape(1, HY_ORDER, l1, l2 * c), 0, consts, c, norms=norms.reshape(HY_ORDER, 1, c))
    hspec = _ct_spectrum(fa, consts, c)
    parts = _short_conv(u, conv_w, conv_b, l2)

    def long_conv_gated(x_stack, x_idx, gate_idx, order, by_position):
        a = _ct_fwd1(x_stack, x_idx, consts, c)
        bsp = _ct_mid(a, hspec, order, consts, c)
        return _ct_inv1(bsp, x_stack, x_idx, parts, gate_idx, skip[order], consts, c, by_position)

    z = long_conv_gated(parts, 0, 1, 0, False)
    return long_conv_gated(z[None], 0, 2, 1, True).reshape(b, n, c)


def _outproj_body(*refs, n_in):
    ins = refs[:n_in]
    ws = refs[n_in:2 * n_in]
    x_ref, gate_ref, g_ref, sc_ref, sh_ref, rhi_ref, rlo_ref, rb_ref, xo_ref, h_ref, rt_ref = refs[2 * n_in:]
    y = None
    for a_ref, w_ref in zip(ins, ws):
        t = jnp.dot(a_ref[0].astype(BF16), w_ref[...], preferred_element_type=F32)
        y = t if y is None else y + t
    x = x_ref[0] + gate_ref[0] * y
    xo_ref[0] = x
    h = x * lax.rsqrt(jnp.mean(x * x, axis=-1, keepdims=True) + NORM_EPS) * g_ref[...]
    h = h * (1.0 + sc_ref[0]) + sh_ref[0]
    hi = h.astype(BF16)
    h_ref[0] = hi
    lo = (h - hi.astype(F32)).astype(BF16)
    lg = (jnp.dot(hi, rhi_ref[...], preferred_element_type=F32)
          + jnp.dot(lo, rhi_ref[...], preferred_element_type=F32)
          + jnp.dot(hi, rlo_ref[...], preferred_element_type=F32) + rb_ref[...])
    rt_ref[0] = _route(lg)


def _route(lg):
    lane = lax.broadcasted_iota(jnp.int32, lg.shape, 1)
    lane_f = lane.astype(F32)
    neg = -jnp.inf

    def top(v):
        m = jnp.max(v, axis=-1, keepdims=True)
        return m, jnp.min(jnp.where(v == m, lane_f, float(LANES)), axis=-1, keepdims=True)

    gl = jnp.where(lane < N_GROUPS, lg, neg)
    gmax, grp = top(gl)
    p_grp = 1.0 / jnp.sum(jnp.exp(gl - gmax), axis=-1, keepdims=True)
    first = N_GROUPS + grp * EXP_PER_GROUP
    el = jnp.where((lane_f >= first) & (lane_f < first + EXP_PER_GROUP), lg, neg)
    m1, i1 = top(el)
    m2, i2 = top(jnp.where(lane_f == i1, neg, el))
    e2 = jnp.exp(m2 - m1)
    den = 1.0 + e2
    vals = (i1 - N_GROUPS, i2 - N_GROUPS, p_grp * (1.0 / den), p_grp * (e2 / den))
    out = jnp.zeros(lg.shape, F32)
    for k, v in enumerate(vals):
        out = jnp.where(lane == k, v, out)
    return out


def _outproj(ins, ws, x, gate, g, scale, shift, r_hi, r_lo, r_b, tm=512):
    b, n, d = x.shape
    tm = min(tm, n)
    bm = gate.shape[0]
    mod_map = (lambda bi, i: (bi, 0, 0)) if bm > 1 else (lambda bi, i: (0, 0, 0))
    row = lambda wd: pl.BlockSpec((1, tm, wd), lambda bi, i: (bi, i, 0))
    full = lambda a: pl.BlockSpec(a.shape, lambda bi, i: (0,) * a.ndim)
    mod = pl.BlockSpec((1, 1, d), mod_map)
    return pl.pallas_call(
        functools.partial(_outproj_body, n_in=len(ins)),
        grid=(b, n // tm),
        in_specs=([row(a.shape[-1]) for a in ins] + [full(w) for w in ws]
                  + [row(d), mod, pl.BlockSpec((1, d), lambda bi, i: (0, 0)), mod, mod,
                     full(r_hi), full(r_lo), full(r_b)]),
        out_specs=[row(d), row(d), row(LANES)],
        out_shape=[jax.ShapeDtypeStruct((b, n, d), F32), jax.ShapeDtypeStruct((b, n, d), BF16),
                   jax.ShapeDtypeStruct((b, n, LANES), F32)],
        compiler_params=_params("parallel", "parallel"),
        name="outproj",
    )(*ins, *ws, x, gate, g.reshape(1, d), scale, shift, r_hi, r_lo, r_b)


def _rank_body(rt_ref, tri_ref, upper_ref, pos_ref, cnt_ref):
    rt = rt_ref[...]
    lane_i = lax.broadcasted_iota(jnp.int32, rt.shape, 1)
    lane = lane_i.astype(F32)
    oh_a = lane == rt[:, 0:1]
    oh_b = lane == rt[:, 1:2]
    one_a = jnp.where(oh_a, 1.0, 0.0)
    one_b = jnp.where(oh_b, 1.0, 0.0)
    before_a = jnp.dot(tri_ref[...], one_a.astype(BF16), preferred_element_type=F32)
    before_b = jnp.dot(tri_ref[...], one_b.astype(BF16), preferred_element_type=F32)
    tot_a = jnp.sum(one_a, axis=0, keepdims=True)
    cnt = tot_a + jnp.sum(one_b, axis=0, keepdims=True)
    padded = jnp.floor((cnt + (CHUNK_ROWS - 1)) * (1.0 / CHUNK_ROWS)) * CHUNK_ROWS
    first = jnp.dot(jnp.broadcast_to(padded, (8, LANES)).astype(BF16), upper_ref[...],
                    preferred_element_type=F32)[0:1]
    pos_a = jnp.sum(jnp.where(oh_a, before_a + first, 0.0), axis=-1, keepdims=True)
    pos_b = jnp.sum(jnp.where(oh_b, before_b + first + tot_a, 0.0), axis=-1, keepdims=True)
    is_gate = (lane_i >= TOP_K) & (lane_i < 2 * TOP_K)
    pos_ref[...] = jnp.where(lane_i == 0, pos_a, jnp.where(lane_i == 1, pos_b, jnp.where(is_gate, rt, 0.0)))
    cnt_ref[0] = jnp.broadcast_to(cnt, (8, LANES))


def _rank(route):
    t = route.shape[0]
    tm = TOKEN_TILE
    tri = jnp.asarray(np.tril(np.ones((tm, tm)), -1), dtype=BF16)
    upper = jnp.asarray(np.triu(np.ones((LANES, LANES)), 1), dtype=BF16)
    return pl.pallas_call(
        _rank_body,
        grid=(t // tm,),
        in_specs=[pl.BlockSpec((tm, LANES), lambda i: (i, 0)), pl.BlockSpec((tm, tm), lambda i: (0, 0)),
                  pl.BlockSpec((LANES, LANES), lambda i: (0, 0))],
        out_specs=[pl.BlockSpec((tm, LANES), lambda i: (i, 0)), pl.BlockSpec((1, 8, LANES), lambda i: (i, 0, 0))],
        out_shape=[jax.ShapeDtypeStruct((t, LANES), F32), jax.ShapeDtypeStruct((t // tm, 8, LANES), F32)],
        compiler_params=_params("parallel"),
        name="moe_rank",
    )(route, tri, upper)


def _chunk_tables(cnt, n_blocks):
    padded = (cnt + CHUNK_ROWS - 1) // CHUNK_ROWS * CHUNK_ROWS
    run_end = jnp.cumsum(padded, axis=1)
    run_start = run_end - padded
    seg_rows = jnp.sum(padded, axis=0)
    seg_rows = (seg_rows + EXPERT_ROWS - 1) // EXPERT_ROWS * EXPERT_ROWS
    seg_end = jnp.cumsum(seg_rows)
    dst_start = (seg_end - seg_rows)[None, :] + jnp.cumsum(padded, axis=0) - padded
    row0 = jnp.arange(BUF_CHUNKS, dtype=jnp.int32) * CHUNK_ROWS
    chunk_exp = jnp.minimum(jnp.sum(run_end[:, None, :] <= row0[None, :, None], axis=-1), N_EXPERTS - 1)
    onehot = chunk_exp[:, :, None] == jnp.arange(N_EXPERTS, dtype=jnp.int32)[None, None, :]
    dst = jnp.sum(jnp.where(onehot, (dst_start - run_start)[:, None, :], 0), axis=-1) + row0[None, :]
    n_chunks = run_end[:, -1:] // CHUNK_ROWS
    table = jnp.concatenate(
        [dst, n_chunks, jnp.zeros((cnt.shape[0], TABLE_WORDS - BUF_CHUNKS - 1), jnp.int32)], axis=1)
    block_row0 = jnp.arange(n_blocks, dtype=jnp.int32) * EXPERT_ROWS
    block_exp = jnp.minimum(jnp.sum(seg_end[None, :] <= block_row0[:, None], axis=1), N_EXPERTS - 1)
    n_used = (seg_end[-1] // EXPERT_ROWS).reshape(1)
    return table.astype(jnp.int32), block_exp.astype(jnp.int32), n_used.astype(jnp.int32)


def _pack_halves(x):
    w = x.shape[-1] // 2
    lo = lax.bitcast_convert_type(x[:, :w].astype(BF16).astype(F32), jnp.int32)
    hi = lax.bitcast_convert_type(x[:, w:].astype(BF16).astype(F32), jnp.int32)
    return lax.shift_right_logical(lo, jnp.int32(16)) | (hi & jnp.int32(-65536))


def _unpack_halves(p):
    lo = lax.bitcast_convert_type(lax.shift_left(p, jnp.int32(16)), F32)
    hi = lax.bitcast_convert_type(p & jnp.int32(-65536), F32)
    return jnp.concatenate([lo, hi], axis=-1).astype(BF16)


def _start_chunks(tab_ref, tile, make_copy):
    base = tile * TABLE_WORDS

    def issue(c):
        make_copy(pl.multiple_of(c * CHUNK_ROWS, CHUNK_ROWS),
                  pl.multiple_of(tab_ref[base + c], CHUNK_ROWS)).start()

    _for_each_chunk(tab_ref[base + BUF_CHUNKS], issue)


def _wait_chunks(tab_ref, tile, make_copy):
    _for_each_chunk(tab_ref[tile * TABLE_WORDS + BUF_CHUNKS], lambda c: make_copy(0, 0).wait())


def _for_each_chunk(n, fn):
    groups = n // LOOP_GROUP

    def grouped(i, carry):
        for k in range(LOOP_GROUP):
            fn(i * LOOP_GROUP + k)
        return carry

    def single(c, carry):
        fn(c)
        return carry

    lax.fori_loop(0, groups, grouped, 0)
    lax.fori_loop(groups * LOOP_GROUP, n, single, 0)


def _dispatch_body(tab_ref, h_ref, pos_ref, xs_in, xs_out, buf, sem_rows, *, tile0, n_steps):
    del xs_in
    step = pl.program_id(0)
    dp = h_ref.shape[1] // 2

    def copies(at_step):
        slot = at_step % 2
        return lambda src, dst: pltpu.make_async_copy(
            buf.at[slot, pl.ds(src, CHUNK_ROWS)], xs_out.at[pl.ds(dst, CHUNK_ROWS)], sem_rows.at[slot])

    @pl.when(step >= 2)
    def _():
        _wait_chunks(tab_ref, tile0 + step - 2, copies(step - 2))

    tm = h_ref.shape[0]
    row = lax.broadcasted_iota(jnp.int32, (BUF_ROWS, tm), 0).astype(F32)
    oh_a = row == pos_ref[0, 0:1, :]
    oh_b = row == pos_ref[0, 1:2, :]
    buf[step % 2, :, :dp] = _pack_halves(jnp.dot(jnp.where(oh_a | oh_b, 1.0, 0.0).astype(BF16), h_ref[...],
                                                 preferred_element_type=F32))
    gate = jnp.sum(jnp.where(oh_a, pos_ref[0, 2:3, :], 0.0) + jnp.where(oh_b, pos_ref[0, 3:4, :], 0.0),
                   axis=-1, keepdims=True)
    buf[step % 2, :, dp:] = lax.bitcast_convert_type(jnp.broadcast_to(gate, (BUF_ROWS, LANES)), jnp.int32)
    _start_chunks(tab_ref, tile0 + step, copies(step))

    @pl.when(step == n_steps - 1)
    def _():
        @pl.when(step >= 1)
        def _():
            _wait_chunks(tab_ref, tile0 + step - 1, copies(step - 1))

        _wait_chunks(tab_ref, tile0 + step, copies(step))


def _dispatch(h, pos_t, table, xs, tile0):
    t, d = h.shape
    tm = TOKEN_TILE
    n_steps = t // tm
    grid_spec = pltpu.PrefetchScalarGridSpec(
        num_scalar_prefetch=1,
        grid=(n_steps,),
        in_specs=[
            pl.BlockSpec((tm, d), lambda i, tab: (i, 0)),
            pl.BlockSpec((1, 8, tm), lambda i, tab: (tile0 + i, 0, 0)),
            pl.BlockSpec(memory_space=pl.ANY),
        ],
        out_specs=pl.BlockSpec(memory_space=pl.ANY),
        scratch_shapes=[
            pltpu.VMEM((2, BUF_ROWS, d // 2 + LANES), jnp.int32),
            pltpu.SemaphoreType.DMA((2,)),
        ],
    )
    return pl.pallas_call(
        functools.partial(_dispatch_body, tile0=tile0, n_steps=n_steps),
        grid_spec=grid_spec,
        out_shape=jax.ShapeDtypeStruct(xs.shape, xs.dtype),
        input_output_aliases={3: 0},
        compiler_params=_params("arbitrary"),
        name="moe_dispatch",
    )(table.reshape(-1), h, pos_t, xs)


def _expert_body(bexp_ref, nused_ref, x_ref, wg_ref, wu_ref, wd_ref, o_ref):
    @pl.when(pl.program_id(0) < nused_ref[0])
    def _():
        dp = o_ref.shape[1]
        xb = _unpack_halves(x_ref[:, :dp])
        gate = lax.bitcast_convert_type(x_ref[:, dp:dp + 1], F32)
        gt = jnp.dot(xb, wg_ref[0, 0].astype(BF16), preferred_element_type=F32)
        up = jnp.dot(xb, wu_ref[0, 0].astype(BF16), preferred_element_type=F32)
        hid = (gt * jax.nn.sigmoid(gt) * up).astype(BF16)
        o_ref[...] = _pack_halves(jnp.dot(hid, wd_ref[0, 0].astype(BF16), preferred_element_type=F32) * gate)

    @pl.when(pl.program_id(0) >= nused_ref[0])
    def _():
        o_ref[...] = jnp.zeros_like(o_ref)


def _experts(xs, block_exp, n_used, w_gate, w_up, w_down, layer):
    rows, width = xs.shape
    d, de = w_gate.shape[2:]
    used = lambda i, be, nu: (jnp.minimum(i, nu[0] - 1), 0)
    expert = lambda i, be, nu: (layer, be[i], 0, 0)
    grid_spec = pltpu.PrefetchScalarGridSpec(
        num_scalar_prefetch=2,
        grid=(rows // EXPERT_ROWS,),
        in_specs=[
            pl.BlockSpec((EXPERT_ROWS, width), used),
            pl.BlockSpec((1, 1, d, de), expert),
            pl.BlockSpec((1, 1, d, de), expert),
            pl.BlockSpec((1, 1, de, d), expert),
        ],
        out_specs=pl.BlockSpec((EXPERT_ROWS, d // 2), lambda i, be, nu: (i, 0)),
    )
    return pl.pallas_call(
        _expert_body,
        grid_spec=grid_spec,
        out_shape=jax.ShapeDtypeStruct((rows, d // 2), jnp.int32),
        compiler_params=_params("arbitrary"),
        name="experts",
    )(block_exp, n_used, xs, w_gate, w_up, w_down)


def _combine_body(tab_ref, ys_hbm, x_ref, gate_ref, pos_ref, fg_ref, o_ref, ybuf, sem_rows,
                  *, tile0, n_tiles, n_steps, final_norm):
    step = pl.program_id(0) * n_tiles + pl.program_id(1)

    def copies(at_step):
        slot = at_step % 2
        return lambda dst, src: pltpu.make_async_copy(
            ys_hbm.at[pl.ds(src, CHUNK_ROWS)], ybuf.at[slot, pl.ds(dst, CHUNK_ROWS)], sem_rows.at[slot])

    @pl.when(step == 0)
    def _():
        ybuf[...] = jnp.zeros_like(ybuf)
        _start_chunks(tab_ref, tile0, copies(0))

    @pl.when(step + 1 < n_steps)
    def _():
        _start_chunks(tab_ref, tile0 + step + 1, copies(step + 1))

    _wait_chunks(tab_ref, tile0 + step, copies(step))
    tm = x_ref.shape[1]
    col = lax.broadcasted_iota(jnp.int32, (tm, BUF_ROWS), 1).astype(F32)
    pick = jnp.where((col == pos_ref[:, 0:1]) | (col == pos_ref[:, 1:2]), 1.0, 0.0).astype(BF16)
    out = x_ref[0] + gate_ref[0] * jnp.dot(pick, _unpack_halves(ybuf[step % 2]), preferred_element_type=F32)
    if final_norm:
        out = out * lax.rsqrt(jnp.mean(out * out, axis=-1, keepdims=True) + NORM_EPS) * fg_ref[...]
    o_ref[0] = out


def _combine(ys, table, pos, x, gate, tile0, final_g=None):
    b, n, d = x.shape
    tm = min(TOKEN_TILE, n)
    n_tiles = n // tm
    bm = gate.shape[0]
    mod_map = (lambda bi, i, tab: (bi, 0, 0)) if bm > 1 else (lambda bi, i, tab: (0, 0, 0))
    final_norm = final_g is not None
    fg = final_g.reshape(1, d) if final_norm else jnp.ones((1, d), F32)
    grid_spec = pltpu.PrefetchScalarGridSpec(
        num_scalar_prefetch=1,
        grid=(b, n_tiles),
        in_specs=[
            pl.BlockSpec(memory_space=pl.ANY),
            pl.BlockSpec((1, tm, d), lambda bi, i, tab: (bi, i, 0)),
            pl.BlockSpec((1, 1, d), mod_map),
            pl.BlockSpec((tm, LANES), lambda bi, i, tab: (tile0 + bi * n_tiles + i, 0)),
            pl.BlockSpec((1, d), lambda bi, i, tab: (0, 0)),
        ],
        out_specs=pl.BlockSpec((1, tm, d), lambda bi, i, tab: (bi, i, 0)),
        scratch_shapes=[
            pltpu.VMEM((2, BUF_ROWS, d // 2), jnp.int32),
            pltpu.SemaphoreType.DMA((2,)),
        ],
    )
    return pl.pallas_call(
        functools.partial(_combine_body, tile0=tile0, n_tiles=n_tiles, n_steps=b * n_tiles,
                          final_norm=final_norm),
        grid_spec=grid_spec,
        out_shape=jax.ShapeDtypeStruct((b, n, d), F32),
        compiler_params=_params("arbitrary", "arbitrary"),
        name="moe_combine",
    )(table.reshape(-1), ys, x, gate, pos, fg)


def _router_weights(wg, bg, we, be):
    d = wg.shape[0]
    w = jnp.zeros((d, LANES), F32).at[:, :N_GROUPS].set(wg).at[:, N_GROUPS:N_GROUPS + N_EXPERTS].set(we)
    bias = jnp.zeros((1, LANES), F32).at[0, :N_GROUPS].set(bg).at[0, N_GROUPS:N_GROUPS + N_EXPERTS].set(be)
    hi = w.astype(BF16)
    lo = (w - hi.astype(F32)).astype(BF16)
    return hi, lo, bias


def kernel(x, c, ctx, c_ctx, ada_w, ada_b, norm1_g, norm2_g, ev_w_in, ev_w_out, hy_conv_w, hy_conv_b, hy_f_w1, hy_f_b1, hy_f_w2, hy_f_b2, hy_f_w3, hy_f_freq, hy_skip, swa_sink, od_w_qkv, od_w_out, od_q_norm_g, od_k_norm_g, rt_group_w, rt_group_b, rt_exp_w, rt_exp_b, moe_w_gate, moe_w_up, moe_w_down, final_norm_g):
    b, n, d = x.shape
    lc = ctx.shape[1]
    depth = ada_w.shape[0]
    rope = _rope_tables(n)
    xc = ctx
    sc = jax.nn.silu(c)
    scc = jax.nn.silu(c_ctx)
    q_scale = HEAD_DIM ** -0.5
    for layer in range(depth):
        with_ctx = layer < depth - 1
        mod = (sc @ ada_w[layer] + ada_b[layer]).reshape(b, N_MOD, 1, d)
        modc = (scc @ ada_w[layer] + ada_b[layer]).reshape(1, N_MOD, 1, d)
        m = [mod[:, k] for k in range(N_MOD)]
        mc = [modc[:, k] for k in range(N_MOD)]
        r_hi, r_lo, r_b = _router_weights(rt_group_w[layer], rt_group_b[layer], rt_exp_w[layer], rt_exp_b[layer])
        if layer % 2 == 0:
            e = layer // 2
            c_hy = hy_conv_w.shape[-1] // 3
            d_hy = 3 * c_hy
            hq = swa_sink.shape[-1]
            d_q = hq * HEAD_DIM
            hkv = hq // 4
            d_kv = hkv * HEAD_DIM
            w_in = ev_w_in[e].astype(BF16)
            w_out = ev_w_out[e].astype(BF16)
            hy_args = (hy_conv_w[e], hy_conv_b[e], hy_f_w1[e], hy_f_b1[e], hy_f_w2[e], hy_f_b2[e],
                       hy_f_w3[e], hy_f_freq[e], hy_skip[e])
            u, q, k, v = _proj(x, norm1_g[layer], m[1], m[0], w_in, [
                (0, d_hy, "f32", None, False, 1.0),
                (d_hy, d_q, "qk", None, True, q_scale),
                (d_hy + d_q, d_kv, "qk", None, True, 1.0),
                (d_hy + d_q + d_kv, d_kv, "bf16", None, False, 1.0)], rope_tabs=rope)
            if with_ctx:
                uc, qc, kc, vc = _proj(xc, norm1_g[layer], mc[1], mc[0], w_in, [
                    (0, d_hy, "f32", None, False, 1.0),
                    (d_hy, d_q, "qk", None, False, q_scale),
                    (d_hy + d_q, d_kv, "bf16", None, False, 1.0),
                    (d_hy + d_q + d_kv, d_kv, "bf16", None, False, 1.0)])
            else:
                kc, vc = _proj(xc, norm1_g[layer], mc[1], mc[0], w_in, [
                    (d_hy + d_q, d_kv, "bf16", None, False, 1.0),
                    (d_hy + d_q + d_kv, d_kv, "bf16", None, False, 1.0)])
            y_hy = _hyena(u, *hy_args)
            y_att = _windowed_attention(q, k, v, kc, vc, swa_sink[e], hkv)
            mix_in, mix_w = [y_hy, y_att], [w_out[:c_hy], w_out[c_hy:]]
            if with_ctx:
                yc_hy = _hyena(uc, *hy_args)
                yc_att = _full_attention(qc, kc, vc, hkv, sink=swa_sink[e])
                mixc_in = [yc_hy, yc_att]
        else:
            o = layer // 2
            hkv = od_w_qkv.shape[-1] // HEAD_DIM // 6
            hq = 4 * hkv
            d_q = hq * HEAD_DIM
            d_kv = hkv * HEAD_DIM
            w_qkv = od_w_qkv[o].astype(BF16)
            w_out = od_w_out[o].astype(BF16)
            norm_g = jnp.zeros((8, LANES), F32).at[0].set(jnp.tile(od_q_norm_g[o], 2)).at[1].set(
                jnp.tile(od_k_norm_g[o], 2))
            q, k, v = _proj(x, norm1_g[layer], m[1], m[0], w_qkv, [
                (0, d_q, "qk", 0, True, q_scale),
                (d_q, d_kv, "qk", 1, True, 1.0),
                (d_q + d_kv, d_kv, "bf16", None, False, 1.0)], rope_tabs=rope, norm_g=norm_g)
            if with_ctx:
                qc, kc, vc = _proj(xc, norm1_g[layer], mc[1], mc[0], w_qkv, [
                    (0, d_q, "qk", 0, False, q_scale),
                    (d_q, d_kv, "qk", 1, False, 1.0),
                    (d_q + d_kv, d_kv, "bf16", None, False, 1.0)], norm_g=norm_g)
            else:
                kc, vc = _proj(xc, norm1_g[layer], mc[1], mc[0], w_qkv, [
                    (d_q, d_kv, "qk", 1, False, 1.0),
                    (d_q + d_kv, d_kv, "bf16", None, False, 1.0)], norm_g=norm_g)
            y_att = _full_attention(q, jnp.concatenate([kc, k], axis=1), jnp.concatenate([vc, v], axis=1), hkv)
            mix_in, mix_w = [y_att], [w_out]
            if with_ctx:
                mixc_in = [_full_attention(qc, kc, vc, hkv)]
        x, h2, rt = _outproj(mix_in, mix_w, x, m[2], norm2_g[layer], m[4], m[3], r_hi, r_lo, r_b)
        route_flat = rt.reshape(b * n, LANES)
        if with_ctx:
            xc, h2c, rtc = _outproj(mixc_in, mix_w, xc, mc[2], norm2_g[layer], mc[4], mc[3], r_hi, r_lo, r_b)
            route_flat = jnp.concatenate([route_flat, rtc.reshape(b * lc, LANES)], axis=0)
        n_tok = route_flat.shape[0]
        n_tiles = n_tok // TOKEN_TILE
        lat_tiles = b * n // TOKEN_TILE
        max_rows = n_tok * TOP_K + n_tiles * N_EXPERTS * (CHUNK_ROWS - 1)
        n_blocks = -(-max_rows // EXPERT_ROWS) + N_EXPERTS
        pos, cnt = _rank(route_flat)
        table, block_exp, n_used = _chunk_tables(cnt[:, 0, :N_EXPERTS].astype(jnp.int32), n_blocks)
        pos_t = jnp.swapaxes(pos[:, :8].reshape(n_tiles, TOKEN_TILE, 8), 1, 2)
        xs = jnp.zeros((n_blocks * EXPERT_ROWS, d // 2 + LANES), jnp.int32)
        xs = _dispatch(h2.reshape(b * n, d), pos_t, table, xs, 0)
        if with_ctx:
            xs = _dispatch(h2c.reshape(b * lc, d), pos_t, table, xs, lat_tiles)
        ys = _experts(xs, block_exp, n_used, moe_w_gate, moe_w_up, moe_w_down, layer)
        x = _combine(ys, table, pos, x, m[5], 0, final_g=None if with_ctx else final_norm_g)
        if with_ctx:
            xc = _combine(ys, table, pos, xc.reshape(b * lc // TOKEN_TILE, TOKEN_TILE, d), mc[5],
                          lat_tiles).reshape(b, lc, d)
    return x
```

```python
import functools
import math

import numpy as np
import jax
import jax.numpy as jnp
from jax import lax
from jax.experimental import pallas as pl
from jax.experimental.pallas import tpu as pltpu

F32 = jnp.float32
BF16 = jnp.bfloat16

HEAD_DIM = 64
GRID_W = 64
ROPE_BASE = 10000.0
Q_BLOCK = 128
NORM_EPS = 1e-6
N_MOD = 6
HY_ORDER = 2
HY_BANDS = 16
HY_DIRS = 2
HY_DECAY_TARGET = 1e-2
HY_FAST_DECAY = 0.3
HY_SLOW_DECAY = 1.5
HY_FILTER_EPS = 1e-6
SWA_WINDOW = 128
N_GROUPS = 4
EXP_PER_GROUP = 8
N_EXPERTS = N_GROUPS * EXP_PER_GROUP
TOP_K = 2
EXPERT_ROWS = 512
SWA_BLOCKS_PER_STEP = 1
LOOP_GROUP = 4
MID_KB = 8
FULL_ATTN_Q_ROWS = 256
FULL_ATTN_UNIT_ROWS = 512
FULL_ATTN_LOOKAHEAD = 1
TOKEN_TILE = 512
CHUNK_ROWS = 8
BUF_ROWS = 1280
BUF_CHUNKS = BUF_ROWS // CHUNK_ROWS
TABLE_WORDS = 256

LANES = 128
VMEM_LIMIT_BYTES = 56 * 1024 * 1024


def _params(*sem):
    return pltpu.CompilerParams(dimension_semantics=sem, vmem_limit_bytes=VMEM_LIMIT_BYTES)


def _rope_tables(n):
    d_axis = HEAD_DIM // 2
    t = jnp.arange(n)
    inv = ROPE_BASE ** (-jnp.arange(0, d_axis, 2, dtype=F32) / d_axis)
    ang_r = (t // GRID_W).astype(F32)[:, None] * inv[None, :]
    ang_c = (t % GRID_W).astype(F32)[:, None] * inv[None, :]
    cos = jnp.concatenate([jnp.cos(ang_r)] * 2 + [jnp.cos(ang_c)] * 2, axis=-1)
    sin = jnp.concatenate([-jnp.sin(ang_r), jnp.sin(ang_r), -jnp.sin(ang_c), jnp.sin(ang_c)], axis=-1)
    return jnp.tile(cos, (1, 2)), jnp.tile(sin, (1, 2))


def _head_mean_matrix():
    i = np.arange(LANES)
    return jnp.asarray((i[:, None] // HEAD_DIM == i[None, :] // HEAD_DIM) / HEAD_DIM, dtype=BF16)


def _proj_body(x_ref, g_ref, sc_ref, sh_ref, w_ref, cos_ref, sin_ref, ng_ref, bd_ref, *out_refs, segs):
    x = x_ref[0]
    h = x * lax.rsqrt(jnp.mean(x * x, axis=-1, keepdims=True) + NORM_EPS) * g_ref[...]
    hb = (h * (1.0 + sc_ref[0]) + sh_ref[0]).astype(BF16)
    for o_ref, (c0, width, kind, norm_row, rope, out_scale) in zip(out_refs, segs):
        seg = jnp.dot(hb, w_ref[:, c0:c0 + width], preferred_element_type=F32)
        if kind == "f32":
            o_ref[0] = seg
            continue
        if kind == "bf16":
            o_ref[0] = seg.astype(BF16)
            continue
        for j in range(width // LANES):
            ch = seg[:, j * LANES:(j + 1) * LANES]
            if norm_row is not None:
                sq = ch * ch
                hi = sq.astype(BF16)
                lo = (sq - hi.astype(F32)).astype(BF16)
                ms = (jnp.dot(hi, bd_ref[...], preferred_element_type=F32)
                      + jnp.dot(lo, bd_ref[...], preferred_element_type=F32))
                ch = ch * lax.rsqrt(ms + NORM_EPS) * ng_ref[norm_row:norm_row + 1, :]
            if rope:
                lane = lax.broadcasted_iota(jnp.int32, ch.shape, 1)
                partner = jnp.where(lane % 32 < 16, pltpu.roll(ch, LANES - 16, 1), pltpu.roll(ch, 16, 1))
                ch = ch * cos_ref[...] + partner * sin_ref[...]
            if out_scale != 1.0:
                ch = ch * out_scale
            o_ref[0, :, j * LANES:(j + 1) * LANES] = ch.astype(BF16)


def _proj(x, g, scale, shift, w, segs, rope_tabs=None, norm_g=None, tm=512):
    b, n, d = x.shape
    tm = min(tm, n)
    bm = scale.shape[0]
    mod_map = (lambda bi, i: (bi, 0, 0)) if bm > 1 else (lambda bi, i: (0, 0, 0))
    if rope_tabs is None:
        cos = sin = jnp.zeros((8, LANES), F32)
        tab_spec = pl.BlockSpec((8, LANES), lambda bi, i: (0, 0))
    else:
        cos, sin = rope_tabs
        tab_spec = pl.BlockSpec((tm, LANES), lambda bi, i: (i, 0))
    if norm_g is None:
        norm_g = jnp.ones((8, LANES), F32)
    out_shape = [jax.ShapeDtypeStruct((b, n, s[1]), F32 if s[2] == "f32" else BF16) for s in segs]
    out_specs = [pl.BlockSpec((1, tm, s[1]), lambda bi, i: (bi, i, 0)) for s in segs]
    return pl.pallas_call(
        functools.partial(_proj_body, segs=tuple(segs)),
        grid=(b, n // tm),
        in_specs=[
            pl.BlockSpec((1, tm, d), lambda bi, i: (bi, i, 0)),
            pl.BlockSpec((1, d), lambda bi, i: (0, 0)),
            pl.BlockSpec((1, 1, d), mod_map),
            pl.BlockSpec((1, 1, d), mod_map),
            pl.BlockSpec(w.shape, lambda bi, i: (0, 0)),
            tab_spec,
            tab_spec,
            pl.BlockSpec(norm_g.shape, lambda bi, i: (0, 0)),
            pl.BlockSpec((LANES, LANES), lambda bi, i: (0, 0)),
        ],
        out_specs=out_specs,
        out_shape=out_shape,
        compiler_params=_params("parallel", "parallel"),
        name="proj",
    )(x, g.reshape(1, d), scale, shift, w, cos, sin, norm_g, _head_mean_matrix())


def _stack_heads(q, j, g):
    return jnp.concatenate(
        [q[:, (j * g + gg) * HEAD_DIM:(j * g + gg + 1) * HEAD_DIM] for gg in range(g)], axis=0)


def _values_with_ones(v, hkv):
    b, nk, _ = v.shape
    ones = jnp.zeros((b, nk, hkv, HEAD_DIM), v.dtype).at[..., 0].set(1)
    return jnp.concatenate([v.reshape(b, nk, hkv, HEAD_DIM), ones], axis=-1).reshape(b, nk, 2 * hkv * HEAD_DIM)


def _sink_column(sink_ref, j, g, qb):
    return jnp.concatenate([jnp.full((qb, 1), sink_ref[j * g + gg], F32) for gg in range(g)], axis=0)


def _swa_body(sink_ref, q_ref, kt_ref, v_ref, kct_ref, vc_ref, o_ref, *, n, hkv, g, qb, win, sub):
    rows = lax.broadcasted_iota(jnp.int32, (g * qb, 3 * qb), 0) % qb
    cols = lax.broadcasted_iota(jnp.int32, (g * qb, 3 * qb), 1)
    in_band = jnp.abs(rows + qb - cols) <= win
    for sb in range(sub):
        blk = pl.program_id(1) * sub + sb
        start = pl.multiple_of(blk * qb, qb)
        qs = slice(sb * qb, (sb + 1) * qb)
        q = q_ref[0, qs, :]
        key_pos = cols + (blk - 1) * qb
        valid = in_band & (key_pos >= 0) & (key_pos < n)

        for j in range(hkv):
            hs = slice(j * HEAD_DIM, (j + 1) * HEAD_DIM)
            q4 = _stack_heads(q, j, g)
            s_lat = jnp.dot(q4, kt_ref[0, hs, pl.ds(start, 3 * qb)], preferred_element_type=F32)
            s_lat = jnp.where(valid, s_lat, -jnp.inf)
            s_ctx = jnp.dot(q4, kct_ref[0, hs, :], preferred_element_type=F32)
            s_sink = _sink_column(sink_ref, j, g, qb)
            m = jnp.maximum(jnp.maximum(jnp.max(s_lat, axis=-1, keepdims=True),
                                        jnp.max(s_ctx, axis=-1, keepdims=True)), s_sink)
            e_lat = jnp.exp((s_lat - m).astype(BF16))
            e_ctx = jnp.exp((s_ctx - m).astype(BF16))
            vs = slice(j * 2 * HEAD_DIM, (j + 1) * 2 * HEAD_DIM)
            o = (jnp.dot(e_ctx, vc_ref[0, :, vs], preferred_element_type=F32)
                 + jnp.dot(e_lat, v_ref[0, pl.ds(start, 3 * qb), vs], preferred_element_type=F32))
            o = o[:, :HEAD_DIM] / (o[:, HEAD_DIM:HEAD_DIM + 1] + jnp.exp(s_sink - m))
            for gg in range(g):
                c0 = (j * g + gg) * HEAD_DIM
                o_ref[0, qs, c0:c0 + HEAD_DIM] = o[gg * qb:(gg + 1) * qb].astype(BF16)


def _windowed_attention(q, k, v, kc, vc, sink, hkv):
    b, n, dq = q.shape
    g = dq // HEAD_DIM // hkv
    qb = Q_BLOCK
    lc = kc.shape[1]
    dkv = hkv * HEAD_DIM
    kt = jnp.swapaxes(jnp.pad(k, ((0, 0), (qb, qb), (0, 0))), 1, 2)
    vp = _values_with_ones(jnp.pad(v, ((0, 0), (qb, qb), (0, 0))), hkv)
    vc = _values_with_ones(vc, hkv)
    kct = jnp.swapaxes(kc, 1, 2)
    sub = min(SWA_BLOCKS_PER_STEP, n // qb)
    return pl.pallas_call(
        functools.partial(_swa_body, n=n, hkv=hkv, g=g, qb=qb, win=SWA_WINDOW, sub=sub),
        grid=(b, n // (sub * qb)),
        in_specs=[
            pl.BlockSpec(memory_space=pltpu.SMEM),
            pl.BlockSpec((1, sub * qb, dq), lambda bi, i: (bi, i, 0)),
            pl.BlockSpec((1, dkv, n + 2 * qb), lambda bi, i: (bi, 0, 0)),
            pl.BlockSpec((1, n + 2 * qb, 2 * dkv), lambda bi, i: (bi, 0, 0)),
            pl.BlockSpec((1, dkv, lc), lambda bi, i: (bi, 0, 0)),
            pl.BlockSpec((1, lc, 2 * dkv), lambda bi, i: (bi, 0, 0)),
        ],
        out_specs=pl.BlockSpec((1, sub * qb, dq), lambda bi, i: (bi, i, 0)),
        out_shape=jax.ShapeDtypeStruct((b, n, dq), BF16),
        compiler_params=_params("parallel", "parallel"),
        name="swa",
    )(sink.astype(F32), q, kt, vp, kct, vc)


def _full_attn_body(sink_ref, q_ref, kt_ref, v_ref, o_ref, *, hkv, g, qb, has_sink, unit):
    q = q_ref[0]
    units = [(j, [j * g + u * unit + t for t in range(unit)]) for j in range(hkv) for u in range(g // unit)]

    def scores(j, heads):
        qu = jnp.concatenate([q[:, h * HEAD_DIM:(h + 1) * HEAD_DIM] for h in heads], axis=0)
        return jnp.dot(qu, kt_ref[0, j * HEAD_DIM:(j + 1) * HEAD_DIM, :], preferred_element_type=F32)

    pending = [scores(*u) for u in units[:FULL_ATTN_LOOKAHEAD]]
    for idx, (j, heads) in enumerate(units):
        s = pending.pop(0)
        if idx + FULL_ATTN_LOOKAHEAD < len(units):
            pending.append(scores(*units[idx + FULL_ATTN_LOOKAHEAD]))
        m = jnp.max(s, axis=-1, keepdims=True)
        if has_sink:
            s_sink = jnp.concatenate([jnp.full((qb, 1), sink_ref[h], F32) for h in heads], axis=0)
            m = jnp.maximum(m, s_sink)
        e = jnp.exp((s - m).astype(BF16))
        o = jnp.dot(e, v_ref[0, :, j * 2 * HEAD_DIM:(j + 1) * 2 * HEAD_DIM], preferred_element_type=F32)
        den = o[:, HEAD_DIM:HEAD_DIM + 1]
        if has_sink:
            den = den + jnp.exp(s_sink - m)
        o = o[:, :HEAD_DIM] / den
        for t, h in enumerate(heads):
            o_ref[0, :, h * HEAD_DIM:(h + 1) * HEAD_DIM] = o[t * qb:(t + 1) * qb].astype(BF16)


def _full_attention(q, k, v, hkv, sink=None):
    b, n, dq = q.shape
    g = dq // HEAD_DIM // hkv
    qb = min(FULL_ATTN_Q_ROWS, n)
    unit = max(1, FULL_ATTN_UNIT_ROWS // qb)
    nk = k.shape[1]
    dkv = hkv * HEAD_DIM
    kt = jnp.swapaxes(k, 1, 2)
    has_sink = sink is not None
    sink = jnp.zeros((dq // HEAD_DIM,), F32) if sink is None else sink.astype(F32)
    return pl.pallas_call(
        functools.partial(_full_attn_body, hkv=hkv, g=g, qb=qb, has_sink=has_sink, unit=min(unit, g)),
        grid=(b, n // qb),
        in_specs=[
            pl.BlockSpec(memory_space=pltpu.SMEM),
            pl.BlockSpec((1, qb, dq), lambda bi, i: (bi, i, 0)),
            pl.BlockSpec((1, dkv, nk), lambda bi, i: (bi, 0, 0)),
            pl.BlockSpec((1, nk, 2 * dkv), lambda bi, i: (bi, 0, 0)),
        ],
        out_specs=pl.BlockSpec((1, qb, dq), lambda bi, i: (bi, i, 0)),
        out_shape=jax.ShapeDtypeStruct((b, n, dq), BF16),
        compiler_params=_params("parallel", "parallel"),
        name="full_attn",
    )(sink, q, kt, _values_with_ones(v, hkv))


def _short_conv_body(u_ref, w_ref, b_ref, o_ref, *, h1, l2, c):
    slab = lambda f: u_ref[0, :, f, :]
    row = lax.broadcasted_iota(jnp.int32, (h1, c), 0)
    for f in range(l2):
        prev = slab(f - 1) if f > 0 else jnp.where(row == 0, 0.0, pltpu.roll(slab(l2 - 1), 1, 0))
        nxt = slab(f + 1) if f < l2 - 1 else jnp.where(row == h1 - 1, 0.0, pltpu.roll(slab(0), h1 - 1, 0))
        o_ref[0, 0, :, f * c:(f + 1) * c] = (prev * w_ref[0:1, :] + slab(f) * w_ref[1:2, :]
                                             + nxt * w_ref[2:3, :] + b_ref[...])


def _short_conv(u, w, bias, l2):
    b, n, c3 = u.shape
    c = c3 // 3
    h1 = n // l2
    return pl.pallas_call(
        functools.partial(_short_conv_body, h1=h1, l2=l2, c=c),
        grid=(b, 3),
        in_specs=[
            pl.BlockSpec((1, h1, l2, c), lambda bi, j: (bi, 0, 0, j)),
            pl.BlockSpec((3, c), lambda bi, j: (0, j)),
            pl.BlockSpec((1, c), lambda bi, j: (0, j)),
        ],
        out_specs=pl.BlockSpec((1, 1, h1, l2 * c), lambda bi, j: (j, bi, 0, 0)),
        out_shape=jax.ShapeDtypeStruct((3, b, h1, l2 * c), F32),
        compiler_params=_params("parallel", "parallel"),
        name="short_conv",
    )(u.reshape(b, h1, l2, c3), w, bias.reshape(1, c3))


def _filter_body(band_ref, w1_ref, b1_ref, w2_ref, b2_ref, w3_ref, fr_ref, dl_ref, o_ref, s_ref, *, n, rt, c):
    i = pl.program_id(0)
    hp = lax.Precision.HIGHEST
    m = i * rt + lax.broadcasted_iota(jnp.int32, (rt, 1), 0)
    pos = jnp.where(m < n, m, 2 * n - m).astype(F32)
    t_norm = pos / max(n - 1, 1)
    ang = (2.0 * math.pi / n) * pos * band_ref[...]
    lane = lax.broadcasted_iota(jnp.int32, (rt, LANES), 1)
    z = jnp.where(lane == 0, t_norm,
                  jnp.where(lane <= HY_BANDS, jnp.cos(ang),
                            jnp.where(lane <= 2 * HY_BANDS, -jnp.sin(ang), 0.0)))
    fr = fr_ref[...]
    hdn = jnp.sin(fr * (jnp.dot(z, w1_ref[...], precision=hp, preferred_element_type=F32) + b1_ref[...]))
    hdn = jnp.sin(fr * (jnp.dot(hdn, w2_ref[...], precision=hp, preferred_element_type=F32) + b2_ref[...]))
    h = jnp.dot(hdn, w3_ref[...], precision=hp, preferred_element_type=F32)
    h = h * jnp.exp(-t_norm * dl_ref[...])
    half = HY_ORDER * c
    sel = jnp.where(m < n, h[:, :half], jnp.where(m > n, -h[:, half:], 0.0))
    for o in range(HY_ORDER):
        o_ref[o] = sel[:, o * c:(o + 1) * c]

    @pl.when(i == 0)
    def _():
        s_ref[...] = jnp.zeros_like(s_ref)

    s_ref[...] += jnp.sum(jnp.abs(sel), axis=0, keepdims=True)


def _hyena_filters(n, w1, b1, w2, b2, w3, freq, c):
    rt = min(1024, n)
    hid = w1.shape[1]
    bands = jnp.linspace(1e-4, HY_BANDS - 1, HY_BANDS, dtype=F32)
    band_row = jnp.zeros((1, LANES), F32).at[0, 1:1 + 2 * HY_BANDS].set(jnp.tile(bands, 2))
    w1p = jnp.zeros((LANES, hid), F32).at[:w1.shape[0]].set(w1)
    max_decay = math.log(HY_DECAY_TARGET) / HY_FAST_DECAY
    min_decay = math.log(HY_DECAY_TARGET) / HY_SLOW_DECAY
    deltas = jnp.abs(jnp.linspace(min_decay, max_decay, c, dtype=F32))
    dl = jnp.tile(deltas, HY_DIRS * HY_ORDER).reshape(1, -1)
    full = lambda a: pl.BlockSpec(a.shape, lambda i: (0,) * a.ndim)
    args = (band_row, w1p, b1.reshape(1, hid), w2, b2.reshape(1, hid), w3, freq.reshape(1, hid), dl)
    return pl.pallas_call(
        functools.partial(_filter_body, n=n, rt=rt, c=c),
        grid=(2 * n // rt,),
        in_specs=[full(a) for a in args],
        out_specs=[pl.BlockSpec((HY_ORDER, rt, c), lambda i: (0, i, 0)),
                   pl.BlockSpec((1, HY_ORDER * c), lambda i: (0, 0))],
        out_shape=[jax.ShapeDtypeStruct((HY_ORDER, 2 * n, c), F32),
                   jax.ShapeDtypeStruct((1, HY_ORDER * c), F32)],
        compiler_params=_params("arbitrary"),
        name="hyena_filter",
    )(*args)


def _dft_split(n):
    l2 = 32 if n >= 2048 else 16
    return 2 * n // l2, l2


def _dft_constants(n):
    l1, l2 = _dft_split(n)
    h1 = l1 // 2
    nn = 2 * n
    k1 = np.arange(h1)[:, None]
    a = 2 * np.pi * (k1 + 0.5) * np.arange(l1)[None, :] / l1
    w1 = np.concatenate([np.cos(a), -np.sin(a)], axis=0)
    t = 2 * np.pi * (k1 + 0.5) * np.arange(l2)[None, :] / nn
    tw_cos, tw_sin = np.cos(t), np.sin(t)
    p = 2 * np.pi * np.arange(l2)[:, None] * np.arange(l2)[None, :] / l2
    eye = np.eye(MID_KB)
    cos_k, sin_k = np.kron(np.cos(p), eye), np.kron(np.sin(p), eye)
    w2 = np.block([[cos_k, sin_k], [-sin_k, cos_k]])
    w2i = np.block([[cos_k, -sin_k], [sin_k, cos_k]])
    tw_rows = lambda m: m.reshape(h1 // MID_KB, MID_KB, l2).transpose(0, 2, 1).reshape(h1 // MID_KB, l2 * MID_KB, 1)
    ai = 2 * np.pi * np.arange(h1)[:, None] * (np.arange(h1)[None, :] + 0.5) / l1
    w1i = (2.0 / nn) * np.concatenate([np.cos(ai), -np.sin(ai)], axis=1)
    c = lambda m, dt: jnp.asarray(m, dtype=dt)
    return dict(
        l1=l1, l2=l2, h1=h1,
        w1=c(w1, BF16), w2=c(w2, BF16), w2i=c(w2i, BF16), w1i=c(w1i, BF16),
        tw_cos_fwd=c(tw_cos.T[:, :, None], F32), tw_sin_fwd=c(tw_sin.T[:, :, None], F32),
        tw_cos_inv=c(tw_rows(tw_cos), F32), tw_sin_inv=c(tw_rows(tw_sin), F32),
    )


def _ct_fwd1_body(x_ref, w_ref, tc_ref, ts_ref, nrm_ref, o_ref, *, lb, c, h1, normalise):
    for q in range(lb):
        x = x_ref[0, 0, :, q * c:(q + 1) * c]
        if normalise:
            x = x / (nrm_ref[0] + HY_FILTER_EPS)
        a = jnp.dot(w_ref[...], x.astype(BF16), preferred_element_type=F32)
        ar, ai = a[:h1], a[h1:]
        tc, ts = tc_ref[q], ts_ref[q]
        o_ref[0, 0, q] = ar * tc + ai * ts
        o_ref[0, 1, q] = ai * tc - ar * ts


def _ct_fwd1(xs, idx, consts, c, norms=None, lb=8):
    _, b, k1n, _ = xs.shape
    l2, h1 = consts["l2"], consts["h1"]
    lb = min(lb, l2)
    w = consts["w1"][:, :k1n]
    normalise = norms is not None
    if norms is None:
        norms = jnp.zeros((b, 1, c), F32)
    return pl.pallas_call(
        functools.partial(_ct_fwd1_body, lb=lb, c=c, h1=h1, normalise=normalise),
        grid=(b, l2 // lb),
        in_specs=[
            pl.BlockSpec((1, 1, k1n, lb * c), lambda bi, i: (idx, bi, 0, i)),
            pl.BlockSpec(w.shape, lambda bi, i: (0, 0)),
            pl.BlockSpec((lb, h1, 1), lambda bi, i: (i, 0, 0)),
            pl.BlockSpec((lb, h1, 1), lambda bi, i: (i, 0, 0)),
            pl.BlockSpec((1, 1, c), lambda bi, i: (bi, 0, 0)),
        ],
        out_specs=pl.BlockSpec((1, 2, lb, h1, c), lambda bi, i: (bi, 0, i, 0, 0)),
        out_shape=jax.ShapeDtypeStruct((b, 2, l2, h1, c), F32),
        compiler_params=_params("parallel", "parallel"),
        name="ct_fwd1",
    )(xs, w, consts["tw_cos_fwd"], consts["tw_sin_fwd"], norms)


def _stacked(ref, lead, rows, c):
    return jnp.concatenate([ref[lead + (0,)].reshape(rows, c), ref[lead + (1,)].reshape(rows, c)], axis=0)


def _ct_spec_body(a_ref, w2_ref, o_ref, *, l2, c):
    rows = l2 * MID_KB
    x = jnp.dot(w2_ref[...], _stacked(a_ref, (0,), rows, c).astype(BF16), preferred_element_type=F32)
    o_ref[0, 0] = x[:rows].reshape(l2, MID_KB, c)
    o_ref[0, 1] = x[rows:].reshape(l2, MID_KB, c)


def _ct_spectrum(a, consts, c):
    b = a.shape[0]
    l2, h1 = consts["l2"], consts["h1"]
    blk = pl.BlockSpec((1, 2, l2, MID_KB, c), lambda bi, i: (bi, 0, 0, i, 0))
    return pl.pallas_call(
        functools.partial(_ct_spec_body, l2=l2, c=c),
        grid=(b, h1 // MID_KB),
        in_specs=[blk, pl.BlockSpec(consts["w2"].shape, lambda bi, i: (0, 0))],
        out_specs=blk,
        out_shape=jax.ShapeDtypeStruct((b, 2, l2, h1, c), F32),
        compiler_params=_params("parallel", "parallel"),
        name="ct_spectrum",
    )(a, consts["w2"])


def _ct_mid_body(a_ref, h_ref, w2_ref, w2i_ref, tc_ref, ts_ref, o_ref, *, l2, c):
    rows = l2 * MID_KB
    x = jnp.dot(w2_ref[...], _stacked(a_ref, (0,), rows, c).astype(BF16), preferred_element_type=F32)
    xr, xi = x[:rows], x[rows:]
    hr, hi = h_ref[0, 0].reshape(rows, c), h_ref[0, 1].reshape(rows, c)
    y = jnp.concatenate([xr * hr - xi * hi, xr * hi + xi * hr], axis=0).astype(BF16)
    bm = jnp.dot(w2i_ref[...], y, preferred_element_type=F32)
    br, bi = bm[:rows], bm[rows:]
    tc, ts = tc_ref[0], ts_ref[0]
    o_ref[0, 0] = (br * tc - bi * ts).reshape(l2, MID_KB, c)
    o_ref[0, 1] = (br * ts + bi * tc).reshape(l2, MID_KB, c)


def _ct_mid(a, hspec, order, consts, c):
    b = a.shape[0]
    l2, h1 = consts["l2"], consts["h1"]
    blk = pl.BlockSpec((1, 2, l2, MID_KB, c), lambda bi, i: (bi, 0, 0, i, 0))
    tw = pl.BlockSpec((1, l2 * MID_KB, 1), lambda bi, i: (i, 0, 0))
    return pl.pallas_call(
        functools.partial(_ct_mid_body, l2=l2, c=c),
        grid=(b, h1 // MID_KB),
        in_specs=[
            blk,
            pl.BlockSpec((1, 2, l2, MID_KB, c), lambda bi, i: (order, 0, 0, i, 0)),
            pl.BlockSpec(consts["w2"].shape, lambda bi, i: (0, 0)),
            pl.BlockSpec(consts["w2i"].shape, lambda bi, i: (0, 0)),
            tw,
            tw,
        ],
        out_specs=blk,
        out_shape=jax.ShapeDtypeStruct((b, 2, l2, h1, c), F32),
        compiler_params=_params("parallel", "parallel"),
        name="ct_mid",
    )(a, hspec, consts["w2"], consts["w2i"], consts["tw_cos_inv"], consts["tw_sin_inv"])


def _ct_inv1_body(b_ref, w_ref, u_ref, gate_ref, skip_ref, o_ref, *, lb, c, by_position):
    for q in range(lb):
        bb = jnp.concatenate([b_ref[0, 0, q], b_ref[0, 1, q]], axis=0).astype(BF16)
        y = jnp.dot(w_ref[...], bb, preferred_element_type=F32)
        cs = slice(q * c, (q + 1) * c)
        out = gate_ref[0, 0, :, cs] * (y + u_ref[0, 0, :, cs] * skip_ref[...])
        if by_position:
            o_ref[0, :, q, :] = out
        else:
            o_ref[0, :, cs] = out


def _ct_inv1(bsp, u, u_idx, gate, gate_idx, skip, consts, c, by_position, lb=8):
    b = bsp.shape[0]
    l2, h1 = consts["l2"], consts["h1"]
    lb = min(lb, l2)
    if by_position:
        out_spec = pl.BlockSpec((1, h1, lb, c), lambda bi, i: (bi, 0, i, 0))
        out_shape = jax.ShapeDtypeStruct((b, h1, l2, c), F32)
    else:
        out_spec = pl.BlockSpec((1, h1, lb * c), lambda bi, i: (bi, 0, i))
        out_shape = jax.ShapeDtypeStruct((b, h1, l2 * c), F32)
    return pl.pallas_call(
        functools.partial(_ct_inv1_body, lb=lb, c=c, by_position=by_position),
        grid=(b, l2 // lb),
        in_specs=[
            pl.BlockSpec((1, 2, lb, h1, c), lambda bi, i: (bi, 0, i, 0, 0)),
            pl.BlockSpec((h1, 2 * h1), lambda bi, i: (0, 0)),
            pl.BlockSpec((1, 1, h1, lb * c), lambda bi, i: (u_idx, bi, 0, i)),
            pl.BlockSpec((1, 1, h1, lb * c), lambda bi, i: (gate_idx, bi, 0, i)),
            pl.BlockSpec((1, c), lambda bi, i: (0, 0)),
        ],
        out_specs=out_spec,
        out_shape=out_shape,
        compiler_params=_params("parallel", "parallel"),
        name="ct_inv1",
    )(bsp, consts["w1i"], u, gate, skip.reshape(1, c))


def _hyena(u, conv_w, conv_b, f_w1, f_b1, f_w2, f_b2, f_w3, f_freq, skip):
    b, n, c3 = u.shape
    c = c3 // 3
    consts = _dft_constants(n)
    l1, l2, h1 = consts["l1"], consts["l2"], consts["h1"]
    filt, norms = _hyena_filters(n, f_w1, f_b1, f_w2, f_b2, f_w3, f_freq, c)
    fa = _ct_fwd1(filt.reshape(1, HY_ORDER, l1, l2 * c), 0, consts, c, norms=norms.reshape(HY_ORDER, 1, c))
    hspec = _ct_spectrum(fa, consts, c)
    parts = _short_conv(u, conv_w, conv_b, l2)

    def long_conv_gated(x_stack, x_idx, gate_idx, order, by_position):
        a = _ct_fwd1(x_stack, x_idx, consts, c)
        bsp = _ct_mid(a, hspec, order, consts, c)
        return _ct_inv1(bsp, x_stack, x_idx, parts, gate_idx, skip[order], consts, c, by_position)

    z = long_conv_gated(parts, 0, 1, 0, False)
    return long_conv_gated(z[None], 0, 2, 1, True).reshape(b, n, c)


def _outproj_body(*refs, n_in):
    ins = refs[:n_in]
    ws = refs[n_in:2 * n_in]
    x_ref, gate_ref, g_ref, sc_ref, sh_ref, rhi_ref, rlo_ref, rb_ref, xo_ref, h_ref, rt_ref = refs[2 * n_in:]
    y = None
    for a_ref, w_ref in zip(ins, ws):
        t = jnp.dot(a_ref[0].astype(BF16), w_ref[...], preferred_element_type=F32)
        y = t if y is None else y + t
    x = x_ref[0] + gate_ref[0] * y
    xo_ref[0] = x
    h = x * lax.rsqrt(jnp.mean(x * x, axis=-1, keepdims=True) + NORM_EPS) * g_ref[...]
    h = h * (1.0 + sc_ref[0]) + sh_ref[0]
    hi = h.astype(BF16)
    h_ref[0] = hi
    lo = (h - hi.astype(F32)).astype(BF16)
    lg = (jnp.dot(hi, rhi_ref[...], preferred_element_type=F32)
          + jnp.dot(lo, rhi_ref[...], preferred_element_type=F32)
          + jnp.dot(hi, rlo_ref[...], preferred_element_type=F32) + rb_ref[...])
    rt_ref[0] = _route(lg)


def _route(lg):
    lane = lax.broadcasted_iota(jnp.int32, lg.shape, 1)
    lane_f = lane.astype(F32)
    neg = -jnp.inf

    def top(v):
        m = jnp.max(v, axis=-1, keepdims=True)
        return m, jnp.min(jnp.where(v == m, lane_f, float(LANES)), axis=-1, keepdims=True)

    gl = jnp.where(lane < N_GROUPS, lg, neg)
    gmax, grp = top(gl)
    p_grp = 1.0 / jnp.sum(jnp.exp(gl - gmax), axis=-1, keepdims=True)
    first = N_GROUPS + grp * EXP_PER_GROUP
    el = jnp.where((lane_f >= first) & (lane_f < first + EXP_PER_GROUP), lg, neg)
    m1, i1 = top(el)
    m2, i2 = top(jnp.where(lane_f == i1, neg, el))
    e2 = jnp.exp(m2 - m1)
    den = 1.0 + e2
    vals = (i1 - N_GROUPS, i2 - N_GROUPS, p_grp * (1.0 / den), p_grp * (e2 / den))
    out = jnp.zeros(lg.shape, F32)
    for k, v in enumerate(vals):
        out = jnp.where(lane == k, v, out)
    return out


def _outproj(ins, ws, x, gate, g, scale, shift, r_hi, r_lo, r_b, tm=512):
    b, n, d = x.shape
    tm = min(tm, n)
    bm = gate.shape[0]
    mod_map = (lambda bi, i: (bi, 0, 0)) if bm > 1 else (lambda bi, i: (0, 0, 0))
    row = lambda wd: pl.BlockSpec((1, tm, wd), lambda bi, i: (bi, i, 0))
    full = lambda a: pl.BlockSpec(a.shape, lambda bi, i: (0,) * a.ndim)
    mod = pl.BlockSpec((1, 1, d), mod_map)
    return pl.pallas_call(
        functools.partial(_outproj_body, n_in=len(ins)),
        grid=(b, n // tm),
        in_specs=([row(a.shape[-1]) for a in ins] + [full(w) for w in ws]
                  + [row(d), mod, pl.BlockSpec((1, d), lambda bi, i: (0, 0)), mod, mod,
                     full(r_hi), full(r_lo), full(r_b)]),
        out_specs=[row(d), row(d), row(LANES)],
        out_shape=[jax.ShapeDtypeStruct((b, n, d), F32), jax.ShapeDtypeStruct((b, n, d), BF16),
                   jax.ShapeDtypeStruct((b, n, LANES), F32)],
        compiler_params=_params("parallel", "parallel"),
        name="outproj",
    )(*ins, *ws, x, gate, g.reshape(1, d), scale, shift, r_hi, r_lo, r_b)


def _rank_body(rt_ref, tri_ref, upper_ref, pos_ref, cnt_ref):
    rt = rt_ref[...]
    lane_i = lax.broadcasted_iota(jnp.int32, rt.shape, 1)
    lane = lane_i.astype(F32)
    oh_a = lane == rt[:, 0:1]
    oh_b = lane == rt[:, 1:2]
    one_a = jnp.where(oh_a, 1.0, 0.0)
    one_b = jnp.where(oh_b, 1.0, 0.0)
    before_a = jnp.dot(tri_ref[...], one_a.astype(BF16), preferred_element_type=F32)
    before_b = jnp.dot(tri_ref[...], one_b.astype(BF16), preferred_element_type=F32)
    tot_a = jnp.sum(one_a, axis=0, keepdims=True)
    cnt = tot_a + jnp.sum(one_b, axis=0, keepdims=True)
    padded = jnp.floor((cnt + (CHUNK_ROWS - 1)) * (1.0 / CHUNK_ROWS)) * CHUNK_ROWS
    first = jnp.dot(jnp.broadcast_to(padded, (8, LANES)).astype(BF16), upper_ref[...],
                    preferred_element_type=F32)[0:1]
    pos_a = jnp.sum(jnp.where(oh_a, before_a + first, 0.0), axis=-1, keepdims=True)
    pos_b = jnp.sum(jnp.where(oh_b, before_b + first + tot_a, 0.0), axis=-1, keepdims=True)
    is_gate = (lane_i >= TOP_K) & (lane_i < 2 * TOP_K)
    pos_ref[...] = jnp.where(lane_i == 0, pos_a, jnp.where(lane_i == 1, pos_b, jnp.where(is_gate, rt, 0.0)))
    cnt_ref[0] = jnp.broadcast_to(cnt, (8, LANES))


def _rank(route):
    t = route.shape[0]
    tm = TOKEN_TILE
    tri = jnp.asarray(np.tril(np.ones((tm, tm)), -1), dtype=BF16)
    upper = jnp.asarray(np.triu(np.ones((LANES, LANES)), 1), dtype=BF16)
    return pl.pallas_call(
        _rank_body,
        grid=(t // tm,),
        in_specs=[pl.BlockSpec((tm, LANES), lambda i: (i, 0)), pl.BlockSpec((tm, tm), lambda i: (0, 0)),
                  pl.BlockSpec((LANES, LANES), lambda i: (0, 0))],
        out_specs=[pl.BlockSpec((tm, LANES), lambda i: (i, 0)), pl.BlockSpec((1, 8, LANES), lambda i: (i, 0, 0))],
        out_shape=[jax.ShapeDtypeStruct((t, LANES), F32), jax.ShapeDtypeStruct((t // tm, 8, LANES), F32)],
        compiler_params=_params("parallel"),
        name="moe_rank",
    )(route, tri, upper)


def _chunk_tables(cnt, n_blocks):
    padded = (cnt + CHUNK_ROWS - 1) // CHUNK_ROWS * CHUNK_ROWS
    run_end = jnp.cumsum(padded, axis=1)
    run_start = run_end - padded
    seg_rows = jnp.sum(padded, axis=0)
    seg_rows = (seg_rows + EXPERT_ROWS - 1) // EXPERT_ROWS * EXPERT_ROWS
    seg_end = jnp.cumsum(seg_rows)
    dst_start = (seg_end - seg_rows)[None, :] + jnp.cumsum(padded, axis=0) - padded
    row0 = jnp.arange(BUF_CHUNKS, dtype=jnp.int32) * CHUNK_ROWS
    chunk_exp = jnp.minimum(jnp.sum(run_end[:, None, :] <= row0[None, :, None], axis=-1), N_EXPERTS - 1)
    onehot = chunk_exp[:, :, None] == jnp.arange(N_EXPERTS, dtype=jnp.int32)[None, None, :]
    dst = jnp.sum(jnp.where(onehot, (dst_start - run_start)[:, None, :], 0), axis=-1) + row0[None, :]
    n_chunks = run_end[:, -1:] // CHUNK_ROWS
    table = jnp.concatenate(
        [dst, n_chunks, jnp.zeros((cnt.shape[0], TABLE_WORDS - BUF_CHUNKS - 1), jnp.int32)], axis=1)
    block_row0 = jnp.arange(n_blocks, dtype=jnp.int32) * EXPERT_ROWS
    block_exp = jnp.minimum(jnp.sum(seg_end[None, :] <= block_row0[:, None], axis=1), N_EXPERTS - 1)
    n_used = (seg_end[-1] // EXPERT_ROWS).reshape(1)
    return table.astype(jnp.int32), block_exp.astype(jnp.int32), n_used.astype(jnp.int32)


def _pack_halves(x):
    w = x.shape[-1] // 2
    lo = lax.bitcast_convert_type(x[:, :w].astype(BF16).astype(F32), jnp.int32)
    hi = lax.bitcast_convert_type(x[:, w:].astype(BF16).astype(F32), jnp.int32)
    return lax.shift_right_logical(lo, jnp.int32(16)) | (hi & jnp.int32(-65536))


def _unpack_halves(p):
    lo = lax.bitcast_convert_type(lax.shift_left(p, jnp.int32(16)), F32)
    hi = lax.bitcast_convert_type(p & jnp.int32(-65536), F32)
    return jnp.concatenate([lo, hi], axis=-1).astype(BF16)


def _start_chunks(tab_ref, tile, make_copy):
    base = tile * TABLE_WORDS

    def issue(c):
        make_copy(pl.multiple_of(c * CHUNK_ROWS, CHUNK_ROWS),
                  pl.multiple_of(tab_ref[base + c], CHUNK_ROWS)).start()

    _for_each_chunk(tab_ref[base + BUF_CHUNKS], issue)


def _wait_chunks(tab_ref, tile, make_copy):
    _for_each_chunk(tab_ref[tile * TABLE_WORDS + BUF_CHUNKS], lambda c: make_copy(0, 0).wait())


def _for_each_chunk(n, fn):
    groups = n // LOOP_GROUP

    def grouped(i, carry):
        for k in range(LOOP_GROUP):
            fn(i * LOOP_GROUP + k)
        return carry

    def single(c, carry):
        fn(c)
        return carry

    lax.fori_loop(0, groups, grouped, 0)
    lax.fori_loop(groups * LOOP_GROUP, n, single, 0)


def _dispatch_body(tab_ref, h_ref, pos_ref, xs_in, xs_out, buf, sem_rows, *, tile0, n_steps):
    del xs_in
    step = pl.program_id(0)
    dp = h_ref.shape[1] // 2

    def copies(at_step):
        slot = at_step % 2
        return lambda src, dst: pltpu.make_async_copy(
            buf.at[slot, pl.ds(src, CHUNK_ROWS)], xs_out.at[pl.ds(dst, CHUNK_ROWS)], sem_rows.at[slot])

    @pl.when(step >= 2)
    def _():
        _wait_chunks(tab_ref, tile0 + step - 2, copies(step - 2))

    tm = h_ref.shape[0]
    row = lax.broadcasted_iota(jnp.int32, (BUF_ROWS, tm), 0).astype(F32)
    oh_a = row == pos_ref[0, 0:1, :]
    oh_b = row == pos_ref[0, 1:2, :]
    buf[step % 2, :, :dp] = _pack_halves(jnp.dot(jnp.where(oh_a | oh_b, 1.0, 0.0).astype(BF16), h_ref[...],
                                                 preferred_element_type=F32))
    gate = jnp.sum(jnp.where(oh_a, pos_ref[0, 2:3, :], 0.0) + jnp.where(oh_b, pos_ref[0, 3:4, :], 0.0),
                   axis=-1, keepdims=True)
    buf[step % 2, :, dp:] = lax.bitcast_convert_type(jnp.broadcast_to(gate, (BUF_ROWS, LANES)), jnp.int32)
    _start_chunks(tab_ref, tile0 + step, copies(step))

    @pl.when(step == n_steps - 1)
    def _():
        @pl.when(step >= 1)
        def _():
            _wait_chunks(tab_ref, tile0 + step - 1, copies(step - 1))

        _wait_chunks(tab_ref, tile0 + step, copies(step))


def _dispatch(h, pos_t, table, xs, tile0):
    t, d = h.shape
    tm = TOKEN_TILE
    n_steps = t // tm
    grid_spec = pltpu.PrefetchScalarGridSpec(
        num_scalar_prefetch=1,
        grid=(n_steps,),
        in_specs=[
            pl.BlockSpec((tm, d), lambda i, tab: (i, 0)),
            pl.BlockSpec((1, 8, tm), lambda i, tab: (tile0 + i, 0, 0)),
            pl.BlockSpec(memory_space=pl.ANY),
        ],
        out_specs=pl.BlockSpec(memory_space=pl.ANY),
        scratch_shapes=[
            pltpu.VMEM((2, BUF_ROWS, d // 2 + LANES), jnp.int32),
            pltpu.SemaphoreType.DMA((2,)),
        ],
    )
    return pl.pallas_call(
        functools.partial(_dispatch_body, tile0=tile0, n_steps=n_steps),
        grid_spec=grid_spec,
        out_shape=jax.ShapeDtypeStruct(xs.shape, xs.dtype),
        input_output_aliases={3: 0},
        compiler_params=_params("arbitrary"),
        name="moe_dispatch",
    )(table.reshape(-1), h, pos_t, xs)


def _expert_body(bexp_ref, nused_ref, x_ref, wg_ref, wu_ref, wd_ref, o_ref):
    @pl.when(pl.program_id(0) < nused_ref[0])
    def _():
        dp = o_ref.shape[1]
        xb = _unpack_halves(x_ref[:, :dp])
        gate = lax.bitcast_convert_type(x_ref[:, dp:dp + 1], F32)
        gt = jnp.dot(xb, wg_ref[0, 0].astype(BF16), preferred_element_type=F32)
        up = jnp.dot(xb, wu_ref[0, 0].astype(BF16), preferred_element_type=F32)
        hid = (gt * jax.nn.sigmoid(gt) * up).astype(BF16)
        o_ref[...] = _pack_halves(jnp.dot(hid, wd_ref[0, 0].astype(BF16), preferred_element_type=F32) * gate)

    @pl.when(pl.program_id(0) >= nused_ref[0])
    def _():
        o_ref[...] = jnp.zeros_like(o_ref)


def _experts(xs, block_exp, n_used, w_gate, w_up, w_down, layer):
    rows, width = xs.shape
    d, de = w_gate.shape[2:]
    used = lambda i, be, nu: (jnp.minimum(i, nu[0] - 1), 0)
    expert = lambda i, be, nu: (layer, be[i], 0, 0)
    grid_spec = pltpu.PrefetchScalarGridSpec(
        num_scalar_prefetch=2,
        grid=(rows // EXPERT_ROWS,),
        in_specs=[
            pl.BlockSpec((EXPERT_ROWS, width), used),
            pl.BlockSpec((1, 1, d, de), expert),
            pl.BlockSpec((1, 1, d, de), expert),
            pl.BlockSpec((1, 1, de, d), expert),
        ],
        out_specs=pl.BlockSpec((EXPERT_ROWS, d // 2), lambda i, be, nu: (i, 0)),
    )
    return pl.pallas_call(
        _expert_body,
        grid_spec=grid_spec,
        out_shape=jax.ShapeDtypeStruct((rows, d // 2), jnp.int32),
        compiler_params=_params("arbitrary"),
        name="experts",
    )(block_exp, n_used, xs, w_gate, w_up, w_down)


def _combine_body(tab_ref, ys_hbm, x_ref, gate_ref, pos_ref, fg_ref, o_ref, ybuf, sem_rows,
                  *, tile0, n_tiles, n_steps, final_norm):
    step = pl.program_id(0) * n_tiles + pl.program_id(1)

    def copies(at_step):
        slot = at_step % 2
        return lambda dst, src: pltpu.make_async_copy(
            ys_hbm.at[pl.ds(src, CHUNK_ROWS)], ybuf.at[slot, pl.ds(dst, CHUNK_ROWS)], sem_rows.at[slot])

    @pl.when(step == 0)
    def _():
        ybuf[...] = jnp.zeros_like(ybuf)
        _start_chunks(tab_ref, tile0, copies(0))

    @pl.when(step + 1 < n_steps)
    def _():
        _start_chunks(tab_ref, tile0 + step + 1, copies(step + 1))

    _wait_chunks(tab_ref, tile0 + step, copies(step))
    tm = x_ref.shape[1]
    col = lax.broadcasted_iota(jnp.int32, (tm, BUF_ROWS), 1).astype(F32)
    pick = jnp.where((col == pos_ref[:, 0:1]) | (col == pos_ref[:, 1:2]), 1.0, 0.0).astype(BF16)
    out = x_ref[0] + gate_ref[0] * jnp.dot(pick, _unpack_halves(ybuf[step % 2]), preferred_element_type=F32)
    if final_norm:
        out = out * lax.rsqrt(jnp.mean(out * out, axis=-1, keepdims=True) + NORM_EPS) * fg_ref[...]
    o_ref[0] = out


def _combine(ys, table, pos, x, gate, tile0, final_g=None):
    b, n, d = x.shape
    tm = min(TOKEN_TILE, n)
    n_tiles = n // tm
    bm = gate.shape[0]
    mod_map = (lambda bi, i, tab: (bi, 0, 0)) if bm > 1 else (lambda bi, i, tab: (0, 0, 0))
    final_norm = final_g is not None
    fg = final_g.reshape(1, d) if final_norm else jnp.ones((1, d), F32)
    grid_spec = pltpu.PrefetchScalarGridSpec(
        num_scalar_prefetch=1,
        grid=(b, n_tiles),
        in_specs=[
            pl.BlockSpec(memory_space=pl.ANY),
            pl.BlockSpec((1, tm, d), lambda bi, i, tab: (bi, i, 0)),
            pl.BlockSpec((1, 1, d), mod_map),
            pl.BlockSpec((tm, LANES), lambda bi, i, tab: (tile0 + bi * n_tiles + i, 0)),
            pl.BlockSpec((1, d), lambda bi, i, tab: (0, 0)),
        ],
        out_specs=pl.BlockSpec((1, tm, d), lambda bi, i, tab: (bi, i, 0)),
        scratch_shapes=[
            pltpu.VMEM((2, BUF_ROWS, d // 2), jnp.int32),
            pltpu.SemaphoreType.DMA((2,)),
        ],
    )
    return pl.pallas_call(
        functools.partial(_combine_body, tile0=tile0, n_tiles=n_tiles, n_steps=b * n_tiles,
                          final_norm=final_norm),
        grid_spec=grid_spec,
        out_shape=jax.ShapeDtypeStruct((b, n, d), F32),
        compiler_params=_params("arbitrary", "arbitrary"),
        name="moe_combine",
    )(table.reshape(-1), ys, x, gate, pos, fg)


def _router_weights(wg, bg, we, be):
    d = wg.shape[0]
    w = jnp.zeros((d, LANES), F32).at[:, :N_GROUPS].set(wg).at[:, N_GROUPS:N_GROUPS + N_EXPERTS].set(we)
    bias = jnp.zeros((1, LANES), F32).at[0, :N_GROUPS].set(bg).at[0, N_GROUPS:N_GROUPS + N_EXPERTS].set(be)
    hi = w.astype(BF16)
    lo = (w - hi.astype(F32)).astype(BF16)
    return hi, lo, bias


def kernel(x, c, ctx, c_ctx, ada_w, ada_b, norm1_g, norm2_g, ev_w_in, ev_w_out, hy_conv_w, hy_conv_b, hy_f_w1, hy_f_b1, hy_f_w2, hy_f_b2, hy_f_w3, hy_f_freq, hy_skip, swa_sink, od_w_qkv, od_w_out, od_q_norm_g, od_k_norm_g, rt_group_w, rt_group_b, rt_exp_w, rt_exp_b, moe_w_gate, moe_w_up, moe_w_down, final_norm_g):
    b, n, d = x.shape
    lc = ctx.shape[1]
    depth = ada_w.shape[0]
    rope = _rope_tables(n)
    xc = ctx
    sc = jax.nn.silu(c)
    scc = jax.nn.silu(c_ctx)
    q_scale = HEAD_DIM ** -0.5
    for layer in range(depth):
        with_ctx = layer < depth - 1
        mod = (sc @ ada_w[layer] + ada_b[layer]).reshape(b, N_MOD, 1, d)
        modc = (scc @ ada_w[layer] + ada_b[layer]).reshape(1, N_MOD, 1, d)
        m = [mod[:, k] for k in range(N_MOD)]
        mc = [modc[:, k] for k in range(N_MOD)]
        r_hi, r_lo, r_b = _router_weights(rt_group_w[layer], rt_group_b[layer], rt_exp_w[layer], rt_exp_b[layer])
        if layer % 2 == 0:
            e = layer // 2
            c_hy = hy_conv_w.shape[-1] // 3
            d_hy = 3 * c_hy
            hq = swa_sink.shape[-1]
            d_q = hq * HEAD_DIM
            hkv = hq // 4
            d_kv = hkv * HEAD_DIM
            w_in = ev_w_in[e].astype(BF16)
            w_out = ev_w_out[e].astype(BF16)
            hy_args = (hy_conv_w[e], hy_conv_b[e], hy_f_w1[e], hy_f_b1[e], hy_f_w2[e], hy_f_b2[e],
                       hy_f_w3[e], hy_f_freq[e], hy_skip[e])
            u, q, k, v = _proj(x, norm1_g[layer], m[1], m[0], w_in, [
                (0, d_hy, "f32", None, False, 1.0),
                (d_hy, d_q, "qk", None, True, q_scale),
                (d_hy + d_q, d_kv, "qk", None, True, 1.0),
                (d_hy + d_q + d_kv, d_kv, "bf16", None, False, 1.0)], rope_tabs=rope)
            if with_ctx:
                uc, qc, kc, vc = _proj(xc, norm1_g[layer], mc[1], mc[0], w_in, [
                    (0, d_hy, "f32", None, False, 1.0),
                    (d_hy, d_q, "qk", None, False, q_scale),
                    (d_hy + d_q, d_kv, "bf16", None, False, 1.0),
                    (d_hy + d_q + d_kv, d_kv, "bf16", None, False, 1.0)])
            else:
                kc, vc = _proj(xc, norm1_g[layer], mc[1], mc[0], w_in, [
                    (d_hy + d_q, d_kv, "bf16", None, False, 1.0),
                    (d_hy + d_q + d_kv, d_kv, "bf16", None, False, 1.0)])
            y_hy = _hyena(u, *hy_args)
            y_att = _windowed_attention(q, k, v, kc, vc, swa_sink[e], hkv)
            mix_in, mix_w = [y_hy, y_att], [w_out[:c_hy], w_out[c_hy:]]
            if with_ctx:
                yc_hy = _hyena(uc, *hy_args)
                yc_att = _full_attention(qc, kc, vc, hkv, sink=swa_sink[e])
                mixc_in = [yc_hy, yc_att]
        else:
            o = layer // 2
            hkv = od_w_qkv.shape[-1] // HEAD_DIM // 6
            hq = 4 * hkv
            d_q = hq * HEAD_DIM
            d_kv = hkv * HEAD_DIM
            w_qkv = od_w_qkv[o].astype(BF16)
            w_out = od_w_out[o].astype(BF16)
            norm_g = jnp.zeros((8, LANES), F32).at[0].set(jnp.tile(od_q_norm_g[o], 2)).at[1].set(
                jnp.tile(od_k_norm_g[o], 2))
            q, k, v = _proj(x, norm1_g[layer], m[1], m[0], w_qkv, [
                (0, d_q, "qk", 0, True, q_scale),
                (d_q, d_kv, "qk", 1, True, 1.0),
                (d_q + d_kv, d_kv, "bf16", None, False, 1.0)], rope_tabs=rope, norm_g=norm_g)
            if with_ctx:
                qc, kc, vc = _proj(xc, norm1_g[layer], mc[1], mc[0], w_qkv, [
                    (0, d_q, "qk", 0, False, q_scale),
                    (d_q, d_kv, "qk", 1, False, 1.0),
                    (d_q + d_kv, d_kv, "bf16", None, False, 1.0)], norm_g=norm_g)
            else:
                kc, vc = _proj(xc, norm1_g[layer], mc[1], mc[0], w_qkv, [
                    (d_q, d_kv, "qk", 1, False, 1.0),
                    (d_q + d_kv, d_kv, "bf16", None, False, 1.0)], norm_g=norm_g)
            y_att = _full_attention(q, jnp.concatenate([kc, k], axis=1), jnp.concatenate([vc, v], axis=1), hkv)
            mix_in, mix_w = [y_att], [w_out]
            if with_ctx:
                mixc_in = [_full_attention(qc, kc, vc, hkv)]
        x, h2, rt = _outproj(mix_in, mix_w, x, m[2], norm2_g[layer], m[4], m[3], r_hi, r_lo, r_b)
        route_flat = rt.reshape(b * n, LANES)
        if with_ctx:
            xc, h2c, rtc = _outproj(mixc_in, mix_w, xc, mc[2], norm2_g[layer], mc[4], mc[3], r_hi, r_lo, r_b)
            route_flat = jnp.concatenate([route_flat, rtc.reshape(b * lc, LANES)], axis=0)
        n_tok = route_flat.shape[0]
        n_tiles = n_tok // TOKEN_TILE
        lat_tiles = b * n // TOKEN_TILE
        max_rows = n_tok * TOP_K + n_tiles * N_EXPERTS * (CHUNK_ROWS - 1)
        n_blocks = -(-max_rows // EXPERT_ROWS) + N_EXPERTS
        pos, cnt = _rank(route_flat)
        table, block_exp, n_used = _chunk_tables(cnt[:, 0, :N_EXPERTS].astype(jnp.int32), n_blocks)
        pos_t = jnp.swapaxes(pos[:, :8].reshape(n_tiles, TOKEN_TILE, 8), 1, 2)
        xs = jnp.zeros((n_blocks * EXPERT_ROWS, d // 2 + LANES), jnp.int32)
        xs = _dispatch(h2.reshape(b * n, d), pos_t, table, xs, 0)
        if with_ctx:
            xs = _dispatch(h2c.reshape(b * lc, d), pos_t, table, xs, lat_tiles)
        ys = _experts(xs, block_exp, n_used, moe_w_gate, moe_w_up, moe_w_down, layer)
        x = _combine(ys, table, pos, x, m[5], 0, final_g=None if with_ctx else final_norm_g)
        if with_ctx:
            xc = _combine(ys, table, pos, xc.reshape(b * lc // TOKEN_TILE, TOKEN_TILE, d), mc[5],
                          lat_tiles).reshape(b, lc, d)
    return x
```

```python
import functools
import math

import numpy as np
import jax
import jax.numpy as jnp
from jax import lax
from jax.experimental import pallas as pl
from jax.experimental.pallas import tpu as pltpu

F32 = jnp.float32
BF16 = jnp.bfloat16

HEAD_DIM = 64
GRID_W = 64
ROPE_BASE = 10000.0
Q_BLOCK = 128
NORM_EPS = 1e-6
N_MOD = 6
HY_ORDER = 2
HY_BANDS = 16
HY_DIRS = 2
HY_DECAY_TARGET = 1e-2
HY_FAST_DECAY = 0.3
HY_SLOW_DECAY = 1.5
HY_FILTER_EPS = 1e-6
SWA_WINDOW = 128
N_GROUPS = 4
EXP_PER_GROUP = 8
N_EXPERTS = N_GROUPS * EXP_PER_GROUP
TOP_K = 2
EXPERT_ROWS = 512
SWA_BLOCKS_PER_STEP = 1
LOOP_GROUP = 4
MID_KB = 8
FULL_ATTN_Q_ROWS = 256
FULL_ATTN_UNIT_ROWS = 512
FULL_ATTN_LOOKAHEAD = 1
TOKEN_TILE = 512
CHUNK_ROWS = 8
BUF_ROWS = 1280
BUF_CHUNKS = BUF_ROWS // CHUNK_ROWS
TABLE_WORDS = 256

LANES = 128
VMEM_LIMIT_BYTES = 56 * 1024 * 1024


def _params(*sem):
    return pltpu.CompilerParams(dimension_semantics=sem, vmem_limit_bytes=VMEM_LIMIT_BYTES)


def _rope_tables(n):
    d_axis = HEAD_DIM // 2
    t = jnp.arange(n)
    inv = ROPE_BASE ** (-jnp.arange(0, d_axis, 2, dtype=F32) / d_axis)
    ang_r = (t // GRID_W).astype(F32)[:, None] * inv[None, :]
    ang_c = (t % GRID_W).astype(F32)[:, None] * inv[None, :]
    cos = jnp.concatenate([jnp.cos(ang_r)] * 2 + [jnp.cos(ang_c)] * 2, axis=-1)
    sin = jnp.concatenate([-jnp.sin(ang_r), jnp.sin(ang_r), -jnp.sin(ang_c), jnp.sin(ang_c)], axis=-1)
    return jnp.tile(cos, (1, 2)), jnp.tile(sin, (1, 2))


def _head_mean_matrix():
    i = np.arange(LANES)
    return jnp.asarray((i[:, None] // HEAD_DIM == i[None, :] // HEAD_DIM) / HEAD_DIM, dtype=BF16)


def _proj_body(x_ref, g_ref, sc_ref, sh_ref, w_ref, cos_ref, sin_ref, ng_ref, bd_ref, *out_refs, segs):
    x = x_ref[0]
    h = x * lax.rsqrt(jnp.mean(x * x, axis=-1, keepdims=True) + NORM_EPS) * g_ref[...]
    hb = (h * (1.0 + sc_ref[0]) + sh_ref[0]).astype(BF16)
    for o_ref, (c0, width, kind, norm_row, rope, out_scale) in zip(out_refs, segs):
        seg = jnp.dot(hb, w_ref[:, c0:c0 + width], preferred_element_type=F32)
        if kind == "f32":
            o_ref[0] = seg
            continue
        if kind == "bf16":
            o_ref[0] = seg.astype(BF16)
            continue
        for j in range(width // LANES):
            ch = seg[:, j * LANES:(j + 1) * LANES]
            if norm_row is not None:
                sq = ch * ch
                hi = sq.astype(BF16)
                lo = (sq - hi.astype(F32)).astype(BF16)
                ms = (jnp.dot(hi, bd_ref[...], preferred_element_type=F32)
                      + jnp.dot(lo, bd_ref[...], preferred_element_type=F32))
                ch = ch * lax.rsqrt(ms + NORM_EPS) * ng_ref[norm_row:norm_row + 1, :]
            if rope:
                lane = lax.broadcasted_iota(jnp.int32, ch.shape, 1)
                partner = jnp.where(lane % 32 < 16, pltpu.roll(ch, LANES - 16, 1), pltpu.roll(ch, 16, 1))
                ch = ch * cos_ref[...] + partner * sin_ref[...]
            if out_scale != 1.0:
                ch = ch * out_scale
            o_ref[0, :, j * LANES:(j + 1) * LANES] = ch.astype(BF16)


def _proj(x, g, scale, shift, w, segs, rope_tabs=None, norm_g=None, tm=512):
    b, n, d = x.shape
    tm = min(tm, n)
    bm = scale.shape[0]
    mod_map = (lambda bi, i: (bi, 0, 0)) if bm > 1 else (lambda bi, i: (0, 0, 0))
    if rope_tabs is None:
        cos = sin = jnp.zeros((8, LANES), F32)
        tab_spec = pl.BlockSpec((8, LANES), lambda bi, i: (0, 0))
    else:
        cos, sin = rope_tabs
        tab_spec = pl.BlockSpec((tm, LANES), lambda bi, i: (i, 0))
    if norm_g is None:
        norm_g = jnp.ones((8, LANES), F32)
    out_shape = [jax.ShapeDtypeStruct((b, n, s[1]), F32 if s[2] == "f32" else BF16) for s in segs]
    out_specs = [pl.BlockSpec((1, tm, s[1]), lambda bi, i: (bi, i, 0)) for s in segs]
    return pl.pallas_call(
        functools.partial(_proj_body, segs=tuple(segs)),
        grid=(b, n // tm),
        in_specs=[
            pl.BlockSpec((1, tm, d), lambda bi, i: (bi, i, 0)),
            pl.BlockSpec((1, d), lambda bi, i: (0, 0)),
            pl.BlockSpec((1, 1, d), mod_map),
            pl.BlockSpec((1, 1, d), mod_map),
            pl.BlockSpec(w.shape, lambda bi, i: (0, 0)),
            tab_spec,
            tab_spec,
            pl.BlockSpec(norm_g.shape, lambda bi, i: (0, 0)),
            pl.BlockSpec((LANES, LANES), lambda bi, i: (0, 0)),
        ],
        out_specs=out_specs,
        out_shape=out_shape,
        compiler_params=_params("parallel", "parallel"),
        name="proj",
    )(x, g.reshape(1, d), scale, shift, w, cos, sin, norm_g, _head_mean_matrix())


def _stack_heads(q, j, g):
    return jnp.concatenate(
        [q[:, (j * g + gg) * HEAD_DIM:(j * g + gg + 1) * HEAD_DIM] for gg in range(g)], axis=0)


def _values_with_ones(v, hkv):
    b, nk, _ = v.shape
    ones = jnp.zeros((b, nk, hkv, HEAD_DIM), v.dtype).at[..., 0].set(1)
    return jnp.concatenate([v.reshape(b, nk, hkv, HEAD_DIM), ones], axis=-1).reshape(b, nk, 2 * hkv * HEAD_DIM)


def _sink_column(sink_ref, j, g, qb):
    return jnp.concatenate([jnp.full((qb, 1), sink_ref[j * g + gg], F32) for gg in range(g)], axis=0)


def _swa_body(sink_ref, q_ref, kt_ref, v_ref, kct_ref, vc_ref, o_ref, *, n, hkv, g, qb, win, sub):
    rows = lax.broadcasted_iota(jnp.int32, (g * qb, 3 * qb), 0) % qb
    cols = lax.broadcasted_iota(jnp.int32, (g * qb, 3 * qb), 1)
    in_band = jnp.abs(rows + qb - cols) <= win
    for sb in range(sub):
        blk = pl.program_id(1) * sub + sb
        start = pl.multiple_of(blk * qb, qb)
        qs = slice(sb * qb, (sb + 1) * qb)
        q = q_ref[0, qs, :]
        key_pos = cols + (blk - 1) * qb
        valid = in_band & (key_pos >= 0) & (key_pos < n)

        for j in range(hkv):
            hs = slice(j * HEAD_DIM, (j + 1) * HEAD_DIM)
            q4 = _stack_heads(q, j, g)
            s_lat = jnp.dot(q4, kt_ref[0, hs, pl.ds(start, 3 * qb)], preferred_element_type=F32)
            s_lat = jnp.where(valid, s_lat, -jnp.inf)
            s_ctx = jnp.dot(q4, kct_ref[0, hs, :], preferred_element_type=F32)
            s_sink = _sink_column(sink_ref, j, g, qb)
            m = jnp.maximum(jnp.maximum(jnp.max(s_lat, axis=-1, keepdims=True),
                                        jnp.max(s_ctx, axis=-1, keepdims=True)), s_sink)
            e_lat = jnp.exp((s_lat - m).astype(BF16))
            e_ctx = jnp.exp((s_ctx - m).astype(BF16))
            vs = slice(j * 2 * HEAD_DIM, (j + 1) * 2 * HEAD_DIM)
            o = (jnp.dot(e_ctx, vc_ref[0, :, vs], preferred_element_type=F32)
                 + jnp.dot(e_lat, v_ref[0, pl.ds(start, 3 * qb), vs], preferred_element_type=F32))
            o = o[:, :HEAD_DIM] / (o[:, HEAD_DIM:HEAD_DIM + 1] + jnp.exp(s_sink - m))
            for gg in range(g):
                c0 = (j * g + gg) * HEAD_DIM
                o_ref[0, qs, c0:c0 + HEAD_DIM] = o[gg * qb:(gg + 1) * qb].astype(BF16)


def _windowed_attention(q, k, v, kc, vc, sink, hkv):
    b, n, dq = q.shape
    g = dq // HEAD_DIM // hkv
    qb = Q_BLOCK
    lc = kc.shape[1]
    dkv = hkv * HEAD_DIM
    kt = jnp.swapaxes(jnp.pad(k, ((0, 0), (qb, qb), (0, 0))), 1, 2)
    vp = _values_with_ones(jnp.pad(v, ((0, 0), (qb, qb), (0, 0))), hkv)
    vc = _values_with_ones(vc, hkv)
    kct = jnp.swapaxes(kc, 1, 2)
    sub = min(SWA_BLOCKS_PER_STEP, n // qb)
    return pl.pallas_call(
        functools.partial(_swa_body, n=n, hkv=hkv, g=g, qb=qb, win=SWA_WINDOW, sub=sub),
        grid=(b, n // (sub * qb)),
        in_specs=[
            pl.BlockSpec(memory_space=pltpu.SMEM),
            pl.BlockSpec((1, sub * qb, dq), lambda bi, i: (bi, i, 0)),
            pl.BlockSpec((1, dkv, n + 2 * qb), lambda bi, i: (bi, 0, 0)),
            pl.BlockSpec((1, n + 2 * qb, 2 * dkv), lambda bi, i: (bi, 0, 0)),
            pl.BlockSpec((1, dkv, lc), lambda bi, i: (bi, 0, 0)),
            pl.BlockSpec((1, lc, 2 * dkv), lambda bi, i: (bi, 0, 0)),
        ],
        out_specs=pl.BlockSpec((1, sub * qb, dq), lambda bi, i: (bi, i, 0)),
        out_shape=jax.ShapeDtypeStruct((b, n, dq), BF16),
        compiler_params=_params("parallel", "parallel"),
        name="swa",
    )(sink.astype(F32), q, kt, vp, kct, vc)


def _full_attn_body(sink_ref, q_ref, kt_ref, v_ref, o_ref, *, hkv, g, qb, has_sink, unit):
    q = q_ref[0]
    units = [(j, [j * g + u * unit + t for t in range(unit)]) for j in range(hkv) for u in range(g // unit)]

    def scores(j, heads):
        qu = jnp.concatenate([q[:, h * HEAD_DIM:(h + 1) * HEAD_DIM] for h in heads], axis=0)
        return jnp.dot(qu, kt_ref[0, j * HEAD_DIM:(j + 1) * HEAD_DIM, :], preferred_element_type=F32)

    pending = [scores(*u) for u in units[:FULL_ATTN_LOOKAHEAD]]
    for idx, (j, heads) in enumerate(units):
        s = pending.pop(0)
        if idx + FULL_ATTN_LOOKAHEAD < len(units):
            pending.append(scores(*units[idx + FULL_ATTN_LOOKAHEAD]))
        m = jnp.max(s, axis=-1, keepdims=True)
        if has_sink:
            s_sink = jnp.concatenate([jnp.full((qb, 1), sink_ref[h], F32) for h in heads], axis=0)
            m = jnp.maximum(m, s_sink)
        e = jnp.exp((s - m).astype(BF16))
        o = jnp.dot(e, v_ref[0, :, j * 2 * HEAD_DIM:(j + 1) * 2 * HEAD_DIM], preferred_element_type=F32)
        den = o[:, HEAD_DIM:HEAD_DIM + 1]
        if has_sink:
            den = den + jnp.exp(s_sink - m)
        o = o[:, :HEAD_DIM] / den
        for t, h in enumerate(heads):
            o_ref[0, :, h * HEAD_DIM:(h + 1) * HEAD_DIM] = o[t * qb:(t + 1) * qb].astype(BF16)


def _full_attention(q, k, v, hkv, sink=None):
    b, n, dq = q.shape
    g = dq // HEAD_DIM // hkv
    qb = min(FULL_ATTN_Q_ROWS, n)
    unit = max(1, FULL_ATTN_UNIT_ROWS // qb)
    nk = k.shape[1]
    dkv = hkv * HEAD_DIM
    kt = jnp.swapaxes(k, 1, 2)
    has_sink = sink is not None
    sink = jnp.zeros((dq // HEAD_DIM,), F32) if sink is None else sink.astype(F32)
    return pl.pallas_call(
        functools.partial(_full_attn_body, hkv=hkv, g=g, qb=qb, has_sink=has_sink, unit=min(unit, g)),
        grid=(b, n // qb),
        in_specs=[
            pl.BlockSpec(memory_space=pltpu.SMEM),
            pl.BlockSpec((1, qb, dq), lambda bi, i: (bi, i, 0)),
            pl.BlockSpec((1, dkv, nk), lambda bi, i: (bi, 0, 0)),
            pl.BlockSpec((1, nk, 2 * dkv), lambda bi, i: (bi, 0, 0)),
        ],
        out_specs=pl.BlockSpec((1, qb, dq), lambda bi, i: (bi, i, 0)),
        out_shape=jax.ShapeDtypeStruct((b, n, dq), BF16),
        compiler_params=_params("parallel", "parallel"),
        name="full_attn",
    )(sink, q, kt, _values_with_ones(v, hkv))


def _short_conv_body(u_ref, w_ref, b_ref, o_ref, *, h1, l2, c):
    slab = lambda f: u_ref[0, :, f, :]
    row = lax.broadcasted_iota(jnp.int32, (h1, c), 0)
    for f in range(l2):
        prev = slab(f - 1) if f > 0 else jnp.where(row == 0, 0.0, pltpu.roll(slab(l2 - 1), 1, 0))
        nxt = slab(f + 1) if f < l2 - 1 else jnp.where(row == h1 - 1, 0.0, pltpu.roll(slab(0), h1 - 1, 0))
        o_ref[0, 0, :, f * c:(f + 1) * c] = (prev * w_ref[0:1, :] + slab(f) * w_ref[1:2, :]
                                             + nxt * w_ref[2:3, :] + b_ref[...])


def _short_conv(u, w, bias, l2):
    b, n, c3 = u.shape
    c = c3 // 3
    h1 = n // l2
    return pl.pallas_call(
        functools.partial(_short_conv_body, h1=h1, l2=l2, c=c),
        grid=(b, 3),
        in_specs=[
            pl.BlockSpec((1, h1, l2, c), lambda bi, j: (bi, 0, 0, j)),
            pl.BlockSpec((3, c), lambda bi, j: (0, j)),
            pl.BlockSpec((1, c), lambda bi, j: (0, j)),
        ],
        out_specs=pl.BlockSpec((1, 1, h1, l2 * c), lambda bi, j: (j, bi, 0, 0)),
        out_shape=jax.ShapeDtypeStruct((3, b, h1, l2 * c), F32),
        compiler_params=_params("parallel", "parallel"),
        name="short_conv",
    )(u.reshape(b, h1, l2, c3), w, bias.reshape(1, c3))


def _filter_body(band_ref, w1_ref, b1_ref, w2_ref, b2_ref, w3_ref, fr_ref, dl_ref, o_ref, s_ref, *, n, rt, c):
    i = pl.program_id(0)
    hp = lax.Precision.HIGHEST
    m = i * rt + lax.broadcasted_iota(jnp.int32, (rt, 1), 0)
    pos = jnp.where(m < n, m, 2 * n - m).astype(F32)
    t_norm = pos / max(n - 1, 1)
    ang = (2.0 * math.pi / n) * pos * band_ref[...]
    lane = lax.broadcasted_iota(jnp.int32, (rt, LANES), 1)
    z = jnp.where(lane == 0, t_norm,
                  jnp.where(lane <= HY_BANDS, jnp.cos(ang),
                            jnp.where(lane <= 2 * HY_BANDS, -jnp.sin(ang), 0.0)))
    fr = fr_ref[...]
    hdn = jnp.sin(fr * (jnp.dot(z, w1_ref[...], precision=hp, preferred_element_type=F32) + b1_ref[...]))
    hdn = jnp.sin(fr * (jnp.dot(hdn, w2_ref[...], precision=hp, preferred_element_type=F32) + b2_ref[...]))
    h = jnp.dot(hdn, w3_ref[...], precision=hp, preferred_element_type=F32)
    h = h * jnp.exp(-t_norm * dl_ref[...])
    half = HY_ORDER * c
    sel = jnp.where(m < n, h[:, :half], jnp.where(m > n, -h[:, half:], 0.0))
    for o in range(HY_ORDER):
        o_ref[o] = sel[:, o * c:(o + 1) * c]

    @pl.when(i == 0)
    def _():
        s_ref[...] = jnp.zeros_like(s_ref)

    s_ref[...] += jnp.sum(jnp.abs(sel), axis=0, keepdims=True)


def _hyena_filters(n, w1, b1, w2, b2, w3, freq, c):
    rt = min(1024, n)
    hid = w1.shape[1]
    bands = jnp.linspace(1e-4, HY_BANDS - 1, HY_BANDS, dtype=F32)
    band_row = jnp.zeros((1, LANES), F32).at[0, 1:1 + 2 * HY_BANDS].set(jnp.tile(bands, 2))
    w1p = jnp.zeros((LANES, hid), F32).at[:w1.shape[0]].set(w1)
    max_decay = math.log(HY_DECAY_TARGET) / HY_FAST_DECAY
    min_decay = math.log(HY_DECAY_TARGET) / HY_SLOW_DECAY
    deltas = jnp.abs(jnp.linspace(min_decay, max_decay, c, dtype=F32))
    dl = jnp.tile(deltas, HY_DIRS * HY_ORDER).reshape(1, -1)
    full = lambda a: pl.BlockSpec(a.shape, lambda i: (0,) * a.ndim)
    args = (band_row, w1p, b1.reshape(1, hid), w2, b2.reshape(1, hid), w3, freq.reshape(1, hid), dl)
    return pl.pallas_call(
        functools.partial(_filter_body, n=n, rt=rt, c=c),
        grid=(2 * n // rt,),
        in_specs=[full(a) for a in args],
        out_specs=[pl.BlockSpec((HY_ORDER, rt, c), lambda i: (0, i, 0)),
                   pl.BlockSpec((1, HY_ORDER * c), lambda i: (0, 0))],
        out_shape=[jax.ShapeDtypeStruct((HY_ORDER, 2 * n, c), F32),
                   jax.ShapeDtypeStruct((1, HY_ORDER * c), F32)],
        compiler_params=_params("arbitrary"),
        name="hyena_filter",
    )(*args)


def _dft_split(n):
    l2 = 32 if n >= 2048 else 16
    return 2 * n // l2, l2


def _dft_constants(n):
    l1, l2 = _dft_split(n)
    h1 = l1 // 2
    nn = 2 * n
    k1 = np.arange(h1)[:, None]
    a = 2 * np.pi * (k1 + 0.5) * np.arange(l1)[None, :] / l1
    w1 = np.concatenate([np.cos(a), -np.sin(a)], axis=0)
    t = 2 * np.pi * (k1 + 0.5) * np.arange(l2)[None, :] / nn
    tw_cos, tw_sin = np.cos(t), np.sin(t)
    p = 2 * np.pi * np.arange(l2)[:, None] * np.arange(l2)[None, :] / l2
    eye = np.eye(MID_KB)
    cos_k, sin_k = np.kron(np.cos(p), eye), np.kron(np.sin(p), eye)
    w2 = np.block([[cos_k, sin_k], [-sin_k, cos_k]])
    w2i = np.block([[cos_k, -sin_k], [sin_k, cos_k]])
    tw_rows = lambda m: m.reshape(h1 // MID_KB, MID_KB, l2).transpose(0, 2, 1).reshape(h1 // MID_KB, l2 * MID_KB, 1)
    ai = 2 * np.pi * np.arange(h1)[:, None] * (np.arange(h1)[None, :] + 0.5) / l1
    w1i = (2.0 / nn) * np.concatenate([np.cos(ai), -np.sin(ai)], axis=1)
    c = lambda m, dt: jnp.asarray(m, dtype=dt)
    return dict(
        l1=l1, l2=l2, h1=h1,
        w1=c(w1, BF16), w2=c(w2, BF16), w2i=c(w2i, BF16), w1i=c(w1i, BF16),
        tw_cos_fwd=c(tw_cos.T[:, :, None], F32), tw_sin_fwd=c(tw_sin.T[:, :, None], F32),
        tw_cos_inv=c(tw_rows(tw_cos), F32), tw_sin_inv=c(tw_rows(tw_sin), F32),
    )


def _ct_fwd1_body(x_ref, w_ref, tc_ref, ts_ref, nrm_ref, o_ref, *, lb, c, h1, normalise):
    for q in range(lb):
        x = x_ref[0, 0, :, q * c:(q + 1) * c]
        if normalise:
            x = x / (nrm_ref[0] + HY_FILTER_EPS)
        a = jnp.dot(w_ref[...], x.astype(BF16), preferred_element_type=F32)
        ar, ai = a[:h1], a[h1:]
        tc, ts = tc_ref[q], ts_ref[q]
        o_ref[0, 0, q] = ar * tc + ai * ts
        o_ref[0, 1, q] = ai * tc - ar * ts


def _ct_fwd1(xs, idx, consts, c, norms=None, lb=8):
    _, b, k1n, _ = xs.shape
    l2, h1 = consts["l2"], consts["h1"]
    lb = min(lb, l2)
    w = consts["w1"][:, :k1n]
    normalise = norms is not None
    if norms is None:
        norms = jnp.zeros((b, 1, c), F32)
    return pl.pallas_call(
        functools.partial(_ct_fwd1_body, lb=lb, c=c, h1=h1, normalise=normalise),
        grid=(b, l2 // lb),
        in_specs=[
            pl.BlockSpec((1, 1, k1n, lb * c), lambda bi, i: (idx, bi, 0, i)),
            pl.BlockSpec(w.shape, lambda bi, i: (0, 0)),
            pl.BlockSpec((lb, h1, 1), lambda bi, i: (i, 0, 0)),
            pl.BlockSpec((lb, h1, 1), lambda bi, i: (i, 0, 0)),
            pl.BlockSpec((1, 1, c), lambda bi, i: (bi, 0, 0)),
        ],
        out_specs=pl.BlockSpec((1, 2, lb, h1, c), lambda bi, i: (bi, 0, i, 0, 0)),
        out_shape=jax.ShapeDtypeStruct((b, 2, l2, h1, c), F32),
        compiler_params=_params("parallel", "parallel"),
        name="ct_fwd1",
    )(xs, w, consts["tw_cos_fwd"], consts["tw_sin_fwd"], norms)


def _stacked(ref, lead, rows, c):
    return jnp.concatenate([ref[lead + (0,)].reshape(rows, c), ref[lead + (1,)].reshape(rows, c)], axis=0)


def _ct_spec_body(a_ref, w2_ref, o_ref, *, l2, c):
    rows = l2 * MID_KB
    x = jnp.dot(w2_ref[...], _stacked(a_ref, (0,), rows, c).astype(BF16), preferred_element_type=F32)
    o_ref[0, 0] = x[:rows].reshape(l2, MID_KB, c)
    o_ref[0, 1] = x[rows:].reshape(l2, MID_KB, c)


def _ct_spectrum(a, consts, c):
    b = a.shape[0]
    l2, h1 = consts["l2"], consts["h1"]
    blk = pl.BlockSpec((1, 2, l2, MID_KB, c), lambda bi, i: (bi, 0, 0, i, 0))
    return pl.pallas_call(
        functools.partial(_ct_spec_body, l2=l2, c=c),
        grid=(b, h1 // MID_KB),
        in_specs=[blk, pl.BlockSpec(consts["w2"].shape, lambda bi, i: (0, 0))],
        out_specs=blk,
        out_shape=jax.ShapeDtypeStruct((b, 2, l2, h1, c), F32),
        compiler_params=_params("parallel", "parallel"),
        name="ct_spectrum",
    )(a, consts["w2"])


def _ct_mid_body(a_ref, h_ref, w2_ref, w2i_ref, tc_ref, ts_ref, o_ref, *, l2, c):
    rows = l2 * MID_KB
    x = jnp.dot(w2_ref[...], _stacked(a_ref, (0,), rows, c).astype(BF16), preferred_element_type=F32)
    xr, xi = x[:rows], x[rows:]
    hr, hi = h_ref[0, 0].reshape(rows, c), h_ref[0, 1].reshape(rows, c)
    y = jnp.concatenate([xr * hr - xi * hi, xr * hi + xi * hr], axis=0).astype(BF16)
    bm = jnp.dot(w2i_ref[...], y, preferred_element_type=F32)
    br, bi = bm[:rows], bm[rows:]
    tc, ts = tc_ref[0], ts_ref[0]
    o_ref[0, 0] = (br * tc - bi * ts).reshape(l2, MID_KB, c)
    o_ref[0, 1] = (br * ts + bi * tc).reshape(l2, MID_KB, c)


def _ct_mid(a, hspec, order, consts, c):
    b = a.shape[0]
    l2, h1 = consts["l2"], consts["h1"]
    blk = pl.BlockSpec((1, 2, l2, MID_KB, c), lambda i, bi: (bi, 0, 0, i, 0))
    tw = pl.BlockSpec((1, l2 * MID_KB, 1), lambda i, bi: (i, 0, 0))
    return pl.pallas_call(
        functools.partial(_ct_mid_body, l2=l2, c=c),
        grid=(h1 // MID_KB, b),
        in_specs=[
            blk,
            pl.BlockSpec((1, 2, l2, MID_KB, c), lambda i, bi: (order, 0, 0, i, 0)),
            pl.BlockSpec(consts["w2"].shape, lambda i, bi: (0, 0)),
            pl.BlockSpec(consts["w2i"].shape, lambda i, bi: (0, 0)),
            tw,
            tw,
        ],
        out_specs=blk,
        out_shape=jax.ShapeDtypeStruct((b, 2, l2, h1, c), F32),
        compiler_params=_params("parallel", "parallel"),
        name="ct_mid",
    )(a, hspec, consts["w2"], consts["w2i"], consts["tw_cos_inv"], consts["tw_sin_inv"])


def _ct_inv1_body(b_ref, w_ref, u_ref, gate_ref, skip_ref, o_ref, *, lb, c, by_position):
    for q in range(lb):
        bb = jnp.concatenate([b_ref[0, 0, q], b_ref[0, 1, q]], axis=0).astype(BF16)
        y = jnp.dot(w_ref[...], bb, preferred_element_type=F32)
        cs = slice(q * c, (q + 1) * c)
        out = gate_ref[0, 0, :, cs] * (y + u_ref[0, 0, :, cs] * skip_ref[...])
        if by_position:
            o_ref[0, :, q, :] = out
        else:
            o_ref[0, :, cs] = out


def _ct_inv1(bsp, u, u_idx, gate, gate_idx, skip, consts, c, by_position, lb=8):
    b = bsp.shape[0]
    l2, h1 = consts["l2"], consts["h1"]
    lb = min(lb, l2)
    if by_position:
        out_spec = pl.BlockSpec((1, h1, lb, c), lambda bi, i: (bi, 0, i, 0))
        out_shape = jax.ShapeDtypeStruct((b, h1, l2, c), F32)
    else:
        out_spec = pl.BlockSpec((1, h1, lb * c), lambda bi, i: (bi, 0, i))
        out_shape = jax.ShapeDtypeStruct((b, h1, l2 * c), F32)
    return pl.pallas_call(
        functools.partial(_ct_inv1_body, lb=lb, c=c, by_position=by_position),
        grid=(b, l2 // lb),
        in_specs=[
            pl.BlockSpec((1, 2, lb, h1, c), lambda bi, i: (bi, 0, i, 0, 0)),
            pl.BlockSpec((h1, 2 * h1), lambda bi, i: (0, 0)),
            pl.BlockSpec((1, 1, h1, lb * c), lambda bi, i: (u_idx, bi, 0, i)),
            pl.BlockSpec((1, 1, h1, lb * c), lambda bi, i: (gate_idx, bi, 0, i)),
            pl.BlockSpec((1, c), lambda bi, i: (0, 0)),
        ],
        out_specs=out_spec,
        out_shape=out_shape,
        compiler_params=_params("parallel", "parallel"),
        name="ct_inv1",
    )(bsp, consts["w1i"], u, gate, skip.reshape(1, c))


def _hyena(u, conv_w, conv_b, f_w1, f_b1, f_w2, f_b2, f_w3, f_freq, skip):
    b, n, c3 = u.shape
    c = c3 // 3
    consts = _dft_constants(n)
    l1, l2, h1 = consts["l1"], consts["l2"], consts["h1"]
    filt, norms = _hyena_filters(n, f_w1, f_b1, f_w2, f_b2, f_w3, f_freq, c)
    fa = _ct_fwd1(filt.reshape(1, HY_ORDER, l1, l2 * c), 0, consts, c, norms=norms.reshape(HY_ORDER, 1, c))
    hspec = _ct_spectrum(fa, consts, c)
    parts = _short_conv(u, conv_w, conv_b, l2)

    def long_conv_gated(x_stack, x_idx, gate_idx, order, by_position):
        a = _ct_fwd1(x_stack, x_idx, consts, c)
        bsp = _ct_mid(a, hspec, order, consts, c)
        return _ct_inv1(bsp, x_stack, x_idx, parts, gate_idx, skip[order], consts, c, by_position)

    z = long_conv_gated(parts, 0, 1, 0, False)
    return long_conv_gated(z[None], 0, 2, 1, True).reshape(b, n, c)


def _outproj_body(*refs, n_in):
    ins = refs[:n_in]
    ws = refs[n_in:2 * n_in]
    x_ref, gate_ref, g_ref, sc_ref, sh_ref, rhi_ref, rlo_ref, rb_ref, xo_ref, h_ref, rt_ref = refs[2 * n_in:]
    y = None
    for a_ref, w_ref in zip(ins, ws):
        t = jnp.dot(a_ref[0].astype(BF16), w_ref[...], preferred_element_type=F32)
        y = t if y is None else y + t
    x = x_ref[0] + gate_ref[0] * y
    xo_ref[0] = x
    h = x * lax.rsqrt(jnp.mean(x * x, axis=-1, keepdims=True) + NORM_EPS) * g_ref[...]
    h = h * (1.0 + sc_ref[0]) + sh_ref[0]
    hi = h.astype(BF16)
    h_ref[0] = hi
    lo = (h - hi.astype(F32)).astype(BF16)
    lg = (jnp.dot(hi, rhi_ref[...], preferred_element_type=F32)
          + jnp.dot(lo, rhi_ref[...], preferred_element_type=F32)
          + jnp.dot(hi, rlo_ref[...], preferred_element_type=F32) + rb_ref[...])
    rt_ref[0] = _route(lg)


def _route(lg):
    lane = lax.broadcasted_iota(jnp.int32, lg.shape, 1)
    lane_f = lane.astype(F32)
    neg = -jnp.inf

    def top(v):
        m = jnp.max(v, axis=-1, keepdims=True)
        return m, jnp.min(jnp.where(v == m, lane_f, float(LANES)), axis=-1, keepdims=True)

    gl = jnp.where(lane < N_GROUPS, lg, neg)
    gmax, grp = top(gl)
    p_grp = 1.0 / jnp.sum(jnp.exp(gl - gmax), axis=-1, keepdims=True)
    first = N_GROUPS + grp * EXP_PER_GROUP
    el = jnp.where((lane_f >= first) & (lane_f < first + EXP_PER_GROUP), lg, neg)
    m1, i1 = top(el)
    m2, i2 = top(jnp.where(lane_f == i1, neg, el))
    e2 = jnp.exp(m2 - m1)
    den = 1.0 + e2
    vals = (i1 - N_GROUPS, i2 - N_GROUPS, p_grp * (1.0 / den), p_grp * (e2 / den))
    out = jnp.zeros(lg.shape, F32)
    for k, v in enumerate(vals):
        out = jnp.where(lane == k, v, out)
    return out


def _outproj(ins, ws, x, gate, g, scale, shift, r_hi, r_lo, r_b, tm=512):
    b, n, d = x.shape
    tm = min(tm, n)
    bm = gate.shape[0]
    mod_map = (lambda bi, i: (bi, 0, 0)) if bm > 1 else (lambda bi, i: (0, 0, 0))
    row = lambda wd: pl.BlockSpec((1, tm, wd), lambda bi, i: (bi, i, 0))
    full = lambda a: pl.BlockSpec(a.shape, lambda bi, i: (0,) * a.ndim)
    mod = pl.BlockSpec((1, 1, d), mod_map)
    return pl.pallas_call(
        functools.partial(_outproj_body, n_in=len(ins)),
        grid=(b, n // tm),
        in_specs=([row(a.shape[-1]) for a in ins] + [full(w) for w in ws]
                  + [row(d), mod, pl.BlockSpec((1, d), lambda bi, i: (0, 0)), mod, mod,
                     full(r_hi), full(r_lo), full(r_b)]),
        out_specs=[row(d), row(d), row(LANES)],
        out_shape=[jax.ShapeDtypeStruct((b, n, d), F32), jax.ShapeDtypeStruct((b, n, d), BF16),
                   jax.ShapeDtypeStruct((b, n, LANES), F32)],
        compiler_params=_params("parallel", "parallel"),
        name="outproj",
    )(*ins, *ws, x, gate, g.reshape(1, d), scale, shift, r_hi, r_lo, r_b)


def _rank_body(rt_ref, tri_ref, upper_ref, pos_ref, cnt_ref):
    rt = rt_ref[...]
    lane_i = lax.broadcasted_iota(jnp.int32, rt.shape, 1)
    lane = lane_i.astype(F32)
    oh_a = lane == rt[:, 0:1]
    oh_b = lane == rt[:, 1:2]
    one_a = jnp.where(oh_a, 1.0, 0.0)
    one_b = jnp.where(oh_b, 1.0, 0.0)
    before_a = jnp.dot(tri_ref[...], one_a.astype(BF16), preferred_element_type=F32)
    before_b = jnp.dot(tri_ref[...], one_b.astype(BF16), preferred_element_type=F32)
    tot_a = jnp.sum(one_a, axis=0, keepdims=True)
    cnt = tot_a + jnp.sum(one_b, axis=0, keepdims=True)
    padded = jnp.floor((cnt + (CHUNK_ROWS - 1)) * (1.0 / CHUNK_ROWS)) * CHUNK_ROWS
    first = jnp.dot(jnp.broadcast_to(padded, (8, LANES)).astype(BF16), upper_ref[...],
                    preferred_element_type=F32)[0:1]
    pos_a = jnp.sum(jnp.where(oh_a, before_a + first, 0.0), axis=-1, keepdims=True)
    pos_b = jnp.sum(jnp.where(oh_b, before_b + first + tot_a, 0.0), axis=-1, keepdims=True)
    is_gate = (lane_i >= TOP_K) & (lane_i < 2 * TOP_K)
    pos_ref[...] = jnp.where(lane_i == 0, pos_a, jnp.where(lane_i == 1, pos_b, jnp.where(is_gate, rt, 0.0)))
    cnt_ref[0] = jnp.broadcast_to(cnt, (8, LANES))


def _rank(route):
    t = route.shape[0]
    tm = TOKEN_TILE
    tri = jnp.asarray(np.tril(np.ones((tm, tm)), -1), dtype=BF16)
    upper = jnp.asarray(np.triu(np.ones((LANES, LANES)), 1), dtype=BF16)
    return pl.pallas_call(
        _rank_body,
        grid=(t // tm,),
        in_specs=[pl.BlockSpec((tm, LANES), lambda i: (i, 0)), pl.BlockSpec((tm, tm), lambda i: (0, 0)),
                  pl.BlockSpec((LANES, LANES), lambda i: (0, 0))],
        out_specs=[pl.BlockSpec((tm, LANES), lambda i: (i, 0)), pl.BlockSpec((1, 8, LANES), lambda i: (i, 0, 0))],
        out_shape=[jax.ShapeDtypeStruct((t, LANES), F32), jax.ShapeDtypeStruct((t // tm, 8, LANES), F32)],
        compiler_params=_params("parallel"),
        name="moe_rank",
    )(route, tri, upper)


def _chunk_tables(cnt, n_blocks):
    padded = (cnt + CHUNK_ROWS - 1) // CHUNK_ROWS * CHUNK_ROWS
    run_end = jnp.cumsum(padded, axis=1)
    run_start = run_end - padded
    seg_rows = jnp.sum(padded, axis=0)
    seg_rows = (seg_rows + EXPERT_ROWS - 1) // EXPERT_ROWS * EXPERT_ROWS
    seg_end = jnp.cumsum(seg_rows)
    dst_start = (seg_end - seg_rows)[None, :] + jnp.cumsum(padded, axis=0) - padded
    row0 = jnp.arange(BUF_CHUNKS, dtype=jnp.int32) * CHUNK_ROWS
    chunk_exp = jnp.minimum(jnp.sum(run_end[:, None, :] <= row0[None, :, None], axis=-1), N_EXPERTS - 1)
    onehot = chunk_exp[:, :, None] == jnp.arange(N_EXPERTS, dtype=jnp.int32)[None, None, :]
    dst = jnp.sum(jnp.where(onehot, (dst_start - run_start)[:, None, :], 0), axis=-1) + row0[None, :]
    n_chunks = run_end[:, -1:] // CHUNK_ROWS
    table = jnp.concatenate(
        [dst, n_chunks, jnp.zeros((cnt.shape[0], TABLE_WORDS - BUF_CHUNKS - 1), jnp.int32)], axis=1)
    block_row0 = jnp.arange(n_blocks, dtype=jnp.int32) * EXPERT_ROWS
    block_exp = jnp.minimum(jnp.sum(seg_end[None, :] <= block_row0[:, None], axis=1), N_EXPERTS - 1)
    n_used = (seg_end[-1] // EXPERT_ROWS).reshape(1)
    return table.astype(jnp.int32), block_exp.astype(jnp.int32), n_used.astype(jnp.int32)


def _pack_halves(x):
    w = x.shape[-1] // 2
    lo = lax.bitcast_convert_type(x[:, :w].astype(BF16).astype(F32), jnp.int32)
    hi = lax.bitcast_convert_type(x[:, w:].astype(BF16).astype(F32), jnp.int32)
    return lax.shift_right_logical(lo, jnp.int32(16)) | (hi & jnp.int32(-65536))


def _unpack_halves(p):
    lo = lax.bitcast_convert_type(lax.shift_left(p, jnp.int32(16)), F32)
    hi = lax.bitcast_convert_type(p & jnp.int32(-65536), F32)
    return jnp.concatenate([lo, hi], axis=-1).astype(BF16)


def _start_chunks(tab_ref, tile, make_copy):
    base = tile * TABLE_WORDS

    def issue(c):
        make_copy(pl.multiple_of(c * CHUNK_ROWS, CHUNK_ROWS),
                  pl.multiple_of(tab_ref[base + c], CHUNK_ROWS)).start()

    _for_each_chunk(tab_ref[base + BUF_CHUNKS], issue)


def _wait_chunks(tab_ref, tile, make_copy):
    _for_each_chunk(tab_ref[tile * TABLE_WORDS + BUF_CHUNKS], lambda c: make_copy(0, 0).wait())


def _for_each_chunk(n, fn):
    groups = n // LOOP_GROUP

    def grouped(i, carry):
        for k in range(LOOP_GROUP):
            fn(i * LOOP_GROUP + k)
        return carry

    def single(c, carry):
        fn(c)
        return carry

    lax.fori_loop(0, groups, grouped, 0)
    lax.fori_loop(groups * LOOP_GROUP, n, single, 0)


def _dispatch_body(tab_ref, h_ref, pos_ref, xs_in, xs_out, buf, sem_rows, *, tile0, n_steps):
    del xs_in
    step = pl.program_id(0)
    dp = h_ref.shape[1] // 2

    def copies(at_step):
        slot = at_step % 2
        return lambda src, dst: pltpu.make_async_copy(
            buf.at[slot, pl.ds(src, CHUNK_ROWS)], xs_out.at[pl.ds(dst, CHUNK_ROWS)], sem_rows.at[slot])

    @pl.when(step >= 2)
    def _():
        _wait_chunks(tab_ref, tile0 + step - 2, copies(step - 2))

    tm = h_ref.shape[0]
    row = lax.broadcasted_iota(jnp.int32, (BUF_ROWS, tm), 0).astype(F32)
    oh_a = row == pos_ref[0, 0:1, :]
    oh_b = row == pos_ref[0, 1:2, :]
    buf[step % 2, :, :dp] = _pack_halves(jnp.dot(jnp.where(oh_a | oh_b, 1.0, 0.0).astype(BF16), h_ref[...],
                                                 preferred_element_type=F32))
    gate = jnp.sum(jnp.where(oh_a, pos_ref[0, 2:3, :], 0.0) + jnp.where(oh_b, pos_ref[0, 3:4, :], 0.0),
                   axis=-1, keepdims=True)
    buf[step % 2, :, dp:] = lax.bitcast_convert_type(jnp.broadcast_to(gate, (BUF_ROWS, LANES)), jnp.int32)
    _start_chunks(tab_ref, tile0 + step, copies(step))

    @pl.when(step == n_steps - 1)
    def _():
        @pl.when(step >= 1)
        def _():
            _wait_chunks(tab_ref, tile0 + step - 1, copies(step - 1))

        _wait_chunks(tab_ref, tile0 + step, copies(step))


def _dispatch(h, pos_t, table, xs, tile0):
    t, d = h.shape
    tm = TOKEN_TILE
    n_steps = t // tm
    grid_spec = pltpu.PrefetchScalarGridSpec(
        num_scalar_prefetch=1,
        grid=(n_steps,),
        in_specs=[
            pl.BlockSpec((tm, d), lambda i, tab: (i, 0)),
            pl.BlockSpec((1, 8, tm), lambda i, tab: (tile0 + i, 0, 0)),
            pl.BlockSpec(memory_space=pl.ANY),
        ],
        out_specs=pl.BlockSpec(memory_space=pl.ANY),
        scratch_shapes=[
            pltpu.VMEM((2, BUF_ROWS, d // 2 + LANES), jnp.int32),
            pltpu.SemaphoreType.DMA((2,)),
        ],
    )
    return pl.pallas_call(
        functools.partial(_dispatch_body, tile0=tile0, n_steps=n_steps),
        grid_spec=grid_spec,
        out_shape=jax.ShapeDtypeStruct(xs.shape, xs.dtype),
        input_output_aliases={3: 0},
        compiler_params=_params("arbitrary"),
        name="moe_dispatch",
    )(table.reshape(-1), h, pos_t, xs)


def _expert_body(bexp_ref, nused_ref, x_ref, wg_ref, wu_ref, wd_ref, o_ref, wg_bf, wu_bf, wd_bf):
    i = pl.program_id(0)

    @pl.when((i == 0) | (bexp_ref[i] != bexp_ref[jnp.maximum(i - 1, 0)]))
    def _():
        wg_bf[...] = wg_ref[0, 0].astype(BF16)
        wu_bf[...] = wu_ref[0, 0].astype(BF16)
        wd_bf[...] = wd_ref[0, 0].astype(BF16)

    @pl.when(i < nused_ref[0])
    def _():
        dp = o_ref.shape[1]
        xb = _unpack_halves(x_ref[:, :dp])
        gate = lax.bitcast_convert_type(x_ref[:, dp:dp + 1], F32)
        gt = jnp.dot(xb, wg_bf[...], preferred_element_type=F32)
        up = jnp.dot(xb, wu_bf[...], preferred_element_type=F32)
        hid = (gt * jax.nn.sigmoid(gt) * up).astype(BF16)
        o_ref[...] = _pack_halves(jnp.dot(hid, wd_bf[...], preferred_element_type=F32) * gate)

    @pl.when(i >= nused_ref[0])
    def _():
        o_ref[...] = jnp.zeros_like(o_ref)


def _experts(xs, block_exp, n_used, w_gate, w_up, w_down, layer):
    rows, width = xs.shape
    d, de = w_gate.shape[2:]
    used = lambda i, be, nu: (jnp.minimum(i, nu[0] - 1), 0)
    expert = lambda i, be, nu: (layer, be[i], 0, 0)
    grid_spec = pltpu.PrefetchScalarGridSpec(
        num_scalar_prefetch=2,
        grid=(rows // EXPERT_ROWS,),
        in_specs=[
            pl.BlockSpec((EXPERT_ROWS, width), used),
            pl.BlockSpec((1, 1, d, de), expert),
            pl.BlockSpec((1, 1, d, de), expert),
            pl.BlockSpec((1, 1, de, d), expert),
        ],
        out_specs=pl.BlockSpec((EXPERT_ROWS, d // 2), lambda i, be, nu: (i, 0)),
        scratch_shapes=[pltpu.VMEM((d, de), BF16), pltpu.VMEM((d, de), BF16), pltpu.VMEM((de, d), BF16)],
    )
    return pl.pallas_call(
        _expert_body,
        grid_spec=grid_spec,
        out_shape=jax.ShapeDtypeStruct((rows, d // 2), jnp.int32),
        compiler_params=_params("arbitrary"),
        name="experts",
    )(block_exp, n_used, xs, w_gate, w_up, w_down)


def _combine_body(tab_ref, ys_hbm, x_ref, gate_ref, pos_ref, fg_ref, o_ref, ybuf, sem_rows,
                  *, tile0, n_tiles, n_steps, final_norm):
    step = pl.program_id(0) * n_tiles + pl.program_id(1)

    def copies(at_step):
        slot = at_step % 2
        return lambda dst, src: pltpu.make_async_copy(
            ys_hbm.at[pl.ds(src, CHUNK_ROWS)], ybuf.at[slot, pl.ds(dst, CHUNK_ROWS)], sem_rows.at[slot])

    @pl.when(step == 0)
    def _():
        ybuf[...] = jnp.zeros_like(ybuf)
        _start_chunks(tab_ref, tile0, copies(0))

    @pl.when(step + 1 < n_steps)
    def _():
        _start_chunks(tab_ref, tile0 + step + 1, copies(step + 1))

    _wait_chunks(tab_ref, tile0 + step, copies(step))
    tm = x_ref.shape[1]
    col = lax.broadcasted_iota(jnp.int32, (tm, BUF_ROWS), 1).astype(F32)
    pick = jnp.where((col == pos_ref[:, 0:1]) | (col == pos_ref[:, 1:2]), 1.0, 0.0).astype(BF16)
    out = x_ref[0] + gate_ref[0] * jnp.dot(pick, _unpack_halves(ybuf[step % 2]), preferred_element_type=F32)
    if final_norm:
        out = out * lax.rsqrt(jnp.mean(out * out, axis=-1, keepdims=True) + NORM_EPS) * fg_ref[...]
    o_ref[0] = out


def _combine(ys, table, pos, x, gate, tile0, final_g=None):
    b, n, d = x.shape
    tm = min(TOKEN_TILE, n)
    n_tiles = n // tm
    bm = gate.shape[0]
    mod_map = (lambda bi, i, tab: (bi, 0, 0)) if bm > 1 else (lambda bi, i, tab: (0, 0, 0))
    final_norm = final_g is not None
    fg = final_g.reshape(1, d) if final_norm else jnp.ones((1, d), F32)
    grid_spec = pltpu.PrefetchScalarGridSpec(
        num_scalar_prefetch=1,
        grid=(b, n_tiles),
        in_specs=[
            pl.BlockSpec(memory_space=pl.ANY),
            pl.BlockSpec((1, tm, d), lambda bi, i, tab: (bi, i, 0)),
            pl.BlockSpec((1, 1, d), mod_map),
            pl.BlockSpec((tm, LANES), lambda bi, i, tab: (tile0 + bi * n_tiles + i, 0)),
            pl.BlockSpec((1, d), lambda bi, i, tab: (0, 0)),
        ],
        out_specs=pl.BlockSpec((1, tm, d), lambda bi, i, tab: (bi, i, 0)),
        scratch_shapes=[
            pltpu.VMEM((2, BUF_ROWS, d // 2), jnp.int32),
            pltpu.SemaphoreType.DMA((2,)),
        ],
    )
    return pl.pallas_call(
        functools.partial(_combine_body, tile0=tile0, n_tiles=n_tiles, n_steps=b * n_tiles,
                          final_norm=final_norm),
        grid_spec=grid_spec,
        out_shape=jax.ShapeDtypeStruct((b, n, d), F32),
        compiler_params=_params("arbitrary", "arbitrary"),
        name="moe_combine",
    )(table.reshape(-1), ys, x, gate, pos, fg)


def _router_weights(wg, bg, we, be):
    d = wg.shape[0]
    w = jnp.zeros((d, LANES), F32).at[:, :N_GROUPS].set(wg).at[:, N_GROUPS:N_GROUPS + N_EXPERTS].set(we)
    bias = jnp.zeros((1, LANES), F32).at[0, :N_GROUPS].set(bg).at[0, N_GROUPS:N_GROUPS + N_EXPERTS].set(be)
    hi = w.astype(BF16)
    lo = (w - hi.astype(F32)).astype(BF16)
    return hi, lo, bias


def kernel(x, c, ctx, c_ctx, ada_w, ada_b, norm1_g, norm2_g, ev_w_in, ev_w_out, hy_conv_w, hy_conv_b, hy_f_w1, hy_f_b1, hy_f_w2, hy_f_b2, hy_f_w3, hy_f_freq, hy_skip, swa_sink, od_w_qkv, od_w_out, od_q_norm_g, od_k_norm_g, rt_group_w, rt_group_b, rt_exp_w, rt_exp_b, moe_w_gate, moe_w_up, moe_w_down, final_norm_g):
    b, n, d = x.shape
    lc = ctx.shape[1]
    depth = ada_w.shape[0]
    rope = _rope_tables(n)
    xc = ctx
    sc = jax.nn.silu(c)
    scc = jax.nn.silu(c_ctx)
    q_scale = HEAD_DIM ** -0.5
    for layer in range(depth):
        with_ctx = layer < depth - 1
        mod = (sc @ ada_w[layer] + ada_b[layer]).reshape(b, N_MOD, 1, d)
        modc = (scc @ ada_w[layer] + ada_b[layer]).reshape(1, N_MOD, 1, d)
        m = [mod[:, k] for k in range(N_MOD)]
        mc = [modc[:, k] for k in range(N_MOD)]
        r_hi, r_lo, r_b = _router_weights(rt_group_w[layer], rt_group_b[layer], rt_exp_w[layer], rt_exp_b[layer])
        if layer % 2 == 0:
            e = layer // 2
            c_hy = hy_conv_w.shape[-1] // 3
            d_hy = 3 * c_hy
            hq = swa_sink.shape[-1]
            d_q = hq * HEAD_DIM
            hkv = hq // 4
            d_kv = hkv * HEAD_DIM
            w_in = ev_w_in[e].astype(BF16)
            w_out = ev_w_out[e].astype(BF16)
            hy_args = (hy_conv_w[e], hy_conv_b[e], hy_f_w1[e], hy_f_b1[e], hy_f_w2[e], hy_f_b2[e],
                       hy_f_w3[e], hy_f_freq[e], hy_skip[e])
            u, q, k, v = _proj(x, norm1_g[layer], m[1], m[0], w_in, [
                (0, d_hy, "f32", None, False, 1.0),
                (d_hy, d_q, "qk", None, True, q_scale),
                (d_hy + d_q, d_kv, "qk", None, True, 1.0),
                (d_hy + d_q + d_kv, d_kv, "bf16", None, False, 1.0)], rope_tabs=rope)
            if with_ctx:
                uc, qc, kc, vc = _proj(xc, norm1_g[layer], mc[1], mc[0], w_in, [
                    (0, d_hy, "f32", None, False, 1.0),
                    (d_hy, d_q, "qk", None, False, q_scale),
                    (d_hy + d_q, d_kv, "bf16", None, False, 1.0),
                    (d_hy + d_q + d_kv, d_kv, "bf16", None, False, 1.0)])
            else:
                kc, vc = _proj(xc, norm1_g[layer], mc[1], mc[0], w_in, [
                    (d_hy + d_q, d_kv, "bf16", None, False, 1.0),
                    (d_hy + d_q + d_kv, d_kv, "bf16", None, False, 1.0)])
            y_hy = _hyena(u, *hy_args)
            y_att = _windowed_attention(q, k, v, kc, vc, swa_sink[e], hkv)
            mix_in, mix_w = [y_hy, y_att], [w_out[:c_hy], w_out[c_hy:]]
            if with_ctx:
                yc_hy = _hyena(uc, *hy_args)
                yc_att = _full_attention(qc, kc, vc, hkv, sink=swa_sink[e])
                mixc_in = [yc_hy, yc_att]
        else:
            o = layer // 2
            hkv = od_w_qkv.shape[-1] // HEAD_DIM // 6
            hq = 4 * hkv
            d_q = hq * HEAD_DIM
            d_kv = hkv * HEAD_DIM
            w_qkv = od_w_qkv[o].astype(BF16)
            w_out = od_w_out[o].astype(BF16)
            norm_g = jnp.zeros((8, LANES), F32).at[0].set(jnp.tile(od_q_norm_g[o], 2)).at[1].set(
                jnp.tile(od_k_norm_g[o], 2))
            q, k, v = _proj(x, norm1_g[layer], m[1], m[0], w_qkv, [
                (0, d_q, "qk", 0, True, q_scale),
                (d_q, d_kv, "qk", 1, True, 1.0),
                (d_q + d_kv, d_kv, "bf16", None, False, 1.0)], rope_tabs=rope, norm_g=norm_g)
            if with_ctx:
                qc, kc, vc = _proj(xc, norm1_g[layer], mc[1], mc[0], w_qkv, [
                    (0, d_q, "qk", 0, False, q_scale),
                    (d_q, d_kv, "qk", 1, False, 1.0),
                    (d_q + d_kv, d_kv, "bf16", None, False, 1.0)], norm_g=norm_g)
            else:
                kc, vc = _proj(xc, norm1_g[layer], mc[1], mc[0], w_qkv, [
                    (d_q, d_kv, "qk", 1, False, 1.0),
                    (d_q + d_kv, d_kv, "bf16", None, False, 1.0)], norm_g=norm_g)
            y_att = _full_attention(q, jnp.concatenate([kc, k], axis=1), jnp.concatenate([vc, v], axis=1), hkv)
            mix_in, mix_w = [y_att], [w_out]
            if with_ctx:
                mixc_in = [_full_attention(qc, kc, vc, hkv)]
        x, h2, rt = _outproj(mix_in, mix_w, x, m[2], norm2_g[layer], m[4], m[3], r_hi, r_lo, r_b)
        route_flat = rt.reshape(b * n, LANES)
        if with_ctx:
            xc, h2c, rtc = _outproj(mixc_in, mix_w, xc, mc[2], norm2_g[layer], mc[4], mc[3], r_hi, r_lo, r_b)
            route_flat = jnp.concatenate([route_flat, rtc.reshape(b * lc, LANES)], axis=0)
        n_tok = route_flat.shape[0]
        n_tiles = n_tok // TOKEN_TILE
        lat_tiles = b * n // TOKEN_TILE
        max_rows = n_tok * TOP_K + n_tiles * N_EXPERTS * (CHUNK_ROWS - 1)
        n_blocks = -(-max_rows // EXPERT_ROWS) + N_EXPERTS
        pos, cnt = _rank(route_flat)
        table, block_exp, n_used = _chunk_tables(cnt[:, 0, :N_EXPERTS].astype(jnp.int32), n_blocks)
        pos_t = jnp.swapaxes(pos[:, :8].reshape(n_tiles, TOKEN_TILE, 8), 1, 2)
        xs = jnp.zeros((n_blocks * EXPERT_ROWS, d // 2 + LANES), jnp.int32)
        xs = _dispatch(h2.reshape(b * n, d), pos_t, table, xs, 0)
        if with_ctx:
            xs = _dispatch(h2c.reshape(b * lc, d), pos_t, table, xs, lat_tiles)
        ys = _experts(xs, block_exp, n_used, moe_w_gate, moe_w_up, moe_w_down, layer)
        x = _combine(ys, table, pos, x, m[5], 0, final_g=None if with_ctx else final_norm_g)
        if with_ctx:
            xc = _combine(ys, table, pos, xc.reshape(b * lc // TOKEN_TILE, TOKEN_TILE, d), mc[5],
                          lat_tiles).reshape(b, lc, d)
    return x
```

```python
import functools
import math

import numpy as np
import jax
import jax.numpy as jnp
from jax import lax
from jax.experimental import pallas as pl
from jax.experimental.pallas import tpu as pltpu

F32 = jnp.float32
BF16 = jnp.bfloat16

HEAD_DIM = 64
GRID_W = 64
ROPE_BASE = 10000.0
NORM_EPS = 1e-6
N_MOD = 6
HY_ORDER = 2
HY_BANDS = 16
HY_DIRS = 2
HY_DECAY_TARGET = 1e-2
HY_FAST_DECAY = 0.3
HY_SLOW_DECAY = 1.5
HY_FILTER_EPS = 1e-6
SWA_WINDOW = 128
N_GROUPS = 4
EXP_PER_GROUP = 8
N_EXPERTS = N_GROUPS * EXP_PER_GROUP
TOP_K = 2
EXPERT_ROWS = 512
SWA_Q_ROWS = 256
LOOP_GROUP = 4
MID_KB = 8
FULL_ATTN_Q_ROWS = 256
FULL_ATTN_UNIT_ROWS = 512
FULL_ATTN_LOOKAHEAD = 1
TOKEN_TILE = 512
CHUNK_ROWS = 8
BUF_ROWS = 1280
BUF_CHUNKS = BUF_ROWS // CHUNK_ROWS
TABLE_WORDS = 256

LANES = 128
VMEM_LIMIT_BYTES = 56 * 1024 * 1024


def _params(*sem):
    return pltpu.CompilerParams(dimension_semantics=sem, vmem_limit_bytes=VMEM_LIMIT_BYTES)


def _rope_tables(n):
    d_axis = HEAD_DIM // 2
    t = jnp.arange(n)
    inv = ROPE_BASE ** (-jnp.arange(0, d_axis, 2, dtype=F32) / d_axis)
    ang_r = (t // GRID_W).astype(F32)[:, None] * inv[None, :]
    ang_c = (t % GRID_W).astype(F32)[:, None] * inv[None, :]
    cos = jnp.concatenate([jnp.cos(ang_r)] * 2 + [jnp.cos(ang_c)] * 2, axis=-1)
    sin = jnp.concatenate([-jnp.sin(ang_r), jnp.sin(ang_r), -jnp.sin(ang_c), jnp.sin(ang_c)], axis=-1)
    return jnp.tile(cos, (1, 2)), jnp.tile(sin, (1, 2))


def _head_mean_matrix():
    i = np.arange(LANES)
    return jnp.asarray((i[:, None] // HEAD_DIM == i[None, :] // HEAD_DIM) / HEAD_DIM, dtype=BF16)


def _proj_body(x_ref, g_ref, sc_ref, sh_ref, w_ref, cos_ref, sin_ref, ng_ref, bd_ref, *out_refs, segs):
    x = x_ref[0]
    h = x * lax.rsqrt(jnp.mean(x * x, axis=-1, keepdims=True) + NORM_EPS) * g_ref[...]
    hb = (h * (1.0 + sc_ref[0]) + sh_ref[0]).astype(BF16)
    for o_ref, (c0, width, kind, norm_row, rope, out_scale) in zip(out_refs, segs):
        seg = jnp.dot(hb, w_ref[:, c0:c0 + width], preferred_element_type=F32)
        if kind == "f32":
            o_ref[0] = seg
            continue
        if kind == "bf16":
            o_ref[0] = seg.astype(BF16)
            continue
        for j in range(width // LANES):
            ch = seg[:, j * LANES:(j + 1) * LANES]
            if norm_row is not None:
                sq = ch * ch
                hi = sq.astype(BF16)
                lo = (sq - hi.astype(F32)).astype(BF16)
                ms = (jnp.dot(hi, bd_ref[...], preferred_element_type=F32)
                      + jnp.dot(lo, bd_ref[...], preferred_element_type=F32))
                ch = ch * lax.rsqrt(ms + NORM_EPS) * ng_ref[norm_row:norm_row + 1, :]
            if rope:
                lane = lax.broadcasted_iota(jnp.int32, ch.shape, 1)
                partner = jnp.where(lane % 32 < 16, pltpu.roll(ch, LANES - 16, 1), pltpu.roll(ch, 16, 1))
                ch = ch * cos_ref[...] + partner * sin_ref[...]
            if out_scale != 1.0:
                ch = ch * out_scale
            o_ref[0, :, j * LANES:(j + 1) * LANES] = ch.astype(BF16)


def _proj(x, g, scale, shift, w, segs, rope_tabs=None, norm_g=None, tm=512):
    b, n, d = x.shape
    tm = min(tm, n)
    bm = scale.shape[0]
    mod_map = (lambda bi, i: (bi, 0, 0)) if bm > 1 else (lambda bi, i: (0, 0, 0))
    if rope_tabs is None:
        cos = sin = jnp.zeros((8, LANES), F32)
        tab_spec = pl.BlockSpec((8, LANES), lambda bi, i: (0, 0))
    else:
        cos, sin = rope_tabs
        tab_spec = pl.BlockSpec((tm, LANES), lambda bi, i: (i, 0))
    if norm_g is None:
        norm_g = jnp.ones((8, LANES), F32)
    out_shape = [jax.ShapeDtypeStruct((b, n, s[1]), F32 if s[2] == "f32" else BF16) for s in segs]
    out_specs = [pl.BlockSpec((1, tm, s[1]), lambda bi, i: (bi, i, 0)) for s in segs]
    return pl.pallas_call(
        functools.partial(_proj_body, segs=tuple(segs)),
        grid=(b, n // tm),
        in_specs=[
            pl.BlockSpec((1, tm, d), lambda bi, i: (bi, i, 0)),
            pl.BlockSpec((1, d), lambda bi, i: (0, 0)),
            pl.BlockSpec((1, 1, d), mod_map),
            pl.BlockSpec((1, 1, d), mod_map),
            pl.BlockSpec(w.shape, lambda bi, i: (0, 0)),
            tab_spec,
            tab_spec,
            pl.BlockSpec(norm_g.shape, lambda bi, i: (0, 0)),
            pl.BlockSpec((LANES, LANES), lambda bi, i: (0, 0)),
        ],
        out_specs=out_specs,
        out_shape=out_shape,
        compiler_params=_params("parallel", "parallel"),
        name="proj",
    )(x, g.reshape(1, d), scale, shift, w, cos, sin, norm_g, _head_mean_matrix())


def _stack_heads(q, j, g):
    return jnp.concatenate(
        [q[:, (j * g + gg) * HEAD_DIM:(j * g + gg + 1) * HEAD_DIM] for gg in range(g)], axis=0)


def _values_with_ones(v, hkv):
    b, nk, _ = v.shape
    ones = jnp.zeros((b, nk, hkv, HEAD_DIM), v.dtype).at[..., 0].set(1)
    return jnp.concatenate([v.reshape(b, nk, hkv, HEAD_DIM), ones], axis=-1).reshape(b, nk, 2 * hkv * HEAD_DIM)


def _sink_column(sink_ref, j, g, qb):
    return jnp.concatenate([jnp.full((qb, 1), sink_ref[j * g + gg], F32) for gg in range(g)], axis=0)


def _swa_body(sink_ref, q_ref, kt_ref, v_ref, kct_ref, vc_ref, o_ref, *, n, hkv, g, qb, win):
    kw = qb + 2 * win
    blk = pl.program_id(1)
    start = pl.multiple_of(blk * qb, LANES)
    rows = lax.broadcasted_iota(jnp.int32, (g * qb, kw), 0) % qb
    cols = lax.broadcasted_iota(jnp.int32, (g * qb, kw), 1)
    key_pos = cols + blk * qb - win
    valid = (jnp.abs(rows + win - cols) <= win) & (key_pos >= 0) & (key_pos < n)
    q = q_ref[0]
    for j in range(hkv):
        hs = slice(j * HEAD_DIM, (j + 1) * HEAD_DIM)
        q4 = _stack_heads(q, j, g)
        s_lat = jnp.dot(q4, kt_ref[0, hs, pl.ds(start, kw)], preferred_element_type=F32)
        s_lat = jnp.where(valid, s_lat, -jnp.inf)
        s_ctx = jnp.dot(q4, kct_ref[0, hs, :], preferred_element_type=F32)
        s_sink = _sink_column(sink_ref, j, g, qb)
        m = jnp.maximum(jnp.maximum(jnp.max(s_lat, axis=-1, keepdims=True),
                                    jnp.max(s_ctx, axis=-1, keepdims=True)), s_sink)
        e_lat = jnp.exp((s_lat - m).astype(BF16))
        e_ctx = jnp.exp((s_ctx - m).astype(BF16))
        vs = slice(j * 2 * HEAD_DIM, (j + 1) * 2 * HEAD_DIM)
        o = (jnp.dot(e_ctx, vc_ref[0, :, vs], preferred_element_type=F32)
             + jnp.dot(e_lat, v_ref[0, pl.ds(start, kw), vs], preferred_element_type=F32))
        o = o[:, :HEAD_DIM] / (o[:, HEAD_DIM:HEAD_DIM + 1] + jnp.exp(s_sink - m))
        for gg in range(g):
            c0 = (j * g + gg) * HEAD_DIM
            o_ref[0, :, c0:c0 + HEAD_DIM] = o[gg * qb:(gg + 1) * qb].astype(BF16)


def _windowed_attention(q, k, v, kc, vc, sink, hkv):
    b, n, dq = q.shape
    g = dq // HEAD_DIM // hkv
    qb = min(SWA_Q_ROWS, n)
    win = SWA_WINDOW
    lc = kc.shape[1]
    dkv = hkv * HEAD_DIM
    kt = jnp.swapaxes(jnp.pad(k, ((0, 0), (win, win), (0, 0))), 1, 2)
    vp = _values_with_ones(jnp.pad(v, ((0, 0), (win, win), (0, 0))), hkv)
    vc = _values_with_ones(vc, hkv)
    kct = jnp.swapaxes(kc, 1, 2)
    return pl.pallas_call(
        functools.partial(_swa_body, n=n, hkv=hkv, g=g, qb=qb, win=win),
        grid=(b, n // qb),
        in_specs=[
            pl.BlockSpec(memory_space=pltpu.SMEM),
            pl.BlockSpec((1, qb, dq), lambda bi, i: (bi, i, 0)),
            pl.BlockSpec((1, dkv, n + 2 * win), lambda bi, i: (bi, 0, 0)),
            pl.BlockSpec((1, n + 2 * win, 2 * dkv), lambda bi, i: (bi, 0, 0)),
            pl.BlockSpec((1, dkv, lc), lambda bi, i: (bi, 0, 0)),
            pl.BlockSpec((1, lc, 2 * dkv), lambda bi, i: (bi, 0, 0)),
        ],
        out_specs=pl.BlockSpec((1, qb, dq), lambda bi, i: (bi, i, 0)),
        out_shape=jax.ShapeDtypeStruct((b, n, dq), BF16),
        compiler_params=_params("parallel", "parallel"),
        name="swa",
    )(sink.astype(F32), q, kt, vp, kct, vc)


def _full_attn_body(sink_ref, q_ref, kt_ref, v_ref, o_ref, *, hkv, g, qb, has_sink, unit):
    q = q_ref[0]
    units = [(j, [j * g + u * unit + t for t in range(unit)]) for j in range(hkv) for u in range(g // unit)]

    def scores(j, heads):
        qu = jnp.concatenate([q[:, h * HEAD_DIM:(h + 1) * HEAD_DIM] for h in heads], axis=0)
        return jnp.dot(qu, kt_ref[0, j * HEAD_DIM:(j + 1) * HEAD_DIM, :], preferred_element_type=F32)

    pending = [scores(*u) for u in units[:FULL_ATTN_LOOKAHEAD]]
    for idx, (j, heads) in enumerate(units):
        s = pending.pop(0)
        if idx + FULL_ATTN_LOOKAHEAD < len(units):
            pending.append(scores(*units[idx + FULL_ATTN_LOOKAHEAD]))
        m = jnp.max(s, axis=-1, keepdims=True)
        if has_sink:
            s_sink = jnp.concatenate([jnp.full((qb, 1), sink_ref[h], F32) for h in heads], axis=0)
            m = jnp.maximum(m, s_sink)
        e = jnp.exp((s - m).astype(BF16))
        o = jnp.dot(e, v_ref[0, :, j * 2 * HEAD_DIM:(j + 1) * 2 * HEAD_DIM], preferred_element_type=F32)
        den = o[:, HEAD_DIM:HEAD_DIM + 1]
        if has_sink:
            den = den + jnp.exp(s_sink - m)
        o = o[:, :HEAD_DIM] / den
        for t, h in enumerate(heads):
            o_ref[0, :, h * HEAD_DIM:(h + 1) * HEAD_DIM] = o[t * qb:(t + 1) * qb].astype(BF16)


def _full_attention(q, k, v, hkv, sink=None):
    b, n, dq = q.shape
    g = dq // HEAD_DIM // hkv
    qb = min(FULL_ATTN_Q_ROWS, n)
    unit = max(1, FULL_ATTN_UNIT_ROWS // qb)
    nk = k.shape[1]
    dkv = hkv * HEAD_DIM
    kt = jnp.swapaxes(k, 1, 2)
    has_sink = sink is not None
    sink = jnp.zeros((dq // HEAD_DIM,), F32) if sink is None else sink.astype(F32)
    return pl.pallas_call(
        functools.partial(_full_attn_body, hkv=hkv, g=g, qb=qb, has_sink=has_sink, unit=min(unit, g)),
        grid=(b, n // qb),
        in_specs=[
            pl.BlockSpec(memory_space=pltpu.SMEM),
            pl.BlockSpec((1, qb, dq), lambda bi, i: (bi, i, 0)),
            pl.BlockSpec((1, dkv, nk), lambda bi, i: (bi, 0, 0)),
            pl.BlockSpec((1, nk, 2 * dkv), lambda bi, i: (bi, 0, 0)),
        ],
        out_specs=pl.BlockSpec((1, qb, dq), lambda bi, i: (bi, i, 0)),
        out_shape=jax.ShapeDtypeStruct((b, n, dq), BF16),
        compiler_params=_params("parallel", "parallel"),
        name="full_attn",
    )(sink, q, kt, _values_with_ones(v, hkv))


def _short_conv_body(u_ref, w_ref, b_ref, o_ref, *, h1, l2, c):
    slab = lambda f: u_ref[0, :, f, :]
    row = lax.broadcasted_iota(jnp.int32, (h1, c), 0)
    for f in range(l2):
        prev = slab(f - 1) if f > 0 else jnp.where(row == 0, 0.0, pltpu.roll(slab(l2 - 1), 1, 0))
        nxt = slab(f + 1) if f < l2 - 1 else jnp.where(row == h1 - 1, 0.0, pltpu.roll(slab(0), h1 - 1, 0))
        o_ref[0, 0, :, f * c:(f + 1) * c] = (prev * w_ref[0:1, :] + slab(f) * w_ref[1:2, :]
                                             + nxt * w_ref[2:3, :] + b_ref[...])


def _short_conv(u, w, bias, l2):
    b, n, c3 = u.shape
    c = c3 // 3
    h1 = n // l2
    return pl.pallas_call(
        functools.partial(_short_conv_body, h1=h1, l2=l2, c=c),
        grid=(b, 3),
        in_specs=[
            pl.BlockSpec((1, h1, l2, c), lambda bi, j: (bi, 0, 0, j)),
            pl.BlockSpec((3, c), lambda bi, j: (0, j)),
            pl.BlockSpec((1, c), lambda bi, j: (0, j)),
        ],
        out_specs=pl.BlockSpec((1, 1, h1, l2 * c), lambda bi, j: (j, bi, 0, 0)),
        out_shape=jax.ShapeDtypeStruct((3, b, h1, l2 * c), F32),
        compiler_params=_params("parallel", "parallel"),
        name="short_conv",
    )(u.reshape(b, h1, l2, c3), w, bias.reshape(1, c3))


def _filter_body(band_ref, w1_ref, b1_ref, w2_ref, b2_ref, w3_ref, fr_ref, dl_ref, o_ref, s_ref, *, n, rt, c):
    i = pl.program_id(0)
    hp = lax.Precision.HIGHEST
    m = i * rt + lax.broadcasted_iota(jnp.int32, (rt, 1), 0)
    pos = jnp.where(m < n, m, 2 * n - m).astype(F32)
    t_norm = pos / max(n - 1, 1)
    ang = (2.0 * math.pi / n) * pos * band_ref[...]
    lane = lax.broadcasted_iota(jnp.int32, (rt, LANES), 1)
    z = jnp.where(lane == 0, t_norm,
                  jnp.where(lane <= HY_BANDS, jnp.cos(ang),
                            jnp.where(lane <= 2 * HY_BANDS, -jnp.sin(ang), 0.0)))
    fr = fr_ref[...]
    hdn = jnp.sin(fr * (jnp.dot(z, w1_ref[...], precision=hp, preferred_element_type=F32) + b1_ref[...]))
    hdn = jnp.sin(fr * (jnp.dot(hdn, w2_ref[...], precision=hp, preferred_element_type=F32) + b2_ref[...]))
    h = jnp.dot(hdn, w3_ref[...], precision=hp, preferred_element_type=F32)
    h = h * jnp.exp(-t_norm * dl_ref[...])
    half = HY_ORDER * c
    sel = jnp.where(m < n, h[:, :half], jnp.where(m > n, -h[:, half:], 0.0))
    for o in range(HY_ORDER):
        o_ref[o] = sel[:, o * c:(o + 1) * c]

    @pl.when(i == 0)
    def _():
        s_ref[...] = jnp.zeros_like(s_ref)

    s_ref[...] += jnp.sum(jnp.abs(sel), axis=0, keepdims=True)


def _hyena_filters(n, w1, b1, w2, b2, w3, freq, c):
    rt = min(1024, n)
    hid = w1.shape[1]
    bands = jnp.linspace(1e-4, HY_BANDS - 1, HY_BANDS, dtype=F32)
    band_row = jnp.zeros((1, LANES), F32).at[0, 1:1 + 2 * HY_BANDS].set(jnp.tile(bands, 2))
    w1p = jnp.zeros((LANES, hid), F32).at[:w1.shape[0]].set(w1)
    max_decay = math.log(HY_DECAY_TARGET) / HY_FAST_DECAY
    min_decay = math.log(HY_DECAY_TARGET) / HY_SLOW_DECAY
    deltas = jnp.abs(jnp.linspace(min_decay, max_decay, c, dtype=F32))
    dl = jnp.tile(deltas, HY_DIRS * HY_ORDER).reshape(1, -1)
    full = lambda a: pl.BlockSpec(a.shape, lambda i: (0,) * a.ndim)
    args = (band_row, w1p, b1.reshape(1, hid), w2, b2.reshape(1, hid), w3, freq.reshape(1, hid), dl)
    return pl.pallas_call(
        functools.partial(_filter_body, n=n, rt=rt, c=c),
        grid=(2 * n // rt,),
        in_specs=[full(a) for a in args],
        out_specs=[pl.BlockSpec((HY_ORDER, rt, c), lambda i: (0, i, 0)),
                   pl.BlockSpec((1, HY_ORDER * c), lambda i: (0, 0))],
        out_shape=[jax.ShapeDtypeStruct((HY_ORDER, 2 * n, c), F32),
                   jax.ShapeDtypeStruct((1, HY_ORDER * c), F32)],
        compiler_params=_params("arbitrary"),
        name="hyena_filter",
    )(*args)


def _dft_split(n):
    l2 = 32 if n >= 2048 else 16
    return 2 * n // l2, l2


def _dft_constants(n):
    l1, l2 = _dft_split(n)
    h1 = l1 // 2
    nn = 2 * n
    k1 = np.arange(h1)[:, None]
    a = 2 * np.pi * (k1 + 0.5) * np.arange(l1)[None, :] / l1
    w1 = np.concatenate([np.cos(a), -np.sin(a)], axis=0)
    t = 2 * np.pi * (k1 + 0.5) * np.arange(l2)[None, :] / nn
    tw_cos, tw_sin = np.cos(t), np.sin(t)
    p = 2 * np.pi * np.arange(l2)[:, None] * np.arange(l2)[None, :] / l2
    eye = np.eye(MID_KB)
    cos_k, sin_k = np.kron(np.cos(p), eye), np.kron(np.sin(p), eye)
    w2 = np.block([[cos_k, sin_k], [-sin_k, cos_k]])
    w2i = np.block([[cos_k, -sin_k], [sin_k, cos_k]])
    tw_rows = lambda m: m.reshape(h1 // MID_KB, MID_KB, l2).transpose(0, 2, 1).reshape(h1 // MID_KB, l2 * MID_KB, 1)
    ai = 2 * np.pi * np.arange(h1)[:, None] * (np.arange(h1)[None, :] + 0.5) / l1
    w1i = (2.0 / nn) * np.concatenate([np.cos(ai), -np.sin(ai)], axis=1)
    c = lambda m, dt: jnp.asarray(m, dtype=dt)
    return dict(
        l1=l1, l2=l2, h1=h1,
        w1=c(w1, BF16), w2=c(w2, BF16), w2i=c(w2i, BF16), w1i=c(w1i, BF16),
        tw_cos_fwd=c(tw_cos.T[:, :, None], F32), tw_sin_fwd=c(tw_sin.T[:, :, None], F32),
        tw_cos_inv=c(tw_rows(tw_cos), F32), tw_sin_inv=c(tw_rows(tw_sin), F32),
    )


def _ct_fwd1_body(x_ref, w_ref, tc_ref, ts_ref, nrm_ref, o_ref, *, lb, c, h1, normalise):
    for q in range(lb):
        x = x_ref[0, 0, :, q * c:(q + 1) * c]
        if normalise:
            x = x / (nrm_ref[0] + HY_FILTER_EPS)
        a = jnp.dot(w_ref[...], x.astype(BF16), preferred_element_type=F32)
        ar, ai = a[:h1], a[h1:]
        tc, ts = tc_ref[q], ts_ref[q]
        o_ref[0, 0, q] = ar * tc + ai * ts
        o_ref[0, 1, q] = ai * tc - ar * ts


def _ct_fwd1(xs, idx, consts, c, norms=None, lb=8):
    _, b, k1n, _ = xs.shape
    l2, h1 = consts["l2"], consts["h1"]
    lb = min(lb, l2)
    w = consts["w1"][:, :k1n]
    normalise = norms is not None
    if norms is None:
        norms = jnp.zeros((b, 1, c), F32)
    return pl.pallas_call(
        functools.partial(_ct_fwd1_body, lb=lb, c=c, h1=h1, normalise=normalise),
        grid=(b, l2 // lb),
        in_specs=[
            pl.BlockSpec((1, 1, k1n, lb * c), lambda bi, i: (idx, bi, 0, i)),
            pl.BlockSpec(w.shape, lambda bi, i: (0, 0)),
            pl.BlockSpec((lb, h1, 1), lambda bi, i: (i, 0, 0)),
            pl.BlockSpec((lb, h1, 1), lambda bi, i: (i, 0, 0)),
            pl.BlockSpec((1, 1, c), lambda bi, i: (bi, 0, 0)),
        ],
        out_specs=pl.BlockSpec((1, 2, lb, h1, c), lambda bi, i: (bi, 0, i, 0, 0)),
        out_shape=jax.ShapeDtypeStruct((b, 2, l2, h1, c), F32),
        compiler_params=_params("parallel", "parallel"),
        name="ct_fwd1",
    )(xs, w, consts["tw_cos_fwd"], consts["tw_sin_fwd"], norms)


def _stacked(ref, lead, rows, c):
    return jnp.concatenate([ref[lead + (0,)].reshape(rows, c), ref[lead + (1,)].reshape(rows, c)], axis=0)


def _ct_spec_body(a_ref, w2_ref, o_ref, *, l2, c):
    rows = l2 * MID_KB
    x = jnp.dot(w2_ref[...], _stacked(a_ref, (0,), rows, c).astype(BF16), preferred_element_type=F32)
    o_ref[0, 0] = x[:rows].reshape(l2, MID_KB, c)
    o_ref[0, 1] = x[rows:].reshape(l2, MID_KB, c)


def _ct_spectrum(a, consts, c):
    b = a.shape[0]
    l2, h1 = consts["l2"], consts["h1"]
    blk = pl.BlockSpec((1, 2, l2, MID_KB, c), lambda bi, i: (bi, 0, 0, i, 0))
    return pl.pallas_call(
        functools.partial(_ct_spec_body, l2=l2, c=c),
        grid=(b, h1 // MID_KB),
        in_specs=[blk, pl.BlockSpec(consts["w2"].shape, lambda bi, i: (0, 0))],
        out_specs=blk,
        out_shape=jax.ShapeDtypeStruct((b, 2, l2, h1, c), F32),
        compiler_params=_params("parallel", "parallel"),
        name="ct_spectrum",
    )(a, consts["w2"])


def _ct_mid_body(a_ref, h_ref, w2_ref, w2i_ref, tc_ref, ts_ref, o_ref, *, l2, c):
    rows = l2 * MID_KB
    x = jnp.dot(w2_ref[...], _stacked(a_ref, (0,), rows, c).astype(BF16), preferred_element_type=F32)
    xr, xi = x[:rows], x[rows:]
    hr, hi = h_ref[0, 0].reshape(rows, c), h_ref[0, 1].reshape(rows, c)
    y = jnp.concatenate([xr * hr - xi * hi, xr * hi + xi * hr], axis=0).astype(BF16)
    bm = jnp.dot(w2i_ref[...], y, preferred_element_type=F32)
    br, bi = bm[:rows], bm[rows:]
    tc, ts = tc_ref[0], ts_ref[0]
    o_ref[0, 0] = (br * tc - bi * ts).reshape(l2, MID_KB, c)
    o_ref[0, 1] = (br * ts + bi * tc).reshape(l2, MID_KB, c)


def _ct_mid(a, hspec, order, consts, c):
    b = a.shape[0]
    l2, h1 = consts["l2"], consts["h1"]
    blk = pl.BlockSpec((1, 2, l2, MID_KB, c), lambda i, bi: (bi, 0, 0, i, 0))
    tw = pl.BlockSpec((1, l2 * MID_KB, 1), lambda i, bi: (i, 0, 0))
    return pl.pallas_call(
        functools.partial(_ct_mid_body, l2=l2, c=c),
        grid=(h1 // MID_KB, b),
        in_specs=[
            blk,
            pl.BlockSpec((1, 2, l2, MID_KB, c), lambda i, bi: (order, 0, 0, i, 0)),
            pl.BlockSpec(consts["w2"].shape, lambda i, bi: (0, 0)),
            pl.BlockSpec(consts["w2i"].shape, lambda i, bi: (0, 0)),
            tw,
            tw,
        ],
        out_specs=blk,
        out_shape=jax.ShapeDtypeStruct((b, 2, l2, h1, c), F32),
        compiler_params=_params("parallel", "parallel"),
        name="ct_mid",
    )(a, hspec, consts["w2"], consts["w2i"], consts["tw_cos_inv"], consts["tw_sin_inv"])


def _ct_inv1_body(b_ref, w_ref, u_ref, gate_ref, skip_ref, o_ref, *, lb, c, by_position):
    for q in range(lb):
        bb = jnp.concatenate([b_ref[0, 0, q], b_ref[0, 1, q]], axis=0).astype(BF16)
        y = jnp.dot(w_ref[...], bb, preferred_element_type=F32)
        cs = slice(q * c, (q + 1) * c)
        out = gate_ref[0, 0, :, cs] * (y + u_ref[0, 0, :, cs] * skip_ref[...])
        if by_position:
            o_ref[0, :, q, :] = out
        else:
            o_ref[0, :, cs] = out


def _ct_inv1(bsp, u, u_idx, gate, gate_idx, skip, consts, c, by_position, lb=8):
    b = bsp.shape[0]
    l2, h1 = consts["l2"], consts["h1"]
    lb = min(lb, l2)
    if by_position:
        out_spec = pl.BlockSpec((1, h1, lb, c), lambda bi, i: (bi, 0, i, 0))
        out_shape = jax.ShapeDtypeStruct((b, h1, l2, c), F32)
    else:
        out_spec = pl.BlockSpec((1, h1, lb * c), lambda bi, i: (bi, 0, i))
        out_shape = jax.ShapeDtypeStruct((b, h1, l2 * c), F32)
    return pl.pallas_call(
        functools.partial(_ct_inv1_body, lb=lb, c=c, by_position=by_position),
        grid=(b, l2 // lb),
        in_specs=[
            pl.BlockSpec((1, 2, lb, h1, c), lambda bi, i: (bi, 0, i, 0, 0)),
            pl.BlockSpec((h1, 2 * h1), lambda bi, i: (0, 0)),
            pl.BlockSpec((1, 1, h1, lb * c), lambda bi, i: (u_idx, bi, 0, i)),
            pl.BlockSpec((1, 1, h1, lb * c), lambda bi, i: (gate_idx, bi, 0, i)),
            pl.BlockSpec((1, c), lambda bi, i: (0, 0)),
        ],
        out_specs=out_spec,
        out_shape=out_shape,
        compiler_params=_params("parallel", "parallel"),
        name="ct_inv1",
    )(bsp, consts["w1i"], u, gate, skip.reshape(1, c))


def _hyena(u, conv_w, conv_b, f_w1, f_b1, f_w2, f_b2, f_w3, f_freq, skip):
    b, n, c3 = u.shape
    c = c3 // 3
    consts = _dft_constants(n)
    l1, l2, h1 = consts["l1"], consts["l2"], consts["h1"]
    filt, norms = _hyena_filters(n, f_w1, f_b1, f_w2, f_b2, f_w3, f_freq, c)
    fa = _ct_fwd1(filt.reshape(1, HY_ORDER, l1, l2 * c), 0, consts, c, norms=norms.reshape(HY_ORDER, 1, c))
    hspec = _ct_spectrum(fa, consts, c)
    parts = _short_conv(u, conv_w, conv_b, l2)

    def long_conv_gated(x_stack, x_idx, gate_idx, order, by_position):
        a = _ct_fwd1(x_stack, x_idx, consts, c)
        bsp = _ct_mid(a, hspec, order, consts, c)
        return _ct_inv1(bsp, x_stack, x_idx, parts, gate_idx, skip[order], consts, c, by_position)

    z = long_conv_gated(parts, 0, 1, 0, False)
    return long_conv_gated(z[None], 0, 2, 1, True).reshape(b, n, c)


def _outproj_body(*refs, n_in):
    ins = refs[:n_in]
    ws = refs[n_in:2 * n_in]
    x_ref, gate_ref, g_ref, sc_ref, sh_ref, rhi_ref, rlo_ref, rb_ref, xo_ref, h_ref, rt_ref = refs[2 * n_in:]
    y = None
    for a_ref, w_ref in zip(ins, ws):
        t = jnp.dot(a_ref[0].astype(BF16), w_ref[...], preferred_element_type=F32)
        y = t if y is None else y + t
    x = x_ref[0] + gate_ref[0] * y
    xo_ref[0] = x
    h = x * lax.rsqrt(jnp.mean(x * x, axis=-1, keepdims=True) + NORM_EPS) * g_ref[...]
    h = h * (1.0 + sc_ref[0]) + sh_ref[0]
    hi = h.astype(BF16)
    h_ref[0] = hi
    lo = (h - hi.astype(F32)).astype(BF16)
    lg = (jnp.dot(hi, rhi_ref[...], preferred_element_type=F32)
          + jnp.dot(lo, rhi_ref[...], preferred_element_type=F32)
          + jnp.dot(hi, rlo_ref[...], preferred_element_type=F32) + rb_ref[...])
    rt_ref[0] = _route(lg)


def _route(lg):
    lane = lax.broadcasted_iota(jnp.int32, lg.shape, 1)
    lane_f = lane.astype(F32)
    neg = -jnp.inf

    def top(v):
        m = jnp.max(v, axis=-1, keepdims=True)
        return m, jnp.min(jnp.where(v == m, lane_f, float(LANES)), axis=-1, keepdims=True)

    gl = jnp.where(lane < N_GROUPS, lg, neg)
    gmax, grp = top(gl)
    p_grp = 1.0 / jnp.sum(jnp.exp(gl - gmax), axis=-1, keepdims=True)
    first = N_GROUPS + grp * EXP_PER_GROUP
    el = jnp.where((lane_f >= first) & (lane_f < first + EXP_PER_GROUP), lg, neg)
    m1, i1 = top(el)
    m2, i2 = top(jnp.where(lane_f == i1, neg, el))
    e2 = jnp.exp(m2 - m1)
    den = 1.0 + e2
    vals = (i1 - N_GROUPS, i2 - N_GROUPS, p_grp * (1.0 / den), p_grp * (e2 / den))
    out = jnp.zeros(lg.shape, F32)
    for k, v in enumerate(vals):
        out = jnp.where(lane == k, v, out)
    return out


def _outproj(ins, ws, x, gate, g, scale, shift, r_hi, r_lo, r_b, tm=512):
    b, n, d = x.shape
    tm = min(tm, n)
    bm = gate.shape[0]
    mod_map = (lambda bi, i: (bi, 0, 0)) if bm > 1 else (lambda bi, i: (0, 0, 0))
    row = lambda wd: pl.BlockSpec((1, tm, wd), lambda bi, i: (bi, i, 0))
    full = lambda a: pl.BlockSpec(a.shape, lambda bi, i: (0,) * a.ndim)
    mod = pl.BlockSpec((1, 1, d), mod_map)
    return pl.pallas_call(
        functools.partial(_outproj_body, n_in=len(ins)),
        grid=(b, n // tm),
        in_specs=([row(a.shape[-1]) for a in ins] + [full(w) for w in ws]
                  + [row(d), mod, pl.BlockSpec((1, d), lambda bi, i: (0, 0)), mod, mod,
                     full(r_hi), full(r_lo), full(r_b)]),
        out_specs=[row(d), row(d), row(LANES)],
        out_shape=[jax.ShapeDtypeStruct((b, n, d), F32), jax.ShapeDtypeStruct((b, n, d), BF16),
                   jax.ShapeDtypeStruct((b, n, LANES), F32)],
        compiler_params=_params("parallel", "parallel"),
        name="outproj",
    )(*ins, *ws, x, gate, g.reshape(1, d), scale, shift, r_hi, r_lo, r_b)


def _rank_body(rt_ref, tri_ref, upper_ref, pos_ref, cnt_ref):
    rt = rt_ref[...]
    lane_i = lax.broadcasted_iota(jnp.int32, rt.shape, 1)
    lane = lane_i.astype(F32)
    oh_a = lane == rt[:, 0:1]
    oh_b = lane == rt[:, 1:2]
    one_a = jnp.where(oh_a, 1.0, 0.0)
    one_b = jnp.where(oh_b, 1.0, 0.0)
    before_a = jnp.dot(tri_ref[...], one_a.astype(BF16), preferred_element_type=F32)
    before_b = jnp.dot(tri_ref[...], one_b.astype(BF16), preferred_element_type=F32)
    tot_a = jnp.sum(one_a, axis=0, keepdims=True)
    cnt = tot_a + jnp.sum(one_b, axis=0, keepdims=True)
    padded = jnp.floor((cnt + (CHUNK_ROWS - 1)) * (1.0 / CHUNK_ROWS)) * CHUNK_ROWS
    first = jnp.dot(jnp.broadcast_to(padded, (8, LANES)).astype(BF16), upper_ref[...],
                    preferred_element_type=F32)[0:1]
    pos_a = jnp.sum(jnp.where(oh_a, before_a + first, 0.0), axis=-1, keepdims=True)
    pos_b = jnp.sum(jnp.where(oh_b, before_b + first + tot_a, 0.0), axis=-1, keepdims=True)
    is_gate = (lane_i >= TOP_K) & (lane_i < 2 * TOP_K)
    pos_ref[...] = jnp.where(lane_i == 0, pos_a, jnp.where(lane_i == 1, pos_b, jnp.where(is_gate, rt, 0.0)))
    cnt_ref[0] = jnp.broadcast_to(cnt, (8, LANES))


def _rank(route):
    t = route.shape[0]
    tm = TOKEN_TILE
    tri = jnp.asarray(np.tril(np.ones((tm, tm)), -1), dtype=BF16)
    upper = jnp.asarray(np.triu(np.ones((LANES, LANES)), 1), dtype=BF16)
    return pl.pallas_call(
        _rank_body,
        grid=(t // tm,),
        in_specs=[pl.BlockSpec((tm, LANES), lambda i: (i, 0)), pl.BlockSpec((tm, tm), lambda i: (0, 0)),
                  pl.BlockSpec((LANES, LANES), lambda i: (0, 0))],
        out_specs=[pl.BlockSpec((tm, LANES), lambda i: (i, 0)), pl.BlockSpec((1, 8, LANES), lambda i: (i, 0, 0))],
        out_shape=[jax.ShapeDtypeStruct((t, LANES), F32), jax.ShapeDtypeStruct((t // tm, 8, LANES), F32)],
        compiler_params=_params("parallel"),
        name="moe_rank",
    )(route, tri, upper)


def _chunk_tables(cnt, n_blocks):
    padded = (cnt + CHUNK_ROWS - 1) // CHUNK_ROWS * CHUNK_ROWS
    run_end = jnp.cumsum(padded, axis=1)
    run_start = run_end - padded
    seg_rows = jnp.sum(padded, axis=0)
    seg_rows = (seg_rows + EXPERT_ROWS - 1) // EXPERT_ROWS * EXPERT_ROWS
    seg_end = jnp.cumsum(seg_rows)
    dst_start = (seg_end - seg_rows)[None, :] + jnp.cumsum(padded, axis=0) - padded
    row0 = jnp.arange(BUF_CHUNKS, dtype=jnp.int32) * CHUNK_ROWS
    chunk_exp = jnp.minimum(jnp.sum(run_end[:, None, :] <= row0[None, :, None], axis=-1), N_EXPERTS - 1)
    onehot = chunk_exp[:, :, None] == jnp.arange(N_EXPERTS, dtype=jnp.int32)[None, None, :]
    dst = jnp.sum(jnp.where(onehot, (dst_start - run_start)[:, None, :], 0), axis=-1) + row0[None, :]
    n_chunks = run_end[:, -1:] // CHUNK_ROWS
    table = jnp.concatenate(
        [dst, n_chunks, jnp.zeros((cnt.shape[0], TABLE_WORDS - BUF_CHUNKS - 1), jnp.int32)], axis=1)
    block_row0 = jnp.arange(n_blocks, dtype=jnp.int32) * EXPERT_ROWS
    block_exp = jnp.minimum(jnp.sum(seg_end[None, :] <= block_row0[:, None], axis=1), N_EXPERTS - 1)
    n_used = (seg_end[-1] // EXPERT_ROWS).reshape(1)
    return table.astype(jnp.int32), block_exp.astype(jnp.int32), n_used.astype(jnp.int32)


def _pack_halves(x):
    w = x.shape[-1] // 2
    lo = lax.bitcast_convert_type(x[:, :w].astype(BF16).astype(F32), jnp.int32)
    hi = lax.bitcast_convert_type(x[:, w:].astype(BF16).astype(F32), jnp.int32)
    return lax.shift_right_logical(lo, jnp.int32(16)) | (hi & jnp.int32(-65536))


def _unpack_halves(p):
    lo = lax.bitcast_convert_type(lax.shift_left(p, jnp.int32(16)), F32)
    hi = lax.bitcast_convert_type(p & jnp.int32(-65536), F32)
    return jnp.concatenate([lo, hi], axis=-1).astype(BF16)


def _start_chunks(tab_ref, tile, make_copy):
    base = tile * TABLE_WORDS

    def issue(c):
        make_copy(pl.multiple_of(c * CHUNK_ROWS, CHUNK_ROWS),
                  pl.multiple_of(tab_ref[base + c], CHUNK_ROWS)).start()

    _for_each_chunk(tab_ref[base + BUF_CHUNKS], issue)


def _wait_chunks(tab_ref, tile, make_copy):
    _for_each_chunk(tab_ref[tile * TABLE_WORDS + BUF_CHUNKS], lambda c: make_copy(0, 0).wait())


def _for_each_chunk(n, fn):
    groups = n // LOOP_GROUP

    def grouped(i, carry):
        for k in range(LOOP_GROUP):
            fn(i * LOOP_GROUP + k)
        return carry

    def single(c, carry):
        fn(c)
        return carry

    lax.fori_loop(0, groups, grouped, 0)
    lax.fori_loop(groups * LOOP_GROUP, n, single, 0)


def _dispatch_body(tab_ref, h_ref, pos_ref, xs_in, xs_out, buf, sem_rows, *, tile0, n_steps):
    del xs_in
    step = pl.program_id(0)
    dp = h_ref.shape[1] // 2

    def copies(at_step):
        slot = at_step % 2
        return lambda src, dst: pltpu.make_async_copy(
            buf.at[slot, pl.ds(src, CHUNK_ROWS)], xs_out.at[pl.ds(dst, CHUNK_ROWS)], sem_rows.at[slot])

    @pl.when(step >= 2)
    def _():
        _wait_chunks(tab_ref, tile0 + step - 2, copies(step - 2))

    tm = h_ref.shape[0]
    row = lax.broadcasted_iota(jnp.int32, (BUF_ROWS, tm), 0).astype(F32)
    oh_a = row == pos_ref[0, 0:1, :]
    oh_b = row == pos_ref[0, 1:2, :]
    buf[step % 2, :, :dp] = _pack_halves(jnp.dot(jnp.where(oh_a | oh_b, 1.0, 0.0).astype(BF16), h_ref[...],
                                                 preferred_element_type=F32))
    gate = jnp.sum(jnp.where(oh_a, pos_ref[0, 2:3, :], 0.0) + jnp.where(oh_b, pos_ref[0, 3:4, :], 0.0),
                   axis=-1, keepdims=True)
    buf[step % 2, :, dp:] = lax.bitcast_convert_type(jnp.broadcast_to(gate, (BUF_ROWS, LANES)), jnp.int32)
    _start_chunks(tab_ref, tile0 + step, copies(step))

    @pl.when(step == n_steps - 1)
    def _():
        @pl.when(step >= 1)
        def _():
            _wait_chunks(tab_ref, tile0 + step - 1, copies(step - 1))

        _wait_chunks(tab_ref, tile0 + step, copies(step))


def _dispatch(h, pos_t, table, xs, tile0):
    t, d = h.shape
    tm = TOKEN_TILE
    n_steps = t // tm
    grid_spec = pltpu.PrefetchScalarGridSpec(
        num_scalar_prefetch=1,
        grid=(n_steps,),
        in_specs=[
            pl.BlockSpec((tm, d), lambda i, tab: (i, 0)),
            pl.BlockSpec((1, 8, tm), lambda i, tab: (tile0 + i, 0, 0)),
            pl.BlockSpec(memory_space=pl.ANY),
        ],
        out_specs=pl.BlockSpec(memory_space=pl.ANY),
        scratch_shapes=[
            pltpu.VMEM((2, BUF_ROWS, d // 2 + LANES), jnp.int32),
            pltpu.SemaphoreType.DMA((2,)),
        ],
    )
    return pl.pallas_call(
        functools.partial(_dispatch_body, tile0=tile0, n_steps=n_steps),
        grid_spec=grid_spec,
        out_shape=jax.ShapeDtypeStruct(xs.shape, xs.dtype),
        input_output_aliases={3: 0},
        compiler_params=_params("arbitrary"),
        name="moe_dispatch",
    )(table.reshape(-1), h, pos_t, xs)


def _expert_body(bexp_ref, nused_ref, x_ref, wg_ref, wu_ref, wd_ref, o_ref, wg_bf, wu_bf, wd_bf):
    i = pl.program_id(0)

    @pl.when((i == 0) | (bexp_ref[i] != bexp_ref[jnp.maximum(i - 1, 0)]))
    def _():
        wg_bf[...] = wg_ref[0, 0].astype(BF16)
        wu_bf[...] = wu_ref[0, 0].astype(BF16)
        wd_bf[...] = wd_ref[0, 0].astype(BF16)

    @pl.when(i < nused_ref[0])
    def _():
        dp = o_ref.shape[1]
        xb = _unpack_halves(x_ref[:, :dp])
        gate = lax.bitcast_convert_type(x_ref[:, dp:dp + 1], F32)
        gt = jnp.dot(xb, wg_bf[...], preferred_element_type=F32)
        up = jnp.dot(xb, wu_bf[...], preferred_element_type=F32)
        hid = (gt * jax.nn.sigmoid(gt) * up).astype(BF16)
        o_ref[...] = _pack_halves(jnp.dot(hid, wd_bf[...], preferred_element_type=F32) * gate)

    @pl.when(i >= nused_ref[0])
    def _():
        o_ref[...] = jnp.zeros_like(o_ref)


def _experts(xs, block_exp, n_used, w_gate, w_up, w_down, layer):
    rows, width = xs.shape
    d, de = w_gate.shape[2:]
    used = lambda i, be, nu: (jnp.minimum(i, nu[0] - 1), 0)
    expert = lambda i, be, nu: (layer, be[i], 0, 0)
    grid_spec = pltpu.PrefetchScalarGridSpec(
        num_scalar_prefetch=2,
        grid=(rows // EXPERT_ROWS,),
        in_specs=[
            pl.BlockSpec((EXPERT_ROWS, width), used),
            pl.BlockSpec((1, 1, d, de), expert),
            pl.BlockSpec((1, 1, d, de), expert),
            pl.BlockSpec((1, 1, de, d), expert),
        ],
        out_specs=pl.BlockSpec((EXPERT_ROWS, d // 2), lambda i, be, nu: (i, 0)),
        scratch_shapes=[pltpu.VMEM((d, de), BF16), pltpu.VMEM((d, de), BF16), pltpu.VMEM((de, d), BF16)],
    )
    return pl.pallas_call(
        _expert_body,
        grid_spec=grid_spec,
        out_shape=jax.ShapeDtypeStruct((rows, d // 2), jnp.int32),
        compiler_params=_params("arbitrary"),
        name="experts",
    )(block_exp, n_used, xs, w_gate, w_up, w_down)


def _combine_body(tab_ref, ys_hbm, x_ref, gate_ref, pos_ref, fg_ref, o_ref, ybuf, sem_rows,
                  *, tile0, n_tiles, n_steps, final_norm):
    step = pl.program_id(0) * n_tiles + pl.program_id(1)

    def copies(at_step):
        slot = at_step % 2
        return lambda dst, src: pltpu.make_async_copy(
            ys_hbm.at[pl.ds(src, CHUNK_ROWS)], ybuf.at[slot, pl.ds(dst, CHUNK_ROWS)], sem_rows.at[slot])

    @pl.when(step == 0)
    def _():
        ybuf[...] = jnp.zeros_like(ybuf)
        _start_chunks(tab_ref, tile0, copies(0))

    @pl.when(step + 1 < n_steps)
    def _():
        _start_chunks(tab_ref, tile0 + step + 1, copies(step + 1))

    _wait_chunks(tab_ref, tile0 + step, copies(step))
    tm = x_ref.shape[1]
    col = lax.broadcasted_iota(jnp.int32, (tm, BUF_ROWS), 1).astype(F32)
    pick = jnp.where((col == pos_ref[:, 0:1]) | (col == pos_ref[:, 1:2]), 1.0, 0.0).astype(BF16)
    out = x_ref[0] + gate_ref[0] * jnp.dot(pick, _unpack_halves(ybuf[step % 2]), preferred_element_type=F32)
    if final_norm:
        out = out * lax.rsqrt(jnp.mean(out * out, axis=-1, keepdims=True) + NORM_EPS) * fg_ref[...]
    o_ref[0] = out


def _combine(ys, table, pos, x, gate, tile0, final_g=None):
    b, n, d = x.shape
    tm = min(TOKEN_TILE, n)
    n_tiles = n // tm
    bm = gate.shape[0]
    mod_map = (lambda bi, i, tab: (bi, 0, 0)) if bm > 1 else (lambda bi, i, tab: (0, 0, 0))
    final_norm = final_g is not None
    fg = final_g.reshape(1, d) if final_norm else jnp.ones((1, d), F32)
    grid_spec = pltpu.PrefetchScalarGridSpec(
        num_scalar_prefetch=1,
        grid=(b, n_tiles),
        in_specs=[
            pl.BlockSpec(memory_space=pl.ANY),
            pl.BlockSpec((1, tm, d), lambda bi, i, tab: (bi, i, 0)),
            pl.BlockSpec((1, 1, d), mod_map),
            pl.BlockSpec((tm, LANES), lambda bi, i, tab: (tile0 + bi * n_tiles + i, 0)),
            pl.BlockSpec((1, d), lambda bi, i, tab: (0, 0)),
        ],
        out_specs=pl.BlockSpec((1, tm, d), lambda bi, i, tab: (bi, i, 0)),
        scratch_shapes=[
            pltpu.VMEM((2, BUF_ROWS, d // 2), jnp.int32),
            pltpu.SemaphoreType.DMA((2,)),
        ],
    )
    return pl.pallas_call(
        functools.partial(_combine_body, tile0=tile0, n_tiles=n_tiles, n_steps=b * n_tiles,
                          final_norm=final_norm),
        grid_spec=grid_spec,
        out_shape=jax.ShapeDtypeStruct((b, n, d), F32),
        compiler_params=_params("arbitrary", "arbitrary"),
        name="moe_combine",
    )(table.reshape(-1), ys, x, gate, pos, fg)


def _router_weights(wg, bg, we, be):
    d = wg.shape[0]
    w = jnp.zeros((d, LANES), F32).at[:, :N_GROUPS].set(wg).at[:, N_GROUPS:N_GROUPS + N_EXPERTS].set(we)
    bias = jnp.zeros((1, LANES), F32).at[0, :N_GROUPS].set(bg).at[0, N_GROUPS:N_GROUPS + N_EXPERTS].set(be)
    hi = w.astype(BF16)
    lo = (w - hi.astype(F32)).astype(BF16)
    return hi, lo, bias


def kernel(x, c, ctx, c_ctx, ada_w, ada_b, norm1_g, norm2_g, ev_w_in, ev_w_out, hy_conv_w, hy_conv_b, hy_f_w1, hy_f_b1, hy_f_w2, hy_f_b2, hy_f_w3, hy_f_freq, hy_skip, swa_sink, od_w_qkv, od_w_out, od_q_norm_g, od_k_norm_g, rt_group_w, rt_group_b, rt_exp_w, rt_exp_b, moe_w_gate, moe_w_up, moe_w_down, final_norm_g):
    b, n, d = x.shape
    lc = ctx.shape[1]
    depth = ada_w.shape[0]
    rope = _rope_tables(n)
    xc = ctx
    sc = jax.nn.silu(c)
    scc = jax.nn.silu(c_ctx)
    q_scale = HEAD_DIM ** -0.5
    for layer in range(depth):
        with_ctx = layer < depth - 1
        mod = (sc @ ada_w[layer] + ada_b[layer]).reshape(b, N_MOD, 1, d)
        modc = (scc @ ada_w[layer] + ada_b[layer]).reshape(1, N_MOD, 1, d)
        m = [mod[:, k] for k in range(N_MOD)]
        mc = [modc[:, k] for k in range(N_MOD)]
        r_hi, r_lo, r_b = _router_weights(rt_group_w[layer], rt_group_b[layer], rt_exp_w[layer], rt_exp_b[layer])
        if layer % 2 == 0:
            e = layer // 2
            c_hy = hy_conv_w.shape[-1] // 3
            d_hy = 3 * c_hy
            hq = swa_sink.shape[-1]
            d_q = hq * HEAD_DIM
            hkv = hq // 4
            d_kv = hkv * HEAD_DIM
            w_in = ev_w_in[e].astype(BF16)
            w_out = ev_w_out[e].astype(BF16)
            hy_args = (hy_conv_w[e], hy_conv_b[e], hy_f_w1[e], hy_f_b1[e], hy_f_w2[e], hy_f_b2[e],
                       hy_f_w3[e], hy_f_freq[e], hy_skip[e])
            u, q, k, v = _proj(x, norm1_g[layer], m[1], m[0], w_in, [
                (0, d_hy, "f32", None, False, 1.0),
                (d_hy, d_q, "qk", None, True, q_scale),
                (d_hy + d_q, d_kv, "qk", None, True, 1.0),
                (d_hy + d_q + d_kv, d_kv, "bf16", None, False, 1.0)], rope_tabs=rope)
            if with_ctx:
                uc, qc, kc, vc = _proj(xc, norm1_g[layer], mc[1], mc[0], w_in, [
                    (0, d_hy, "f32", None, False, 1.0),
                    (d_hy, d_q, "qk", None, False, q_scale),
                    (d_hy + d_q, d_kv, "bf16", None, False, 1.0),
                    (d_hy + d_q + d_kv, d_kv, "bf16", None, False, 1.0)])
            else:
                kc, vc = _proj(xc, norm1_g[layer], mc[1], mc[0], w_in, [
                    (d_hy + d_q, d_kv, "bf16", None, False, 1.0),
                    (d_hy + d_q + d_kv, d_kv, "bf16", None, False, 1.0)])
            y_hy = _hyena(u, *hy_args)
            y_att = _windowed_attention(q, k, v, kc, vc, swa_sink[e], hkv)
            mix_in, mix_w = [y_hy, y_att], [w_out[:c_hy], w_out[c_hy:]]
            if with_ctx:
                yc_hy = _hyena(uc, *hy_args)
                yc_att = _full_attention(qc, kc, vc, hkv, sink=swa_sink[e])
                mixc_in = [yc_hy, yc_att]
        else:
            o = layer // 2
            hkv = od_w_qkv.shape[-1] // HEAD_DIM // 6
            hq = 4 * hkv
            d_q = hq * HEAD_DIM
            d_kv = hkv * HEAD_DIM
            w_qkv = od_w_qkv[o].astype(BF16)
            w_out = od_w_out[o].astype(BF16)
            norm_g = jnp.zeros((8, LANES), F32).at[0].set(jnp.tile(od_q_norm_g[o], 2)).at[1].set(
                jnp.tile(od_k_norm_g[o], 2))
            q, k, v = _proj(x, norm1_g[layer], m[1], m[0], w_qkv, [
                (0, d_q, "qk", 0, True, q_scale),
                (d_q, d_kv, "qk", 1, True, 1.0),
                (d_q + d_kv, d_kv, "bf16", None, False, 1.0)], rope_tabs=rope, norm_g=norm_g)
            if with_ctx:
                qc, kc, vc = _proj(xc, norm1_g[layer], mc[1], mc[0], w_qkv, [
                    (0, d_q, "qk", 0, False, q_scale),
                    (d_q, d_kv, "qk", 1, False, 1.0),
                    (d_q + d_kv, d_kv, "bf16", None, False, 1.0)], norm_g=norm_g)
            else:
                kc, vc = _proj(xc, norm1_g[layer], mc[1], mc[0], w_qkv, [
                    (d_q, d_kv, "qk", 1, False, 1.0),
                    (d_q + d_kv, d_kv, "bf16", None, False, 1.0)], norm_g=norm_g)
            y_att = _full_attention(q, jnp.concatenate([kc, k], axis=1), jnp.concatenate([vc, v], axis=1), hkv)
            mix_in, mix_w = [y_att], [w_out]
            if with_ctx:
                mixc_in = [_full_attention(qc, kc, vc, hkv)]
        x, h2, rt = _outproj(mix_in, mix_w, x, m[2], norm2_g[layer], m[4], m[3], r_hi, r_lo, r_b)
        route_flat = rt.reshape(b * n, LANES)
        if with_ctx:
            xc, h2c, rtc = _outproj(mixc_in, mix_w, xc, mc[2], norm2_g[layer], mc[4], mc[3], r_hi, r_lo, r_b)
            route_flat = jnp.concatenate([route_flat, rtc.reshape(b * lc, LANES)], axis=0)
        n_tok = route_flat.shape[0]
        n_tiles = n_tok // TOKEN_TILE
        lat_tiles = b * n // TOKEN_TILE
        max_rows = n_tok * TOP_K + n_tiles * N_EXPERTS * (CHUNK_ROWS - 1)
        n_blocks = -(-max_rows // EXPERT_ROWS) + N_EXPERTS
        pos, cnt = _rank(route_flat)
        table, block_exp, n_used = _chunk_tables(cnt[:, 0, :N_EXPERTS].astype(jnp.int32), n_blocks)
        pos_t = jnp.swapaxes(pos[:, :8].reshape(n_tiles, TOKEN_TILE, 8), 1, 2)
        xs = jnp.zeros((n_blocks * EXPERT_ROWS, d // 2 + LANES), jnp.int32)
        xs = _dispatch(h2.reshape(b * n, d), pos_t, table, xs, 0)
        if with_ctx:
            xs = _dispatch(h2c.reshape(b * lc, d), pos_t, table, xs, lat_tiles)
        ys = _experts(xs, block_exp, n_used, moe_w_gate, moe_w_up, moe_w_down, layer)
        x = _combine(ys, table, pos, x, m[5], 0, final_g=None if with_ctx else final_norm_g)
        if with_ctx:
            xc = _combine(ys, table, pos, xc.reshape(b * lc // TOKEN_TILE, TOKEN_TILE, d), mc[5],
                          lat_tiles).reshape(b, lc, d)
    return x
```

```python
import functools
import math

import numpy as np
import jax
import jax.numpy as jnp
from jax import lax
from jax.experimental import pallas as pl
from jax.experimental.pallas import tpu as pltpu

F32 = jnp.float32
BF16 = jnp.bfloat16

HEAD_DIM = 64
GRID_W = 64
ROPE_BASE = 10000.0
NORM_EPS = 1e-6
N_MOD = 6
HY_ORDER = 2
HY_BANDS = 16
HY_DIRS = 2
HY_DECAY_TARGET = 1e-2
HY_FAST_DECAY = 0.3
HY_SLOW_DECAY = 1.5
HY_FILTER_EPS = 1e-6
SWA_WINDOW = 128
N_GROUPS = 4
EXP_PER_GROUP = 8
N_EXPERTS = N_GROUPS * EXP_PER_GROUP
TOP_K = 2
EXPERT_ROWS = 512
SWA_Q_ROWS = 256
LOOP_GROUP = 4
MID_KB = 8
FULL_ATTN_Q_ROWS = 512
FULL_ATTN_UNIT_ROWS = 512
FULL_ATTN_LOOKAHEAD = 1
TOKEN_TILE = 512
CHUNK_ROWS = 8
BUF_ROWS = 1280
BUF_CHUNKS = BUF_ROWS // CHUNK_ROWS
TABLE_WORDS = 256

LANES = 128
VMEM_LIMIT_BYTES = 56 * 1024 * 1024


def _params(*sem):
    return pltpu.CompilerParams(dimension_semantics=sem, vmem_limit_bytes=VMEM_LIMIT_BYTES)


def _rope_tables(n):
    d_axis = HEAD_DIM // 2
    t = jnp.arange(n)
    inv = ROPE_BASE ** (-jnp.arange(0, d_axis, 2, dtype=F32) / d_axis)
    ang_r = (t // GRID_W).astype(F32)[:, None] * inv[None, :]
    ang_c = (t % GRID_W).astype(F32)[:, None] * inv[None, :]
    cos = jnp.concatenate([jnp.cos(ang_r)] * 2 + [jnp.cos(ang_c)] * 2, axis=-1)
    sin = jnp.concatenate([-jnp.sin(ang_r), jnp.sin(ang_r), -jnp.sin(ang_c), jnp.sin(ang_c)], axis=-1)
    return jnp.tile(cos, (1, 2)), jnp.tile(sin, (1, 2))


def _head_mean_matrix():
    i = np.arange(LANES)
    return jnp.asarray((i[:, None] // HEAD_DIM == i[None, :] // HEAD_DIM) / HEAD_DIM, dtype=BF16)


def _proj_body(x_ref, g_ref, sc_ref, sh_ref, w_ref, cos_ref, sin_ref, ng_ref, bd_ref, *out_refs, segs):
    x = x_ref[0]
    h = x * lax.rsqrt(jnp.mean(x * x, axis=-1, keepdims=True) + NORM_EPS) * g_ref[...]
    hb = (h * (1.0 + sc_ref[0]) + sh_ref[0]).astype(BF16)
    for o_ref, (c0, width, kind, norm_row, rope, out_scale) in zip(out_refs, segs):
        seg = jnp.dot(hb, w_ref[:, c0:c0 + width], preferred_element_type=F32)
        if kind == "f32":
            o_ref[0] = seg
            continue
        if kind == "bf16":
            o_ref[0] = seg.astype(BF16)
            continue
        for j in range(width // LANES):
            ch = seg[:, j * LANES:(j + 1) * LANES]
            if norm_row is not None:
                sq = ch * ch
                hi = sq.astype(BF16)
                lo = (sq - hi.astype(F32)).astype(BF16)
                ms = (jnp.dot(hi, bd_ref[...], preferred_element_type=F32)
                      + jnp.dot(lo, bd_ref[...], preferred_element_type=F32))
                ch = ch * lax.rsqrt(ms + NORM_EPS) * ng_ref[norm_row:norm_row + 1, :]
            if rope:
                lane = lax.broadcasted_iota(jnp.int32, ch.shape, 1)
                partner = jnp.where(lane % 32 < 16, pltpu.roll(ch, LANES - 16, 1), pltpu.roll(ch, 16, 1))
                ch = ch * cos_ref[...] + partner * sin_ref[...]
            if out_scale != 1.0:
                ch = ch * out_scale
            o_ref[0, :, j * LANES:(j + 1) * LANES] = ch.astype(BF16)


def _proj(x, g, scale, shift, w, segs, rope_tabs=None, norm_g=None, tm=512):
    b, n, d = x.shape
    tm = min(tm, n)
    bm = scale.shape[0]
    mod_map = (lambda bi, i: (bi, 0, 0)) if bm > 1 else (lambda bi, i: (0, 0, 0))
    if rope_tabs is None:
        cos = sin = jnp.zeros((8, LANES), F32)
        tab_spec = pl.BlockSpec((8, LANES), lambda bi, i: (0, 0))
    else:
        cos, sin = rope_tabs
        tab_spec = pl.BlockSpec((tm, LANES), lambda bi, i: (i, 0))
    if norm_g is None:
        norm_g = jnp.ones((8, LANES), F32)
    out_shape = [jax.ShapeDtypeStruct((b, n, s[1]), F32 if s[2] == "f32" else BF16) for s in segs]
    out_specs = [pl.BlockSpec((1, tm, s[1]), lambda bi, i: (bi, i, 0)) for s in segs]
    return pl.pallas_call(
        functools.partial(_proj_body, segs=tuple(segs)),
        grid=(b, n // tm),
        in_specs=[
            pl.BlockSpec((1, tm, d), lambda bi, i: (bi, i, 0)),
            pl.BlockSpec((1, d), lambda bi, i: (0, 0)),
            pl.BlockSpec((1, 1, d), mod_map),
            pl.BlockSpec((1, 1, d), mod_map),
            pl.BlockSpec(w.shape, lambda bi, i: (0, 0)),
            tab_spec,
            tab_spec,
            pl.BlockSpec(norm_g.shape, lambda bi, i: (0, 0)),
            pl.BlockSpec((LANES, LANES), lambda bi, i: (0, 0)),
        ],
        out_specs=out_specs,
        out_shape=out_shape,
        compiler_params=_params("parallel", "parallel"),
        name="proj",
    )(x, g.reshape(1, d), scale, shift, w, cos, sin, norm_g, _head_mean_matrix())


def _stack_heads(q, j, g):
    return jnp.concatenate(
        [q[:, (j * g + gg) * HEAD_DIM:(j * g + gg + 1) * HEAD_DIM] for gg in range(g)], axis=0)


def _values_with_ones(v, hkv):
    b, nk, _ = v.shape
    ones = jnp.zeros((b, nk, hkv, HEAD_DIM), v.dtype).at[..., 0].set(1)
    return jnp.concatenate([v.reshape(b, nk, hkv, HEAD_DIM), ones], axis=-1).reshape(b, nk, 2 * hkv * HEAD_DIM)


def _sink_column(sink_ref, j, g, qb):
    return jnp.concatenate([jnp.full((qb, 1), sink_ref[j * g + gg], F32) for gg in range(g)], axis=0)


def _swa_body(sink_ref, q_ref, kt_ref, v_ref, kct_ref, vc_ref, o_ref, *, n, hkv, g, qb, win):
    kw = qb + 2 * win
    blk = pl.program_id(1)
    start = pl.multiple_of(blk * qb, LANES)
    rows = lax.broadcasted_iota(jnp.int32, (g * qb, kw), 0) % qb
    cols = lax.broadcasted_iota(jnp.int32, (g * qb, kw), 1)
    key_pos = cols + blk * qb - win
    valid = (jnp.abs(rows + win - cols) <= win) & (key_pos >= 0) & (key_pos < n)
    q = q_ref[0]
    for j in range(hkv):
        hs = slice(j * HEAD_DIM, (j + 1) * HEAD_DIM)
        q4 = _stack_heads(q, j, g)
        s_lat = jnp.dot(q4, kt_ref[0, hs, pl.ds(start, kw)], preferred_element_type=F32)
        s_lat = jnp.where(valid, s_lat, -jnp.inf)
        s_ctx = jnp.dot(q4, kct_ref[0, hs, :], preferred_element_type=F32)
        s_sink = _sink_column(sink_ref, j, g, qb)
        m = jnp.maximum(jnp.maximum(jnp.max(s_lat, axis=-1, keepdims=True),
                                    jnp.max(s_ctx, axis=-1, keepdims=True)), s_sink)
        e_lat = jnp.exp((s_lat - m).astype(BF16))
        e_ctx = jnp.exp((s_ctx - m).astype(BF16))
        vs = slice(j * 2 * HEAD_DIM, (j + 1) * 2 * HEAD_DIM)
        o = (jnp.dot(e_ctx, vc_ref[0, :, vs], preferred_element_type=F32)
             + jnp.dot(e_lat, v_ref[0, pl.ds(start, kw), vs], preferred_element_type=F32))
        o = o[:, :HEAD_DIM] / (o[:, HEAD_DIM:HEAD_DIM + 1] + jnp.exp(s_sink - m))
        for gg in range(g):
            c0 = (j * g + gg) * HEAD_DIM
            o_ref[0, :, c0:c0 + HEAD_DIM] = o[gg * qb:(gg + 1) * qb].astype(BF16)


def _windowed_attention(q, k, v, kc, vc, sink, hkv):
    b, n, dq = q.shape
    g = dq // HEAD_DIM // hkv
    qb = min(SWA_Q_ROWS, n)
    win = SWA_WINDOW
    lc = kc.shape[1]
    dkv = hkv * HEAD_DIM
    kt = jnp.swapaxes(jnp.pad(k, ((0, 0), (win, win), (0, 0))), 1, 2)
    vp = _values_with_ones(jnp.pad(v, ((0, 0), (win, win), (0, 0))), hkv)
    vc = _values_with_ones(vc, hkv)
    kct = jnp.swapaxes(kc, 1, 2)
    return pl.pallas_call(
        functools.partial(_swa_body, n=n, hkv=hkv, g=g, qb=qb, win=win),
        grid=(b, n // qb),
        in_specs=[
            pl.BlockSpec(memory_space=pltpu.SMEM),
            pl.BlockSpec((1, qb, dq), lambda bi, i: (bi, i, 0)),
            pl.BlockSpec((1, dkv, n + 2 * win), lambda bi, i: (bi, 0, 0)),
            pl.BlockSpec((1, n + 2 * win, 2 * dkv), lambda bi, i: (bi, 0, 0)),
            pl.BlockSpec((1, dkv, lc), lambda bi, i: (bi, 0, 0)),
            pl.BlockSpec((1, lc, 2 * dkv), lambda bi, i: (bi, 0, 0)),
        ],
        out_specs=pl.BlockSpec((1, qb, dq), lambda bi, i: (bi, i, 0)),
        out_shape=jax.ShapeDtypeStruct((b, n, dq), BF16),
        compiler_params=_params("parallel", "parallel"),
        name="swa",
    )(sink.astype(F32), q, kt, vp, kct, vc)


def _full_attn_body(sink_ref, q_ref, kt_ref, v_ref, o_ref, *, hkv, g, qb, has_sink, unit):
    q = q_ref[0]
    units = [(j, [j * g + u * unit + t for t in range(unit)]) for j in range(hkv) for u in range(g // unit)]

    def scores(j, heads):
        qu = jnp.concatenate([q[:, h * HEAD_DIM:(h + 1) * HEAD_DIM] for h in heads], axis=0)
        return jnp.dot(qu, kt_ref[0, j * HEAD_DIM:(j + 1) * HEAD_DIM, :], preferred_element_type=F32)

    pending = [scores(*u) for u in units[:FULL_ATTN_LOOKAHEAD]]
    for idx, (j, heads) in enumerate(units):
        s = pending.pop(0)
        if idx + FULL_ATTN_LOOKAHEAD < len(units):
            pending.append(scores(*units[idx + FULL_ATTN_LOOKAHEAD]))
        m = jnp.max(s, axis=-1, keepdims=True)
        if has_sink:
            s_sink = jnp.concatenate([jnp.full((qb, 1), sink_ref[h], F32) for h in heads], axis=0)
            m = jnp.maximum(m, s_sink)
        e = jnp.exp((s - m).astype(BF16))
        o = jnp.dot(e, v_ref[0, :, j * 2 * HEAD_DIM:(j + 1) * 2 * HEAD_DIM], preferred_element_type=F32)
        den = o[:, HEAD_DIM:HEAD_DIM + 1]
        if has_sink:
            den = den + jnp.exp(s_sink - m)
        o = o[:, :HEAD_DIM] / den
        for t, h in enumerate(heads):
            o_ref[0, :, h * HEAD_DIM:(h + 1) * HEAD_DIM] = o[t * qb:(t + 1) * qb].astype(BF16)


def _full_attention(q, k, v, hkv, sink=None):
    b, n, dq = q.shape
    g = dq // HEAD_DIM // hkv
    qb = min(FULL_ATTN_Q_ROWS, n)
    unit = max(1, FULL_ATTN_UNIT_ROWS // qb)
    nk = k.shape[1]
    dkv = hkv * HEAD_DIM
    kt = jnp.swapaxes(k, 1, 2)
    has_sink = sink is not None
    sink = jnp.zeros((dq // HEAD_DIM,), F32) if sink is None else sink.astype(F32)
    return pl.pallas_call(
        functools.partial(_full_attn_body, hkv=hkv, g=g, qb=qb, has_sink=has_sink, unit=min(unit, g)),
        grid=(b, n // qb),
        in_specs=[
            pl.BlockSpec(memory_space=pltpu.SMEM),
            pl.BlockSpec((1, qb, dq), lambda bi, i: (bi, i, 0)),
            pl.BlockSpec((1, dkv, nk), lambda bi, i: (bi, 0, 0)),
            pl.BlockSpec((1, nk, 2 * dkv), lambda bi, i: (bi, 0, 0)),
        ],
        out_specs=pl.BlockSpec((1, qb, dq), lambda bi, i: (bi, i, 0)),
        out_shape=jax.ShapeDtypeStruct((b, n, dq), BF16),
        compiler_params=_params("parallel", "parallel"),
        name="full_attn",
    )(sink, q, kt, _values_with_ones(v, hkv))


def _short_conv_body(u_ref, w_ref, b_ref, o_ref, *, h1, l2, c):
    slab = lambda f: u_ref[0, :, f, :]
    row = lax.broadcasted_iota(jnp.int32, (h1, c), 0)
    for f in range(l2):
        prev = slab(f - 1) if f > 0 else jnp.where(row == 0, 0.0, pltpu.roll(slab(l2 - 1), 1, 0))
        nxt = slab(f + 1) if f < l2 - 1 else jnp.where(row == h1 - 1, 0.0, pltpu.roll(slab(0), h1 - 1, 0))
        o_ref[0, 0, :, f * c:(f + 1) * c] = (prev * w_ref[0:1, :] + slab(f) * w_ref[1:2, :]
                                             + nxt * w_ref[2:3, :] + b_ref[...])


def _short_conv(u, w, bias, l2):
    b, n, c3 = u.shape
    c = c3 // 3
    h1 = n // l2
    return pl.pallas_call(
        functools.partial(_short_conv_body, h1=h1, l2=l2, c=c),
        grid=(b, 3),
        in_specs=[
            pl.BlockSpec((1, h1, l2, c), lambda bi, j: (bi, 0, 0, j)),
            pl.BlockSpec((3, c), lambda bi, j: (0, j)),
            pl.BlockSpec((1, c), lambda bi, j: (0, j)),
        ],
        out_specs=pl.BlockSpec((1, 1, h1, l2 * c), lambda bi, j: (j, bi, 0, 0)),
        out_shape=jax.ShapeDtypeStruct((3, b, h1, l2 * c), F32),
        compiler_params=_params("parallel", "parallel"),
        name="short_conv",
    )(u.reshape(b, h1, l2, c3), w, bias.reshape(1, c3))


def _filter_body(band_ref, w1_ref, b1_ref, w2_ref, b2_ref, w3_ref, fr_ref, dl_ref, o_ref, s_ref, *, n, rt, c):
    i = pl.program_id(0)
    hp = lax.Precision.HIGHEST
    m = i * rt + lax.broadcasted_iota(jnp.int32, (rt, 1), 0)
    pos = jnp.where(m < n, m, 2 * n - m).astype(F32)
    t_norm = pos / max(n - 1, 1)
    ang = (2.0 * math.pi / n) * pos * band_ref[...]
    lane = lax.broadcasted_iota(jnp.int32, (rt, LANES), 1)
    z = jnp.where(lane == 0, t_norm,
                  jnp.where(lane <= HY_BANDS, jnp.cos(ang),
                            jnp.where(lane <= 2 * HY_BANDS, -jnp.sin(ang), 0.0)))
    fr = fr_ref[...]
    hdn = jnp.sin(fr * (jnp.dot(z, w1_ref[...], precision=hp, preferred_element_type=F32) + b1_ref[...]))
    hdn = jnp.sin(fr * (jnp.dot(hdn, w2_ref[...], precision=hp, preferred_element_type=F32) + b2_ref[...]))
    h = jnp.dot(hdn, w3_ref[...], precision=hp, preferred_element_type=F32)
    h = h * jnp.exp(-t_norm * dl_ref[...])
    half = HY_ORDER * c
    sel = jnp.where(m < n, h[:, :half], jnp.where(m > n, -h[:, half:], 0.0))
    for o in range(HY_ORDER):
        o_ref[o] = sel[:, o * c:(o + 1) * c]

    @pl.when(i == 0)
    def _():
        s_ref[...] = jnp.zeros_like(s_ref)

    s_ref[...] += jnp.sum(jnp.abs(sel), axis=0, keepdims=True)


def _hyena_filters(n, w1, b1, w2, b2, w3, freq, c):
    rt = min(1024, n)
    hid = w1.shape[1]
    bands = jnp.linspace(1e-4, HY_BANDS - 1, HY_BANDS, dtype=F32)
    band_row = jnp.zeros((1, LANES), F32).at[0, 1:1 + 2 * HY_BANDS].set(jnp.tile(bands, 2))
    w1p = jnp.zeros((LANES, hid), F32).at[:w1.shape[0]].set(w1)
    max_decay = math.log(HY_DECAY_TARGET) / HY_FAST_DECAY
    min_decay = math.log(HY_DECAY_TARGET) / HY_SLOW_DECAY
    deltas = jnp.abs(jnp.linspace(min_decay, max_decay, c, dtype=F32))
    dl = jnp.tile(deltas, HY_DIRS * HY_ORDER).reshape(1, -1)
    full = lambda a: pl.BlockSpec(a.shape, lambda i: (0,) * a.ndim)
    args = (band_row, w1p, b1.reshape(1, hid), w2, b2.reshape(1, hid), w3, freq.reshape(1, hid), dl)
    return pl.pallas_call(
        functools.partial(_filter_body, n=n, rt=rt, c=c),
        grid=(2 * n // rt,),
        in_specs=[full(a) for a in args],
        out_specs=[pl.BlockSpec((HY_ORDER, rt, c), lambda i: (0, i, 0)),
                   pl.BlockSpec((1, HY_ORDER * c), lambda i: (0, 0))],
        out_shape=[jax.ShapeDtypeStruct((HY_ORDER, 2 * n, c), F32),
                   jax.ShapeDtypeStruct((1, HY_ORDER * c), F32)],
        compiler_params=_params("arbitrary"),
        name="hyena_filter",
    )(*args)


def _dft_split(n):
    l2 = 32 if n >= 2048 else 16
    return 2 * n // l2, l2


def _dft_constants(n):
    l1, l2 = _dft_split(n)
    h1 = l1 // 2
    nn = 2 * n
    k1 = np.arange(h1)[:, None]
    a = 2 * np.pi * (k1 + 0.5) * np.arange(l1)[None, :] / l1
    w1 = np.concatenate([np.cos(a), -np.sin(a)], axis=0)
    t = 2 * np.pi * (k1 + 0.5) * np.arange(l2)[None, :] / nn
    tw_cos, tw_sin = np.cos(t), np.sin(t)
    p = 2 * np.pi * np.arange(l2)[:, None] * np.arange(l2)[None, :] / l2
    eye = np.eye(MID_KB)
    cos_k, sin_k = np.kron(np.cos(p), eye), np.kron(np.sin(p), eye)
    w2 = np.block([[cos_k, sin_k], [-sin_k, cos_k]])
    w2i = np.block([[cos_k, -sin_k], [sin_k, cos_k]])
    tw_rows = lambda m: m.reshape(h1 // MID_KB, MID_KB, l2).transpose(0, 2, 1).reshape(h1 // MID_KB, l2 * MID_KB, 1)
    ai = 2 * np.pi * np.arange(h1)[:, None] * (np.arange(h1)[None, :] + 0.5) / l1
    w1i = (2.0 / nn) * np.concatenate([np.cos(ai), -np.sin(ai)], axis=1)
    c = lambda m, dt: jnp.asarray(m, dtype=dt)
    return dict(
        l1=l1, l2=l2, h1=h1,
        w1=c(w1, BF16), w2=c(w2, BF16), w2i=c(w2i, BF16), w1i=c(w1i, BF16),
        tw_cos_fwd=c(tw_cos.T[:, :, None], F32), tw_sin_fwd=c(tw_sin.T[:, :, None], F32),
        tw_cos_inv=c(tw_rows(tw_cos), F32), tw_sin_inv=c(tw_rows(tw_sin), F32),
    )


def _ct_fwd1_body(x_ref, w_ref, tc_ref, ts_ref, nrm_ref, o_ref, *, lb, c, h1, normalise):
    for q in range(lb):
        x = x_ref[0, 0, :, q * c:(q + 1) * c]
        if normalise:
            x = x / (nrm_ref[0] + HY_FILTER_EPS)
        a = jnp.dot(w_ref[...], x.astype(BF16), preferred_element_type=F32)
        ar, ai = a[:h1], a[h1:]
        tc, ts = tc_ref[q], ts_ref[q]
        o_ref[0, 0, q] = ar * tc + ai * ts
        o_ref[0, 1, q] = ai * tc - ar * ts


def _ct_fwd1(xs, idx, consts, c, norms=None, lb=8):
    _, b, k1n, _ = xs.shape
    l2, h1 = consts["l2"], consts["h1"]
    lb = min(lb, l2)
    w = consts["w1"][:, :k1n]
    normalise = norms is not None
    if norms is None:
        norms = jnp.zeros((b, 1, c), F32)
    return pl.pallas_call(
        functools.partial(_ct_fwd1_body, lb=lb, c=c, h1=h1, normalise=normalise),
        grid=(b, l2 // lb),
        in_specs=[
            pl.BlockSpec((1, 1, k1n, lb * c), lambda bi, i: (idx, bi, 0, i)),
            pl.BlockSpec(w.shape, lambda bi, i: (0, 0)),
            pl.BlockSpec((lb, h1, 1), lambda bi, i: (i, 0, 0)),
            pl.BlockSpec((lb, h1, 1), lambda bi, i: (i, 0, 0)),
            pl.BlockSpec((1, 1, c), lambda bi, i: (bi, 0, 0)),
        ],
        out_specs=pl.BlockSpec((1, 2, lb, h1, c), lambda bi, i: (bi, 0, i, 0, 0)),
        out_shape=jax.ShapeDtypeStruct((b, 2, l2, h1, c), F32),
        compiler_params=_params("parallel", "parallel"),
        name="ct_fwd1",
    )(xs, w, consts["tw_cos_fwd"], consts["tw_sin_fwd"], norms)


def _stacked(ref, lead, rows, c):
    return jnp.concatenate([ref[lead + (0,)].reshape(rows, c), ref[lead + (1,)].reshape(rows, c)], axis=0)


def _ct_spec_body(a_ref, w2_ref, o_ref, *, l2, c):
    rows = l2 * MID_KB
    x = jnp.dot(w2_ref[...], _stacked(a_ref, (0,), rows, c).astype(BF16), preferred_element_type=F32)
    o_ref[0, 0] = x[:rows].reshape(l2, MID_KB, c)
    o_ref[0, 1] = x[rows:].reshape(l2, MID_KB, c)


def _ct_spectrum(a, consts, c):
    b = a.shape[0]
    l2, h1 = consts["l2"], consts["h1"]
    blk = pl.BlockSpec((1, 2, l2, MID_KB, c), lambda bi, i: (bi, 0, 0, i, 0))
    return pl.pallas_call(
        functools.partial(_ct_spec_body, l2=l2, c=c),
        grid=(b, h1 // MID_KB),
        in_specs=[blk, pl.BlockSpec(consts["w2"].shape, lambda bi, i: (0, 0))],
        out_specs=blk,
        out_shape=jax.ShapeDtypeStruct((b, 2, l2, h1, c), F32),
        compiler_params=_params("parallel", "parallel"),
        name="ct_spectrum",
    )(a, consts["w2"])


def _ct_mid_body(a_ref, h_ref, w2_ref, w2i_ref, tc_ref, ts_ref, o_ref, *, l2, c):
    rows = l2 * MID_KB
    x = jnp.dot(w2_ref[...], _stacked(a_ref, (0,), rows, c).astype(BF16), preferred_element_type=F32)
    xr, xi = x[:rows], x[rows:]
    hr, hi = h_ref[0, 0].reshape(rows, c), h_ref[0, 1].reshape(rows, c)
    y = jnp.concatenate([xr * hr - xi * hi, xr * hi + xi * hr], axis=0).astype(BF16)
    bm = jnp.dot(w2i_ref[...], y, preferred_element_type=F32)
    br, bi = bm[:rows], bm[rows:]
    tc, ts = tc_ref[0], ts_ref[0]
    o_ref[0, 0] = (br * tc - bi * ts).reshape(l2, MID_KB, c)
    o_ref[0, 1] = (br * ts + bi * tc).reshape(l2, MID_KB, c)


def _ct_mid(a, hspec, order, consts, c):
    b = a.shape[0]
    l2, h1 = consts["l2"], consts["h1"]
    blk = pl.BlockSpec((1, 2, l2, MID_KB, c), lambda i, bi: (bi, 0, 0, i, 0))
    tw = pl.BlockSpec((1, l2 * MID_KB, 1), lambda i, bi: (i, 0, 0))
    return pl.pallas_call(
        functools.partial(_ct_mid_body, l2=l2, c=c),
        grid=(h1 // MID_KB, b),
        in_specs=[
            blk,
            pl.BlockSpec((1, 2, l2, MID_KB, c), lambda i, bi: (order, 0, 0, i, 0)),
            pl.BlockSpec(consts["w2"].shape, lambda i, bi: (0, 0)),
            pl.BlockSpec(consts["w2i"].shape, lambda i, bi: (0, 0)),
            tw,
            tw,
        ],
        out_specs=blk,
        out_shape=jax.ShapeDtypeStruct((b, 2, l2, h1, c), F32),
        compiler_params=_params("parallel", "parallel"),
        name="ct_mid",
    )(a, hspec, consts["w2"], consts["w2i"], consts["tw_cos_inv"], consts["tw_sin_inv"])


def _ct_inv1_body(b_ref, w_ref, u_ref, gate_ref, skip_ref, o_ref, *, lb, c, by_position):
    for q in range(lb):
        bb = jnp.concatenate([b_ref[0, 0, q], b_ref[0, 1, q]], axis=0).astype(BF16)
        y = jnp.dot(w_ref[...], bb, preferred_element_type=F32)
        cs = slice(q * c, (q + 1) * c)
        out = gate_ref[0, 0, :, cs] * (y + u_ref[0, 0, :, cs] * skip_ref[...])
        if by_position:
            o_ref[0, :, q, :] = out
        else:
            o_ref[0, :, cs] = out


def _ct_inv1(bsp, u, u_idx, gate, gate_idx, skip, consts, c, by_position, lb=8):
    b = bsp.shape[0]
    l2, h1 = consts["l2"], consts["h1"]
    lb = min(lb, l2)
    if by_position:
        out_spec = pl.BlockSpec((1, h1, lb, c), lambda bi, i: (bi, 0, i, 0))
        out_shape = jax.ShapeDtypeStruct((b, h1, l2, c), F32)
    else:
        out_spec = pl.BlockSpec((1, h1, lb * c), lambda bi, i: (bi, 0, i))
        out_shape = jax.ShapeDtypeStruct((b, h1, l2 * c), F32)
    return pl.pallas_call(
        functools.partial(_ct_inv1_body, lb=lb, c=c, by_position=by_position),
        grid=(b, l2 // lb),
        in_specs=[
            pl.BlockSpec((1, 2, lb, h1, c), lambda bi, i: (bi, 0, i, 0, 0)),
            pl.BlockSpec((h1, 2 * h1), lambda bi, i: (0, 0)),
            pl.BlockSpec((1, 1, h1, lb * c), lambda bi, i: (u_idx, bi, 0, i)),
            pl.BlockSpec((1, 1, h1, lb * c), lambda bi, i: (gate_idx, bi, 0, i)),
            pl.BlockSpec((1, c), lambda bi, i: (0, 0)),
        ],
        out_specs=out_spec,
        out_shape=out_shape,
        compiler_params=_params("parallel", "parallel"),
        name="ct_inv1",
    )(bsp, consts["w1i"], u, gate, skip.reshape(1, c))


def _hyena(u, conv_w, conv_b, f_w1, f_b1, f_w2, f_b2, f_w3, f_freq, skip):
    b, n, c3 = u.shape
    c = c3 // 3
    consts = _dft_constants(n)
    l1, l2, h1 = consts["l1"], consts["l2"], consts["h1"]
    filt, norms = _hyena_filters(n, f_w1, f_b1, f_w2, f_b2, f_w3, f_freq, c)
    fa = _ct_fwd1(filt.reshape(1, HY_ORDER, l1, l2 * c), 0, consts, c, norms=norms.reshape(HY_ORDER, 1, c))
    hspec = _ct_spectrum(fa, consts, c)
    parts = _short_conv(u, conv_w, conv_b, l2)

    def long_conv_gated(x_stack, x_idx, gate_idx, order, by_position):
        a = _ct_fwd1(x_stack, x_idx, consts, c)
        bsp = _ct_mid(a, hspec, order, consts, c)
        return _ct_inv1(bsp, x_stack, x_idx, parts, gate_idx, skip[order], consts, c, by_position)

    z = long_conv_gated(parts, 0, 1, 0, False)
    return long_conv_gated(z[None], 0, 2, 1, True).reshape(b, n, c)


def _outproj_body(*refs, n_in):
    ins = refs[:n_in]
    ws = refs[n_in:2 * n_in]
    x_ref, gate_ref, g_ref, sc_ref, sh_ref, rhi_ref, rlo_ref, rb_ref, xo_ref, h_ref, rt_ref = refs[2 * n_in:]
    y = None
    for a_ref, w_ref in zip(ins, ws):
        t = jnp.dot(a_ref[0].astype(BF16), w_ref[...], preferred_element_type=F32)
        y = t if y is None else y + t
    x = x_ref[0] + gate_ref[0] * y
    xo_ref[0] = x
    h = x * lax.rsqrt(jnp.mean(x * x, axis=-1, keepdims=True) + NORM_EPS) * g_ref[...]
    h = h * (1.0 + sc_ref[0]) + sh_ref[0]
    hi = h.astype(BF16)
    h_ref[0] = hi
    lo = (h - hi.astype(F32)).astype(BF16)
    lg = (jnp.dot(hi, rhi_ref[...], preferred_element_type=F32)
          + jnp.dot(lo, rhi_ref[...], preferred_element_type=F32)
          + jnp.dot(hi, rlo_ref[...], preferred_element_type=F32) + rb_ref[...])
    rt_ref[0] = _route(lg)


def _route(lg):
    lane = lax.broadcasted_iota(jnp.int32, lg.shape, 1)
    lane_f = lane.astype(F32)
    neg = -jnp.inf

    def top(v):
        m = jnp.max(v, axis=-1, keepdims=True)
        return m, jnp.min(jnp.where(v == m, lane_f, float(LANES)), axis=-1, keepdims=True)

    gl = jnp.where(lane < N_GROUPS, lg, neg)
    gmax, grp = top(gl)
    p_grp = 1.0 / jnp.sum(jnp.exp(gl - gmax), axis=-1, keepdims=True)
    first = N_GROUPS + grp * EXP_PER_GROUP
    el = jnp.where((lane_f >= first) & (lane_f < first + EXP_PER_GROUP), lg, neg)
    m1, i1 = top(el)
    m2, i2 = top(jnp.where(lane_f == i1, neg, el))
    e2 = jnp.exp(m2 - m1)
    den = 1.0 + e2
    vals = (i1 - N_GROUPS, i2 - N_GROUPS, p_grp * (1.0 / den), p_grp * (e2 / den))
    out = jnp.zeros(lg.shape, F32)
    for k, v in enumerate(vals):
        out = jnp.where(lane == k, v, out)
    return out


def _outproj(ins, ws, x, gate, g, scale, shift, r_hi, r_lo, r_b, tm=512):
    b, n, d = x.shape
    tm = min(tm, n)
    bm = gate.shape[0]
    mod_map = (lambda bi, i: (bi, 0, 0)) if bm > 1 else (lambda bi, i: (0, 0, 0))
    row = lambda wd: pl.BlockSpec((1, tm, wd), lambda bi, i: (bi, i, 0))
    full = lambda a: pl.BlockSpec(a.shape, lambda bi, i: (0,) * a.ndim)
    mod = pl.BlockSpec((1, 1, d), mod_map)
    return pl.pallas_call(
        functools.partial(_outproj_body, n_in=len(ins)),
        grid=(b, n // tm),
        in_specs=([row(a.shape[-1]) for a in ins] + [full(w) for w in ws]
                  + [row(d), mod, pl.BlockSpec((1, d), lambda bi, i: (0, 0)), mod, mod,
                     full(r_hi), full(r_lo), full(r_b)]),
        out_specs=[row(d), row(d), row(LANES)],
        out_shape=[jax.ShapeDtypeStruct((b, n, d), F32), jax.ShapeDtypeStruct((b, n, d), BF16),
                   jax.ShapeDtypeStruct((b, n, LANES), F32)],
        compiler_params=_params("parallel", "parallel"),
        name="outproj",
    )(*ins, *ws, x, gate, g.reshape(1, d), scale, shift, r_hi, r_lo, r_b)


def _rank_body(rt_ref, tri_ref, upper_ref, pos_ref, cnt_ref):
    rt = rt_ref[...]
    lane_i = lax.broadcasted_iota(jnp.int32, rt.shape, 1)
    lane = lane_i.astype(F32)
    oh_a = lane == rt[:, 0:1]
    oh_b = lane == rt[:, 1:2]
    one_a = jnp.where(oh_a, 1.0, 0.0)
    one_b = jnp.where(oh_b, 1.0, 0.0)
    before_a = jnp.dot(tri_ref[...], one_a.astype(BF16), preferred_element_type=F32)
    before_b = jnp.dot(tri_ref[...], one_b.astype(BF16), preferred_element_type=F32)
    tot_a = jnp.sum(one_a, axis=0, keepdims=True)
    cnt = tot_a + jnp.sum(one_b, axis=0, keepdims=True)
    padded = jnp.floor((cnt + (CHUNK_ROWS - 1)) * (1.0 / CHUNK_ROWS)) * CHUNK_ROWS
    first = jnp.dot(jnp.broadcast_to(padded, (8, LANES)).astype(BF16), upper_ref[...],
                    preferred_element_type=F32)[0:1]
    pos_a = jnp.sum(jnp.where(oh_a, before_a + first, 0.0), axis=-1, keepdims=True)
    pos_b = jnp.sum(jnp.where(oh_b, before_b + first + tot_a, 0.0), axis=-1, keepdims=True)
    is_gate = (lane_i >= TOP_K) & (lane_i < 2 * TOP_K)
    pos_ref[...] = jnp.where(lane_i == 0, pos_a, jnp.where(lane_i == 1, pos_b, jnp.where(is_gate, rt, 0.0)))
    cnt_ref[0] = jnp.broadcast_to(cnt, (8, LANES))


def _rank(route):
    t = route.shape[0]
    tm = TOKEN_TILE
    tri = jnp.asarray(np.tril(np.ones((tm, tm)), -1), dtype=BF16)
    upper = jnp.asarray(np.triu(np.ones((LANES, LANES)), 1), dtype=BF16)
    return pl.pallas_call(
        _rank_body,
        grid=(t // tm,),
        in_specs=[pl.BlockSpec((tm, LANES), lambda i: (i, 0)), pl.BlockSpec((tm, tm), lambda i: (0, 0)),
                  pl.BlockSpec((LANES, LANES), lambda i: (0, 0))],
        out_specs=[pl.BlockSpec((tm, LANES), lambda i: (i, 0)), pl.BlockSpec((1, 8, LANES), lambda i: (i, 0, 0))],
        out_shape=[jax.ShapeDtypeStruct((t, LANES), F32), jax.ShapeDtypeStruct((t // tm, 8, LANES), F32)],
        compiler_params=_params("parallel"),
        name="moe_rank",
    )(route, tri, upper)


def _chunk_tables(cnt, n_blocks):
    padded = (cnt + CHUNK_ROWS - 1) // CHUNK_ROWS * CHUNK_ROWS
    run_end = jnp.cumsum(padded, axis=1)
    run_start = run_end - padded
    seg_rows = jnp.sum(padded, axis=0)
    seg_rows = (seg_rows + EXPERT_ROWS - 1) // EXPERT_ROWS * EXPERT_ROWS
    seg_end = jnp.cumsum(seg_rows)
    dst_start = (seg_end - seg_rows)[None, :] + jnp.cumsum(padded, axis=0) - padded
    row0 = jnp.arange(BUF_CHUNKS, dtype=jnp.int32) * CHUNK_ROWS
    chunk_exp = jnp.minimum(jnp.sum(run_end[:, None, :] <= row0[None, :, None], axis=-1), N_EXPERTS - 1)
    onehot = chunk_exp[:, :, None] == jnp.arange(N_EXPERTS, dtype=jnp.int32)[None, None, :]
    dst = jnp.sum(jnp.where(onehot, (dst_start - run_start)[:, None, :], 0), axis=-1) + row0[None, :]
    n_chunks = run_end[:, -1:] // CHUNK_ROWS
    table = jnp.concatenate(
        [dst, n_chunks, jnp.zeros((cnt.shape[0], TABLE_WORDS - BUF_CHUNKS - 1), jnp.int32)], axis=1)
    block_row0 = jnp.arange(n_blocks, dtype=jnp.int32) * EXPERT_ROWS
    block_exp = jnp.minimum(jnp.sum(seg_end[None, :] <= block_row0[:, None], axis=1), N_EXPERTS - 1)
    n_used = (seg_end[-1] // EXPERT_ROWS).reshape(1)
    return table.astype(jnp.int32), block_exp.astype(jnp.int32), n_used.astype(jnp.int32)


def _pack_halves(x):
    w = x.shape[-1] // 2
    lo = lax.bitcast_convert_type(x[:, :w].astype(BF16).astype(F32), jnp.int32)
    hi = lax.bitcast_convert_type(x[:, w:].astype(BF16).astype(F32), jnp.int32)
    return lax.shift_right_logical(lo, jnp.int32(16)) | (hi & jnp.int32(-65536))


def _unpack_halves(p):
    lo = lax.bitcast_convert_type(lax.shift_left(p, jnp.int32(16)), F32)
    hi = lax.bitcast_convert_type(p & jnp.int32(-65536), F32)
    return jnp.concatenate([lo, hi], axis=-1).astype(BF16)


def _start_chunks(tab_ref, tile, make_copy):
    base = tile * TABLE_WORDS

    def issue(c):
        make_copy(pl.multiple_of(c * CHUNK_ROWS, CHUNK_ROWS),
                  pl.multiple_of(tab_ref[base + c], CHUNK_ROWS)).start()

    _for_each_chunk(tab_ref[base + BUF_CHUNKS], issue)


def _wait_chunks(tab_ref, tile, make_copy):
    _for_each_chunk(tab_ref[tile * TABLE_WORDS + BUF_CHUNKS], lambda c: make_copy(0, 0).wait())


def _for_each_chunk(n, fn):
    groups = n // LOOP_GROUP

    def grouped(i, carry):
        for k in range(LOOP_GROUP):
            fn(i * LOOP_GROUP + k)
        return carry

    def single(c, carry):
        fn(c)
        return carry

    lax.fori_loop(0, groups, grouped, 0)
    lax.fori_loop(groups * LOOP_GROUP, n, single, 0)


def _dispatch_body(tab_ref, h_ref, pos_ref, xs_in, xs_out, buf, sem_rows, *, tile0, n_steps):
    del xs_in
    step = pl.program_id(0)
    dp = h_ref.shape[1] // 2

    def copies(at_step):
        slot = at_step % 2
        return lambda src, dst: pltpu.make_async_copy(
            buf.at[slot, pl.ds(src, CHUNK_ROWS)], xs_out.at[pl.ds(dst, CHUNK_ROWS)], sem_rows.at[slot])

    @pl.when(step >= 2)
    def _():
        _wait_chunks(tab_ref, tile0 + step - 2, copies(step - 2))

    tm = h_ref.shape[0]
    row = lax.broadcasted_iota(jnp.int32, (BUF_ROWS, tm), 0).astype(F32)
    oh_a = row == pos_ref[0, 0:1, :]
    oh_b = row == pos_ref[0, 1:2, :]
    buf[step % 2, :, :dp] = _pack_halves(jnp.dot(jnp.where(oh_a | oh_b, 1.0, 0.0).astype(BF16), h_ref[...],
                                                 preferred_element_type=F32))
    gate = jnp.sum(jnp.where(oh_a, pos_ref[0, 2:3, :], 0.0) + jnp.where(oh_b, pos_ref[0, 3:4, :], 0.0),
                   axis=-1, keepdims=True)
    buf[step % 2, :, dp:] = lax.bitcast_convert_type(jnp.broadcast_to(gate, (BUF_ROWS, LANES)), jnp.int32)
    _start_chunks(tab_ref, tile0 + step, copies(step))

    @pl.when(step == n_steps - 1)
    def _():
        @pl.when(step >= 1)
        def _():
            _wait_chunks(tab_ref, tile0 + step - 1, copies(step - 1))

        _wait_chunks(tab_ref, tile0 + step, copies(step))


def _dispatch(h, pos_t, table, xs, tile0):
    t, d = h.shape
    tm = TOKEN_TILE
    n_steps = t // tm
    grid_spec = pltpu.PrefetchScalarGridSpec(
        num_scalar_prefetch=1,
        grid=(n_steps,),
        in_specs=[
            pl.BlockSpec((tm, d), lambda i, tab: (i, 0)),
            pl.BlockSpec((1, 8, tm), lambda i, tab: (tile0 + i, 0, 0)),
            pl.BlockSpec(memory_space=pl.ANY),
        ],
        out_specs=pl.BlockSpec(memory_space=pl.ANY),
        scratch_shapes=[
            pltpu.VMEM((2, BUF_ROWS, d // 2 + LANES), jnp.int32),
            pltpu.SemaphoreType.DMA((2,)),
        ],
    )
    return pl.pallas_call(
        functools.partial(_dispatch_body, tile0=tile0, n_steps=n_steps),
        grid_spec=grid_spec,
        out_shape=jax.ShapeDtypeStruct(xs.shape, xs.dtype),
        input_output_aliases={3: 0},
        compiler_params=_params("arbitrary"),
        name="moe_dispatch",
    )(table.reshape(-1), h, pos_t, xs)


def _expert_body(bexp_ref, nused_ref, x_ref, wg_ref, wu_ref, wd_ref, o_ref, wg_bf, wu_bf, wd_bf):
    i = pl.program_id(0)

    @pl.when((i == 0) | (bexp_ref[i] != bexp_ref[jnp.maximum(i - 1, 0)]))
    def _():
        wg_bf[...] = wg_ref[0, 0].astype(BF16)
        wu_bf[...] = wu_ref[0, 0].astype(BF16)
        wd_bf[...] = wd_ref[0, 0].astype(BF16)

    @pl.when(i < nused_ref[0])
    def _():
        dp = o_ref.shape[1]
        xb = _unpack_halves(x_ref[:, :dp])
        gate = lax.bitcast_convert_type(x_ref[:, dp:dp + 1], F32)
        gt = jnp.dot(xb, wg_bf[...], preferred_element_type=F32)
        up = jnp.dot(xb, wu_bf[...], preferred_element_type=F32)
        hid = (gt * jax.nn.sigmoid(gt) * up).astype(BF16)
        o_ref[...] = _pack_halves(jnp.dot(hid, wd_bf[...], preferred_element_type=F32) * gate)

    @pl.when(i >= nused_ref[0])
    def _():
        o_ref[...] = jnp.zeros_like(o_ref)


def _experts(xs, block_exp, n_used, w_gate, w_up, w_down, layer):
    rows, width = xs.shape
    d, de = w_gate.shape[2:]
    used = lambda i, be, nu: (jnp.minimum(i, nu[0] - 1), 0)
    expert = lambda i, be, nu: (layer, be[i], 0, 0)
    grid_spec = pltpu.PrefetchScalarGridSpec(
        num_scalar_prefetch=2,
        grid=(rows // EXPERT_ROWS,),
        in_specs=[
            pl.BlockSpec((EXPERT_ROWS, width), used),
            pl.BlockSpec((1, 1, d, de), expert),
            pl.BlockSpec((1, 1, d, de), expert),
            pl.BlockSpec((1, 1, de, d), expert),
        ],
        out_specs=pl.BlockSpec((EXPERT_ROWS, d // 2), lambda i, be, nu: (i, 0)),
        scratch_shapes=[pltpu.VMEM((d, de), BF16), pltpu.VMEM((d, de), BF16), pltpu.VMEM((de, d), BF16)],
    )
    return pl.pallas_call(
        _expert_body,
        grid_spec=grid_spec,
        out_shape=jax.ShapeDtypeStruct((rows, d // 2), jnp.int32),
        compiler_params=_params("arbitrary"),
        name="experts",
    )(block_exp, n_used, xs, w_gate, w_up, w_down)


def _combine_body(tab_ref, ys_hbm, x_ref, gate_ref, pos_ref, fg_ref, o_ref, ybuf, sem_rows,
                  *, tile0, n_tiles, n_steps, final_norm):
    step = pl.program_id(0) * n_tiles + pl.program_id(1)

    def copies(at_step):
        slot = at_step % 2
        return lambda dst, src: pltpu.make_async_copy(
            ys_hbm.at[pl.ds(src, CHUNK_ROWS)], ybuf.at[slot, pl.ds(dst, CHUNK_ROWS)], sem_rows.at[slot])

    @pl.when(step == 0)
    def _():
        ybuf[...] = jnp.zeros_like(ybuf)
        _start_chunks(tab_ref, tile0, copies(0))

    @pl.when(step + 1 < n_steps)
    def _():
        _start_chunks(tab_ref, tile0 + step + 1, copies(step + 1))

    _wait_chunks(tab_ref, tile0 + step, copies(step))
    tm = x_ref.shape[1]
    col = lax.broadcasted_iota(jnp.int32, (tm, BUF_ROWS), 1).astype(F32)
    pick = jnp.where((col == pos_ref[:, 0:1]) | (col == pos_ref[:, 1:2]), 1.0, 0.0).astype(BF16)
    out = x_ref[0] + gate_ref[0] * jnp.dot(pick, _unpack_halves(ybuf[step % 2]), preferred_element_type=F32)
    if final_norm:
        out = out * lax.rsqrt(jnp.mean(out * out, axis=-1, keepdims=True) + NORM_EPS) * fg_ref[...]
    o_ref[0] = out


def _combine(ys, table, pos, x, gate, tile0, final_g=None):
    b, n, d = x.shape
    tm = min(TOKEN_TILE, n)
    n_tiles = n // tm
    bm = gate.shape[0]
    mod_map = (lambda bi, i, tab: (bi, 0, 0)) if bm > 1 else (lambda bi, i, tab: (0, 0, 0))
    final_norm = final_g is not None
    fg = final_g.reshape(1, d) if final_norm else jnp.ones((1, d), F32)
    grid_spec = pltpu.PrefetchScalarGridSpec(
        num_scalar_prefetch=1,
        grid=(b, n_tiles),
        in_specs=[
            pl.BlockSpec(memory_space=pl.ANY),
            pl.BlockSpec((1, tm, d), lambda bi, i, tab: (bi, i, 0)),
            pl.BlockSpec((1, 1, d), mod_map),
            pl.BlockSpec((tm, LANES), lambda bi, i, tab: (tile0 + bi * n_tiles + i, 0)),
            pl.BlockSpec((1, d), lambda bi, i, tab: (0, 0)),
        ],
        out_specs=pl.BlockSpec((1, tm, d), lambda bi, i, tab: (bi, i, 0)),
        scratch_shapes=[
            pltpu.VMEM((2, BUF_ROWS, d // 2), jnp.int32),
            pltpu.SemaphoreType.DMA((2,)),
        ],
    )
    return pl.pallas_call(
        functools.partial(_combine_body, tile0=tile0, n_tiles=n_tiles, n_steps=b * n_tiles,
                          final_norm=final_norm),
        grid_spec=grid_spec,
        out_shape=jax.ShapeDtypeStruct((b, n, d), F32),
        compiler_params=_params("arbitrary", "arbitrary"),
        name="moe_combine",
    )(table.reshape(-1), ys, x, gate, pos, fg)


def _router_weights(wg, bg, we, be):
    d = wg.shape[0]
    w = jnp.zeros((d, LANES), F32).at[:, :N_GROUPS].set(wg).at[:, N_GROUPS:N_GROUPS + N_EXPERTS].set(we)
    bias = jnp.zeros((1, LANES), F32).at[0, :N_GROUPS].set(bg).at[0, N_GROUPS:N_GROUPS + N_EXPERTS].set(be)
    hi = w.astype(BF16)
    lo = (w - hi.astype(F32)).astype(BF16)
    return hi, lo, bias


def kernel(x, c, ctx, c_ctx, ada_w, ada_b, norm1_g, norm2_g, ev_w_in, ev_w_out, hy_conv_w, hy_conv_b, hy_f_w1, hy_f_b1, hy_f_w2, hy_f_b2, hy_f_w3, hy_f_freq, hy_skip, swa_sink, od_w_qkv, od_w_out, od_q_norm_g, od_k_norm_g, rt_group_w, rt_group_b, rt_exp_w, rt_exp_b, moe_w_gate, moe_w_up, moe_w_down, final_norm_g):
    b, n, d = x.shape
    lc = ctx.shape[1]
    depth = ada_w.shape[0]
    rope = _rope_tables(n)
    xc = ctx
    sc = jax.nn.silu(c)
    scc = jax.nn.silu(c_ctx)
    q_scale = HEAD_DIM ** -0.5
    for layer in range(depth):
        with_ctx = layer < depth - 1
        mod = (sc @ ada_w[layer] + ada_b[layer]).reshape(b, N_MOD, 1, d)
        modc = (scc @ ada_w[layer] + ada_b[layer]).reshape(1, N_MOD, 1, d)
        m = [mod[:, k] for k in range(N_MOD)]
        mc = [modc[:, k] for k in range(N_MOD)]
        r_hi, r_lo, r_b = _router_weights(rt_group_w[layer], rt_group_b[layer], rt_exp_w[layer], rt_exp_b[layer])
        if layer % 2 == 0:
            e = layer // 2
            c_hy = hy_conv_w.shape[-1] // 3
            d_hy = 3 * c_hy
            hq = swa_sink.shape[-1]
            d_q = hq * HEAD_DIM
            hkv = hq // 4
            d_kv = hkv * HEAD_DIM
            w_in = ev_w_in[e].astype(BF16)
            w_out = ev_w_out[e].astype(BF16)
            hy_args = (hy_conv_w[e], hy_conv_b[e], hy_f_w1[e], hy_f_b1[e], hy_f_w2[e], hy_f_b2[e],
                       hy_f_w3[e], hy_f_freq[e], hy_skip[e])
            u, q, k, v = _proj(x, norm1_g[layer], m[1], m[0], w_in, [
                (0, d_hy, "f32", None, False, 1.0),
                (d_hy, d_q, "qk", None, True, q_scale),
                (d_hy + d_q, d_kv, "qk", None, True, 1.0),
                (d_hy + d_q + d_kv, d_kv, "bf16", None, False, 1.0)], rope_tabs=rope)
            if with_ctx:
                uc, qc, kc, vc = _proj(xc, norm1_g[layer], mc[1], mc[0], w_in, [
                    (0, d_hy, "f32", None, False, 1.0),
                    (d_hy, d_q, "qk", None, False, q_scale),
                    (d_hy + d_q, d_kv, "bf16", None, False, 1.0),
                    (d_hy + d_q + d_kv, d_kv, "bf16", None, False, 1.0)])
            else:
                kc, vc = _proj(xc, norm1_g[layer], mc[1], mc[0], w_in, [
                    (d_hy + d_q, d_kv, "bf16", None, False, 1.0),
                    (d_hy + d_q + d_kv, d_kv, "bf16", None, False, 1.0)])
            y_hy = _hyena(u, *hy_args)
            y_att = _windowed_attention(q, k, v, kc, vc, swa_sink[e], hkv)
            mix_in, mix_w = [y_hy, y_att], [w_out[:c_hy], w_out[c_hy:]]
            if with_ctx:
                yc_hy = _hyena(uc, *hy_args)
                yc_att = _full_attention(qc, kc, vc, hkv, sink=swa_sink[e])
                mixc_in = [yc_hy, yc_att]
        else:
            o = layer // 2
            hkv = od_w_qkv.shape[-1] // HEAD_DIM // 6
            hq = 4 * hkv
            d_q = hq * HEAD_DIM
            d_kv = hkv * HEAD_DIM
            w_qkv = od_w_qkv[o].astype(BF16)
            w_out = od_w_out[o].astype(BF16)
            norm_g = jnp.zeros((8, LANES), F32).at[0].set(jnp.tile(od_q_norm_g[o], 2)).at[1].set(
                jnp.tile(od_k_norm_g[o], 2))
            q, k, v = _proj(x, norm1_g[layer], m[1], m[0], w_qkv, [
                (0, d_q, "qk", 0, True, q_scale),
                (d_q, d_kv, "qk", 1, True, 1.0),
                (d_q + d_kv, d_kv, "bf16", None, False, 1.0)], rope_tabs=rope, norm_g=norm_g)
            if with_ctx:
                qc, kc, vc = _proj(xc, norm1_g[layer], mc[1], mc[0], w_qkv, [
                    (0, d_q, "qk", 0, False, q_scale),
                    (d_q, d_kv, "qk", 1, False, 1.0),
                    (d_q + d_kv, d_kv, "bf16", None, False, 1.0)], norm_g=norm_g)
            else:
                kc, vc = _proj(xc, norm1_g[layer], mc[1], mc[0], w_qkv, [
                    (d_q, d_kv, "qk", 1, False, 1.0),
                    (d_q + d_kv, d_kv, "bf16", None, False, 1.0)], norm_g=norm_g)
            y_att = _full_attention(q, jnp.concatenate([kc, k], axis=1), jnp.concatenate([vc, v], axis=1), hkv)
            mix_in, mix_w = [y_att], [w_out]
            if with_ctx:
                mixc_in = [_full_attention(qc, kc, vc, hkv)]
        x, h2, rt = _outproj(mix_in, mix_w, x, m[2], norm2_g[layer], m[4], m[3], r_hi, r_lo, r_b)
        route_flat = rt.reshape(b * n, LANES)
        if with_ctx:
            xc, h2c, rtc = _outproj(mixc_in, mix_w, xc, mc[2], norm2_g[layer], mc[4], mc[3], r_hi, r_lo, r_b)
            route_flat = jnp.concatenate([route_flat, rtc.reshape(b * lc, LANES)], axis=0)
        n_tok = route_flat.shape[0]
        n_tiles = n_tok // TOKEN_TILE
        lat_tiles = b * n // TOKEN_TILE
        max_rows = n_tok * TOP_K + n_tiles * N_EXPERTS * (CHUNK_ROWS - 1)
        n_blocks = -(-max_rows // EXPERT_ROWS) + N_EXPERTS
        pos, cnt = _rank(route_flat)
        table, block_exp, n_used = _chunk_tables(cnt[:, 0, :N_EXPERTS].astype(jnp.int32), n_blocks)
        pos_t = jnp.swapaxes(pos[:, :8].reshape(n_tiles, TOKEN_TILE, 8), 1, 2)
        xs = jnp.zeros((n_blocks * EXPERT_ROWS, d // 2 + LANES), jnp.int32)
        xs = _dispatch(h2.reshape(b * n, d), pos_t, table, xs, 0)
        if with_ctx:
            xs = _dispatch(h2c.reshape(b * lc, d), pos_t, table, xs, lat_tiles)
        ys = _experts(xs, block_exp, n_used, moe_w_gate, moe_w_up, moe_w_down, layer)
        x = _combine(ys, table, pos, x, m[5], 0, final_g=None if with_ctx else final_norm_g)
        if with_ctx:
            xc = _combine(ys, table, pos, xc.reshape(b * lc // TOKEN_TILE, TOKEN_TILE, d), mc[5],
                          lat_tiles).reshape(b, lc, d)
    return x
```

```python
import functools
import math

import numpy as np
import jax
import jax.numpy as jnp
from jax import lax
from jax.experimental import pallas as pl
from jax.experimental.pallas import tpu as pltpu

F32 = jnp.float32
BF16 = jnp.bfloat16

HEAD_DIM = 64
GRID_W = 64
ROPE_BASE = 10000.0
NORM_EPS = 1e-6
N_MOD = 6
HY_ORDER = 2
HY_BANDS = 16
HY_DIRS = 2
HY_DECAY_TARGET = 1e-2
HY_FAST_DECAY = 0.3
HY_SLOW_DECAY = 1.5
HY_FILTER_EPS = 1e-6
SWA_WINDOW = 128
N_GROUPS = 4
EXP_PER_GROUP = 8
N_EXPERTS = N_GROUPS * EXP_PER_GROUP
TOP_K = 2
EXPERT_ROWS = 1024
SWA_Q_ROWS = 256
LOOP_GROUP = 8
MID_KB = 8
FULL_ATTN_Q_ROWS = 256
FULL_ATTN_UNIT_ROWS = 512
FULL_ATTN_LOOKAHEAD = 1
TOKEN_TILE = 512
CHUNK_ROWS = 8
BUF_ROWS = 1280
BUF_CHUNKS = BUF_ROWS // CHUNK_ROWS
TABLE_WORDS = 256

LANES = 128
VMEM_LIMIT_BYTES = 56 * 1024 * 1024


def _params(*sem):
    return pltpu.CompilerParams(dimension_semantics=sem, vmem_limit_bytes=VMEM_LIMIT_BYTES)


def _rope_tables(n):
    d_axis = HEAD_DIM // 2
    t = jnp.arange(n)
    inv = ROPE_BASE ** (-jnp.arange(0, d_axis, 2, dtype=F32) / d_axis)
    ang_r = (t // GRID_W).astype(F32)[:, None] * inv[None, :]
    ang_c = (t % GRID_W).astype(F32)[:, None] * inv[None, :]
    cos = jnp.concatenate([jnp.cos(ang_r)] * 2 + [jnp.cos(ang_c)] * 2, axis=-1)
    sin = jnp.concatenate([-jnp.sin(ang_r), jnp.sin(ang_r), -jnp.sin(ang_c), jnp.sin(ang_c)], axis=-1)
    return jnp.tile(cos, (1, 2)), jnp.tile(sin, (1, 2))


def _head_mean_matrix():
    i = np.arange(LANES)
    return jnp.asarray((i[:, None] // HEAD_DIM == i[None, :] // HEAD_DIM) / HEAD_DIM, dtype=BF16)


def _proj_body(x_ref, g_ref, sc_ref, sh_ref, w_ref, cos_ref, sin_ref, ng_ref, bd_ref, *out_refs, segs):
    x = x_ref[0]
    h = x * lax.rsqrt(jnp.mean(x * x, axis=-1, keepdims=True) + NORM_EPS) * g_ref[...]
    hb = (h * (1.0 + sc_ref[0]) + sh_ref[0]).astype(BF16)
    for o_ref, (c0, width, kind, norm_row, rope, out_scale) in zip(out_refs, segs):
        seg = jnp.dot(hb, w_ref[:, c0:c0 + width], preferred_element_type=F32)
        if kind == "f32":
            o_ref[0] = seg
            continue
        if kind == "bf16":
            o_ref[0] = seg.astype(BF16)
            continue
        for j in range(width // LANES):
            ch = seg[:, j * LANES:(j + 1) * LANES]
            if norm_row is not None:
                sq = ch * ch
                hi = sq.astype(BF16)
                lo = (sq - hi.astype(F32)).astype(BF16)
                ms = (jnp.dot(hi, bd_ref[...], preferred_element_type=F32)
                      + jnp.dot(lo, bd_ref[...], preferred_element_type=F32))
                ch = ch * lax.rsqrt(ms + NORM_EPS) * ng_ref[norm_row:norm_row + 1, :]
            if rope:
                lane = lax.broadcasted_iota(jnp.int32, ch.shape, 1)
                partner = jnp.where(lane % 32 < 16, pltpu.roll(ch, LANES - 16, 1), pltpu.roll(ch, 16, 1))
                ch = ch * cos_ref[...] + partner * sin_ref[...]
            if out_scale != 1.0:
                ch = ch * out_scale
            o_ref[0, :, j * LANES:(j + 1) * LANES] = ch.astype(BF16)


def _proj(x, g, scale, shift, w, segs, rope_tabs=None, norm_g=None, tm=512):
    b, n, d = x.shape
    tm = min(tm, n)
    bm = scale.shape[0]
    mod_map = (lambda bi, i: (bi, 0, 0)) if bm > 1 else (lambda bi, i: (0, 0, 0))
    if rope_tabs is None:
        cos = sin = jnp.zeros((8, LANES), F32)
        tab_spec = pl.BlockSpec((8, LANES), lambda bi, i: (0, 0))
    else:
        cos, sin = rope_tabs
        tab_spec = pl.BlockSpec((tm, LANES), lambda bi, i: (i, 0))
    if norm_g is None:
        norm_g = jnp.ones((8, LANES), F32)
    out_shape = [jax.ShapeDtypeStruct((b, n, s[1]), F32 if s[2] == "f32" else BF16) for s in segs]
    out_specs = [pl.BlockSpec((1, tm, s[1]), lambda bi, i: (bi, i, 0)) for s in segs]
    return pl.pallas_call(
        functools.partial(_proj_body, segs=tuple(segs)),
        grid=(b, n // tm),
        in_specs=[
            pl.BlockSpec((1, tm, d), lambda bi, i: (bi, i, 0)),
            pl.BlockSpec((1, d), lambda bi, i: (0, 0)),
            pl.BlockSpec((1, 1, d), mod_map),
            pl.BlockSpec((1, 1, d), mod_map),
            pl.BlockSpec(w.shape, lambda bi, i: (0, 0)),
            tab_spec,
            tab_spec,
            pl.BlockSpec(norm_g.shape, lambda bi, i: (0, 0)),
            pl.BlockSpec((LANES, LANES), lambda bi, i: (0, 0)),
        ],
        out_specs=out_specs,
        out_shape=out_shape,
        compiler_params=_params("parallel", "parallel"),
        name="proj",
    )(x, g.reshape(1, d), scale, shift, w, cos, sin, norm_g, _head_mean_matrix())


def _stack_heads(q, j, g):
    return jnp.concatenate(
        [q[:, (j * g + gg) * HEAD_DIM:(j * g + gg + 1) * HEAD_DIM] for gg in range(g)], axis=0)


def _values_with_ones(v, hkv):
    b, nk, _ = v.shape
    ones = jnp.zeros((b, nk, hkv, HEAD_DIM), v.dtype).at[..., 0].set(1)
    return jnp.concatenate([v.reshape(b, nk, hkv, HEAD_DIM), ones], axis=-1).reshape(b, nk, 2 * hkv * HEAD_DIM)


def _sink_column(sink_ref, j, g, qb):
    return jnp.concatenate([jnp.full((qb, 1), sink_ref[j * g + gg], F32) for gg in range(g)], axis=0)


def _swa_body(sink_ref, q_ref, kt_ref, v_ref, kct_ref, vc_ref, o_ref, *, n, hkv, g, qb, win):
    kw = qb + 2 * win
    blk = pl.program_id(1)
    start = pl.multiple_of(blk * qb, LANES)
    rows = lax.broadcasted_iota(jnp.int32, (g * qb, kw), 0) % qb
    cols = lax.broadcasted_iota(jnp.int32, (g * qb, kw), 1)
    key_pos = cols + blk * qb - win
    valid = (jnp.abs(rows + win - cols) <= win) & (key_pos >= 0) & (key_pos < n)
    q = q_ref[0]
    for j in range(hkv):
        hs = slice(j * HEAD_DIM, (j + 1) * HEAD_DIM)
        q4 = _stack_heads(q, j, g)
        s_lat = jnp.dot(q4, kt_ref[0, hs, pl.ds(start, kw)], preferred_element_type=F32)
        s_lat = jnp.where(valid, s_lat, -jnp.inf)
        s_ctx = jnp.dot(q4, kct_ref[0, hs, :], preferred_element_type=F32)
        s_sink = _sink_column(sink_ref, j, g, qb)
        m = jnp.maximum(jnp.maximum(jnp.max(s_lat, axis=-1, keepdims=True),
                                    jnp.max(s_ctx, axis=-1, keepdims=True)), s_sink)
        e_lat = jnp.exp((s_lat - m).astype(BF16))
        e_ctx = jnp.exp((s_ctx - m).astype(BF16))
        vs = slice(j * 2 * HEAD_DIM, (j + 1) * 2 * HEAD_DIM)
        o = (jnp.dot(e_ctx, vc_ref[0, :, vs], preferred_element_type=F32)
             + jnp.dot(e_lat, v_ref[0, pl.ds(start, kw), vs], preferred_element_type=F32))
        o = o[:, :HEAD_DIM] / (o[:, HEAD_DIM:HEAD_DIM + 1] + jnp.exp(s_sink - m))
        for gg in range(g):
            c0 = (j * g + gg) * HEAD_DIM
            o_ref[0, :, c0:c0 + HEAD_DIM] = o[gg * qb:(gg + 1) * qb].astype(BF16)


def _windowed_attention(q, k, v, kc, vc, sink, hkv):
    b, n, dq = q.shape
    g = dq // HEAD_DIM // hkv
    qb = min(SWA_Q_ROWS, n)
    win = SWA_WINDOW
    lc = kc.shape[1]
    dkv = hkv * HEAD_DIM
    kt = jnp.swapaxes(jnp.pad(k, ((0, 0), (win, win), (0, 0))), 1, 2)
    vp = _values_with_ones(jnp.pad(v, ((0, 0), (win, win), (0, 0))), hkv)
    vc = _values_with_ones(vc, hkv)
    kct = jnp.swapaxes(kc, 1, 2)
    return pl.pallas_call(
        functools.partial(_swa_body, n=n, hkv=hkv, g=g, qb=qb, win=win),
        grid=(b, n // qb),
        in_specs=[
            pl.BlockSpec(memory_space=pltpu.SMEM),
            pl.BlockSpec((1, qb, dq), lambda bi, i: (bi, i, 0)),
            pl.BlockSpec((1, dkv, n + 2 * win), lambda bi, i: (bi, 0, 0)),
            pl.BlockSpec((1, n + 2 * win, 2 * dkv), lambda bi, i: (bi, 0, 0)),
            pl.BlockSpec((1, dkv, lc), lambda bi, i: (bi, 0, 0)),
            pl.BlockSpec((1, lc, 2 * dkv), lambda bi, i: (bi, 0, 0)),
        ],
        out_specs=pl.BlockSpec((1, qb, dq), lambda bi, i: (bi, i, 0)),
        out_shape=jax.ShapeDtypeStruct((b, n, dq), BF16),
        compiler_params=_params("parallel", "parallel"),
        name="swa",
    )(sink.astype(F32), q, kt, vp, kct, vc)


def _full_attn_body(sink_ref, q_ref, kt_ref, v_ref, o_ref, *, hkv, g, qb, has_sink, unit):
    q = q_ref[0]
    units = [(j, [j * g + u * unit + t for t in range(unit)]) for j in range(hkv) for u in range(g // unit)]

    def scores(j, heads):
        qu = jnp.concatenate([q[:, h * HEAD_DIM:(h + 1) * HEAD_DIM] for h in heads], axis=0)
        return jnp.dot(qu, kt_ref[0, j * HEAD_DIM:(j + 1) * HEAD_DIM, :], preferred_element_type=F32)

    pending = [scores(*u) for u in units[:FULL_ATTN_LOOKAHEAD]]
    for idx, (j, heads) in enumerate(units):
        s = pending.pop(0)
        if idx + FULL_ATTN_LOOKAHEAD < len(units):
            pending.append(scores(*units[idx + FULL_ATTN_LOOKAHEAD]))
        m = jnp.max(s, axis=-1, keepdims=True)
        if has_sink:
            s_sink = jnp.concatenate([jnp.full((qb, 1), sink_ref[h], F32) for h in heads], axis=0)
            m = jnp.maximum(m, s_sink)
        e = jnp.exp((s - m).astype(BF16))
        o = jnp.dot(e, v_ref[0, :, j * 2 * HEAD_DIM:(j + 1) * 2 * HEAD_DIM], preferred_element_type=F32)
        den = o[:, HEAD_DIM:HEAD_DIM + 1]
        if has_sink:
            den = den + jnp.exp(s_sink - m)
        o = o[:, :HEAD_DIM] / den
        for t, h in enumerate(heads):
            o_ref[0, :, h * HEAD_DIM:(h + 1) * HEAD_DIM] = o[t * qb:(t + 1) * qb].astype(BF16)


def _full_attention(q, k, v, hkv, sink=None):
    b, n, dq = q.shape
    g = dq // HEAD_DIM // hkv
    qb = min(FULL_ATTN_Q_ROWS, n)
    unit = max(1, FULL_ATTN_UNIT_ROWS // qb)
    nk = k.shape[1]
    dkv = hkv * HEAD_DIM
    kt = jnp.swapaxes(k, 1, 2)
    has_sink = sink is not None
    sink = jnp.zeros((dq // HEAD_DIM,), F32) if sink is None else sink.astype(F32)
    return pl.pallas_call(
        functools.partial(_full_attn_body, hkv=hkv, g=g, qb=qb, has_sink=has_sink, unit=min(unit, g)),
        grid=(b, n // qb),
        in_specs=[
            pl.BlockSpec(memory_space=pltpu.SMEM),
            pl.BlockSpec((1, qb, dq), lambda bi, i: (bi, i, 0)),
            pl.BlockSpec((1, dkv, nk), lambda bi, i: (bi, 0, 0)),
            pl.BlockSpec((1, nk, 2 * dkv), lambda bi, i: (bi, 0, 0)),
        ],
        out_specs=pl.BlockSpec((1, qb, dq), lambda bi, i: (bi, i, 0)),
        out_shape=jax.ShapeDtypeStruct((b, n, dq), BF16),
        compiler_params=_params("parallel", "parallel"),
        name="full_attn",
    )(sink, q, kt, _values_with_ones(v, hkv))


def _short_conv_body(u_ref, w_ref, b_ref, o_ref, *, h1, l2, c):
    slab = lambda f: u_ref[0, :, f, :]
    row = lax.broadcasted_iota(jnp.int32, (h1, c), 0)
    for f in range(l2):
        prev = slab(f - 1) if f > 0 else jnp.where(row == 0, 0.0, pltpu.roll(slab(l2 - 1), 1, 0))
        nxt = slab(f + 1) if f < l2 - 1 else jnp.where(row == h1 - 1, 0.0, pltpu.roll(slab(0), h1 - 1, 0))
        o_ref[0, 0, :, f * c:(f + 1) * c] = (prev * w_ref[0:1, :] + slab(f) * w_ref[1:2, :]
                                             + nxt * w_ref[2:3, :] + b_ref[...])


def _short_conv(u, w, bias, l2):
    b, n, c3 = u.shape
    c = c3 // 3
    h1 = n // l2
    return pl.pallas_call(
        functools.partial(_short_conv_body, h1=h1, l2=l2, c=c),
        grid=(b, 3),
        in_specs=[
            pl.BlockSpec((1, h1, l2, c), lambda bi, j: (bi, 0, 0, j)),
            pl.BlockSpec((3, c), lambda bi, j: (0, j)),
            pl.BlockSpec((1, c), lambda bi, j: (0, j)),
        ],
        out_specs=pl.BlockSpec((1, 1, h1, l2 * c), lambda bi, j: (j, bi, 0, 0)),
        out_shape=jax.ShapeDtypeStruct((3, b, h1, l2 * c), F32),
        compiler_params=_params("parallel", "parallel"),
        name="short_conv",
    )(u.reshape(b, h1, l2, c3), w, bias.reshape(1, c3))


def _filter_body(band_ref, w1_ref, b1_ref, w2_ref, b2_ref, w3_ref, fr_ref, dl_ref, o_ref, s_ref, *, n, rt, c):
    i = pl.program_id(0)
    hp = lax.Precision.HIGHEST
    m = i * rt + lax.broadcasted_iota(jnp.int32, (rt, 1), 0)
    pos = jnp.where(m < n, m, 2 * n - m).astype(F32)
    t_norm = pos / max(n - 1, 1)
    ang = (2.0 * math.pi / n) * pos * band_ref[...]
    lane = lax.broadcasted_iota(jnp.int32, (rt, LANES), 1)
    z = jnp.where(lane == 0, t_norm,
                  jnp.where(lane <= HY_BANDS, jnp.cos(ang),
                            jnp.where(lane <= 2 * HY_BANDS, -jnp.sin(ang), 0.0)))
    fr = fr_ref[...]
    hdn = jnp.sin(fr * (jnp.dot(z, w1_ref[...], precision=hp, preferred_element_type=F32) + b1_ref[...]))
    hdn = jnp.sin(fr * (jnp.dot(hdn, w2_ref[...], precision=hp, preferred_element_type=F32) + b2_ref[...]))
    h = jnp.dot(hdn, w3_ref[...], precision=hp, preferred_element_type=F32)
    h = h * jnp.exp(-t_norm * dl_ref[...])
    half = HY_ORDER * c
    sel = jnp.where(m < n, h[:, :half], jnp.where(m > n, -h[:, half:], 0.0))
    for o in range(HY_ORDER):
        o_ref[o] = sel[:, o * c:(o + 1) * c]

    @pl.when(i == 0)
    def _():
        s_ref[...] = jnp.zeros_like(s_ref)

    s_ref[...] += jnp.sum(jnp.abs(sel), axis=0, keepdims=True)


def _hyena_filters(n, w1, b1, w2, b2, w3, freq, c):
    rt = min(1024, n)
    hid = w1.shape[1]
    bands = jnp.linspace(1e-4, HY_BANDS - 1, HY_BANDS, dtype=F32)
    band_row = jnp.zeros((1, LANES), F32).at[0, 1:1 + 2 * HY_BANDS].set(jnp.tile(bands, 2))
    w1p = jnp.zeros((LANES, hid), F32).at[:w1.shape[0]].set(w1)
    max_decay = math.log(HY_DECAY_TARGET) / HY_FAST_DECAY
    min_decay = math.log(HY_DECAY_TARGET) / HY_SLOW_DECAY
    deltas = jnp.abs(jnp.linspace(min_decay, max_decay, c, dtype=F32))
    dl = jnp.tile(deltas, HY_DIRS * HY_ORDER).reshape(1, -1)
    full = lambda a: pl.BlockSpec(a.shape, lambda i: (0,) * a.ndim)
    args = (band_row, w1p, b1.reshape(1, hid), w2, b2.reshape(1, hid), w3, freq.reshape(1, hid), dl)
    return pl.pallas_call(
        functools.partial(_filter_body, n=n, rt=rt, c=c),
        grid=(2 * n // rt,),
        in_specs=[full(a) for a in args],
        out_specs=[pl.BlockSpec((HY_ORDER, rt, c), lambda i: (0, i, 0)),
                   pl.BlockSpec((1, HY_ORDER * c), lambda i: (0, 0))],
        out_shape=[jax.ShapeDtypeStruct((HY_ORDER, 2 * n, c), F32),
                   jax.ShapeDtypeStruct((1, HY_ORDER * c), F32)],
        compiler_params=_params("arbitrary"),
        name="hyena_filter",
    )(*args)


def _dft_split(n):
    l2 = 32 if n >= 2048 else 16
    return 2 * n // l2, l2


def _dft_constants(n):
    l1, l2 = _dft_split(n)
    h1 = l1 // 2
    nn = 2 * n
    k1 = np.arange(h1)[:, None]
    a = 2 * np.pi * (k1 + 0.5) * np.arange(l1)[None, :] / l1
    w1 = np.concatenate([np.cos(a), -np.sin(a)], axis=0)
    t = 2 * np.pi * (k1 + 0.5) * np.arange(l2)[None, :] / nn
    tw_cos, tw_sin = np.cos(t), np.sin(t)
    p = 2 * np.pi * np.arange(l2)[:, None] * np.arange(l2)[None, :] / l2
    eye = np.eye(MID_KB)
    cos_k, sin_k = np.kron(np.cos(p), eye), np.kron(np.sin(p), eye)
    w2 = np.block([[cos_k, sin_k], [-sin_k, cos_k]])
    w2i = np.block([[cos_k, -sin_k], [sin_k, cos_k]])
    tw_rows = lambda m: m.reshape(h1 // MID_KB, MID_KB, l2).transpose(0, 2, 1).reshape(h1 // MID_KB, l2 * MID_KB, 1)
    ai = 2 * np.pi * np.arange(h1)[:, None] * (np.arange(h1)[None, :] + 0.5) / l1
    w1i = (2.0 / nn) * np.concatenate([np.cos(ai), -np.sin(ai)], axis=1)
    c = lambda m, dt: jnp.asarray(m, dtype=dt)
    return dict(
        l1=l1, l2=l2, h1=h1,
        w1=c(w1, BF16), w2=c(w2, BF16), w2i=c(w2i, BF16), w1i=c(w1i, BF16),
        tw_cos_fwd=c(tw_cos.T[:, :, None], F32), tw_sin_fwd=c(tw_sin.T[:, :, None], F32),
        tw_cos_inv=c(tw_rows(tw_cos), F32), tw_sin_inv=c(tw_rows(tw_sin), F32),
    )


def _ct_fwd1_body(x_ref, w_ref, tc_ref, ts_ref, nrm_ref, o_ref, *, lb, c, h1, normalise):
    for q in range(lb):
        x = x_ref[0, 0, :, q * c:(q + 1) * c]
        if normalise:
            x = x / (nrm_ref[0] + HY_FILTER_EPS)
        a = jnp.dot(w_ref[...], x.astype(BF16), preferred_element_type=F32)
        ar, ai = a[:h1], a[h1:]
        tc, ts = tc_ref[q], ts_ref[q]
        o_ref[0, 0, q] = ar * tc + ai * ts
        o_ref[0, 1, q] = ai * tc - ar * ts


def _ct_fwd1(xs, idx, consts, c, norms=None, lb=8):
    _, b, k1n, _ = xs.shape
    l2, h1 = consts["l2"], consts["h1"]
    lb = min(lb, l2)
    w = consts["w1"][:, :k1n]
    normalise = norms is not None
    if norms is None:
        norms = jnp.zeros((b, 1, c), F32)
    return pl.pallas_call(
        functools.partial(_ct_fwd1_body, lb=lb, c=c, h1=h1, normalise=normalise),
        grid=(b, l2 // lb),
        in_specs=[
            pl.BlockSpec((1, 1, k1n, lb * c), lambda bi, i: (idx, bi, 0, i)),
            pl.BlockSpec(w.shape, lambda bi, i: (0, 0)),
            pl.BlockSpec((lb, h1, 1), lambda bi, i: (i, 0, 0)),
            pl.BlockSpec((lb, h1, 1), lambda bi, i: (i, 0, 0)),
            pl.BlockSpec((1, 1, c), lambda bi, i: (bi, 0, 0)),
        ],
        out_specs=pl.BlockSpec((1, 2, lb, h1, c), lambda bi, i: (bi, 0, i, 0, 0)),
        out_shape=jax.ShapeDtypeStruct((b, 2, l2, h1, c), F32),
        compiler_params=_params("parallel", "parallel"),
        name="ct_fwd1",
    )(xs, w, consts["tw_cos_fwd"], consts["tw_sin_fwd"], norms)


def _stacked(ref, lead, rows, c):
    return jnp.concatenate([ref[lead + (0,)].reshape(rows, c), ref[lead + (1,)].reshape(rows, c)], axis=0)


def _ct_spec_body(a_ref, w2_ref, o_ref, *, l2, c):
    rows = l2 * MID_KB
    x = jnp.dot(w2_ref[...], _stacked(a_ref, (0,), rows, c).astype(BF16), preferred_element_type=F32)
    o_ref[0, 0] = x[:rows].reshape(l2, MID_KB, c)
    o_ref[0, 1] = x[rows:].reshape(l2, MID_KB, c)


def _ct_spectrum(a, consts, c):
    b = a.shape[0]
    l2, h1 = consts["l2"], consts["h1"]
    blk = pl.BlockSpec((1, 2, l2, MID_KB, c), lambda bi, i: (bi, 0, 0, i, 0))
    return pl.pallas_call(
        functools.partial(_ct_spec_body, l2=l2, c=c),
        grid=(b, h1 // MID_KB),
        in_specs=[blk, pl.BlockSpec(consts["w2"].shape, lambda bi, i: (0, 0))],
        out_specs=blk,
        out_shape=jax.ShapeDtypeStruct((b, 2, l2, h1, c), F32),
        compiler_params=_params("parallel", "parallel"),
        name="ct_spectrum",
    )(a, consts["w2"])


def _ct_mid_body(a_ref, h_ref, w2_ref, w2i_ref, tc_ref, ts_ref, o_ref, *, l2, c):
    rows = l2 * MID_KB
    x = jnp.dot(w2_ref[...], _stacked(a_ref, (0,), rows, c).astype(BF16), preferred_element_type=F32)
    xr, xi = x[:rows], x[rows:]
    hr, hi = h_ref[0, 0].reshape(rows, c), h_ref[0, 1].reshape(rows, c)
    y = jnp.concatenate([xr * hr - xi * hi, xr * hi + xi * hr], axis=0).astype(BF16)
    bm = jnp.dot(w2i_ref[...], y, preferred_element_type=F32)
    br, bi = bm[:rows], bm[rows:]
    tc, ts = tc_ref[0], ts_ref[0]
    o_ref[0, 0] = (br * tc - bi * ts).reshape(l2, MID_KB, c)
    o_ref[0, 1] = (br * ts + bi * tc).reshape(l2, MID_KB, c)


def _ct_mid(a, hspec, order, consts, c):
    b = a.shape[0]
    l2, h1 = consts["l2"], consts["h1"]
    blk = pl.BlockSpec((1, 2, l2, MID_KB, c), lambda i, bi: (bi, 0, 0, i, 0))
    tw = pl.BlockSpec((1, l2 * MID_KB, 1), lambda i, bi: (i, 0, 0))
    return pl.pallas_call(
        functools.partial(_ct_mid_body, l2=l2, c=c),
        grid=(h1 // MID_KB, b),
        in_specs=[
            blk,
            pl.BlockSpec((1, 2, l2, MID_KB, c), lambda i, bi: (order, 0, 0, i, 0)),
            pl.BlockSpec(consts["w2"].shape, lambda i, bi: (0, 0)),
            pl.BlockSpec(consts["w2i"].shape, lambda i, bi: (0, 0)),
            tw,
            tw,
        ],
        out_specs=blk,
        out_shape=jax.ShapeDtypeStruct((b, 2, l2, h1, c), F32),
        compiler_params=_params("parallel", "parallel"),
        name="ct_mid",
    )(a, hspec, consts["w2"], consts["w2i"], consts["tw_cos_inv"], consts["tw_sin_inv"])


def _ct_inv1_body(b_ref, w_ref, u_ref, gate_ref, skip_ref, o_ref, *, lb, c, by_position):
    for q in range(lb):
        bb = jnp.concatenate([b_ref[0, 0, q], b_ref[0, 1, q]], axis=0).astype(BF16)
        y = jnp.dot(w_ref[...], bb, preferred_element_type=F32)
        cs = slice(q * c, (q + 1) * c)
        out = gate_ref[0, 0, :, cs] * (y + u_ref[0, 0, :, cs] * skip_ref[...])
        if by_position:
            o_ref[0, :, q, :] = out
        else:
            o_ref[0, :, cs] = out


def _ct_inv1(bsp, u, u_idx, gate, gate_idx, skip, consts, c, by_position, lb=8):
    b = bsp.shape[0]
    l2, h1 = consts["l2"], consts["h1"]
    lb = min(lb, l2)
    if by_position:
        out_spec = pl.BlockSpec((1, h1, lb, c), lambda bi, i: (bi, 0, i, 0))
        out_shape = jax.ShapeDtypeStruct((b, h1, l2, c), F32)
    else:
        out_spec = pl.BlockSpec((1, h1, lb * c), lambda bi, i: (bi, 0, i))
        out_shape = jax.ShapeDtypeStruct((b, h1, l2 * c), F32)
    return pl.pallas_call(
        functools.partial(_ct_inv1_body, lb=lb, c=c, by_position=by_position),
        grid=(b, l2 // lb),
        in_specs=[
            pl.BlockSpec((1, 2, lb, h1, c), lambda bi, i: (bi, 0, i, 0, 0)),
            pl.BlockSpec((h1, 2 * h1), lambda bi, i: (0, 0)),
            pl.BlockSpec((1, 1, h1, lb * c), lambda bi, i: (u_idx, bi, 0, i)),
            pl.BlockSpec((1, 1, h1, lb * c), lambda bi, i: (gate_idx, bi, 0, i)),
            pl.BlockSpec((1, c), lambda bi, i: (0, 0)),
        ],
        out_specs=out_spec,
        out_shape=out_shape,
        compiler_params=_params("parallel", "parallel"),
        name="ct_inv1",
    )(bsp, consts["w1i"], u, gate, skip.reshape(1, c))


def _hyena(u, conv_w, conv_b, f_w1, f_b1, f_w2, f_b2, f_w3, f_freq, skip):
    b, n, c3 = u.shape
    c = c3 // 3
    consts = _dft_constants(n)
    l1, l2, h1 = consts["l1"], consts["l2"], consts["h1"]
    filt, norms = _hyena_filters(n, f_w1, f_b1, f_w2, f_b2, f_w3, f_freq, c)
    fa = _ct_fwd1(filt.reshape(1, HY_ORDER, l1, l2 * c), 0, consts, c, norms=norms.reshape(HY_ORDER, 1, c))
    hspec = _ct_spectrum(fa, consts, c)
    parts = _short_conv(u, conv_w, conv_b, l2)

    def long_conv_gated(x_stack, x_idx, gate_idx, order, by_position):
        a = _ct_fwd1(x_stack, x_idx, consts, c)
        bsp = _ct_mid(a, hspec, order, consts, c)
        return _ct_inv1(bsp, x_stack, x_idx, parts, gate_idx, skip[order], consts, c, by_position)

    z = long_conv_gated(parts, 0, 1, 0, False)
    return long_conv_gated(z[None], 0, 2, 1, True).reshape(b, n, c)


def _outproj_body(*refs, n_in):
    ins = refs[:n_in]
    ws = refs[n_in:2 * n_in]
    x_ref, gate_ref, g_ref, sc_ref, sh_ref, rhi_ref, rlo_ref, rb_ref, xo_ref, h_ref, rt_ref = refs[2 * n_in:]
    y = None
    for a_ref, w_ref in zip(ins, ws):
        t = jnp.dot(a_ref[0].astype(BF16), w_ref[...], preferred_element_type=F32)
        y = t if y is None else y + t
    x = x_ref[0] + gate_ref[0] * y
    xo_ref[0] = x
    h = x * lax.rsqrt(jnp.mean(x * x, axis=-1, keepdims=True) + NORM_EPS) * g_ref[...]
    h = h * (1.0 + sc_ref[0]) + sh_ref[0]
    hi = h.astype(BF16)
    h_ref[0] = hi
    lo = (h - hi.astype(F32)).astype(BF16)
    lg = (jnp.dot(hi, rhi_ref[...], preferred_element_type=F32)
          + jnp.dot(lo, rhi_ref[...], preferred_element_type=F32)
          + jnp.dot(hi, rlo_ref[...], preferred_element_type=F32) + rb_ref[...])
    rt_ref[0] = _route(lg)


def _route(lg):
    lane = lax.broadcasted_iota(jnp.int32, lg.shape, 1)
    lane_f = lane.astype(F32)
    neg = -jnp.inf

    def top(v):
        m = jnp.max(v, axis=-1, keepdims=True)
        return m, jnp.min(jnp.where(v == m, lane_f, float(LANES)), axis=-1, keepdims=True)

    gl = jnp.where(lane < N_GROUPS, lg, neg)
    gmax, grp = top(gl)
    p_grp = 1.0 / jnp.sum(jnp.exp(gl - gmax), axis=-1, keepdims=True)
    first = N_GROUPS + grp * EXP_PER_GROUP
    el = jnp.where((lane_f >= first) & (lane_f < first + EXP_PER_GROUP), lg, neg)
    m1, i1 = top(el)
    m2, i2 = top(jnp.where(lane_f == i1, neg, el))
    e2 = jnp.exp(m2 - m1)
    den = 1.0 + e2
    vals = (i1 - N_GROUPS, i2 - N_GROUPS, p_grp * (1.0 / den), p_grp * (e2 / den))
    out = jnp.zeros(lg.shape, F32)
    for k, v in enumerate(vals):
        out = jnp.where(lane == k, v, out)
    return out


def _outproj(ins, ws, x, gate, g, scale, shift, r_hi, r_lo, r_b, tm=512):
    b, n, d = x.shape
    tm = min(tm, n)
    bm = gate.shape[0]
    mod_map = (lambda bi, i: (bi, 0, 0)) if bm > 1 else (lambda bi, i: (0, 0, 0))
    row = lambda wd: pl.BlockSpec((1, tm, wd), lambda bi, i: (bi, i, 0))
    full = lambda a: pl.BlockSpec(a.shape, lambda bi, i: (0,) * a.ndim)
    mod = pl.BlockSpec((1, 1, d), mod_map)
    return pl.pallas_call(
        functools.partial(_outproj_body, n_in=len(ins)),
        grid=(b, n // tm),
        in_specs=([row(a.shape[-1]) for a in ins] + [full(w) for w in ws]
                  + [row(d), mod, pl.BlockSpec((1, d), lambda bi, i: (0, 0)), mod, mod,
                     full(r_hi), full(r_lo), full(r_b)]),
        out_specs=[row(d), row(d), row(LANES)],
        out_shape=[jax.ShapeDtypeStruct((b, n, d), F32), jax.ShapeDtypeStruct((b, n, d), BF16),
                   jax.ShapeDtypeStruct((b, n, LANES), F32)],
        compiler_params=_params("parallel", "parallel"),
        name="outproj",
    )(*ins, *ws, x, gate, g.reshape(1, d), scale, shift, r_hi, r_lo, r_b)


def _rank_body(rt_ref, tri_ref, upper_ref, pos_ref, cnt_ref):
    rt = rt_ref[...]
    lane_i = lax.broadcasted_iota(jnp.int32, rt.shape, 1)
    lane = lane_i.astype(F32)
    oh_a = lane == rt[:, 0:1]
    oh_b = lane == rt[:, 1:2]
    one_a = jnp.where(oh_a, 1.0, 0.0)
    one_b = jnp.where(oh_b, 1.0, 0.0)
    before_a = jnp.dot(tri_ref[...], one_a.astype(BF16), preferred_element_type=F32)
    before_b = jnp.dot(tri_ref[...], one_b.astype(BF16), preferred_element_type=F32)
    tot_a = jnp.sum(one_a, axis=0, keepdims=True)
    cnt = tot_a + jnp.sum(one_b, axis=0, keepdims=True)
    padded = jnp.floor((cnt + (CHUNK_ROWS - 1)) * (1.0 / CHUNK_ROWS)) * CHUNK_ROWS
    first = jnp.dot(jnp.broadcast_to(padded, (8, LANES)).astype(BF16), upper_ref[...],
                    preferred_element_type=F32)[0:1]
    pos_a = jnp.sum(jnp.where(oh_a, before_a + first, 0.0), axis=-1, keepdims=True)
    pos_b = jnp.sum(jnp.where(oh_b, before_b + first + tot_a, 0.0), axis=-1, keepdims=True)
    is_gate = (lane_i >= TOP_K) & (lane_i < 2 * TOP_K)
    pos_ref[...] = jnp.where(lane_i == 0, pos_a, jnp.where(lane_i == 1, pos_b, jnp.where(is_gate, rt, 0.0)))
    cnt_ref[0] = jnp.broadcast_to(cnt, (8, LANES))


def _rank(route):
    t = route.shape[0]
    tm = TOKEN_TILE
    tri = jnp.asarray(np.tril(np.ones((tm, tm)), -1), dtype=BF16)
    upper = jnp.asarray(np.triu(np.ones((LANES, LANES)), 1), dtype=BF16)
    return pl.pallas_call(
        _rank_body,
        grid=(t // tm,),
        in_specs=[pl.BlockSpec((tm, LANES), lambda i: (i, 0)), pl.BlockSpec((tm, tm), lambda i: (0, 0)),
                  pl.BlockSpec((LANES, LANES), lambda i: (0, 0))],
        out_specs=[pl.BlockSpec((tm, LANES), lambda i: (i, 0)), pl.BlockSpec((1, 8, LANES), lambda i: (i, 0, 0))],
        out_shape=[jax.ShapeDtypeStruct((t, LANES), F32), jax.ShapeDtypeStruct((t // tm, 8, LANES), F32)],
        compiler_params=_params("parallel"),
        name="moe_rank",
    )(route, tri, upper)


def _chunk_tables(cnt, n_blocks):
    padded = (cnt + CHUNK_ROWS - 1) // CHUNK_ROWS * CHUNK_ROWS
    run_end = jnp.cumsum(padded, axis=1)
    run_start = run_end - padded
    seg_rows = jnp.sum(padded, axis=0)
    seg_rows = (seg_rows + EXPERT_ROWS - 1) // EXPERT_ROWS * EXPERT_ROWS
    seg_end = jnp.cumsum(seg_rows)
    dst_start = (seg_end - seg_rows)[None, :] + jnp.cumsum(padded, axis=0) - padded
    row0 = jnp.arange(BUF_CHUNKS, dtype=jnp.int32) * CHUNK_ROWS
    chunk_exp = jnp.minimum(jnp.sum(run_end[:, None, :] <= row0[None, :, None], axis=-1), N_EXPERTS - 1)
    onehot = chunk_exp[:, :, None] == jnp.arange(N_EXPERTS, dtype=jnp.int32)[None, None, :]
    dst = jnp.sum(jnp.where(onehot, (dst_start - run_start)[:, None, :], 0), axis=-1) + row0[None, :]
    n_chunks = run_end[:, -1:] // CHUNK_ROWS
    table = jnp.concatenate(
        [dst, n_chunks, jnp.zeros((cnt.shape[0], TABLE_WORDS - BUF_CHUNKS - 1), jnp.int32)], axis=1)
    block_row0 = jnp.arange(n_blocks, dtype=jnp.int32) * EXPERT_ROWS
    block_exp = jnp.minimum(jnp.sum(seg_end[None, :] <= block_row0[:, None], axis=1), N_EXPERTS - 1)
    n_used = (seg_end[-1] // EXPERT_ROWS).reshape(1)
    return table.astype(jnp.int32), block_exp.astype(jnp.int32), n_used.astype(jnp.int32)


def _pack_halves(x):
    w = x.shape[-1] // 2
    lo = lax.bitcast_convert_type(x[:, :w].astype(BF16).astype(F32), jnp.int32)
    hi = lax.bitcast_convert_type(x[:, w:].astype(BF16).astype(F32), jnp.int32)
    return lax.shift_right_logical(lo, jnp.int32(16)) | (hi & jnp.int32(-65536))


def _unpack_halves(p):
    lo = lax.bitcast_convert_type(lax.shift_left(p, jnp.int32(16)), F32)
    hi = lax.bitcast_convert_type(p & jnp.int32(-65536), F32)
    return jnp.concatenate([lo, hi], axis=-1).astype(BF16)


def _start_chunks(tab_ref, tile, make_copy):
    base = tile * TABLE_WORDS

    def issue(c):
        make_copy(pl.multiple_of(c * CHUNK_ROWS, CHUNK_ROWS),
                  pl.multiple_of(tab_ref[base + c], CHUNK_ROWS)).start()

    _for_each_chunk(tab_ref[base + BUF_CHUNKS], issue)


def _wait_chunks(tab_ref, tile, make_copy):
    _for_each_chunk(tab_ref[tile * TABLE_WORDS + BUF_CHUNKS], lambda c: make_copy(0, 0).wait())


def _for_each_chunk(n, fn):
    groups = n // LOOP_GROUP

    def grouped(i, carry):
        for k in range(LOOP_GROUP):
            fn(i * LOOP_GROUP + k)
        return carry

    def single(c, carry):
        fn(c)
        return carry

    lax.fori_loop(0, groups, grouped, 0)
    lax.fori_loop(groups * LOOP_GROUP, n, single, 0)


def _dispatch_body(tab_ref, h_ref, pos_ref, xs_in, xs_out, buf, sem_rows, *, tile0, n_steps):
    del xs_in
    step = pl.program_id(0)
    dp = h_ref.shape[1] // 2

    def copies(at_step):
        slot = at_step % 2
        return lambda src, dst: pltpu.make_async_copy(
            buf.at[slot, pl.ds(src, CHUNK_ROWS)], xs_out.at[pl.ds(dst, CHUNK_ROWS)], sem_rows.at[slot])

    @pl.when(step >= 2)
    def _():
        _wait_chunks(tab_ref, tile0 + step - 2, copies(step - 2))

    tm = h_ref.shape[0]
    row = lax.broadcasted_iota(jnp.int32, (BUF_ROWS, tm), 0).astype(F32)
    oh_a = row == pos_ref[0, 0:1, :]
    oh_b = row == pos_ref[0, 1:2, :]
    buf[step % 2, :, :dp] = _pack_halves(jnp.dot(jnp.where(oh_a | oh_b, 1.0, 0.0).astype(BF16), h_ref[...],
                                                 preferred_element_type=F32))
    gate = jnp.sum(jnp.where(oh_a, pos_ref[0, 2:3, :], 0.0) + jnp.where(oh_b, pos_ref[0, 3:4, :], 0.0),
                   axis=-1, keepdims=True)
    buf[step % 2, :, dp:] = lax.bitcast_convert_type(jnp.broadcast_to(gate, (BUF_ROWS, LANES)), jnp.int32)
    _start_chunks(tab_ref, tile0 + step, copies(step))

    @pl.when(step == n_steps - 1)
    def _():
        @pl.when(step >= 1)
        def _():
            _wait_chunks(tab_ref, tile0 + step - 1, copies(step - 1))

        _wait_chunks(tab_ref, tile0 + step, copies(step))


def _dispatch(h, pos_t, table, xs, tile0):
    t, d = h.shape
    tm = TOKEN_TILE
    n_steps = t // tm
    grid_spec = pltpu.PrefetchScalarGridSpec(
        num_scalar_prefetch=1,
        grid=(n_steps,),
        in_specs=[
            pl.BlockSpec((tm, d), lambda i, tab: (i, 0)),
            pl.BlockSpec((1, 8, tm), lambda i, tab: (tile0 + i, 0, 0)),
            pl.BlockSpec(memory_space=pl.ANY),
        ],
        out_specs=pl.BlockSpec(memory_space=pl.ANY),
        scratch_shapes=[
            pltpu.VMEM((2, BUF_ROWS, d // 2 + LANES), jnp.int32),
            pltpu.SemaphoreType.DMA((2,)),
        ],
    )
    return pl.pallas_call(
        functools.partial(_dispatch_body, tile0=tile0, n_steps=n_steps),
        grid_spec=grid_spec,
        out_shape=jax.ShapeDtypeStruct(xs.shape, xs.dtype),
        input_output_aliases={3: 0},
        compiler_params=_params("arbitrary"),
        name="moe_dispatch",
    )(table.reshape(-1), h, pos_t, xs)


def _expert_body(bexp_ref, nused_ref, x_ref, wg_ref, wu_ref, wd_ref, o_ref, wg_bf, wu_bf, wd_bf):
    i = pl.program_id(0)

    @pl.when((i == 0) | (bexp_ref[i] != bexp_ref[jnp.maximum(i - 1, 0)]))
    def _():
        wg_bf[...] = wg_ref[0, 0].astype(BF16)
        wu_bf[...] = wu_ref[0, 0].astype(BF16)
        wd_bf[...] = wd_ref[0, 0].astype(BF16)

    @pl.when(i < nused_ref[0])
    def _():
        dp = o_ref.shape[1]
        xb = _unpack_halves(x_ref[:, :dp])
        gate = lax.bitcast_convert_type(x_ref[:, dp:dp + 1], F32)
        gt = jnp.dot(xb, wg_bf[...], preferred_element_type=F32)
        up = jnp.dot(xb, wu_bf[...], preferred_element_type=F32)
        hid = (gt * jax.nn.sigmoid(gt) * up).astype(BF16)
        o_ref[...] = _pack_halves(jnp.dot(hid, wd_bf[...], preferred_element_type=F32) * gate)

    @pl.when(i >= nused_ref[0])
    def _():
        o_ref[...] = jnp.zeros_like(o_ref)


def _experts(xs, block_exp, n_used, w_gate, w_up, w_down, layer):
    rows, width = xs.shape
    d, de = w_gate.shape[2:]
    used = lambda i, be, nu: (jnp.minimum(i, nu[0] - 1), 0)
    expert = lambda i, be, nu: (layer, be[i], 0, 0)
    grid_spec = pltpu.PrefetchScalarGridSpec(
        num_scalar_prefetch=2,
        grid=(rows // EXPERT_ROWS,),
        in_specs=[
            pl.BlockSpec((EXPERT_ROWS, width), used),
            pl.BlockSpec((1, 1, d, de), expert),
            pl.BlockSpec((1, 1, d, de), expert),
            pl.BlockSpec((1, 1, de, d), expert),
        ],
        out_specs=pl.BlockSpec((EXPERT_ROWS, d // 2), lambda i, be, nu: (i, 0)),
        scratch_shapes=[pltpu.VMEM((d, de), BF16), pltpu.VMEM((d, de), BF16), pltpu.VMEM((de, d), BF16)],
    )
    return pl.pallas_call(
        _expert_body,
        grid_spec=grid_spec,
        out_shape=jax.ShapeDtypeStruct((rows, d // 2), jnp.int32),
        compiler_params=_params("arbitrary"),
        name="experts",
    )(block_exp, n_used, xs, w_gate, w_up, w_down)


def _combine_body(tab_ref, ys_hbm, x_ref, gate_ref, pos_ref, fg_ref, o_ref, ybuf, sem_rows,
                  *, tile0, n_tiles, n_steps, final_norm):
    step = pl.program_id(0) * n_tiles + pl.program_id(1)

    def copies(at_step):
        slot = at_step % 2
        return lambda dst, src: pltpu.make_async_copy(
            ys_hbm.at[pl.ds(src, CHUNK_ROWS)], ybuf.at[slot, pl.ds(dst, CHUNK_ROWS)], sem_rows.at[slot])

    @pl.when(step == 0)
    def _():
        ybuf[...] = jnp.zeros_like(ybuf)
        _start_chunks(tab_ref, tile0, copies(0))

    @pl.when(step + 1 < n_steps)
    def _():
        _start_chunks(tab_ref, tile0 + step + 1, copies(step + 1))

    _wait_chunks(tab_ref, tile0 + step, copies(step))
    tm = x_ref.shape[1]
    col = lax.broadcasted_iota(jnp.int32, (tm, BUF_ROWS), 1).astype(F32)
    pick = jnp.where((col == pos_ref[:, 0:1]) | (col == pos_ref[:, 1:2]), 1.0, 0.0).astype(BF16)
    out = x_ref[0] + gate_ref[0] * jnp.dot(pick, _unpack_halves(ybuf[step % 2]), preferred_element_type=F32)
    if final_norm:
        out = out * lax.rsqrt(jnp.mean(out * out, axis=-1, keepdims=True) + NORM_EPS) * fg_ref[...]
    o_ref[0] = out


def _combine(ys, table, pos, x, gate, tile0, final_g=None):
    b, n, d = x.shape
    tm = min(TOKEN_TILE, n)
    n_tiles = n // tm
    bm = gate.shape[0]
    mod_map = (lambda bi, i, tab: (bi, 0, 0)) if bm > 1 else (lambda bi, i, tab: (0, 0, 0))
    final_norm = final_g is not None
    fg = final_g.reshape(1, d) if final_norm else jnp.ones((1, d), F32)
    grid_spec = pltpu.PrefetchScalarGridSpec(
        num_scalar_prefetch=1,
        grid=(b, n_tiles),
        in_specs=[
            pl.BlockSpec(memory_space=pl.ANY),
            pl.BlockSpec((1, tm, d), lambda bi, i, tab: (bi, i, 0)),
            pl.BlockSpec((1, 1, d), mod_map),
            pl.BlockSpec((tm, LANES), lambda bi, i, tab: (tile0 + bi * n_tiles + i, 0)),
            pl.BlockSpec((1, d), lambda bi, i, tab: (0, 0)),
        ],
        out_specs=pl.BlockSpec((1, tm, d), lambda bi, i, tab: (bi, i, 0)),
        scratch_shapes=[
            pltpu.VMEM((2, BUF_ROWS, d // 2), jnp.int32),
            pltpu.SemaphoreType.DMA((2,)),
        ],
    )
    return pl.pallas_call(
        functools.partial(_combine_body, tile0=tile0, n_tiles=n_tiles, n_steps=b * n_tiles,
                          final_norm=final_norm),
        grid_spec=grid_spec,
        out_shape=jax.ShapeDtypeStruct((b, n, d), F32),
        compiler_params=_params("arbitrary", "arbitrary"),
        name="moe_combine",
    )(table.reshape(-1), ys, x, gate, pos, fg)


def _router_weights(wg, bg, we, be):
    d = wg.shape[0]
    w = jnp.zeros((d, LANES), F32).at[:, :N_GROUPS].set(wg).at[:, N_GROUPS:N_GROUPS + N_EXPERTS].set(we)
    bias = jnp.zeros((1, LANES), F32).at[0, :N_GROUPS].set(bg).at[0, N_GROUPS:N_GROUPS + N_EXPERTS].set(be)
    hi = w.astype(BF16)
    lo = (w - hi.astype(F32)).astype(BF16)
    return hi, lo, bias


def kernel(x, c, ctx, c_ctx, ada_w, ada_b, norm1_g, norm2_g, ev_w_in, ev_w_out, hy_conv_w, hy_conv_b, hy_f_w1, hy_f_b1, hy_f_w2, hy_f_b2, hy_f_w3, hy_f_freq, hy_skip, swa_sink, od_w_qkv, od_w_out, od_q_norm_g, od_k_norm_g, rt_group_w, rt_group_b, rt_exp_w, rt_exp_b, moe_w_gate, moe_w_up, moe_w_down, final_norm_g):
    b, n, d = x.shape
    lc = ctx.shape[1]
    depth = ada_w.shape[0]
    rope = _rope_tables(n)
    xc = ctx
    sc = jax.nn.silu(c)
    scc = jax.nn.silu(c_ctx)
    q_scale = HEAD_DIM ** -0.5
    for layer in range(depth):
        with_ctx = layer < depth - 1
        mod = (sc @ ada_w[layer] + ada_b[layer]).reshape(b, N_MOD, 1, d)
        modc = (scc @ ada_w[layer] + ada_b[layer]).reshape(1, N_MOD, 1, d)
        m = [mod[:, k] for k in range(N_MOD)]
        mc = [modc[:, k] for k in range(N_MOD)]
        r_hi, r_lo, r_b = _router_weights(rt_group_w[layer], rt_group_b[layer], rt_exp_w[layer], rt_exp_b[layer])
        if layer % 2 == 0:
            e = layer // 2
            c_hy = hy_conv_w.shape[-1] // 3
            d_hy = 3 * c_hy
            hq = swa_sink.shape[-1]
            d_q = hq * HEAD_DIM
            hkv = hq // 4
            d_kv = hkv * HEAD_DIM
            w_in = ev_w_in[e].astype(BF16)
            w_out = ev_w_out[e].astype(BF16)
            hy_args = (hy_conv_w[e], hy_conv_b[e], hy_f_w1[e], hy_f_b1[e], hy_f_w2[e], hy_f_b2[e],
                       hy_f_w3[e], hy_f_freq[e], hy_skip[e])
            u, q, k, v = _proj(x, norm1_g[layer], m[1], m[0], w_in, [
                (0, d_hy, "f32", None, False, 1.0),
                (d_hy, d_q, "qk", None, True, q_scale),
                (d_hy + d_q, d_kv, "qk", None, True, 1.0),
                (d_hy + d_q + d_kv, d_kv, "bf16", None, False, 1.0)], rope_tabs=rope)
            if with_ctx:
                uc, qc, kc, vc = _proj(xc, norm1_g[layer], mc[1], mc[0], w_in, [
                    (0, d_hy, "f32", None, False, 1.0),
                    (d_hy, d_q, "qk", None, False, q_scale),
                    (d_hy + d_q, d_kv, "bf16", None, False, 1.0),
                    (d_hy + d_q + d_kv, d_kv, "bf16", None, False, 1.0)])
            else:
                kc, vc = _proj(xc, norm1_g[layer], mc[1], mc[0], w_in, [
                    (d_hy + d_q, d_kv, "bf16", None, False, 1.0),
                    (d_hy + d_q + d_kv, d_kv, "bf16", None, False, 1.0)])
            y_hy = _hyena(u, *hy_args)
            y_att = _windowed_attention(q, k, v, kc, vc, swa_sink[e], hkv)
            mix_in, mix_w = [y_hy, y_att], [w_out[:c_hy], w_out[c_hy:]]
            if with_ctx:
                yc_hy = _hyena(uc, *hy_args)
                yc_att = _full_attention(qc, kc, vc, hkv, sink=swa_sink[e])
                mixc_in = [yc_hy, yc_att]
        else:
            o = layer // 2
            hkv = od_w_qkv.shape[-1] // HEAD_DIM // 6
            hq = 4 * hkv
            d_q = hq * HEAD_DIM
            d_kv = hkv * HEAD_DIM
            w_qkv = od_w_qkv[o].astype(BF16)
            w_out = od_w_out[o].astype(BF16)
            norm_g = jnp.zeros((8, LANES), F32).at[0].set(jnp.tile(od_q_norm_g[o], 2)).at[1].set(
                jnp.tile(od_k_norm_g[o], 2))
            q, k, v = _proj(x, norm1_g[layer], m[1], m[0], w_qkv, [
                (0, d_q, "qk", 0, True, q_scale),
                (d_q, d_kv, "qk", 1, True, 1.0),
                (d_q + d_kv, d_kv, "bf16", None, False, 1.0)], rope_tabs=rope, norm_g=norm_g)
            if with_ctx:
                qc, kc, vc = _proj(xc, norm1_g[layer], mc[1], mc[0], w_qkv, [
                    (0, d_q, "qk", 0, False, q_scale),
                    (d_q, d_kv, "qk", 1, False, 1.0),
                    (d_q + d_kv, d_kv, "bf16", None, False, 1.0)], norm_g=norm_g)
            else:
                kc, vc = _proj(xc, norm1_g[layer], mc[1], mc[0], w_qkv, [
                    (d_q, d_kv, "qk", 1, False, 1.0),
                    (d_q + d_kv, d_kv, "bf16", None, False, 1.0)], norm_g=norm_g)
            y_att = _full_attention(q, jnp.concatenate([kc, k], axis=1), jnp.concatenate([vc, v], axis=1), hkv)
            mix_in, mix_w = [y_att], [w_out]
            if with_ctx:
                mixc_in = [_full_attention(qc, kc, vc, hkv)]
        x, h2, rt = _outproj(mix_in, mix_w, x, m[2], norm2_g[layer], m[4], m[3], r_hi, r_lo, r_b)
        route_flat = rt.reshape(b * n, LANES)
        if with_ctx:
            xc, h2c, rtc = _outproj(mixc_in, mix_w, xc, mc[2], norm2_g[layer], mc[4], mc[3], r_hi, r_lo, r_b)
            route_flat = jnp.concatenate([route_flat, rtc.reshape(b * lc, LANES)], axis=0)
        n_tok = route_flat.shape[0]
        n_tiles = n_tok // TOKEN_TILE
        lat_tiles = b * n // TOKEN_TILE
        max_rows = n_tok * TOP_K + n_tiles * N_EXPERTS * (CHUNK_ROWS - 1)
        n_blocks = -(-max_rows // EXPERT_ROWS) + N_EXPERTS
        pos, cnt = _rank(route_flat)
        table, block_exp, n_used = _chunk_tables(cnt[:, 0, :N_EXPERTS].astype(jnp.int32), n_blocks)
        pos_t = jnp.swapaxes(pos[:, :8].reshape(n_tiles, TOKEN_TILE, 8), 1, 2)
        xs = jnp.zeros((n_blocks * EXPERT_ROWS, d // 2 + LANES), jnp.int32)
        xs = _dispatch(h2.reshape(b * n, d), pos_t, table, xs, 0)
        if with_ctx:
            xs = _dispatch(h2c.reshape(b * lc, d), pos_t, table, xs, lat_tiles)
        ys = _experts(xs, block_exp, n_used, moe_w_gate, moe_w_up, moe_w_down, layer)
        x = _combine(ys, table, pos, x, m[5], 0, final_g=None if with_ctx else final_norm_g)
        if with_ctx:
            xc = _combine(ys, table, pos, xc.reshape(b * lc // TOKEN_TILE, TOKEN_TILE, d), mc[5],
                          lat_tiles).reshape(b, lc, d)
    return x
```
